```python
import jax, jax.numpy as jnp
from jax import lax
import numpy as np

D_MODEL = 1024
BATCH = 32
SEQ = 256
DEPTH = 2
DEC_BATCH = 2
DEC_SEQ = 4096
PAST_LEN = 512

GRID_W = 64
EPS = 1e-6
N_BRANCH = 3
FN_WIDTH = D_MODEL // 4
FN_GROUPS = 4
POOL_WINDOWS = (2, 4, 8, 16)
POOL_WIDTH = D_MODEL // 4
POOL_GC = POOL_WIDTH // len(POOL_WINDOWS)
N_HEADS = 8
QK_NOPE = D_MODEL // 16
QK_ROPE = D_MODEL // 32
V_DIM = D_MODEL // 16
Q_RANK = D_MODEL // 4
KV_RANK = D_MODEL // 8
ATT_WIDTH = N_HEADS * V_DIM
ROPE_THETA = 10000.0
Q_BLOCK = 128
IN_SIZES = (FN_WIDTH, FN_WIDTH, POOL_WIDTH, POOL_WIDTH, Q_RANK, KV_RANK, QK_ROPE, ATT_WIDTH, N_BRANCH * D_MODEL)
IN_WIDTH = sum(IN_SIZES)

kernel_name = "hybrid_fourier_pool_mla_diffusion_step"


def _rmsnorm(x, g):
    xf = x.astype(jnp.float32)
    r = lax.rsqrt(jnp.mean(xf * xf, axis=-1, keepdims=True) + EPS)
    return (xf * r).astype(x.dtype) * g


def _split_cols(p):
    parts, off = [], 0
    for s in IN_SIZES:
        parts.append(p[..., off:off + s])
        off += s
    return parts


def _axial_rope(L, dtype):
    rows = L // GRID_W
    t = jnp.arange(L)
    row = jnp.repeat(jnp.arange(rows), GRID_W).astype(jnp.float32)
    col = (t % GRID_W).astype(jnp.float32)
    half = QK_ROPE // 2
    freqs = ROPE_THETA ** (-jnp.arange(0, half, 2, dtype=jnp.float32) / half)
    ar = row[:, None] * freqs
    ac = col[:, None] * freqs
    cos = jnp.concatenate([jnp.cos(ar), jnp.cos(ar), jnp.cos(ac), jnp.cos(ac)], axis=-1)
    sin = jnp.concatenate([jnp.sin(ar), jnp.sin(ar), jnp.sin(ac), jnp.sin(ac)], axis=-1)
    return cos.astype(dtype), sin.astype(dtype)


def _rot_half(x):
    h = x.shape[-1] // 2
    return jnp.concatenate([-x[..., h:], x[..., :h]], axis=-1)


def _apply_rope(x, cos, sin):
    half = QK_ROPE // 2
    rx = jnp.concatenate([_rot_half(x[..., :half]), _rot_half(x[..., half:])], axis=-1)
    return x * cos + rx * sin


def _fourier_mix(u):
    B, L, _ = u.shape
    ug = u.reshape(B, L, FN_GROUPS, FN_WIDTH // FN_GROUPS).astype(jnp.float32)
    f = jnp.fft.fft2(ug, axes=(1, 3), norm="ortho").real
    return f.reshape(B, L, FN_WIDTH).astype(u.dtype)


def _pool_mix(u, pool_w, pool_scale):
    B, L, _ = u.shape
    uf = u.astype(jnp.float32)
    cs = jnp.concatenate([jnp.zeros((B, 1, POOL_WIDTH), jnp.float32), jnp.cumsum(uf, axis=1)], axis=1)
    t = jnp.arange(L)
    outs = []
    for gi, w in enumerate(POOL_WINDOWS):
        left = w // 2
        right = w - 1 - left
        lo = jnp.clip(t - left, 0, L - 1)
        hi = jnp.clip(t + right, 0, L - 1)
        cnt = (hi - lo + 1).astype(jnp.float32)
        csg = cs[..., gi * POOL_GC:(gi + 1) * POOL_GC]
        outs.append((csg[:, hi + 1] - csg[:, lo]) / cnt[None, :, None])
    pooled = (jnp.concatenate(outs, axis=-1) - uf).astype(u.dtype)
    pg = pooled.reshape(B, L, len(POOL_WINDOWS), POOL_GC)
    mixed = jnp.einsum('blgc,gcd->blgd', pg, pool_w).reshape(B, L, POOL_WIDTH)
    return mixed * pool_scale


def _block_attention(q, k, v):
    B, Lq, H, Dq = q.shape
    nb = Lq // Q_BLOCK
    qb = q.reshape(B, nb, Q_BLOCK, H, Dq).transpose(1, 0, 2, 3, 4)
    scale = Dq ** -0.5

    def one(qblk):
        s = jnp.einsum('bqhd,bkhd->bhqk', qblk, k, preferred_element_type=jnp.float32) * scale
        p = jax.nn.softmax(s, axis=-1).astype(v.dtype)
        return jnp.einsum('bhqk,bkhd->bqhd', p, v)

    o = lax.map(one, qb)
    return o.transpose(1, 0, 2, 3, 4).reshape(B, Lq, H, v.shape[-1])


def _layer(h, mod, w, rope, ctx):
    B, L, _ = h.shape
    shift, scale, gate = jnp.split(mod, 3, axis=-1)
    xn = _rmsnorm(h, w['norm_g']) * (1.0 + scale[..., None, :]) + shift[..., None, :]
    a_in, a_z, b_in, b_z, q_lat, kv_lat, k_rope, c_z, g_all = _split_cols(xn @ w['w_in'])

    ya = (_fourier_mix(a_in) * jax.nn.silu(a_z)) @ w['w_br_a']
    yb = (_pool_mix(b_in, w['pool_w'], w['pool_scale']) * jax.nn.silu(b_z)) @ w['w_br_b']

    q = (_rmsnorm(q_lat, w['q_norm_g']) @ w['w_q_up']).reshape(B, L, N_HEADS, QK_NOPE + QK_ROPE)
    q_nope, q_rope = q[..., :QK_NOPE], q[..., QK_NOPE:]
    ckv = _rmsnorm(kv_lat, w['kv_norm_g'])
    if rope is not None:
        cos, sin = rope
        q_rope = _apply_rope(q_rope, cos[:, None, :], sin[:, None, :])
        k_rope_p = _apply_rope(k_rope, cos, sin)
    else:
        k_rope_p = k_rope
    q_full = jnp.concatenate([q_nope, q_rope], axis=-1)
    if ctx is not None:
        ckv_all = jnp.concatenate([ckv, ctx[0]], axis=1)
        kr_all = jnp.concatenate([k_rope_p, ctx[1]], axis=1)
    else:
        ckv_all, kr_all = ckv, k_rope_p
    Lk = ckv_all.shape[1]
    kv = (ckv_all @ w['w_kv_up']).reshape(B, Lk, N_HEADS, QK_NOPE + V_DIM)
    k_nope, v = kv[..., :QK_NOPE], kv[..., QK_NOPE:]
    k = jnp.concatenate([k_nope, jnp.broadcast_to(kr_all[:, :, None, :], (B, Lk, N_HEADS, QK_ROPE))], axis=-1)
    o = _block_attention(q_full, k, v).reshape(B, L, ATT_WIDTH)
    yc = (o * jax.nn.silu(c_z)) @ w['w_br_c']

    g = jax.nn.sigmoid(g_all).reshape(B, L, N_BRANCH, D_MODEL)
    y = g[:, :, 0] * ya + g[:, :, 1] * yb + g[:, :, 2] * yc
    h_new = h + gate[..., None, :] * (y @ w['w_out'])
    return h_new, ckv, k_rope


def setup_inputs(seed: int = 0) -> dict:
    key = jax.random.key(seed)
    ks = jax.random.split(key, 24)
    f32 = jnp.float32

    def nrm(k, shape, s):
        return jax.random.normal(k, shape, f32) * s

    return {
        'x_prompt': nrm(ks[0], (BATCH, SEQ, D_MODEL), 1.0),
        'x_sample': nrm(ks[1], (DEC_BATCH, DEC_SEQ, D_MODEL), 1.0),
        'cache_ckv': nrm(ks[2], (DEC_BATCH, DEPTH, PAST_LEN, KV_RANK), 1.0),
        'cache_krope': nrm(ks[3], (DEC_BATCH, DEPTH, PAST_LEN, QK_ROPE), 1.0),
        'c': nrm(ks[4], (DEC_BATCH, D_MODEL), 1.0),
        'c_ctx': nrm(ks[5], (D_MODEL,), 1.0),
        'norm_g': 1.0 + nrm(ks[6], (DEPTH, D_MODEL), 0.02),
        'w_mod': nrm(ks[7], (DEPTH, D_MODEL, 3 * D_MODEL), 0.5 * D_MODEL ** -0.5),
        'b_mod': nrm(ks[8], (DEPTH, 3 * D_MODEL), 0.01),
        'w_in': nrm(ks[9], (DEPTH, D_MODEL, IN_WIDTH), D_MODEL ** -0.5),
        'pool_w': nrm(ks[10], (DEPTH, len(POOL_WINDOWS), POOL_GC, POOL_GC), POOL_GC ** -0.5),
        'pool_scale': 1.0 + nrm(ks[11], (DEPTH, POOL_WIDTH), 0.02),
        'q_norm_g': 1.0 + nrm(ks[12], (DEPTH, Q_RANK), 0.02),
        'w_q_up': nrm(ks[13], (DEPTH, Q_RANK, N_HEADS * (QK_NOPE + QK_ROPE)), Q_RANK ** -0.5),
        'kv_norm_g': 1.0 + nrm(ks[14], (DEPTH, KV_RANK), 0.02),
        'w_kv_up': nrm(ks[15], (DEPTH, KV_RANK, N_HEADS * (QK_NOPE + V_DIM)), KV_RANK ** -0.5),
        'w_br_a': nrm(ks[16], (DEPTH, FN_WIDTH, D_MODEL), FN_WIDTH ** -0.5),
        'w_br_b': nrm(ks[17], (DEPTH, POOL_WIDTH, D_MODEL), POOL_WIDTH ** -0.5),
        'w_br_c': nrm(ks[18], (DEPTH, ATT_WIDTH, D_MODEL), ATT_WIDTH ** -0.5),
        'w_out': nrm(ks[19], (DEPTH, D_MODEL, D_MODEL), D_MODEL ** -0.5),
        'final_norm_g': 1.0 + nrm(ks[20], (D_MODEL,), 0.02),
    }


def reference(x_prompt, x_sample, cache_ckv, cache_krope, c, c_ctx, norm_g, w_mod, b_mod, w_in, pool_w,
              pool_scale, q_norm_g, w_q_up, kv_norm_g, w_kv_up, w_br_a, w_br_b, w_br_c, w_out, final_norm_g):
    rope = _axial_rope(x_sample.shape[1], x_sample.dtype)
    hp, hs = x_prompt, x_sample
    ckv_list, kr_list = [], []
    for l in range(DEPTH):
        w = {'norm_g': norm_g[l], 'w_in': w_in[l], 'pool_w': pool_w[l], 'pool_scale': pool_scale[l],
             'q_norm_g': q_norm_g[l], 'w_q_up': w_q_up[l], 'kv_norm_g': kv_norm_g[l], 'w_kv_up': w_kv_up[l],
             'w_br_a': w_br_a[l], 'w_br_b': w_br_b[l], 'w_br_c': w_br_c[l], 'w_out': w_out[l]}
        mod_ctx = jax.nn.silu(c_ctx) @ w_mod[l] + b_mod[l]
        mod_lat = jax.nn.silu(c) @ w_mod[l] + b_mod[l]
        hp, ckv_l, kr_l = _layer(hp, mod_ctx, w, None, None)
        ckv_list.append(ckv_l)
        kr_list.append(kr_l)
        hs, _, _ = _layer(hs, mod_lat, w, rope, (cache_ckv[:, l], cache_krope[:, l]))
    y_prompt = _rmsnorm(hp, final_norm_g)
    y_sample = _rmsnorm(hs, final_norm_g)
    new_ckv = jnp.stack(ckv_list, axis=1)
    new_krope = jnp.stack(kr_list, axis=1)
    return (y_prompt, y_sample, new_ckv, new_krope)
```

```python
import functools

import numpy as np
import jax
import jax.numpy as jnp
from jax import lax
from jax.experimental import pallas as pl
from jax.experimental.pallas import tpu as pltpu

D_MODEL = 1024
DEPTH = 2
GRID_W = 64
EPS = 1e-6
FN_WIDTH = 256
FN_GC = 64
POOL_WINDOWS = (2, 4, 8, 16)
POOL_WIDTH = 256
POOL_GC = 64
N_HEADS = 8
QK_NOPE = 64
QK_ROPE = 32
V_DIM = 64
Q_RANK = 256
KV_RANK = 128
ATT_WIDTH = 512
ROPE_THETA = 10000.0
HEAD_SLAB = 128
QK_SCALE = (QK_NOPE + QK_ROPE) ** -0.5

VMEM_LIMIT_BYTES = 56 * 1024 * 1024

F32 = jnp.float32
BF16 = jnp.bfloat16

_OFF_A, _OFF_B, _OFF_Q, _OFF_KV, _OFF_KR, _OFF_CZ, _OFF_G = 0, 512, 1024, 1280, 1408, 1440, 1952
_W1_A = (0, 512)
_W1_B = (512, 1024)
_W1_Q = (1024, 1280)
_W1_CZ = (1280, 1792)
_W1_KV = (1792, 2176)
W1_WIDTH = 2176


def _params(n_parallel):
    return pltpu.CompilerParams(dimension_semantics=("arbitrary",) * n_parallel,
                                vmem_limit_bytes=VMEM_LIMIT_BYTES)


def _dot(a, b):
    return jnp.dot(a, b, preferred_element_type=F32)


def _silu(x):
    return x * jax.nn.sigmoid(x)


def _rms(x, g):
    r = lax.rsqrt(jnp.mean(x * x, axis=-1, keepdims=True) + EPS)
    return (x * r) * g


def _modulated_norm(h, norm_g, mod_ref):
    shift = mod_ref[0:1, :]
    scale = mod_ref[1:2, :]
    return _rms(h, norm_g) * (1.0 + scale) + shift


def _split_bf16(x):
    hi = x.astype(BF16)
    lo = (x - hi.astype(F32)).astype(BF16)
    return hi, lo


def _dot3_right(x, m_hi, m_lo):
    x_hi, x_lo = _split_bf16(x)
    return _dot(x_hi, m_hi) + _dot(x_lo, m_hi) + _dot(x_hi, m_lo)


def _dot3_left(m_hi, m_lo, x):
    x_hi, x_lo = _split_bf16(x)
    return _dot(m_hi, x_hi) + _dot(m_hi, x_lo) + _dot(m_lo, x_hi)


def _mod_kernel(c_ref, w_ref, b_ref, o_ref):
    s = _silu(c_ref[...]).astype(BF16)
    o_ref[...] = _dot(s, w_ref[...].astype(BF16)) + b_ref[...]


def _modulation(cvec, w_mod, b_mod):
    rows = cvec.shape[0]
    tn = 768
    return pl.pallas_call(
        _mod_kernel,
        grid=(DEPTH, 3 * D_MODEL // tn),
        in_specs=[pl.BlockSpec((rows, D_MODEL), lambda l, j: (0, 0)),
                  pl.BlockSpec((None, D_MODEL, tn), lambda l, j: (l, 0, j)),
                  pl.BlockSpec((None, 1, tn), lambda l, j: (l, 0, j))],
        out_specs=pl.BlockSpec((None, rows, tn), lambda l, j: (l, 0, j)),
        out_shape=jax.ShapeDtypeStruct((DEPTH, rows, 3 * D_MODEL), F32),
        compiler_params=_params(2),
        name="modulation",
    )(cvec, w_mod, b_mod.reshape(DEPTH, 1, 3 * D_MODEL))


def _key_value_heads(ckv_bf16, kr_slab, wk_ref, wv_ref, k_ref, v_ref):
    kn = _dot(ckv_bf16, wk_ref[...])
    for h in range(N_HEADS):
        sl = slice(HEAD_SLAB * h, HEAD_SLAB * (h + 1))
        k_ref[:, sl] = (kn[:, sl] + kr_slab).astype(BF16)
    v_ref[...] = _dot(ckv_bf16, wv_ref[...]).astype(BF16)


def _inproj_kernel(*refs, use_rope):
    if use_rope:
        (h_ref, mod_ref, ng_ref, w1_ref, qg_ref, wq_ref, wqr_ref, kvg_ref, wk_ref, wv_ref, cos_ref, sin_ref,
         ain_ref, sa_ref, bin_ref, sb_ref, q_ref, k_ref, v_ref, ckv_ref, kr_ref, sc_ref) = refs
    else:
        (h_ref, mod_ref, ng_ref, w1_ref, qg_ref, wq_ref, kvg_ref, wk_ref, wv_ref,
         ain_ref, sa_ref, bin_ref, sb_ref, q_ref, k_ref, v_ref, ckv_ref, kr_ref, sc_ref) = refs

    xn = _modulated_norm(h_ref[...], ng_ref[...], mod_ref).astype(BF16)

    def proj(cols):
        return _dot(xn, w1_ref[:, cols[0]:cols[1]])

    a = proj(_W1_A)
    ain_ref[...] = a[:, :FN_WIDTH]
    sa_ref[...] = _silu(a[:, FN_WIDTH:]).astype(BF16)
    b = proj(_W1_B)
    bin_ref[...] = b[:, :POOL_WIDTH]
    sb_ref[...] = _silu(b[:, POOL_WIDTH:]).astype(BF16)
    sc_ref[...] = _silu(proj(_W1_CZ)).astype(BF16)

    qn = _rms(proj(_W1_Q), qg_ref[...]).astype(BF16)
    q = _dot(qn, wq_ref[...])
    if use_rope:
        cos = cos_ref[...]
        sin = sin_ref[...]
        qr = _dot(qn, wqr_ref[...])
        for h in range(N_HEADS):
            sl = slice(HEAD_SLAB * h, HEAD_SLAB * (h + 1))
            q_ref[:, sl] = ((q[:, sl] * cos + qr[:, sl] * sin) * QK_SCALE).astype(BF16)
    else:
        q_ref[...] = (q * QK_SCALE).astype(BF16)

    kv = proj(_W1_KV)
    ckv = _rms(kv[:, :KV_RANK], kvg_ref[...])
    ckv_ref[...] = ckv
    kr = kv[:, KV_RANK:KV_RANK + HEAD_SLAB]
    kr_ref[...] = kr
    if use_rope:
        kr = kr * cos + kv[:, KV_RANK + HEAD_SLAB:] * sin
    _key_value_heads(ckv.astype(BF16), kr, wk_ref, wv_ref, k_ref, v_ref)


def _inproj(h, mod3, mod_row, lw, rope, tm):
    t = h.shape[0]
    use_rope = rope is not None
    const = lambda i: (0, 0)
    row = lambda i: (i, 0)
    in_specs = [pl.BlockSpec((tm, D_MODEL), row),
                pl.BlockSpec((None, 3, D_MODEL), lambda i: (mod_row(i), 0, 0)),
                pl.BlockSpec((1, D_MODEL), const),
                pl.BlockSpec((D_MODEL, W1_WIDTH), const),
                pl.BlockSpec((1, Q_RANK), const),
                pl.BlockSpec((Q_RANK, N_HEADS * HEAD_SLAB), const)]
    args = [h, mod3, lw['norm_g'], lw['w1'], lw['q_norm_g'], lw['wq']]
    if use_rope:
        in_specs.append(pl.BlockSpec((Q_RANK, N_HEADS * HEAD_SLAB), const))
        args.append(lw['wq_rot'])
    in_specs += [pl.BlockSpec((1, KV_RANK), const),
                 pl.BlockSpec((KV_RANK, N_HEADS * HEAD_SLAB), const),
                 pl.BlockSpec((KV_RANK, N_HEADS * HEAD_SLAB), const)]
    args += [lw['kv_norm_g'], lw['wk'], lw['wv']]
    if use_rope:
        in_specs += [pl.BlockSpec((tm, HEAD_SLAB), row), pl.BlockSpec((tm, HEAD_SLAB), row)]
        args += [rope[0], rope[1]]
    widths = [(FN_WIDTH, F32), (FN_WIDTH, BF16), (POOL_WIDTH, F32), (POOL_WIDTH, BF16),
              (N_HEADS * HEAD_SLAB, BF16), (N_HEADS * HEAD_SLAB, BF16), (N_HEADS * HEAD_SLAB, BF16),
              (KV_RANK, F32), (HEAD_SLAB, F32), (ATT_WIDTH, BF16)]
    return pl.pallas_call(
        functools.partial(_inproj_kernel, use_rope=use_rope),
        grid=(t // tm,),
        in_specs=in_specs,
        out_specs=[pl.BlockSpec((tm, w), row) for w, _ in widths],
        out_shape=[jax.ShapeDtypeStruct((t, w), dt) for w, dt in widths],
        compiler_params=_params(1),
        name="inproj_rope" if use_rope else "inproj",
    )(*args)


def _cache_kv_kernel(ckv_ref, kr_ref, wk_ref, wv_ref, k_ref, v_ref):
    _key_value_heads(ckv_ref[...].astype(BF16), kr_ref[...], wk_ref, wv_ref, k_ref, v_ref)


def _cache_kv(ckv, kr_slab, lw, rows):
    t = ckv.shape[0]
    const = lambda i: (0, 0)
    row = lambda i: (i, 0)
    wide = N_HEADS * HEAD_SLAB
    return pl.pallas_call(
        _cache_kv_kernel,
        grid=(t // rows,),
        in_specs=[pl.BlockSpec((rows, KV_RANK), row), pl.BlockSpec((rows, HEAD_SLAB), row),
                  pl.BlockSpec((KV_RANK, wide), const), pl.BlockSpec((KV_RANK, wide), const)],
        out_specs=[pl.BlockSpec((rows, wide), row), pl.BlockSpec((rows, wide), row)],
        out_shape=[jax.ShapeDtypeStruct((t, wide), BF16)] * 2,
        compiler_params=_params(1),
        name="cache_kv",
    )(ckv, kr_slab, lw['wk'], lw['wv'])


def _hi_lo(m):
    m = jnp.asarray(m, F32)
    hi = m.astype(BF16)
    return hi, (m - hi.astype(F32)).astype(BF16)


def _dft_cos_sin(n):
    k = np.arange(n)
    ang = 2.0 * np.pi * ((k[:, None] * k[None, :]) % n) / n
    return np.cos(ang), np.sin(ang)


def _channel_dft_tables():
    c, s = _dft_cos_sin(FN_GC)
    eye = np.eye(FN_WIDTH // FN_GC)
    return np.kron(eye, c), np.kron(eye, s)


def _fourier_direct_kernel(a_ref, sa_ref, cc_hi, cc_lo, ss_hi, ss_lo, cl_hi, cl_lo, sl_hi, sl_lo, o_ref, *, norm):
    a = a_ref[...]
    tc = _dot3_right(a, cc_hi[...], cc_lo[...])
    ts = _dot3_right(a, ss_hi[...], ss_lo[...])
    f = _dot3_left(cl_hi[...], cl_lo[...], tc) - _dot3_left(sl_hi[...], sl_lo[...], ts)
    o_ref[...] = ((f * norm) * sa_ref[...].astype(F32)).astype(BF16)


def _fourier_direct(a, sa, seq):
    t = a.shape[0]
    cc, ss = _channel_dft_tables()
    cl, sl = _dft_cos_sin(seq)
    tables = [x for m in (cc, ss, cl, sl) for x in _hi_lo(m)]
    row = lambda i: (i, 0)
    const = lambda i: (0, 0)
    tspecs = [pl.BlockSpec((FN_WIDTH, FN_WIDTH), const)] * 4 + [pl.BlockSpec((seq, seq), const)] * 4
    return pl.pallas_call(
        functools.partial(_fourier_direct_kernel, norm=float((seq * FN_GC) ** -0.5)),
        grid=(t // seq,),
        in_specs=[pl.BlockSpec((seq, FN_WIDTH), row), pl.BlockSpec((seq, FN_WIDTH), row)] + tspecs,
        out_specs=pl.BlockSpec((seq, FN_WIDTH), row),
        out_shape=jax.ShapeDtypeStruct((t, FN_WIDTH), BF16),
        compiler_params=_params(1),
        name="fourier_direct",
    )(a, sa, *tables)


FFT_R = 64


def _fourier_fft_kernel(a_ref, sa_ref, cc_hi, cc_lo, ss_hi, ss_lo, m1_hi, m1_lo, m2_hi, m2_lo, twc_ref, tws_ref,
                        o_ref, zr0, zr1, zi0, zi1, yr0, yr1, yi0, yi1, *, seq, norm):
    r = FFT_R
    half = FN_WIDTH // 2
    chunk = 512

    def put(refs, rows, x):
        refs[0][rows, :] = x[:, :half]
        refs[1][rows, :] = x[:, half:]

    def get(refs, rows):
        return jnp.concatenate([refs[0][rows, :], refs[1][rows, :]], axis=1)

    zr, zi, yr_s, yi_s = (zr0, zr1), (zi0, zi1), (yr0, yr1), (yi0, yi1)
    for c in range(seq // chunk):
        rows = slice(c * chunk, (c + 1) * chunk)
        a = a_ref[rows, :]
        put(zr, rows, _dot3_right(a, cc_hi[...], cc_lo[...]))
        put(zi, rows, -_dot3_right(a, ss_hi[...], ss_lo[...]))
    for n2 in range(r):
        strided = pl.ds(n2, r, stride=r)
        z = jnp.concatenate([get(zr, strided), get(zi, strided)], axis=0)
        y = _dot3_left(m1_hi[...], m1_lo[...], z)
        yr, yi = y[:r], y[r:]
        blk = slice(n2 * r, (n2 + 1) * r)
        cos = jnp.concatenate([twc_ref[blk, :]] * 2, axis=1)
        sin = jnp.concatenate([tws_ref[blk, :]] * 2, axis=1)
        put(yr_s, blk, yr * cos + yi * sin)
        put(yi_s, blk, yi * cos - yr * sin)
    for k1 in range(r):
        strided = pl.ds(k1, r, stride=r)
        y = jnp.concatenate([get(yr_s, strided), get(yi_s, strided)], axis=0)
        put(zr, strided, _dot3_left(m2_hi[...], m2_lo[...], y))
    for c in range(seq // chunk):
        rows = slice(c * chunk, (c + 1) * chunk)
        o_ref[rows, :] = ((get(zr, rows) * norm) * sa_ref[rows, :].astype(F32)).astype(BF16)


def _fourier_fft(a, sa, seq):
    assert seq == FFT_R * FFT_R
    t = a.shape[0]
    cc, ss = _channel_dft_tables()
    c, s = _dft_cos_sin(FFT_R)
    m1 = np.block([[c, s], [-s, c]])
    m2 = np.concatenate([c, s], axis=1)
    n2 = np.arange(FFT_R)[:, None]
    k1 = np.arange(FFT_R)[None, :]
    ang = (2.0 * np.pi * (n2 * k1) / seq).reshape(seq, 1)
    twc = jnp.broadcast_to(jnp.asarray(np.cos(ang), F32), (seq, 128))
    tws = jnp.broadcast_to(jnp.asarray(np.sin(ang), F32), (seq, 128))
    tables = [x for m in (cc, ss, m1, m2) for x in _hi_lo(m)]
    row = lambda i: (i, 0)
    const = lambda i: (0, 0)
    tspecs = ([pl.BlockSpec((FN_WIDTH, FN_WIDTH), const)] * 4 + [pl.BlockSpec((2 * FFT_R, 2 * FFT_R), const)] * 2
              + [pl.BlockSpec((FFT_R, 2 * FFT_R), const)] * 2 + [pl.BlockSpec((seq, 128), const)] * 2)
    return pl.pallas_call(
        functools.partial(_fourier_fft_kernel, seq=seq, norm=float((seq * FN_GC) ** -0.5)),
        grid=(t // seq,),
        in_specs=[pl.BlockSpec((seq, FN_WIDTH), row), pl.BlockSpec((seq, FN_WIDTH), row)] + tspecs,
        out_specs=pl.BlockSpec((seq, FN_WIDTH), row),
        out_shape=jax.ShapeDtypeStruct((t, FN_WIDTH), BF16),
        scratch_shapes=[pltpu.VMEM((seq, FN_WIDTH // 2), F32)] * 8,
        compiler_params=_params(1),
        name="fourier_fft",
    )(a, sa, *tables, twc, tws)


POOL_HALO = 8
POOL_CHUNK = 256


def _pool_kernel(b_ref, sb_ref, pw_ref, ps_ref, o_ref, pad_ref, *, seq):
    zeros = jnp.zeros((POOL_HALO, POOL_WIDTH), F32)
    pad_ref[0:POOL_HALO, :] = zeros
    pad_ref[POOL_HALO + seq:, :] = zeros
    pad_ref[POOL_HALO:POOL_HALO + seq, :] = b_ref[...]
    lane = lax.broadcasted_iota(jnp.int32, (POOL_CHUNK, 128), 1)
    low_group = lane < POOL_GC

    for c in range(seq // POOL_CHUNK):
        r0 = c * POOL_CHUNK
        t = lax.broadcasted_iota(jnp.int32, (POOL_CHUNK, 128), 0) + r0

        def inv_count(w):
            left = w // 2
            right = w - 1 - left
            lo = jnp.maximum(t - left, 0)
            hi = jnp.minimum(t + right, seq - 1)
            return 1.0 / (hi - lo + 1).astype(F32)

        def ld(off, col):
            start = POOL_HALO + r0 + off
            return pad_ref[start:start + POOL_CHUNK, 128 * col:128 * (col + 1)]

        u0 = ld(0, 0)
        p2 = ld(-1, 0) + u0
        p4 = p2 + ld(-2, 0) + ld(1, 0)
        pooled0 = jnp.where(low_group, p2 * inv_count(2), p4 * inv_count(4)) - u0
        u1 = ld(0, 1)
        p8 = u1
        for off in (-4, -3, -2, -1, 1, 2, 3):
            p8 = p8 + ld(off, 1)
        p16 = p8
        for off in (-8, -7, -6, -5, 4, 5, 6, 7):
            p16 = p16 + ld(off, 1)
        pooled1 = jnp.where(low_group, p8 * inv_count(8), p16 * inv_count(16)) - u1

        pooled = jnp.concatenate([pooled0, pooled1], axis=1).astype(BF16)
        mixed = _dot(pooled, pw_ref[...]) * ps_ref[...]
        rows = slice(r0, r0 + POOL_CHUNK)
        o_ref[rows, :] = (mixed * sb_ref[rows, :].astype(F32)).astype(BF16)


def _pool(b, sb, lw, seq):
    t = b.shape[0]
    row = lambda i: (i, 0)
    const = lambda i: (0, 0)
    return pl.pallas_call(
        functools.partial(_pool_kernel, seq=seq),
        grid=(t // seq,),
        in_specs=[pl.BlockSpec((seq, POOL_WIDTH), row), pl.BlockSpec((seq, POOL_WIDTH), row),
                  pl.BlockSpec((POOL_WIDTH, POOL_WIDTH), const), pl.BlockSpec((1, POOL_WIDTH), const)],
        out_specs=pl.BlockSpec((seq, POOL_WIDTH), row),
        out_shape=jax.ShapeDtypeStruct((t, POOL_WIDTH), BF16),
        scratch_shapes=[pltpu.VMEM((seq + 2 * POOL_HALO, POOL_WIDTH), F32)],
        compiler_params=_params(1),
        name="pool",
    )(b, sb, lw['pool_w'], lw['pool_scale'])


_NT = (((1,), (1,)), ((), ()))


def _attn_kernel(*refs, heads, use_cache):
    if use_cache:
        q_ref, k_ref, v_ref, kc_ref, vc_ref, sc_ref, o_ref = refs
    else:
        q_ref, k_ref, v_ref, sc_ref, o_ref = refs
    for pair in range(heads // 2):
        acc = None
        for h in (2 * pair, 2 * pair + 1):
            sl = slice(HEAD_SLAB * h, HEAD_SLAB * (h + 1))
            q = q_ref[:, sl]
            s = lax.dot_general(q, k_ref[:, sl], _NT, preferred_element_type=F32)
            m = jnp.max(s, axis=-1, keepdims=True)
            if use_cache:
                s2 = lax.dot_general(q, kc_ref[:, sl], _NT, preferred_element_type=F32)
                m = jnp.maximum(m, jnp.max(s2, axis=-1, keepdims=True))
            p = jnp.exp(s - m)
            denom = jnp.sum(p, axis=-1, keepdims=True)
            o = _dot(p.astype(BF16), v_ref[:, sl])
            if use_cache:
                p2 = jnp.exp(s2 - m)
                denom = denom + jnp.sum(p2, axis=-1, keepdims=True)
                o = o + _dot(p2.astype(BF16), vc_ref[:, sl])
            o = o * (1.0 / denom)
            acc = o if acc is None else acc + o
        sl = slice(HEAD_SLAB * pair, HEAD_SLAB * (pair + 1))
        o_ref[:, sl] = (acc * sc_ref[:, sl].astype(F32)).astype(BF16)


def _attention(q, k, v, sc, cache, batch, lq, lk, tq, heads_per_step):
    use_cache = cache is not None
    nq = lq // tq
    n_hp = N_HEADS // heads_per_step
    qw = heads_per_step * HEAD_SLAB
    ow = heads_per_step * V_DIM
    q_map = lambda b, g, i: (b * nq + i, g)
    kv_map = lambda b, g, i: (b, g)
    in_specs = [pl.BlockSpec((tq, qw), q_map), pl.BlockSpec((lk, qw), kv_map), pl.BlockSpec((lk, qw), kv_map)]
    args = [q, k, v]
    if use_cache:
        lc = cache[0].shape[0] // batch
        in_specs += [pl.BlockSpec((lc, qw), kv_map), pl.BlockSpec((lc, qw), kv_map)]
        args += list(cache)
    in_specs.append(pl.BlockSpec((tq, ow), q_map))
    args.append(sc)
    return pl.pallas_call(
        functools.partial(_attn_kernel, heads=heads_per_step, use_cache=use_cache),
        grid=(batch, n_hp, nq),
        in_specs=in_specs,
        out_specs=pl.BlockSpec((tq, ow), q_map),
        out_shape=jax.ShapeDtypeStruct((batch * lq, ATT_WIDTH), BF16),
        compiler_params=_params(3),
        name="attention_cache" if use_cache else "attention",
    )(*args)


def _out_kernel(*refs, final):
    if final:
        h_ref, mod_ref, ng_ref, xa_ref, xb_ref, xc_ref, wa_ref, wb_ref, wc_ref, wg_ref, wo_ref, fg_ref, o_ref = refs
    else:
        h_ref, mod_ref, ng_ref, xa_ref, xb_ref, xc_ref, wa_ref, wb_ref, wc_ref, wg_ref, wo_ref, o_ref = refs
    h = h_ref[...]
    xn = _modulated_norm(h, ng_ref[...], mod_ref).astype(BF16)
    y = None
    for i, (x_ref, w_ref) in enumerate(((xa_ref, wa_ref), (xb_ref, wb_ref), (xc_ref, wc_ref))):
        g = jax.nn.sigmoid(_dot(xn, wg_ref[:, i * D_MODEL:(i + 1) * D_MODEL]))
        term = g * _dot(x_ref[...], w_ref[...])
        y = term if y is None else y + term
    h_new = h + mod_ref[2:3, :] * _dot(y.astype(BF16), wo_ref[...])
    if final:
        o_ref[...] = _rms(h_new, fg_ref[...])
    else:
        o_ref[...] = h_new


def _out(h, mod3, mod_row, xa, xb, xc, lw, final_g, tm):
    t = h.shape[0]
    final = final_g is not None
    const = lambda i: (0, 0)
    row = lambda i: (i, 0)
    in_specs = [pl.BlockSpec((tm, D_MODEL), row),
                pl.BlockSpec((None, 3, D_MODEL), lambda i: (mod_row(i), 0, 0)),
                pl.BlockSpec((1, D_MODEL), const),
                pl.BlockSpec((tm, FN_WIDTH), row), pl.BlockSpec((tm, POOL_WIDTH), row),
                pl.BlockSpec((tm, ATT_WIDTH), row),
                pl.BlockSpec((FN_WIDTH, D_MODEL), const), pl.BlockSpec((POOL_WIDTH, D_MODEL), const),
                pl.BlockSpec((ATT_WIDTH, D_MODEL), const), pl.BlockSpec((D_MODEL, 3 * D_MODEL), const),
                pl.BlockSpec((D_MODEL, D_MODEL), const)]
    args = [h, mod3, lw['norm_g'], xa, xb, xc, lw['w_br_a'], lw['w_br_b'], lw['w_br_c'], lw['wg'], lw['w_out']]
    if final:
        in_specs.append(pl.BlockSpec((1, D_MODEL), const))
        args.append(final_g)
    return pl.pallas_call(
        functools.partial(_out_kernel, final=final),
        grid=(t // tm,),
        in_specs=in_specs,
        out_specs=pl.BlockSpec((tm, D_MODEL), row),
        out_shape=jax.ShapeDtypeStruct((t, D_MODEL), F32),
        compiler_params=_params(1),
        name="out_final" if final else "out",
    )(*args)


def _rot_cols(w):
    q = QK_ROPE // 4
    return jnp.concatenate([-w[..., q:2 * q], w[..., :q], -w[..., 3 * q:], w[..., 2 * q:3 * q]], axis=-1)


def _rope_slab(w):
    return jnp.pad(w, ((0, 0), (QK_NOPE, HEAD_SLAB - QK_NOPE - QK_ROPE)))


def _pack_layer(l, norm_g, w_in, pool_w, pool_scale, q_norm_g, w_q_up, kv_norm_g, w_kv_up, w_br_a, w_br_b, w_br_c,
                w_out):
    wi = w_in[l]
    w_kr = wi[:, _OFF_KR:_OFF_CZ]
    w1 = jnp.concatenate([wi[:, :_OFF_KV], wi[:, _OFF_CZ:_OFF_G], wi[:, _OFF_KV:_OFF_KR], _rope_slab(w_kr),
                          _rope_slab(_rot_cols(w_kr))], axis=1).astype(BF16)
    wq_h = w_q_up[l].reshape(Q_RANK, N_HEADS, QK_NOPE + QK_ROPE)
    pad_q = HEAD_SLAB - QK_NOPE - QK_ROPE
    wq = jnp.pad(wq_h, ((0, 0), (0, 0), (0, pad_q))).reshape(Q_RANK, N_HEADS * HEAD_SLAB).astype(BF16)
    wq_rot = jnp.pad(_rot_cols(wq_h[..., QK_NOPE:]), ((0, 0), (0, 0), (QK_NOPE, pad_q)))
    wq_rot = wq_rot.reshape(Q_RANK, N_HEADS * HEAD_SLAB).astype(BF16)
    wkv_h = w_kv_up[l].reshape(KV_RANK, N_HEADS, QK_NOPE + V_DIM)
    wk = jnp.pad(wkv_h[..., :QK_NOPE], ((0, 0), (0, 0), (0, HEAD_SLAB - QK_NOPE)))
    wk = wk.reshape(KV_RANK, N_HEADS * HEAD_SLAB).astype(BF16)
    wv_h = wkv_h[..., QK_NOPE:].reshape(KV_RANK, N_HEADS // 2, 2, V_DIM)
    zeros = jnp.zeros_like(wv_h[:, :, 0])
    wv_even = jnp.concatenate([wv_h[:, :, 0], zeros], axis=-1)
    wv_odd = jnp.concatenate([zeros, wv_h[:, :, 1]], axis=-1)
    wv = jnp.stack([wv_even, wv_odd], axis=2).reshape(KV_RANK, N_HEADS * HEAD_SLAB).astype(BF16)
    n_groups = len(POOL_WINDOWS)
    pw = jnp.zeros((POOL_WIDTH, POOL_WIDTH), F32)
    for g in range(n_groups):
        pw = pw.at[g * POOL_GC:(g + 1) * POOL_GC, g * POOL_GC:(g + 1) * POOL_GC].set(pool_w[l, g])
    return {
        'norm_g': norm_g[l].reshape(1, D_MODEL), 'w1': w1, 'wg': wi[:, _OFF_G:].astype(BF16),
        'q_norm_g': q_norm_g[l].reshape(1, Q_RANK), 'wq': wq, 'wq_rot': wq_rot,
        'kv_norm_g': kv_norm_g[l].reshape(1, KV_RANK), 'wk': wk, 'wv': wv,
        'pool_w': pw.astype(BF16), 'pool_scale': pool_scale[l].reshape(1, POOL_WIDTH),
        'w_br_a': w_br_a[l].astype(BF16), 'w_br_b': w_br_b[l].astype(BF16), 'w_br_c': w_br_c[l].astype(BF16),
        'w_out': w_out[l].astype(BF16),
    }


def _rope_tables(seq):
    rows = seq // GRID_W
    t = jnp.arange(seq)
    row = jnp.repeat(jnp.arange(rows), GRID_W).astype(F32)
    col = (t % GRID_W).astype(F32)
    half = QK_ROPE // 2
    freqs = ROPE_THETA ** (-jnp.arange(0, half, 2, dtype=F32) / half)
    ar = row[:, None] * freqs
    ac = col[:, None] * freqs
    cos = jnp.concatenate([jnp.cos(ar), jnp.cos(ar), jnp.cos(ac), jnp.cos(ac)], axis=-1)
    sin = jnp.concatenate([jnp.sin(ar), jnp.sin(ar), jnp.sin(ac), jnp.sin(ac)], axis=-1)
    pad = HEAD_SLAB - QK_NOPE - QK_ROPE
    cos = jnp.concatenate([jnp.ones((seq, QK_NOPE), F32), cos, jnp.ones((seq, pad), F32)], axis=-1)
    sin = jnp.concatenate([jnp.zeros((seq, QK_NOPE), F32), sin, jnp.zeros((seq, pad), F32)], axis=-1)
    return cos, sin


TOKEN_TILE = 512
SAMPLE_Q_TILE = 256


def kernel(x_prompt, x_sample, cache_ckv, cache_krope, c, c_ctx, norm_g, w_mod, b_mod, w_in, pool_w, pool_scale,
           q_norm_g, w_q_up, kv_norm_g, w_kv_up, w_br_a, w_br_b, w_br_c, w_out, final_norm_g):
    batch, seq, _ = x_prompt.shape
    dec_batch, dec_seq, _ = x_sample.shape
    past = cache_ckv.shape[2]
    tm = TOKEN_TILE

    mod_rows = 8
    cvec = jnp.concatenate([c_ctx[None, :], c, jnp.zeros((mod_rows - 1 - dec_batch, D_MODEL), F32)], axis=0)
    mod = _modulation(cvec, w_mod, b_mod).reshape(DEPTH, mod_rows, 3, D_MODEL)
    prompt_row = lambda i: 0
    tiles_per_sample = dec_seq // tm
    sample_row = lambda i: 1 + i // tiles_per_sample

    cos, sin = _rope_tables(dec_seq)
    rope = (jnp.tile(cos, (dec_batch, 1)), jnp.tile(sin, (dec_batch, 1)))
    final_g = final_norm_g.reshape(1, D_MODEL)

    hp = x_prompt.reshape(batch * seq, D_MODEL)
    hs = x_sample.reshape(dec_batch * dec_seq, D_MODEL)
    ckv_list, kr_list = [], []
    for l in range(DEPTH):
        lw = _pack_layer(l, norm_g, w_in, pool_w, pool_scale, q_norm_g, w_q_up, kv_norm_g, w_kv_up, w_br_a, w_br_b,
                         w_br_c, w_out)
        last = final_g if l == DEPTH - 1 else None

        a_in, sa, b_in, sb, q, k, v, ckv, kr, sc = _inproj(hp, mod[l], prompt_row, lw, None, tm)
        ckv_list.append(ckv.reshape(batch, seq, KV_RANK))
        kr_list.append(kr[:, QK_NOPE:QK_NOPE + QK_ROPE].reshape(batch, seq, QK_ROPE))
        xa = _fourier_direct(a_in, sa, seq)
        xb = _pool(b_in, sb, lw, seq)
        xc = _attention(q, k, v, sc, None, batch, seq, seq, seq, N_HEADS)
        hp = _out(hp, mod[l], prompt_row, xa, xb, xc, lw, last, tm)

        a_in, sa, b_in, sb, q, k, v, _, _, sc = _inproj(hs, mod[l], sample_row, lw, rope, tm)
        kr_cache = jnp.pad(cache_krope[:, l], ((0, 0), (0, 0), (QK_NOPE, HEAD_SLAB - QK_NOPE - QK_ROPE)))
        cache = _cache_kv(cache_ckv[:, l].reshape(dec_batch * past, KV_RANK),
                          kr_cache.reshape(dec_batch * past, HEAD_SLAB), lw, past)
        xa = _fourier_fft(a_in, sa, dec_seq)
        xb = _pool(b_in, sb, lw, dec_seq)
        xc = _attention(q, k, v, sc, cache, dec_batch, dec_seq, dec_seq, SAMPLE_Q_TILE, 2)
        hs = _out(hs, mod[l], sample_row, xa, xb, xc, lw, last, tm)

    y_prompt = hp.reshape(batch, seq, D_MODEL)
    y_sample = hs.reshape(dec_batch, dec_seq, D_MODEL)
    return (y_prompt, y_sample, jnp.stack(ckv_list, axis=1), jnp.stack(kr_list, axis=1))
```

```python
import functools

import numpy as np
import jax
import jax.numpy as jnp
from jax import lax
from jax.experimental import pallas as pl
from jax.experimental.pallas import tpu as pltpu

D_MODEL = 1024
DEPTH = 2
GRID_W = 64
EPS = 1e-6
FN_WIDTH = 256
FN_GC = 64
POOL_WINDOWS = (2, 4, 8, 16)
POOL_WIDTH = 256
POOL_GC = 64
N_HEADS = 8
QK_NOPE = 64
QK_ROPE = 32
V_DIM = 64
Q_RANK = 256
KV_RANK = 128
ATT_WIDTH = 512
ROPE_THETA = 10000.0
HEAD_SLAB = 128
QK_SCALE = (QK_NOPE + QK_ROPE) ** -0.5
Q_PRESCALE = QK_SCALE * float(np.log2(np.e))

VMEM_LIMIT_BYTES = 56 * 1024 * 1024

F32 = jnp.float32
BF16 = jnp.bfloat16

_OFF_A, _OFF_B, _OFF_Q, _OFF_KV, _OFF_KR, _OFF_CZ, _OFF_G = 0, 512, 1024, 1280, 1408, 1440, 1952
_W1_A = (0, 512)
_W1_B = (512, 1024)
_W1_Q = (1024, 1280)
_W1_CZ = (1280, 1792)
_W1_KV = (1792, 2176)
W1_WIDTH = 2176


def _params(n_parallel):
    return pltpu.CompilerParams(dimension_semantics=("arbitrary",) * n_parallel,
                                vmem_limit_bytes=VMEM_LIMIT_BYTES)


def _dot(a, b):
    return jnp.dot(a, b, preferred_element_type=F32)


def _silu(x):
    return x * jax.nn.sigmoid(x)


def _rms(x, g):
    r = lax.rsqrt(jnp.mean(x * x, axis=-1, keepdims=True) + EPS)
    return (x * r) * g


def _modulated_norm(h, norm_g, mod_ref):
    shift = mod_ref[0:1, :]
    scale = mod_ref[1:2, :]
    return _rms(h, norm_g) * (1.0 + scale) + shift


def _split_bf16(x):
    hi = x.astype(BF16)
    lo = (x - hi.astype(F32)).astype(BF16)
    return hi, lo


def _dot3_right(x, m_hi, m_lo):
    x_hi, x_lo = _split_bf16(x)
    return _dot(x_hi, m_hi) + _dot(x_lo, m_hi) + _dot(x_hi, m_lo)


def _dot3_left(m_hi, m_lo, x):
    x_hi, x_lo = _split_bf16(x)
    return _dot(m_hi, x_hi) + _dot(m_hi, x_lo) + _dot(m_lo, x_hi)


def _mod_kernel(c_ref, w_ref, b_ref, o_ref):
    s = _silu(c_ref[...]).astype(BF16)
    o_ref[...] = _dot(s, w_ref[...].astype(BF16)) + b_ref[...]


def _modulation(cvec, w_mod, b_mod):
    rows = cvec.shape[0]
    tn = 768
    return pl.pallas_call(
        _mod_kernel,
        grid=(DEPTH, 3 * D_MODEL // tn),
        in_specs=[pl.BlockSpec((rows, D_MODEL), lambda l, j: (0, 0)),
                  pl.BlockSpec((None, D_MODEL, tn), lambda l, j: (l, 0, j)),
                  pl.BlockSpec((None, 1, tn), lambda l, j: (l, 0, j))],
        out_specs=pl.BlockSpec((None, rows, tn), lambda l, j: (l, 0, j)),
        out_shape=jax.ShapeDtypeStruct((DEPTH, rows, 3 * D_MODEL), F32),
        compiler_params=_params(2),
        name="modulation",
    )(cvec, w_mod, b_mod.reshape(DEPTH, 1, 3 * D_MODEL))


_NT = (((1,), (1,)), ((), ()))


def _key_value_heads(ckv_bf16, kr_slab, wk_ref, wvt_ref, k_ref, vt_ref):
    kn = _dot(ckv_bf16, wk_ref[...])
    for h in range(N_HEADS):
        sl = slice(HEAD_SLAB * h, HEAD_SLAB * (h + 1))
        k_ref[:, sl] = (kn[:, sl] + kr_slab).astype(BF16)
    vt_ref[...] = lax.dot_general(wvt_ref[...], ckv_bf16, _NT, preferred_element_type=F32).astype(BF16)


def _inproj_kernel(*refs, use_rope):
    if use_rope:
        (h_ref, mod_ref, ng_ref, w1_ref, qg_ref, wq_ref, wqr_ref, kvg_ref, wk_ref, wvt_ref, cos_ref, sin_ref,
         ain_ref, sa_ref, bin_ref, sb_ref, q_ref, k_ref, vt_ref, ckv_ref, kr_ref, sc_ref) = refs
    else:
        (h_ref, mod_ref, ng_ref, w1_ref, qg_ref, wq_ref, kvg_ref, wk_ref, wvt_ref,
         ain_ref, sa_ref, bin_ref, sb_ref, q_ref, k_ref, vt_ref, ckv_ref, kr_ref, sc_ref) = refs

    xn = _modulated_norm(h_ref[...], ng_ref[...], mod_ref).astype(BF16)

    def proj(cols):
        return _dot(xn, w1_ref[:, cols[0]:cols[1]])

    a = proj(_W1_A)
    ain_ref[...] = a[:, :FN_WIDTH]
    sa_ref[...] = _silu(a[:, FN_WIDTH:]).astype(BF16)
    b = proj(_W1_B)
    bin_ref[...] = b[:, :POOL_WIDTH]
    sb_ref[...] = _silu(b[:, POOL_WIDTH:]).astype(BF16)
    sc_ref[...] = _silu(proj(_W1_CZ)).astype(BF16)

    qn = _rms(proj(_W1_Q), qg_ref[...]).astype(BF16)
    q = _dot(qn, wq_ref[...])
    if use_rope:
        cos = cos_ref[...]
        sin = sin_ref[...]
        qr = _dot(qn, wqr_ref[...])
        for h in range(N_HEADS):
            sl = slice(HEAD_SLAB * h, HEAD_SLAB * (h + 1))
            q_ref[:, sl] = ((q[:, sl] * cos + qr[:, sl] * sin) * Q_PRESCALE).astype(BF16)
    else:
        q_ref[...] = (q * Q_PRESCALE).astype(BF16)

    kv = proj(_W1_KV)
    ckv = _rms(kv[:, :KV_RANK], kvg_ref[...])
    ckv_ref[...] = ckv
    kr = kv[:, KV_RANK:KV_RANK + HEAD_SLAB]
    kr_ref[...] = kr
    if use_rope:
        kr = kr * cos + kv[:, KV_RANK + HEAD_SLAB:] * sin
    _key_value_heads(ckv.astype(BF16), kr, wk_ref, wvt_ref, k_ref, vt_ref)


def _inproj(h, mod3, mod_row, lw, rope, tm):
    t = h.shape[0]
    use_rope = rope is not None
    const = lambda i: (0, 0)
    row = lambda i: (i, 0)
    in_specs = [pl.BlockSpec((tm, D_MODEL), row),
                pl.BlockSpec((None, 3, D_MODEL), lambda i: (mod_row(i), 0, 0)),
                pl.BlockSpec((1, D_MODEL), const),
                pl.BlockSpec((D_MODEL, W1_WIDTH), const),
                pl.BlockSpec((1, Q_RANK), const),
                pl.BlockSpec((Q_RANK, N_HEADS * HEAD_SLAB), const)]
    args = [h, mod3, lw['norm_g'], lw['w1'], lw['q_norm_g'], lw['wq']]
    if use_rope:
        in_specs.append(pl.BlockSpec((Q_RANK, N_HEADS * HEAD_SLAB), const))
        args.append(lw['wq_rot'])
    in_specs += [pl.BlockSpec((1, KV_RANK), const),
                 pl.BlockSpec((KV_RANK, N_HEADS * HEAD_SLAB), const),
                 pl.BlockSpec((ATT_WIDTH, KV_RANK), const)]
    args += [lw['kv_norm_g'], lw['wk'], lw['wvt']]
    if use_rope:
        in_specs += [pl.BlockSpec((tm, HEAD_SLAB), row), pl.BlockSpec((tm, HEAD_SLAB), row)]
        args += [rope[0], rope[1]]
    col = lambda i: (0, i)
    outs = [(FN_WIDTH, F32), (FN_WIDTH, BF16), (POOL_WIDTH, F32), (POOL_WIDTH, BF16),
            (N_HEADS * HEAD_SLAB, BF16), (N_HEADS * HEAD_SLAB, BF16), None,
            (KV_RANK, F32), (HEAD_SLAB, F32), (ATT_WIDTH, BF16)]
    out_specs = [pl.BlockSpec((ATT_WIDTH, tm), col) if o is None else pl.BlockSpec((tm, o[0]), row) for o in outs]
    out_shape = [jax.ShapeDtypeStruct((ATT_WIDTH, t), BF16) if o is None else jax.ShapeDtypeStruct((t, o[0]), o[1])
                 for o in outs]
    return pl.pallas_call(
        functools.partial(_inproj_kernel, use_rope=use_rope),
        grid=(t // tm,),
        in_specs=in_specs,
        out_specs=out_specs,
        out_shape=out_shape,
        compiler_params=_params(1),
        name="inproj_rope" if use_rope else "inproj",
    )(*args)


def _cache_kv_kernel(ckv_ref, kr_ref, wk_ref, wvt_ref, k_ref, vt_ref):
    _key_value_heads(ckv_ref[...].astype(BF16), kr_ref[...], wk_ref, wvt_ref, k_ref, vt_ref)


def _cache_kv(ckv, kr_slab, lw, rows):
    t = ckv.shape[0]
    const = lambda i: (0, 0)
    row = lambda i: (i, 0)
    wide = N_HEADS * HEAD_SLAB
    return pl.pallas_call(
        _cache_kv_kernel,
        grid=(t // rows,),
        in_specs=[pl.BlockSpec((rows, KV_RANK), row), pl.BlockSpec((rows, HEAD_SLAB), row),
                  pl.BlockSpec((KV_RANK, wide), const), pl.BlockSpec((ATT_WIDTH, KV_RANK), const)],
        out_specs=[pl.BlockSpec((rows, wide), row), pl.BlockSpec((ATT_WIDTH, rows), lambda i: (0, i))],
        out_shape=[jax.ShapeDtypeStruct((t, wide), BF16), jax.ShapeDtypeStruct((ATT_WIDTH, t), BF16)],
        compiler_params=_params(1),
        name="cache_kv",
    )(ckv, kr_slab, lw['wk'], lw['wvt'])


def _hi_lo(m):
    m = jnp.asarray(m, F32)
    hi = m.astype(BF16)
    return hi, (m - hi.astype(F32)).astype(BF16)


def _dft_cos_sin(n):
    k = np.arange(n)
    ang = 2.0 * np.pi * ((k[:, None] * k[None, :]) % n) / n
    return np.cos(ang), np.sin(ang)


def _channel_dft_tables():
    c, s = _dft_cos_sin(FN_GC)
    eye = np.eye(FN_WIDTH // FN_GC)
    return np.kron(eye, c), np.kron(eye, s)


def _fourier_direct_kernel(a_ref, sa_ref, cc_hi, cc_lo, ss_hi, ss_lo, cl_hi, cl_lo, sl_hi, sl_lo, o_ref, *, norm):
    a = a_ref[...]
    tc = _dot3_right(a, cc_hi[...], cc_lo[...])
    ts = _dot3_right(a, ss_hi[...], ss_lo[...])
    f = _dot3_left(cl_hi[...], cl_lo[...], tc) - _dot3_left(sl_hi[...], sl_lo[...], ts)
    o_ref[...] = ((f * norm) * sa_ref[...].astype(F32)).astype(BF16)


def _fourier_direct(a, sa, seq):
    t = a.shape[0]
    cc, ss = _channel_dft_tables()
    cl, sl = _dft_cos_sin(seq)
    tables = [x for m in (cc, ss, cl, sl) for x in _hi_lo(m)]
    row = lambda i: (i, 0)
    const = lambda i: (0, 0)
    tspecs = [pl.BlockSpec((FN_WIDTH, FN_WIDTH), const)] * 4 + [pl.BlockSpec((seq, seq), const)] * 4
    return pl.pallas_call(
        functools.partial(_fourier_direct_kernel, norm=float((seq * FN_GC) ** -0.5)),
        grid=(t // seq,),
        in_specs=[pl.BlockSpec((seq, FN_WIDTH), row), pl.BlockSpec((seq, FN_WIDTH), row)] + tspecs,
        out_specs=pl.BlockSpec((seq, FN_WIDTH), row),
        out_shape=jax.ShapeDtypeStruct((t, FN_WIDTH), BF16),
        compiler_params=_params(1),
        name="fourier_direct",
    )(a, sa, *tables)


FFT_R = 64


def _fourier_fft_kernel(a_ref, sa_ref, cc_hi, cc_lo, ss_hi, ss_lo, m1_hi, m1_lo, m2_hi, m2_lo, twc_ref, tws_ref,
                        o_ref, zr0, zr1, zi0, zi1, yr0, yr1, yi0, yi1, *, seq, norm):
    r = FFT_R
    half = FN_WIDTH // 2
    chunk = 512

    def put(refs, rows, x):
        refs[0][rows, :] = x[:, :half]
        refs[1][rows, :] = x[:, half:]

    def get(refs, rows):
        return jnp.concatenate([refs[0][rows, :], refs[1][rows, :]], axis=1)

    zr, zi, yr_s, yi_s = (zr0, zr1), (zi0, zi1), (yr0, yr1), (yi0, yi1)
    for c in range(seq // chunk):
        rows = slice(c * chunk, (c + 1) * chunk)
        a = a_ref[rows, :]
        put(zr, rows, _dot3_right(a, cc_hi[...], cc_lo[...]))
        put(zi, rows, -_dot3_right(a, ss_hi[...], ss_lo[...]))
    for n2 in range(r):
        strided = pl.ds(n2, r, stride=r)
        z = jnp.concatenate([get(zr, strided), get(zi, strided)], axis=0)
        y = _dot3_left(m1_hi[...], m1_lo[...], z)
        yr, yi = y[:r], y[r:]
        blk = slice(n2 * r, (n2 + 1) * r)
        cos = jnp.concatenate([twc_ref[blk, :]] * 2, axis=1)
        sin = jnp.concatenate([tws_ref[blk, :]] * 2, axis=1)
        put(yr_s, blk, yr * cos + yi * sin)
        put(yi_s, blk, yi * cos - yr * sin)
    for k1 in range(r):
        strided = pl.ds(k1, r, stride=r)
        y = jnp.concatenate([get(yr_s, strided), get(yi_s, strided)], axis=0)
        put(zr, strided, _dot3_left(m2_hi[...], m2_lo[...], y))
    for c in range(seq // chunk):
        rows = slice(c * chunk, (c + 1) * chunk)
        o_ref[rows, :] = ((get(zr, rows) * norm) * sa_ref[rows, :].astype(F32)).astype(BF16)


def _fourier_fft(a, sa, seq):
    assert seq == FFT_R * FFT_R
    t = a.shape[0]
    cc, ss = _channel_dft_tables()
    c, s = _dft_cos_sin(FFT_R)
    m1 = np.block([[c, s], [-s, c]])
    m2 = np.concatenate([c, s], axis=1)
    n2 = np.arange(FFT_R)[:, None]
    k1 = np.arange(FFT_R)[None, :]
    ang = (2.0 * np.pi * (n2 * k1) / seq).reshape(seq, 1)
    twc = jnp.broadcast_to(jnp.asarray(np.cos(ang), F32), (seq, 128))
    tws = jnp.broadcast_to(jnp.asarray(np.sin(ang), F32), (seq, 128))
    tables = [x for m in (cc, ss, m1, m2) for x in _hi_lo(m)]
    row = lambda i: (i, 0)
    const = lambda i: (0, 0)
    tspecs = ([pl.BlockSpec((FN_WIDTH, FN_WIDTH), const)] * 4 + [pl.BlockSpec((2 * FFT_R, 2 * FFT_R), const)] * 2
              + [pl.BlockSpec((FFT_R, 2 * FFT_R), const)] * 2 + [pl.BlockSpec((seq, 128), const)] * 2)
    return pl.pallas_call(
        functools.partial(_fourier_fft_kernel, seq=seq, norm=float((seq * FN_GC) ** -0.5)),
        grid=(t // seq,),
        in_specs=[pl.BlockSpec((seq, FN_WIDTH), row), pl.BlockSpec((seq, FN_WIDTH), row)] + tspecs,
        out_specs=pl.BlockSpec((seq, FN_WIDTH), row),
        out_shape=jax.ShapeDtypeStruct((t, FN_WIDTH), BF16),
        scratch_shapes=[pltpu.VMEM((seq, FN_WIDTH // 2), F32)] * 8,
        compiler_params=_params(1),
        name="fourier_fft",
    )(a, sa, *tables, twc, tws)


POOL_HALO = 8
POOL_CHUNK = 256


def _pool_kernel(b_ref, sb_ref, pw_ref, ps_ref, o_ref, pad_ref, *, seq):
    zeros = jnp.zeros((POOL_HALO, POOL_WIDTH), F32)
    pad_ref[0:POOL_HALO, :] = zeros
    pad_ref[POOL_HALO + seq:, :] = zeros
    pad_ref[POOL_HALO:POOL_HALO + seq, :] = b_ref[...]
    lane = lax.broadcasted_iota(jnp.int32, (POOL_CHUNK, 128), 1)
    low_group = lane < POOL_GC

    for c in range(seq // POOL_CHUNK):
        r0 = c * POOL_CHUNK
        t = lax.broadcasted_iota(jnp.int32, (POOL_CHUNK, 128), 0) + r0

        def inv_count(w):
            left = w // 2
            right = w - 1 - left
            lo = jnp.maximum(t - left, 0)
            hi = jnp.minimum(t + right, seq - 1)
            return 1.0 / (hi - lo + 1).astype(F32)

        def ld(off, col):
            start = POOL_HALO + r0 + off
            return pad_ref[start:start + POOL_CHUNK, 128 * col:128 * (col + 1)]

        u0 = ld(0, 0)
        p2 = ld(-1, 0) + u0
        p4 = p2 + ld(-2, 0) + ld(1, 0)
        pooled0 = jnp.where(low_group, p2 * inv_count(2), p4 * inv_count(4)) - u0
        u1 = ld(0, 1)
        p8 = u1
        for off in (-4, -3, -2, -1, 1, 2, 3):
            p8 = p8 + ld(off, 1)
        p16 = p8
        for off in (-8, -7, -6, -5, 4, 5, 6, 7):
            p16 = p16 + ld(off, 1)
        pooled1 = jnp.where(low_group, p8 * inv_count(8), p16 * inv_count(16)) - u1

        pooled = jnp.concatenate([pooled0, pooled1], axis=1).astype(BF16)
        mixed = _dot(pooled, pw_ref[...]) * ps_ref[...]
        rows = slice(r0, r0 + POOL_CHUNK)
        o_ref[rows, :] = (mixed * sb_ref[rows, :].astype(F32)).astype(BF16)


def _pool(b, sb, lw, seq):
    t = b.shape[0]
    row = lambda i: (i, 0)
    const = lambda i: (0, 0)
    return pl.pallas_call(
        functools.partial(_pool_kernel, seq=seq),
        grid=(t // seq,),
        in_specs=[pl.BlockSpec((seq, POOL_WIDTH), row), pl.BlockSpec((seq, POOL_WIDTH), row),
                  pl.BlockSpec((POOL_WIDTH, POOL_WIDTH), const), pl.BlockSpec((1, POOL_WIDTH), const)],
        out_specs=pl.BlockSpec((seq, POOL_WIDTH), row),
        out_shape=jax.ShapeDtypeStruct((t, POOL_WIDTH), BF16),
        scratch_shapes=[pltpu.VMEM((seq + 2 * POOL_HALO, POOL_WIDTH), F32)],
        compiler_params=_params(1),
        name="pool",
    )(b, sb, lw['pool_w'], lw['pool_scale'])


def _attn_kernel(*refs, heads, use_cache):
    if use_cache:
        q_ref, k_ref, vt_ref, kc_ref, vct_ref, sc_ref, o_ref = refs
    else:
        q_ref, k_ref, vt_ref, sc_ref, o_ref = refs
    for pair in range(heads // 2):
        outs = []
        for h in (2 * pair, 2 * pair + 1):
            sl = slice(HEAD_SLAB * h, HEAD_SLAB * (h + 1))
            vrows = slice(V_DIM * h, V_DIM * (h + 1))
            q = q_ref[:, sl]
            s = lax.dot_general(k_ref[:, sl], q, _NT, preferred_element_type=F32)
            m = jnp.max(s, axis=0, keepdims=True)
            if use_cache:
                s2 = lax.dot_general(kc_ref[:, sl], q, _NT, preferred_element_type=F32)
                m = jnp.maximum(m, jnp.max(s2, axis=0, keepdims=True))
            p = jnp.exp2(s - m)
            denom = jnp.sum(p, axis=0, keepdims=True)
            o = _dot(vt_ref[vrows, :], p.astype(BF16))
            if use_cache:
                p2 = jnp.exp2(s2 - m)
                denom = denom + jnp.sum(p2, axis=0, keepdims=True)
                o = o + _dot(vct_ref[vrows, :], p2.astype(BF16))
            outs.append(o * (1.0 / denom))
        o_pair = jnp.concatenate(outs, axis=0).T
        sl = slice(HEAD_SLAB * pair, HEAD_SLAB * (pair + 1))
        o_ref[:, sl] = (o_pair * sc_ref[:, sl].astype(F32)).astype(BF16)


def _attention(q, k, vt, sc, cache, batch, lq, lk, tq, heads_per_step):
    use_cache = cache is not None
    nq = lq // tq
    n_hp = N_HEADS // heads_per_step
    qw = heads_per_step * HEAD_SLAB
    ow = heads_per_step * V_DIM
    q_map = lambda b, g, i: (b * nq + i, g)
    kv_map = lambda b, g, i: (b, g)
    vt_map = lambda b, g, i: (g, b)
    in_specs = [pl.BlockSpec((tq, qw), q_map), pl.BlockSpec((lk, qw), kv_map), pl.BlockSpec((ow, lk), vt_map)]
    args = [q, k, vt]
    if use_cache:
        lc = cache[0].shape[0] // batch
        in_specs += [pl.BlockSpec((lc, qw), kv_map), pl.BlockSpec((ow, lc), vt_map)]
        args += list(cache)
    in_specs.append(pl.BlockSpec((tq, ow), q_map))
    args.append(sc)
    return pl.pallas_call(
        functools.partial(_attn_kernel, heads=heads_per_step, use_cache=use_cache),
        grid=(batch, n_hp, nq),
        in_specs=in_specs,
        out_specs=pl.BlockSpec((tq, ow), q_map),
        out_shape=jax.ShapeDtypeStruct((batch * lq, ATT_WIDTH), BF16),
        compiler_params=_params(3),
        name="attention_cache" if use_cache else "attention",
    )(*args)


def _out_kernel(*refs, final):
    if final:
        h_ref, mod_ref, ng_ref, xa_ref, xb_ref, xc_ref, wa_ref, wb_ref, wc_ref, wg_ref, wo_ref, fg_ref, o_ref = refs
    else:
        h_ref, mod_ref, ng_ref, xa_ref, xb_ref, xc_ref, wa_ref, wb_ref, wc_ref, wg_ref, wo_ref, o_ref = refs
    h = h_ref[...]
    xn = _modulated_norm(h, ng_ref[...], mod_ref).astype(BF16)
    y = None
    for i, (x_ref, w_ref) in enumerate(((xa_ref, wa_ref), (xb_ref, wb_ref), (xc_ref, wc_ref))):
        g = jax.nn.sigmoid(_dot(xn, wg_ref[:, i * D_MODEL:(i + 1) * D_MODEL]))
        term = g * _dot(x_ref[...], w_ref[...])
        y = term if y is None else y + term
    h_new = h + mod_ref[2:3, :] * _dot(y.astype(BF16), wo_ref[...])
    if final:
        o_ref[...] = _rms(h_new, fg_ref[...])
    else:
        o_ref[...] = h_new


def _out(h, mod3, mod_row, xa, xb, xc, lw, final_g, tm):
    t = h.shape[0]
    final = final_g is not None
    const = lambda i: (0, 0)
    row = lambda i: (i, 0)
    in_specs = [pl.BlockSpec((tm, D_MODEL), row),
                pl.BlockSpec((None, 3, D_MODEL), lambda i: (mod_row(i), 0, 0)),
                pl.BlockSpec((1, D_MODEL), const),
                pl.BlockSpec((tm, FN_WIDTH), row), pl.BlockSpec((tm, POOL_WIDTH), row),
                pl.BlockSpec((tm, ATT_WIDTH), row),
                pl.BlockSpec((FN_WIDTH, D_MODEL), const), pl.BlockSpec((POOL_WIDTH, D_MODEL), const),
                pl.BlockSpec((ATT_WIDTH, D_MODEL), const), pl.BlockSpec((D_MODEL, 3 * D_MODEL), const),
                pl.BlockSpec((D_MODEL, D_MODEL), const)]
    args = [h, mod3, lw['norm_g'], xa, xb, xc, lw['w_br_a'], lw['w_br_b'], lw['w_br_c'], lw['wg'], lw['w_out']]
    if final:
        in_specs.append(pl.BlockSpec((1, D_MODEL), const))
        args.append(final_g)
    return pl.pallas_call(
        functools.partial(_out_kernel, final=final),
        grid=(t // tm,),
        in_specs=in_specs,
        out_specs=pl.BlockSpec((tm, D_MODEL), row),
        out_shape=jax.ShapeDtypeStruct((t, D_MODEL), F32),
        compiler_params=_params(1),
        name="out_final" if final else "out",
    )(*args)


def _rot_cols(w):
    q = QK_ROPE // 4
    return jnp.concatenate([-w[..., q:2 * q], w[..., :q], -w[..., 3 * q:], w[..., 2 * q:3 * q]], axis=-1)


def _rope_slab(w):
    return jnp.pad(w, ((0, 0), (QK_NOPE, HEAD_SLAB - QK_NOPE - QK_ROPE)))


def _pack_layer(l, norm_g, w_in, pool_w, pool_scale, q_norm_g, w_q_up, kv_norm_g, w_kv_up, w_br_a, w_br_b, w_br_c,
                w_out):
    wi = w_in[l]
    w_kr = wi[:, _OFF_KR:_OFF_CZ]
    w1 = jnp.concatenate([wi[:, :_OFF_KV], wi[:, _OFF_CZ:_OFF_G], wi[:, _OFF_KV:_OFF_KR], _rope_slab(w_kr),
                          _rope_slab(_rot_cols(w_kr))], axis=1).astype(BF16)
    wq_h = w_q_up[l].reshape(Q_RANK, N_HEADS, QK_NOPE + QK_ROPE)
    pad_q = HEAD_SLAB - QK_NOPE - QK_ROPE
    wq = jnp.pad(wq_h, ((0, 0), (0, 0), (0, pad_q))).reshape(Q_RANK, N_HEADS * HEAD_SLAB).astype(BF16)
    wq_rot = jnp.pad(_rot_cols(wq_h[..., QK_NOPE:]), ((0, 0), (0, 0), (QK_NOPE, pad_q)))
    wq_rot = wq_rot.reshape(Q_RANK, N_HEADS * HEAD_SLAB).astype(BF16)
    wkv_h = w_kv_up[l].reshape(KV_RANK, N_HEADS, QK_NOPE + V_DIM)
    wk = jnp.pad(wkv_h[..., :QK_NOPE], ((0, 0), (0, 0), (0, HEAD_SLAB - QK_NOPE)))
    wk = wk.reshape(KV_RANK, N_HEADS * HEAD_SLAB).astype(BF16)
    wvt = wkv_h[..., QK_NOPE:].reshape(KV_RANK, ATT_WIDTH).T.astype(BF16)
    n_groups = len(POOL_WINDOWS)
    pw = jnp.zeros((POOL_WIDTH, POOL_WIDTH), F32)
    for g in range(n_groups):
        pw = pw.at[g * POOL_GC:(g + 1) * POOL_GC, g * POOL_GC:(g + 1) * POOL_GC].set(pool_w[l, g])
    return {
        'norm_g': norm_g[l].reshape(1, D_MODEL), 'w1': w1, 'wg': wi[:, _OFF_G:].astype(BF16),
        'q_norm_g': q_norm_g[l].reshape(1, Q_RANK), 'wq': wq, 'wq_rot': wq_rot,
        'kv_norm_g': kv_norm_g[l].reshape(1, KV_RANK), 'wk': wk, 'wvt': wvt,
        'pool_w': pw.astype(BF16), 'pool_scale': pool_scale[l].reshape(1, POOL_WIDTH),
        'w_br_a': w_br_a[l].astype(BF16), 'w_br_b': w_br_b[l].astype(BF16), 'w_br_c': w_br_c[l].astype(BF16),
        'w_out': w_out[l].astype(BF16),
    }


def _rope_tables(seq):
    rows = seq // GRID_W
    t = jnp.arange(seq)
    row = jnp.repeat(jnp.arange(rows), GRID_W).astype(F32)
    col = (t % GRID_W).astype(F32)
    half = QK_ROPE // 2
    freqs = ROPE_THETA ** (-jnp.arange(0, half, 2, dtype=F32) / half)
    ar = row[:, None] * freqs
    ac = col[:, None] * freqs
    cos = jnp.concatenate([jnp.cos(ar), jnp.cos(ar), jnp.cos(ac), jnp.cos(ac)], axis=-1)
    sin = jnp.concatenate([jnp.sin(ar), jnp.sin(ar), jnp.sin(ac), jnp.sin(ac)], axis=-1)
    pad = HEAD_SLAB - QK_NOPE - QK_ROPE
    cos = jnp.concatenate([jnp.ones((seq, QK_NOPE), F32), cos, jnp.ones((seq, pad), F32)], axis=-1)
    sin = jnp.concatenate([jnp.zeros((seq, QK_NOPE), F32), sin, jnp.zeros((seq, pad), F32)], axis=-1)
    return cos, sin


TOKEN_TILE = 512
SAMPLE_Q_TILE = 256
SAMPLE_HEADS_PER_STEP = 2


def kernel(x_prompt, x_sample, cache_ckv, cache_krope, c, c_ctx, norm_g, w_mod, b_mod, w_in, pool_w, pool_scale,
           q_norm_g, w_q_up, kv_norm_g, w_kv_up, w_br_a, w_br_b, w_br_c, w_out, final_norm_g):
    batch, seq, _ = x_prompt.shape
    dec_batch, dec_seq, _ = x_sample.shape
    past = cache_ckv.shape[2]
    tm = TOKEN_TILE

    mod_rows = 8
    cvec = jnp.concatenate([c_ctx[None, :], c, jnp.zeros((mod_rows - 1 - dec_batch, D_MODEL), F32)], axis=0)
    mod = _modulation(cvec, w_mod, b_mod).reshape(DEPTH, mod_rows, 3, D_MODEL)
    prompt_row = lambda i: 0
    tiles_per_sample = dec_seq // tm
    sample_row = lambda i: 1 + i // tiles_per_sample

    cos, sin = _rope_tables(dec_seq)
    rope = (jnp.tile(cos, (dec_batch, 1)), jnp.tile(sin, (dec_batch, 1)))
    final_g = final_norm_g.reshape(1, D_MODEL)

    hp = x_prompt.reshape(batch * seq, D_MODEL)
    hs = x_sample.reshape(dec_batch * dec_seq, D_MODEL)
    ckv_list, kr_list = [], []
    for l in range(DEPTH):
        lw = _pack_layer(l, norm_g, w_in, pool_w, pool_scale, q_norm_g, w_q_up, kv_norm_g, w_kv_up, w_br_a, w_br_b,
                         w_br_c, w_out)
        last = final_g if l == DEPTH - 1 else None

        a_in, sa, b_in, sb, q, k, vt, ckv, kr, sc = _inproj(hp, mod[l], prompt_row, lw, None, tm)
        ckv_list.append(ckv.reshape(batch, seq, KV_RANK))
        kr_list.append(kr[:, QK_NOPE:QK_NOPE + QK_ROPE].reshape(batch, seq, QK_ROPE))
        xa = _fourier_direct(a_in, sa, seq)
        xb = _pool(b_in, sb, lw, seq)
        xc = _attention(q, k, vt, sc, None, batch, seq, seq, seq, N_HEADS)
        hp = _out(hp, mod[l], prompt_row, xa, xb, xc, lw, last, tm)

        a_in, sa, b_in, sb, q, k, vt, _, _, sc = _inproj(hs, mod[l], sample_row, lw, rope, tm)
        kr_cache = jnp.pad(cache_krope[:, l], ((0, 0), (0, 0), (QK_NOPE, HEAD_SLAB - QK_NOPE - QK_ROPE)))
        cache = _cache_kv(cache_ckv[:, l].reshape(dec_batch * past, KV_RANK),
                          kr_cache.reshape(dec_batch * past, HEAD_SLAB), lw, past)
        xa = _fourier_fft(a_in, sa, dec_seq)
        xb = _pool(b_in, sb, lw, dec_seq)
        xc = _attention(q, k, vt, sc, cache, dec_batch, dec_seq, dec_seq, SAMPLE_Q_TILE, SAMPLE_HEADS_PER_STEP)
        hs = _out(hs, mod[l], sample_row, xa, xb, xc, lw, last, tm)

    y_prompt = hp.reshape(batch, seq, D_MODEL)
    y_sample = hs.reshape(dec_batch, dec_seq, D_MODEL)
    return (y_prompt, y_sample, jnp.stack(ckv_list, axis=1), jnp.stack(kr_list, axis=1))
```

```python
import functools

import numpy as np
import jax
import jax.numpy as jnp
from jax import lax
from jax.experimental import pallas as pl
from jax.experimental.pallas import tpu as pltpu

D_MODEL = 1024
DEPTH = 2
GRID_W = 64
EPS = 1e-6
FN_WIDTH = 256
FN_GC = 64
POOL_WINDOWS = (2, 4, 8, 16)
POOL_WIDTH = 256
POOL_GC = 64
N_HEADS = 8
QK_NOPE = 64
QK_ROPE = 32
V_DIM = 64
Q_RANK = 256
KV_RANK = 128
ATT_WIDTH = 512
ROPE_THETA = 10000.0
HEAD_SLAB = 128
QK_SCALE = (QK_NOPE + QK_ROPE) ** -0.5
Q_PRESCALE = QK_SCALE * float(np.log2(np.e))

VMEM_LIMIT_BYTES = 56 * 1024 * 1024

F32 = jnp.float32
BF16 = jnp.bfloat16

_OFF_A, _OFF_B, _OFF_Q, _OFF_KV, _OFF_KR, _OFF_CZ, _OFF_G = 0, 512, 1024, 1280, 1408, 1440, 1952
_W1_A = (0, 512)
_W1_B = (512, 1024)
_W1_Q = (1024, 1280)
_W1_CZ = (1280, 1792)
_W1_KV = (1792, 2176)
W1_WIDTH = 2176


def _params(n_parallel):
    return pltpu.CompilerParams(dimension_semantics=("arbitrary",) * n_parallel,
                                vmem_limit_bytes=VMEM_LIMIT_BYTES)


def _dot(a, b):
    return jnp.dot(a, b, preferred_element_type=F32)


def _silu(x):
    return x * jax.nn.sigmoid(x)


def _rms(x, g):
    r = lax.rsqrt(jnp.mean(x * x, axis=-1, keepdims=True) + EPS)
    return (x * r) * g


def _modulated_norm(h, norm_g, mod_ref):
    shift = mod_ref[0:1, :]
    scale = mod_ref[1:2, :]
    return _rms(h, norm_g) * (1.0 + scale) + shift


def _split_bf16(x):
    hi = x.astype(BF16)
    lo = (x - hi.astype(F32)).astype(BF16)
    return hi, lo


def _dot3_right(x, m_hi, m_lo):
    x_hi, x_lo = _split_bf16(x)
    return _dot(x_hi, m_hi) + _dot(x_lo, m_hi) + _dot(x_hi, m_lo)


def _dot3_left(m_hi, m_lo, x):
    x_hi, x_lo = _split_bf16(x)
    return _dot(m_hi, x_hi) + _dot(m_hi, x_lo) + _dot(m_lo, x_hi)


def _mod_kernel(c_ref, w_ref, b_ref, o_ref):
    s = _silu(c_ref[...]).astype(BF16)
    o_ref[...] = _dot(s, w_ref[...].astype(BF16)) + b_ref[...]


def _modulation(cvec, w_mod, b_mod):
    rows = cvec.shape[0]
    tn = 768
    return pl.pallas_call(
        _mod_kernel,
        grid=(DEPTH, 3 * D_MODEL // tn),
        in_specs=[pl.BlockSpec((rows, D_MODEL), lambda l, j: (0, 0)),
                  pl.BlockSpec((None, D_MODEL, tn), lambda l, j: (l, 0, j)),
                  pl.BlockSpec((None, 1, tn), lambda l, j: (l, 0, j))],
        out_specs=pl.BlockSpec((None, rows, tn), lambda l, j: (l, 0, j)),
        out_shape=jax.ShapeDtypeStruct((DEPTH, rows, 3 * D_MODEL), F32),
        compiler_params=_params(2),
        name="modulation",
    )(cvec, w_mod, b_mod.reshape(DEPTH, 1, 3 * D_MODEL))


_NT = (((1,), (1,)), ((), ()))


def _key_value_heads(ckv_bf16, kr_slab, wk_ref, wvt_ref, k_ref, vt_ref):
    kn = _dot(ckv_bf16, wk_ref[...])
    for h in range(N_HEADS):
        sl = slice(HEAD_SLAB * h, HEAD_SLAB * (h + 1))
        k_ref[:, sl] = (kn[:, sl] + kr_slab).astype(BF16)
    vt_ref[...] = lax.dot_general(wvt_ref[...], ckv_bf16, _NT, preferred_element_type=F32).astype(BF16)


def _inproj_kernel(*refs, use_rope):
    if use_rope:
        (h_ref, mod_ref, ng_ref, w1_ref, qg_ref, wq_ref, wqr_ref, kvg_ref, wk_ref, wvt_ref, cos_ref, sin_ref,
         ain_ref, sa_ref, bin_ref, sb_ref, q_ref, k_ref, vt_ref, ckv_ref, kr_ref, sc_ref) = refs
    else:
        (h_ref, mod_ref, ng_ref, w1_ref, qg_ref, wq_ref, kvg_ref, wk_ref, wvt_ref,
         ain_ref, sa_ref, bin_ref, sb_ref, q_ref, k_ref, vt_ref, ckv_ref, kr_ref, sc_ref) = refs

    xn = _modulated_norm(h_ref[...], ng_ref[...], mod_ref).astype(BF16)

    def proj(cols):
        return _dot(xn, w1_ref[:, cols[0]:cols[1]])

    a = proj(_W1_A)
    ain_ref[...] = a[:, :FN_WIDTH]
    sa_ref[...] = _silu(a[:, FN_WIDTH:]).astype(BF16)
    b = proj(_W1_B)
    bin_ref[...] = b[:, :POOL_WIDTH]
    sb_ref[...] = _silu(b[:, POOL_WIDTH:]).astype(BF16)
    sc_ref[...] = _silu(proj(_W1_CZ)).astype(BF16)

    qn = _rms(proj(_W1_Q), qg_ref[...]).astype(BF16)
    q = _dot(qn, wq_ref[...])
    if use_rope:
        cos = cos_ref[...]
        sin = sin_ref[...]
        qr = _dot(qn, wqr_ref[...])
        for h in range(N_HEADS):
            sl = slice(HEAD_SLAB * h, HEAD_SLAB * (h + 1))
            q_ref[:, sl] = ((q[:, sl] * cos + qr[:, sl] * sin) * Q_PRESCALE).astype(BF16)
    else:
        q_ref[...] = (q * Q_PRESCALE).astype(BF16)

    kv = proj(_W1_KV)
    ckv = _rms(kv[:, :KV_RANK], kvg_ref[...])
    ckv_ref[...] = ckv
    kr = kv[:, KV_RANK:KV_RANK + HEAD_SLAB]
    kr_ref[...] = kr
    if use_rope:
        kr = kr * cos + kv[:, KV_RANK + HEAD_SLAB:] * sin
    _key_value_heads(ckv.astype(BF16), kr, wk_ref, wvt_ref, k_ref, vt_ref)


def _inproj(h, mod3, mod_row, lw, rope, tm, key_tile=None, n_key_tiles=None):
    t = h.shape[0]
    use_rope = rope is not None
    const = lambda i: (0, 0)
    row = lambda i: (i, 0)
    if key_tile is None:
        key_tile, n_key_tiles = (lambda i: i), t // tm
    t_keys = n_key_tiles * tm
    in_specs = [pl.BlockSpec((tm, D_MODEL), row),
                pl.BlockSpec((None, 3, D_MODEL), lambda i: (mod_row(i), 0, 0)),
                pl.BlockSpec((1, D_MODEL), const),
                pl.BlockSpec((D_MODEL, W1_WIDTH), const),
                pl.BlockSpec((1, Q_RANK), const),
                pl.BlockSpec((Q_RANK, N_HEADS * HEAD_SLAB), const)]
    args = [h, mod3, lw['norm_g'], lw['w1'], lw['q_norm_g'], lw['wq']]
    if use_rope:
        in_specs.append(pl.BlockSpec((Q_RANK, N_HEADS * HEAD_SLAB), const))
        args.append(lw['wq_rot'])
    in_specs += [pl.BlockSpec((1, KV_RANK), const),
                 pl.BlockSpec((KV_RANK, N_HEADS * HEAD_SLAB), const),
                 pl.BlockSpec((ATT_WIDTH, KV_RANK), const)]
    args += [lw['kv_norm_g'], lw['wk'], lw['wvt']]
    if use_rope:
        in_specs += [pl.BlockSpec((tm, HEAD_SLAB), row), pl.BlockSpec((tm, HEAD_SLAB), row)]
        args += [rope[0], rope[1]]
    wide = N_HEADS * HEAD_SLAB
    token_outs = lambda ws: ([pl.BlockSpec((tm, w), row) for w, _ in ws],
                             [jax.ShapeDtypeStruct((t, w), dt) for w, dt in ws])
    specs_a, shapes_a = token_outs([(FN_WIDTH, F32), (FN_WIDTH, BF16), (POOL_WIDTH, F32), (POOL_WIDTH, BF16),
                                    (wide, BF16)])
    specs_b, shapes_b = token_outs([(KV_RANK, F32), (HEAD_SLAB, F32), (ATT_WIDTH, BF16)])
    out_specs = specs_a + [pl.BlockSpec((tm, wide), lambda i: (key_tile(i), 0)),
                           pl.BlockSpec((ATT_WIDTH, tm), lambda i: (0, key_tile(i)))] + specs_b
    out_shape = shapes_a + [jax.ShapeDtypeStruct((t_keys, wide), BF16),
                            jax.ShapeDtypeStruct((ATT_WIDTH, t_keys), BF16)] + shapes_b
    return pl.pallas_call(
        functools.partial(_inproj_kernel, use_rope=use_rope),
        grid=(t // tm,),
        in_specs=in_specs,
        out_specs=out_specs,
        out_shape=out_shape,
        compiler_params=_params(1),
        name="inproj_rope" if use_rope else "inproj",
    )(*args)


def _cache_kv_kernel(ckv_ref, kr_ref, wk_ref, wvt_ref, k_in_ref, vt_in_ref, k_ref, vt_ref):
    del k_in_ref, vt_in_ref
    _key_value_heads(ckv_ref[...].astype(BF16), kr_ref[...], wk_ref, wvt_ref, k_ref, vt_ref)


def _cache_kv(ckv, kr_slab, lw, k_all, vt_all, tile, key_tile):
    const = lambda b: (0, 0)
    row = lambda b: (b, 0)
    wide = N_HEADS * HEAD_SLAB
    return pl.pallas_call(
        _cache_kv_kernel,
        grid=(ckv.shape[0] // tile,),
        in_specs=[pl.BlockSpec((tile, KV_RANK), row), pl.BlockSpec((tile, HEAD_SLAB), row),
                  pl.BlockSpec((KV_RANK, wide), const), pl.BlockSpec((ATT_WIDTH, KV_RANK), const),
                  pl.BlockSpec(memory_space=pl.ANY), pl.BlockSpec(memory_space=pl.ANY)],
        out_specs=[pl.BlockSpec((tile, wide), lambda b: (key_tile(b), 0)),
                   pl.BlockSpec((ATT_WIDTH, tile), lambda b: (0, key_tile(b)))],
        out_shape=[jax.ShapeDtypeStruct(k_all.shape, BF16), jax.ShapeDtypeStruct(vt_all.shape, BF16)],
        input_output_aliases={4: 0, 5: 1},
        compiler_params=_params(1),
        name="cache_kv",
    )(ckv, kr_slab, lw['wk'], lw['wvt'], k_all, vt_all)


def _hi_lo(m):
    m = jnp.asarray(m, F32)
    hi = m.astype(BF16)
    return hi, (m - hi.astype(F32)).astype(BF16)


def _dft_cos_sin(n):
    k = np.arange(n)
    ang = 2.0 * np.pi * ((k[:, None] * k[None, :]) % n) / n
    return np.cos(ang), np.sin(ang)


def _channel_dft_tables():
    c, s = _dft_cos_sin(FN_GC)
    eye = np.eye(FN_WIDTH // FN_GC)
    return np.kron(eye, c), np.kron(eye, s)


def _fourier_direct_kernel(a_ref, sa_ref, cc_hi, cc_lo, ss_hi, ss_lo, cl_hi, cl_lo, sl_hi, sl_lo, o_ref, *, norm):
    a = a_ref[...]
    tc = _dot3_right(a, cc_hi[...], cc_lo[...])
    ts = _dot3_right(a, ss_hi[...], ss_lo[...])
    f = _dot3_left(cl_hi[...], cl_lo[...], tc) - _dot3_left(sl_hi[...], sl_lo[...], ts)
    o_ref[...] = ((f * norm) * sa_ref[...].astype(F32)).astype(BF16)


def _fourier_direct(a, sa, seq):
    t = a.shape[0]
    cc, ss = _channel_dft_tables()
    cl, sl = _dft_cos_sin(seq)
    tables = [x for m in (cc, ss, cl, sl) for x in _hi_lo(m)]
    row = lambda i: (i, 0)
    const = lambda i: (0, 0)
    tspecs = [pl.BlockSpec((FN_WIDTH, FN_WIDTH), const)] * 4 + [pl.BlockSpec((seq, seq), const)] * 4
    return pl.pallas_call(
        functools.partial(_fourier_direct_kernel, norm=float((seq * FN_GC) ** -0.5)),
        grid=(t // seq,),
        in_specs=[pl.BlockSpec((seq, FN_WIDTH), row), pl.BlockSpec((seq, FN_WIDTH), row)] + tspecs,
        out_specs=pl.BlockSpec((seq, FN_WIDTH), row),
        out_shape=jax.ShapeDtypeStruct((t, FN_WIDTH), BF16),
        compiler_params=_params(1),
        name="fourier_direct",
    )(a, sa, *tables)


FFT_R = 64


def _fourier_fft_kernel(a_ref, sa_ref, cc_hi, cc_lo, ss_hi, ss_lo, m1_hi, m1_lo, m2_hi, m2_lo, twc_ref, tws_ref,
                        o_ref, zr0, zr1, zi0, zi1, yr0, yr1, yi0, yi1, *, seq, norm):
    r = FFT_R
    half = FN_WIDTH // 2
    chunk = 512

    def put(refs, rows, x):
        refs[0][rows, :] = x[:, :half]
        refs[1][rows, :] = x[:, half:]

    def get(refs, rows):
        return jnp.concatenate([refs[0][rows, :], refs[1][rows, :]], axis=1)

    zr, zi, yr_s, yi_s = (zr0, zr1), (zi0, zi1), (yr0, yr1), (yi0, yi1)
    for c in range(seq // chunk):
        rows = slice(c * chunk, (c + 1) * chunk)
        a = a_ref[rows, :]
        put(zr, rows, _dot3_right(a, cc_hi[...], cc_lo[...]))
        put(zi, rows, -_dot3_right(a, ss_hi[...], ss_lo[...]))
    for n2 in range(r):
        strided = pl.ds(n2, r, stride=r)
        z = jnp.concatenate([get(zr, strided), get(zi, strided)], axis=0)
        y = _dot3_left(m1_hi[...], m1_lo[...], z)
        yr, yi = y[:r], y[r:]
        blk = slice(n2 * r, (n2 + 1) * r)
        cos = jnp.concatenate([twc_ref[blk, :]] * 2, axis=1)
        sin = jnp.concatenate([tws_ref[blk, :]] * 2, axis=1)
        put(yr_s, blk, yr * cos + yi * sin)
        put(yi_s, blk, yi * cos - yr * sin)
    for k1 in range(r):
        strided = pl.ds(k1, r, stride=r)
        y = jnp.concatenate([get(yr_s, strided), get(yi_s, strided)], axis=0)
        put(zr, strided, _dot3_left(m2_hi[...], m2_lo[...], y))
    for c in range(seq // chunk):
        rows = slice(c * chunk, (c + 1) * chunk)
        o_ref[rows, :] = ((get(zr, rows) * norm) * sa_ref[rows, :].astype(F32)).astype(BF16)


def _fourier_fft(a, sa, seq):
    assert seq == FFT_R * FFT_R
    t = a.shape[0]
    cc, ss = _channel_dft_tables()
    c, s = _dft_cos_sin(FFT_R)
    m1 = np.block([[c, s], [-s, c]])
    m2 = np.concatenate([c, s], axis=1)
    n2 = np.arange(FFT_R)[:, None]
    k1 = np.arange(FFT_R)[None, :]
    ang = (2.0 * np.pi * (n2 * k1) / seq).reshape(seq, 1)
    twc = jnp.broadcast_to(jnp.asarray(np.cos(ang), F32), (seq, 128))
    tws = jnp.broadcast_to(jnp.asarray(np.sin(ang), F32), (seq, 128))
    tables = [x for m in (cc, ss, m1, m2) for x in _hi_lo(m)]
    row = lambda i: (i, 0)
    const = lambda i: (0, 0)
    tspecs = ([pl.BlockSpec((FN_WIDTH, FN_WIDTH), const)] * 4 + [pl.BlockSpec((2 * FFT_R, 2 * FFT_R), const)] * 2
              + [pl.BlockSpec((FFT_R, 2 * FFT_R), const)] * 2 + [pl.BlockSpec((seq, 128), const)] * 2)
    return pl.pallas_call(
        functools.partial(_fourier_fft_kernel, seq=seq, norm=float((seq * FN_GC) ** -0.5)),
        grid=(t // seq,),
        in_specs=[pl.BlockSpec((seq, FN_WIDTH), row), pl.BlockSpec((seq, FN_WIDTH), row)] + tspecs,
        out_specs=pl.BlockSpec((seq, FN_WIDTH), row),
        out_shape=jax.ShapeDtypeStruct((t, FN_WIDTH), BF16),
        scratch_shapes=[pltpu.VMEM((seq, FN_WIDTH // 2), F32)] * 8,
        compiler_params=_params(1),
        name="fourier_fft",
    )(a, sa, *tables, twc, tws)


POOL_HALO = 8
POOL_CHUNK = 256


def _pool_kernel(b_ref, sb_ref, pw_ref, ps_ref, o_ref, pad_ref, *, seq):
    zeros = jnp.zeros((POOL_HALO, POOL_WIDTH), F32)
    pad_ref[0:POOL_HALO, :] = zeros
    pad_ref[POOL_HALO + seq:, :] = zeros
    pad_ref[POOL_HALO:POOL_HALO + seq, :] = b_ref[...]
    lane = lax.broadcasted_iota(jnp.int32, (POOL_CHUNK, 128), 1)
    low_group = lane < POOL_GC

    for c in range(seq // POOL_CHUNK):
        r0 = c * POOL_CHUNK
        t = lax.broadcasted_iota(jnp.int32, (POOL_CHUNK, 128), 0) + r0

        def inv_count(w):
            left = w // 2
            right = w - 1 - left
            lo = jnp.maximum(t - left, 0)
            hi = jnp.minimum(t + right, seq - 1)
            return 1.0 / (hi - lo + 1).astype(F32)

        def ld(off, col):
            start = POOL_HALO + r0 + off
            return pad_ref[start:start + POOL_CHUNK, 128 * col:128 * (col + 1)]

        u0 = ld(0, 0)
        p2 = ld(-1, 0) + u0
        p4 = p2 + ld(-2, 0) + ld(1, 0)
        pooled0 = jnp.where(low_group, p2 * inv_count(2), p4 * inv_count(4)) - u0
        u1 = ld(0, 1)
        p8 = u1
        for off in (-4, -3, -2, -1, 1, 2, 3):
            p8 = p8 + ld(off, 1)
        p16 = p8
        for off in (-8, -7, -6, -5, 4, 5, 6, 7):
            p16 = p16 + ld(off, 1)
        pooled1 = jnp.where(low_group, p8 * inv_count(8), p16 * inv_count(16)) - u1

        pooled = jnp.concatenate([pooled0, pooled1], axis=1).astype(BF16)
        mixed = _dot(pooled, pw_ref[...]) * ps_ref[...]
        rows = slice(r0, r0 + POOL_CHUNK)
        o_ref[rows, :] = (mixed * sb_ref[rows, :].astype(F32)).astype(BF16)


def _pool(b, sb, lw, seq):
    t = b.shape[0]
    row = lambda i: (i, 0)
    const = lambda i: (0, 0)
    return pl.pallas_call(
        functools.partial(_pool_kernel, seq=seq),
        grid=(t // seq,),
        in_specs=[pl.BlockSpec((seq, POOL_WIDTH), row), pl.BlockSpec((seq, POOL_WIDTH), row),
                  pl.BlockSpec((POOL_WIDTH, POOL_WIDTH), const), pl.BlockSpec((1, POOL_WIDTH), const)],
        out_specs=pl.BlockSpec((seq, POOL_WIDTH), row),
        out_shape=jax.ShapeDtypeStruct((t, POOL_WIDTH), BF16),
        scratch_shapes=[pltpu.VMEM((seq + 2 * POOL_HALO, POOL_WIDTH), F32)],
        compiler_params=_params(1),
        name="pool",
    )(b, sb, lw['pool_w'], lw['pool_scale'])


def _attn_kernel(q_ref, k_ref, vt_ref, sc_ref, o_ref, s_ref, m_ref, l_ref, acc_ref, *, heads, n_chunks, chunk):
    tq = q_ref.shape[0]
    groups = chunk // 8

    def key_rows(c):
        return pl.ds(c * chunk, chunk) if isinstance(c, int) else pl.ds(pl.multiple_of(c * chunk, chunk), chunk)

    def scores(h, c):
        sl = slice(HEAD_SLAB * h, HEAD_SLAB * (h + 1))
        rows = key_rows(c)
        s = lax.dot_general(k_ref[rows, sl], q_ref[:, sl], _NT, preferred_element_type=F32)
        s_ref[h % 2, rows, :] = s
        m_ref[h % 2] = jnp.maximum(m_ref[h % 2], jnp.max(s.reshape(groups, 8, tq), axis=0))

    def weigh(h, c, m):
        rows = key_rows(c)
        p = jnp.exp2(s_ref[h % 2, rows, :] - m)
        l_ref[...] += jnp.sum(p.reshape(groups, 8, tq), axis=0)
        acc_ref[...] += _dot(vt_ref[V_DIM * h:V_DIM * (h + 1), rows], p.astype(BF16))

    def over_chunks(body):
        if n_chunks == 1:
            body(0)
        else:
            lax.fori_loop(0, n_chunks, lambda c, carry: (body(c), carry)[1], 0)

    neg_inf = jnp.full((8, tq), -jnp.inf, F32)
    m_ref[0] = neg_inf
    over_chunks(lambda c: scores(0, c))
    outs = []
    for h in range(heads):
        m = jnp.max(m_ref[h % 2], axis=0, keepdims=True)
        l_ref[...] = jnp.zeros((8, tq), F32)
        acc_ref[...] = jnp.zeros((V_DIM, tq), F32)
        if h + 1 < heads:
            m_ref[(h + 1) % 2] = neg_inf

        def both(c, h=h, m=m):
            weigh(h, c, m)
            if h + 1 < heads:
                scores(h + 1, c)

        over_chunks(both)
        denom = jnp.sum(l_ref[...], axis=0, keepdims=True)
        outs.append(acc_ref[...] * (1.0 / denom))
        if h % 2 == 1:
            o_pair = jnp.concatenate(outs, axis=0).T
            outs = []
            sl = slice(HEAD_SLAB * (h // 2), HEAD_SLAB * (h // 2 + 1))
            o_ref[:, sl] = (o_pair * sc_ref[:, sl].astype(F32)).astype(BF16)


def _attention(q, k, vt, sc, batch, lq, lk, tq, heads_per_step, chunk):
    nq = lq // tq
    n_hp = N_HEADS // heads_per_step
    qw = heads_per_step * HEAD_SLAB
    ow = heads_per_step * V_DIM
    q_map = lambda b, g, i: (b * nq + i, g)
    return pl.pallas_call(
        functools.partial(_attn_kernel, heads=heads_per_step, n_chunks=lk // chunk, chunk=chunk),
        grid=(batch, n_hp, nq),
        in_specs=[pl.BlockSpec((tq, qw), q_map), pl.BlockSpec((lk, qw), lambda b, g, i: (b, g)),
                  pl.BlockSpec((ow, lk), lambda b, g, i: (g, b)), pl.BlockSpec((tq, ow), q_map)],
        out_specs=pl.BlockSpec((tq, ow), q_map),
        out_shape=jax.ShapeDtypeStruct((batch * lq, ATT_WIDTH), BF16),
        scratch_shapes=[pltpu.VMEM((2, lk, tq), F32), pltpu.VMEM((2, 8, tq), F32), pltpu.VMEM((8, tq), F32),
                        pltpu.VMEM((V_DIM, tq), F32)],
        compiler_params=_params(3),
        name=f"attention_{lk}",
    )(q, k, vt, sc)


def _out_kernel(*refs, final):
    if final:
        h_ref, mod_ref, ng_ref, xa_ref, xb_ref, xc_ref, wa_ref, wb_ref, wc_ref, wg_ref, wo_ref, fg_ref, o_ref = refs
    else:
        h_ref, mod_ref, ng_ref, xa_ref, xb_ref, xc_ref, wa_ref, wb_ref, wc_ref, wg_ref, wo_ref, o_ref = refs
    h = h_ref[...]
    xn = _modulated_norm(h, ng_ref[...], mod_ref).astype(BF16)
    y = None
    for i, (x_ref, w_ref) in enumerate(((xa_ref, wa_ref), (xb_ref, wb_ref), (xc_ref, wc_ref))):
        g = jax.nn.sigmoid(_dot(xn, wg_ref[:, i * D_MODEL:(i + 1) * D_MODEL]))
        term = g * _dot(x_ref[...], w_ref[...])
        y = term if y is None else y + term
    h_new = h + mod_ref[2:3, :] * _dot(y.astype(BF16), wo_ref[...])
    if final:
        o_ref[...] = _rms(h_new, fg_ref[...])
    else:
        o_ref[...] = h_new


def _out(h, mod3, mod_row, xa, xb, xc, lw, final_g, tm):
    t = h.shape[0]
    final = final_g is not None
    const = lambda i: (0, 0)
    row = lambda i: (i, 0)
    in_specs = [pl.BlockSpec((tm, D_MODEL), row),
                pl.BlockSpec((None, 3, D_MODEL), lambda i: (mod_row(i), 0, 0)),
                pl.BlockSpec((1, D_MODEL), const),
                pl.BlockSpec((tm, FN_WIDTH), row), pl.BlockSpec((tm, POOL_WIDTH), row),
                pl.BlockSpec((tm, ATT_WIDTH), row),
                pl.BlockSpec((FN_WIDTH, D_MODEL), const), pl.BlockSpec((POOL_WIDTH, D_MODEL), const),
                pl.BlockSpec((ATT_WIDTH, D_MODEL), const), pl.BlockSpec((D_MODEL, 3 * D_MODEL), const),
                pl.BlockSpec((D_MODEL, D_MODEL), const)]
    args = [h, mod3, lw['norm_g'], xa, xb, xc, lw['w_br_a'], lw['w_br_b'], lw['w_br_c'], lw['wg'], lw['w_out']]
    if final:
        in_specs.append(pl.BlockSpec((1, D_MODEL), const))
        args.append(final_g)
    return pl.pallas_call(
        functools.partial(_out_kernel, final=final),
        grid=(t // tm,),
        in_specs=in_specs,
        out_specs=pl.BlockSpec((tm, D_MODEL), row),
        out_shape=jax.ShapeDtypeStruct((t, D_MODEL), F32),
        compiler_params=_params(1),
        name="out_final" if final else "out",
    )(*args)


def _rot_cols(w):
    q = QK_ROPE // 4
    return jnp.concatenate([-w[..., q:2 * q], w[..., :q], -w[..., 3 * q:], w[..., 2 * q:3 * q]], axis=-1)


def _rope_slab(w):
    return jnp.pad(w, ((0, 0), (QK_NOPE, HEAD_SLAB - QK_NOPE - QK_ROPE)))


def _pack_layer(l, norm_g, w_in, pool_w, pool_scale, q_norm_g, w_q_up, kv_norm_g, w_kv_up, w_br_a, w_br_b, w_br_c,
                w_out):
    wi = w_in[l]
    w_kr = wi[:, _OFF_KR:_OFF_CZ]
    w1 = jnp.concatenate([wi[:, :_OFF_KV], wi[:, _OFF_CZ:_OFF_G], wi[:, _OFF_KV:_OFF_KR], _rope_slab(w_kr),
                          _rope_slab(_rot_cols(w_kr))], axis=1).astype(BF16)
    wq_h = w_q_up[l].reshape(Q_RANK, N_HEADS, QK_NOPE + QK_ROPE)
    pad_q = HEAD_SLAB - QK_NOPE - QK_ROPE
    wq = jnp.pad(wq_h, ((0, 0), (0, 0), (0, pad_q))).reshape(Q_RANK, N_HEADS * HEAD_SLAB).astype(BF16)
    wq_rot = jnp.pad(_rot_cols(wq_h[..., QK_NOPE:]), ((0, 0), (0, 0), (QK_NOPE, pad_q)))
    wq_rot = wq_rot.reshape(Q_RANK, N_HEADS * HEAD_SLAB).astype(BF16)
    wkv_h = w_kv_up[l].reshape(KV_RANK, N_HEADS, QK_NOPE + V_DIM)
    wk = jnp.pad(wkv_h[..., :QK_NOPE], ((0, 0), (0, 0), (0, HEAD_SLAB - QK_NOPE)))
    wk = wk.reshape(KV_RANK, N_HEADS * HEAD_SLAB).astype(BF16)
    wvt = wkv_h[..., QK_NOPE:].reshape(KV_RANK, ATT_WIDTH).T.astype(BF16)
    n_groups = len(POOL_WINDOWS)
    pw = jnp.zeros((POOL_WIDTH, POOL_WIDTH), F32)
    for g in range(n_groups):
        pw = pw.at[g * POOL_GC:(g + 1) * POOL_GC, g * POOL_GC:(g + 1) * POOL_GC].set(pool_w[l, g])
    return {
        'norm_g': norm_g[l].reshape(1, D_MODEL), 'w1': w1, 'wg': wi[:, _OFF_G:].astype(BF16),
        'q_norm_g': q_norm_g[l].reshape(1, Q_RANK), 'wq': wq, 'wq_rot': wq_rot,
        'kv_norm_g': kv_norm_g[l].reshape(1, KV_RANK), 'wk': wk, 'wvt': wvt,
        'pool_w': pw.astype(BF16), 'pool_scale': pool_scale[l].reshape(1, POOL_WIDTH),
        'w_br_a': w_br_a[l].astype(BF16), 'w_br_b': w_br_b[l].astype(BF16), 'w_br_c': w_br_c[l].astype(BF16),
        'w_out': w_out[l].astype(BF16),
    }


def _rope_tables(seq):
    rows = seq // GRID_W
    t = jnp.arange(seq)
    row = jnp.repeat(jnp.arange(rows), GRID_W).astype(F32)
    col = (t % GRID_W).astype(F32)
    half = QK_ROPE // 2
    freqs = ROPE_THETA ** (-jnp.arange(0, half, 2, dtype=F32) / half)
    ar = row[:, None] * freqs
    ac = col[:, None] * freqs
    cos = jnp.concatenate([jnp.cos(ar), jnp.cos(ar), jnp.cos(ac), jnp.cos(ac)], axis=-1)
    sin = jnp.concatenate([jnp.sin(ar), jnp.sin(ar), jnp.sin(ac), jnp.sin(ac)], axis=-1)
    pad = HEAD_SLAB - QK_NOPE - QK_ROPE
    cos = jnp.concatenate([jnp.ones((seq, QK_NOPE), F32), cos, jnp.ones((seq, pad), F32)], axis=-1)
    sin = jnp.concatenate([jnp.zeros((seq, QK_NOPE), F32), sin, jnp.zeros((seq, pad), F32)], axis=-1)
    return cos, sin


TOKEN_TILE = 512
SAMPLE_Q_TILE = 512
SAMPLE_HEADS_PER_STEP = 4
SAMPLE_KEY_CHUNK = 512


def kernel(x_prompt, x_sample, cache_ckv, cache_krope, c, c_ctx, norm_g, w_mod, b_mod, w_in, pool_w, pool_scale,
           q_norm_g, w_q_up, kv_norm_g, w_kv_up, w_br_a, w_br_b, w_br_c, w_out, final_norm_g):
    batch, seq, _ = x_prompt.shape
    dec_batch, dec_seq, _ = x_sample.shape
    past = cache_ckv.shape[2]
    tm = TOKEN_TILE

    mod_rows = 8
    cvec = jnp.concatenate([c_ctx[None, :], c, jnp.zeros((mod_rows - 1 - dec_batch, D_MODEL), F32)], axis=0)
    mod = _modulation(cvec, w_mod, b_mod).reshape(DEPTH, mod_rows, 3, D_MODEL)
    prompt_row = lambda i: 0
    tiles_per_sample = dec_seq // tm
    assert past == tm, "the cached context must fill exactly one key tile"
    sample_row = lambda i: 1 + i // tiles_per_sample

    cos, sin = _rope_tables(dec_seq)
    rope = (jnp.tile(cos, (dec_batch, 1)), jnp.tile(sin, (dec_batch, 1)))
    final_g = final_norm_g.reshape(1, D_MODEL)

    hp = x_prompt.reshape(batch * seq, D_MODEL)
    hs = x_sample.reshape(dec_batch * dec_seq, D_MODEL)
    ckv_list, kr_list = [], []
    for l in range(DEPTH):
        lw = _pack_layer(l, norm_g, w_in, pool_w, pool_scale, q_norm_g, w_q_up, kv_norm_g, w_kv_up, w_br_a, w_br_b,
                         w_br_c, w_out)
        last = final_g if l == DEPTH - 1 else None

        a_in, sa, b_in, sb, q, k, vt, ckv, kr, sc = _inproj(hp, mod[l], prompt_row, lw, None, tm)
        ckv_list.append(ckv.reshape(batch, seq, KV_RANK))
        kr_list.append(kr[:, QK_NOPE:QK_NOPE + QK_ROPE].reshape(batch, seq, QK_ROPE))
        xa = _fourier_direct(a_in, sa, seq)
        xb = _pool(b_in, sb, lw, seq)
        xc = _attention(q, k, vt, sc, batch, seq, seq, seq, N_HEADS, seq)
        hp = _out(hp, mod[l], prompt_row, xa, xb, xc, lw, last, tm)

        key_tiles = tiles_per_sample + 1
        a_in, sa, b_in, sb, q, k, vt, _, _, sc = _inproj(
            hs, mod[l], sample_row, lw, rope, tm,
            key_tile=lambda i: (i // tiles_per_sample) * key_tiles + i % tiles_per_sample,
            n_key_tiles=dec_batch * key_tiles)
        kr_cache = jnp.pad(cache_krope[:, l], ((0, 0), (0, 0), (QK_NOPE, HEAD_SLAB - QK_NOPE - QK_ROPE)))
        k, vt = _cache_kv(cache_ckv[:, l].reshape(dec_batch * past, KV_RANK),
                          kr_cache.reshape(dec_batch * past, HEAD_SLAB), lw, k, vt, past,
                          key_tile=lambda b: b * key_tiles + tiles_per_sample)
        xa = _fourier_fft(a_in, sa, dec_seq)
        xb = _pool(b_in, sb, lw, dec_seq)
        xc = _attention(q, k, vt, sc, dec_batch, dec_seq, dec_seq + past, SAMPLE_Q_TILE, SAMPLE_HEADS_PER_STEP,
                        SAMPLE_KEY_CHUNK)
        hs = _out(hs, mod[l], sample_row, xa, xb, xc, lw, last, tm)

    y_prompt = hp.reshape(batch, seq, D_MODEL)
    y_sample = hs.reshape(dec_batch, dec_seq, D_MODEL)
    return (y_prompt, y_sample, jnp.stack(ckv_list, axis=1), jnp.stack(kr_list, axis=1))
```

```python
import functools

import numpy as np
import jax
import jax.numpy as jnp
from jax import lax
from jax.experimental import pallas as pl
from jax.experimental.pallas import tpu as pltpu

D_MODEL = 1024
DEPTH = 2
GRID_W = 64
EPS = 1e-6
FN_WIDTH = 256
FN_GC = 64
POOL_WINDOWS = (2, 4, 8, 16)
POOL_WIDTH = 256
POOL_GC = 64
N_HEADS = 8
QK_NOPE = 64
QK_ROPE = 32
V_DIM = 64
Q_RANK = 256
KV_RANK = 128
ATT_WIDTH = 512
ROPE_THETA = 10000.0
HEAD_SLAB = 128
QK_SCALE = (QK_NOPE + QK_ROPE) ** -0.5
Q_PRESCALE = QK_SCALE * float(np.log2(np.e))

VMEM_LIMIT_BYTES = 56 * 1024 * 1024

F32 = jnp.float32
BF16 = jnp.bfloat16

_OFF_A, _OFF_B, _OFF_Q, _OFF_KV, _OFF_KR, _OFF_CZ, _OFF_G = 0, 512, 1024, 1280, 1408, 1440, 1952
_W1_A = (0, 512)
_W1_B = (512, 1024)
_W1_Q = (1024, 1280)
_W1_CZ = (1280, 1792)
_W1_KV = (1792, 2176)
W1_WIDTH = 2176


def _params(n_parallel):
    return pltpu.CompilerParams(dimension_semantics=("arbitrary",) * n_parallel,
                                vmem_limit_bytes=VMEM_LIMIT_BYTES)


def _dot(a, b):
    return jnp.dot(a, b, preferred_element_type=F32)


def _silu(x):
    return x * jax.nn.sigmoid(x)


def _rms(x, g):
    r = lax.rsqrt(jnp.mean(x * x, axis=-1, keepdims=True) + EPS)
    return (x * r) * g


def _modulated_norm(h, norm_g, mod_ref):
    shift = mod_ref[0:1, :]
    scale = mod_ref[1:2, :]
    return _rms(h, norm_g) * (1.0 + scale) + shift


def _split_bf16(x):
    hi = x.astype(BF16)
    lo = (x - hi.astype(F32)).astype(BF16)
    return hi, lo


def _dot3_right(x, m_hi, m_lo):
    x_hi, x_lo = _split_bf16(x)
    return _dot(x_hi, m_hi) + _dot(x_lo, m_hi) + _dot(x_hi, m_lo)


def _dot3_left(m_hi, m_lo, x):
    x_hi, x_lo = _split_bf16(x)
    return _dot(m_hi, x_hi) + _dot(m_hi, x_lo) + _dot(m_lo, x_hi)


def _mod_kernel(c_ref, w_ref, b_ref, o_ref):
    s = _silu(c_ref[...]).astype(BF16)
    o_ref[...] = _dot(s, w_ref[...].astype(BF16)) + b_ref[...]


def _modulation(cvec, w_mod, b_mod):
    rows = cvec.shape[0]
    tn = 768
    return pl.pallas_call(
        _mod_kernel,
        grid=(DEPTH, 3 * D_MODEL // tn),
        in_specs=[pl.BlockSpec((rows, D_MODEL), lambda l, j: (0, 0)),
                  pl.BlockSpec((None, D_MODEL, tn), lambda l, j: (l, 0, j)),
                  pl.BlockSpec((None, 1, tn), lambda l, j: (l, 0, j))],
        out_specs=pl.BlockSpec((None, rows, tn), lambda l, j: (l, 0, j)),
        out_shape=jax.ShapeDtypeStruct((DEPTH, rows, 3 * D_MODEL), F32),
        compiler_params=_params(2),
        name="modulation",
    )(cvec, w_mod, b_mod.reshape(DEPTH, 1, 3 * D_MODEL))


_NT = (((1,), (1,)), ((), ()))


def _key_value_heads(ckv_bf16, kr_slab, wk_ref, wvt_ref, k_ref, vt_ref):
    kn = _dot(ckv_bf16, wk_ref[...])
    for h in range(N_HEADS):
        sl = slice(HEAD_SLAB * h, HEAD_SLAB * (h + 1))
        k_ref[:, sl] = (kn[:, sl] + kr_slab).astype(BF16)
    vt_ref[...] = lax.dot_general(wvt_ref[...], ckv_bf16, _NT, preferred_element_type=F32).astype(BF16)


def _inproj_kernel(*refs, use_rope):
    if use_rope:
        (h_ref, mod_ref, ng_ref, w1_ref, qg_ref, wq_ref, wqr_ref, kvg_ref, wk_ref, wvt_ref, cos_ref, sin_ref,
         ain_ref, sa_ref, bin_ref, sb_ref, q_ref, k_ref, vt_ref, ckv_ref, kr_ref, sc_ref) = refs
    else:
        (h_ref, mod_ref, ng_ref, w1_ref, qg_ref, wq_ref, kvg_ref, wk_ref, wvt_ref,
         ain_ref, sa_ref, bin_ref, sb_ref, q_ref, k_ref, vt_ref, ckv_ref, kr_ref, sc_ref) = refs

    xn = _modulated_norm(h_ref[...], ng_ref[...], mod_ref).astype(BF16)

    def proj(cols):
        return _dot(xn, w1_ref[:, cols[0]:cols[1]])

    a = proj(_W1_A)
    ain_ref[...] = a[:, :FN_WIDTH]
    sa_ref[...] = _silu(a[:, FN_WIDTH:]).astype(BF16)
    b = proj(_W1_B)
    bin_ref[...] = b[:, :POOL_WIDTH]
    sb_ref[...] = _silu(b[:, POOL_WIDTH:]).astype(BF16)
    sc_ref[...] = _silu(proj(_W1_CZ)).astype(BF16)

    qn = _rms(proj(_W1_Q), qg_ref[...]).astype(BF16)
    q = _dot(qn, wq_ref[...])
    if use_rope:
        cos = cos_ref[...]
        sin = sin_ref[...]
        qr = _dot(qn, wqr_ref[...])
        for h in range(N_HEADS):
            sl = slice(HEAD_SLAB * h, HEAD_SLAB * (h + 1))
            q_ref[:, sl] = ((q[:, sl] * cos + qr[:, sl] * sin) * Q_PRESCALE).astype(BF16)
    else:
        q_ref[...] = (q * Q_PRESCALE).astype(BF16)

    kv = proj(_W1_KV)
    ckv = _rms(kv[:, :KV_RANK], kvg_ref[...])
    ckv_ref[...] = ckv
    kr = kv[:, KV_RANK:KV_RANK + HEAD_SLAB]
    kr_ref[...] = kr
    if use_rope:
        kr = kr * cos + kv[:, KV_RANK + HEAD_SLAB:] * sin
    _key_value_heads(ckv.astype(BF16), kr, wk_ref, wvt_ref, k_ref, vt_ref)


def _inproj(h, mod3, mod_row, lw, rope, tm, key_tile=None, n_key_tiles=None):
    t = h.shape[0]
    use_rope = rope is not None
    const = lambda i: (0, 0)
    row = lambda i: (i, 0)
    if key_tile is None:
        key_tile, n_key_tiles = (lambda i: i), t // tm
    t_keys = n_key_tiles * tm
    in_specs = [pl.BlockSpec((tm, D_MODEL), row),
                pl.BlockSpec((None, 3, D_MODEL), lambda i: (mod_row(i), 0, 0)),
                pl.BlockSpec((1, D_MODEL), const),
                pl.BlockSpec((D_MODEL, W1_WIDTH), const),
                pl.BlockSpec((1, Q_RANK), const),
                pl.BlockSpec((Q_RANK, N_HEADS * HEAD_SLAB), const)]
    args = [h, mod3, lw['norm_g'], lw['w1'], lw['q_norm_g'], lw['wq']]
    if use_rope:
        in_specs.append(pl.BlockSpec((Q_RANK, N_HEADS * HEAD_SLAB), const))
        args.append(lw['wq_rot'])
    in_specs += [pl.BlockSpec((1, KV_RANK), const),
                 pl.BlockSpec((KV_RANK, N_HEADS * HEAD_SLAB), const),
                 pl.BlockSpec((ATT_WIDTH, KV_RANK), const)]
    args += [lw['kv_norm_g'], lw['wk'], lw['wvt']]
    if use_rope:
        in_specs += [pl.BlockSpec((tm, HEAD_SLAB), row), pl.BlockSpec((tm, HEAD_SLAB), row)]
        args += [rope[0], rope[1]]
    wide = N_HEADS * HEAD_SLAB
    token_outs = lambda ws: ([pl.BlockSpec((tm, w), row) for w, _ in ws],
                             [jax.ShapeDtypeStruct((t, w), dt) for w, dt in ws])
    specs_a, shapes_a = token_outs([(FN_WIDTH, F32), (FN_WIDTH, BF16), (POOL_WIDTH, F32), (POOL_WIDTH, BF16),
                                    (wide, BF16)])
    specs_b, shapes_b = token_outs([(KV_RANK, F32), (HEAD_SLAB, F32), (ATT_WIDTH, BF16)])
    out_specs = specs_a + [pl.BlockSpec((tm, wide), lambda i: (key_tile(i), 0)),
                           pl.BlockSpec((ATT_WIDTH, tm), lambda i: (0, key_tile(i)))] + specs_b
    out_shape = shapes_a + [jax.ShapeDtypeStruct((t_keys, wide), BF16),
                            jax.ShapeDtypeStruct((ATT_WIDTH, t_keys), BF16)] + shapes_b
    return pl.pallas_call(
        functools.partial(_inproj_kernel, use_rope=use_rope),
        grid=(t // tm,),
        in_specs=in_specs,
        out_specs=out_specs,
        out_shape=out_shape,
        compiler_params=_params(1),
        name="inproj_rope" if use_rope else "inproj",
    )(*args)


def _cache_kv_kernel(ckv_ref, kr_ref, wk_ref, wvt_ref, k_in_ref, vt_in_ref, k_ref, vt_ref):
    del k_in_ref, vt_in_ref
    _key_value_heads(ckv_ref[...].astype(BF16), kr_ref[...], wk_ref, wvt_ref, k_ref, vt_ref)


def _cache_kv(ckv, kr_slab, lw, k_all, vt_all, tile, key_tile):
    const = lambda b: (0, 0)
    row = lambda b: (b, 0)
    wide = N_HEADS * HEAD_SLAB
    return pl.pallas_call(
        _cache_kv_kernel,
        grid=(ckv.shape[0] // tile,),
        in_specs=[pl.BlockSpec((tile, KV_RANK), row), pl.BlockSpec((tile, HEAD_SLAB), row),
                  pl.BlockSpec((KV_RANK, wide), const), pl.BlockSpec((ATT_WIDTH, KV_RANK), const),
                  pl.BlockSpec(memory_space=pl.ANY), pl.BlockSpec(memory_space=pl.ANY)],
        out_specs=[pl.BlockSpec((tile, wide), lambda b: (key_tile(b), 0)),
                   pl.BlockSpec((ATT_WIDTH, tile), lambda b: (0, key_tile(b)))],
        out_shape=[jax.ShapeDtypeStruct(k_all.shape, BF16), jax.ShapeDtypeStruct(vt_all.shape, BF16)],
        input_output_aliases={4: 0, 5: 1},
        compiler_params=_params(1),
        name="cache_kv",
    )(ckv, kr_slab, lw['wk'], lw['wvt'], k_all, vt_all)


def _hi_lo(m):
    m = jnp.asarray(m, F32)
    hi = m.astype(BF16)
    return hi, (m - hi.astype(F32)).astype(BF16)


def _dft_cos_sin(n):
    k = np.arange(n)
    ang = 2.0 * np.pi * ((k[:, None] * k[None, :]) % n) / n
    return np.cos(ang), np.sin(ang)


def _channel_dft_tables():
    c, s = _dft_cos_sin(FN_GC)
    eye = np.eye(FN_WIDTH // FN_GC)
    return np.kron(eye, c), np.kron(eye, s)


def _fourier_direct_kernel(a_ref, sa_ref, cc_hi, cc_lo, ss_hi, ss_lo, cl_hi, cl_lo, sl_hi, sl_lo, o_ref, *, norm):
    a = a_ref[...]
    tc = _dot3_right(a, cc_hi[...], cc_lo[...])
    ts = _dot3_right(a, ss_hi[...], ss_lo[...])
    f = _dot3_left(cl_hi[...], cl_lo[...], tc) - _dot3_left(sl_hi[...], sl_lo[...], ts)
    o_ref[...] = ((f * norm) * sa_ref[...].astype(F32)).astype(BF16)


def _fourier_direct(a, sa, seq):
    t = a.shape[0]
    cc, ss = _channel_dft_tables()
    cl, sl = _dft_cos_sin(seq)
    tables = [x for m in (cc, ss, cl, sl) for x in _hi_lo(m)]
    row = lambda i: (i, 0)
    const = lambda i: (0, 0)
    tspecs = [pl.BlockSpec((FN_WIDTH, FN_WIDTH), const)] * 4 + [pl.BlockSpec((seq, seq), const)] * 4
    return pl.pallas_call(
        functools.partial(_fourier_direct_kernel, norm=float((seq * FN_GC) ** -0.5)),
        grid=(t // seq,),
        in_specs=[pl.BlockSpec((seq, FN_WIDTH), row), pl.BlockSpec((seq, FN_WIDTH), row)] + tspecs,
        out_specs=pl.BlockSpec((seq, FN_WIDTH), row),
        out_shape=jax.ShapeDtypeStruct((t, FN_WIDTH), BF16),
        compiler_params=_params(1),
        name="fourier_direct",
    )(a, sa, *tables)


FFT_R = 64


def _fourier_fft_kernel(a_ref, sa_ref, cc_hi, cc_lo, ss_hi, ss_lo, m1_hi, m1_lo, m2_hi, m2_lo, twc_ref, tws_ref,
                        o_ref, zr0, zr1, zi0, zi1, yr0, yr1, yi0, yi1, *, seq, norm):
    r = FFT_R
    half = FN_WIDTH // 2
    chunk = 512

    def put(refs, rows, x):
        refs[0][rows, :] = x[:, :half]
        refs[1][rows, :] = x[:, half:]

    def get(refs, rows):
        return jnp.concatenate([refs[0][rows, :], refs[1][rows, :]], axis=1)

    zr, zi, yr_s, yi_s = (zr0, zr1), (zi0, zi1), (yr0, yr1), (yi0, yi1)
    for c in range(seq // chunk):
        rows = slice(c * chunk, (c + 1) * chunk)
        a = a_ref[rows, :]
        put(zr, rows, _dot3_right(a, cc_hi[...], cc_lo[...]))
        put(zi, rows, -_dot3_right(a, ss_hi[...], ss_lo[...]))
    for n2 in range(r):
        strided = pl.ds(n2, r, stride=r)
        z = jnp.concatenate([get(zr, strided), get(zi, strided)], axis=0)
        y = _dot3_left(m1_hi[...], m1_lo[...], z)
        yr, yi = y[:r], y[r:]
        blk = slice(n2 * r, (n2 + 1) * r)
        cos = jnp.concatenate([twc_ref[blk, :]] * 2, axis=1)
        sin = jnp.concatenate([tws_ref[blk, :]] * 2, axis=1)
        put(yr_s, blk, yr * cos + yi * sin)
        put(yi_s, blk, yi * cos - yr * sin)
    for k1 in range(r):
        strided = pl.ds(k1, r, stride=r)
        y = jnp.concatenate([get(yr_s, strided), get(yi_s, strided)], axis=0)
        put(zr, strided, _dot3_left(m2_hi[...], m2_lo[...], y))
    for c in range(seq // chunk):
        rows = slice(c * chunk, (c + 1) * chunk)
        o_ref[rows, :] = ((get(zr, rows) * norm) * sa_ref[rows, :].astype(F32)).astype(BF16)


def _fourier_fft(a, sa, seq):
    assert seq == FFT_R * FFT_R
    t = a.shape[0]
    cc, ss = _channel_dft_tables()
    c, s = _dft_cos_sin(FFT_R)
    m1 = np.block([[c, s], [-s, c]])
    m2 = np.concatenate([c, s], axis=1)
    n2 = np.arange(FFT_R)[:, None]
    k1 = np.arange(FFT_R)[None, :]
    ang = (2.0 * np.pi * (n2 * k1) / seq).reshape(seq, 1)
    twc = jnp.broadcast_to(jnp.asarray(np.cos(ang), F32), (seq, 128))
    tws = jnp.broadcast_to(jnp.asarray(np.sin(ang), F32), (seq, 128))
    tables = [x for m in (cc, ss, m1, m2) for x in _hi_lo(m)]
    row = lambda i: (i, 0)
    const = lambda i: (0, 0)
    tspecs = ([pl.BlockSpec((FN_WIDTH, FN_WIDTH), const)] * 4 + [pl.BlockSpec((2 * FFT_R, 2 * FFT_R), const)] * 2
              + [pl.BlockSpec((FFT_R, 2 * FFT_R), const)] * 2 + [pl.BlockSpec((seq, 128), const)] * 2)
    return pl.pallas_call(
        functools.partial(_fourier_fft_kernel, seq=seq, norm=float((seq * FN_GC) ** -0.5)),
        grid=(t // seq,),
        in_specs=[pl.BlockSpec((seq, FN_WIDTH), row), pl.BlockSpec((seq, FN_WIDTH), row)] + tspecs,
        out_specs=pl.BlockSpec((seq, FN_WIDTH), row),
        out_shape=jax.ShapeDtypeStruct((t, FN_WIDTH), BF16),
        scratch_shapes=[pltpu.VMEM((seq, FN_WIDTH // 2), F32)] * 8,
        compiler_params=_params(1),
        name="fourier_fft",
    )(a, sa, *tables, twc, tws)


POOL_HALO = 8
POOL_CHUNK = 256


def _pool_kernel(b_ref, sb_ref, pw_ref, ps_ref, o_ref, pad_ref, *, seq):
    zeros = jnp.zeros((POOL_HALO, POOL_WIDTH), F32)
    pad_ref[0:POOL_HALO, :] = zeros
    pad_ref[POOL_HALO + seq:, :] = zeros
    pad_ref[POOL_HALO:POOL_HALO + seq, :] = b_ref[...]
    lane = lax.broadcasted_iota(jnp.int32, (POOL_CHUNK, 128), 1)
    low_group = lane < POOL_GC

    for c in range(seq // POOL_CHUNK):
        r0 = c * POOL_CHUNK
        t = lax.broadcasted_iota(jnp.int32, (POOL_CHUNK, 128), 0) + r0

        def inv_count(w):
            left = w // 2
            right = w - 1 - left
            lo = jnp.maximum(t - left, 0)
            hi = jnp.minimum(t + right, seq - 1)
            return 1.0 / (hi - lo + 1).astype(F32)

        def ld(off, col):
            start = POOL_HALO + r0 + off
            return pad_ref[start:start + POOL_CHUNK, 128 * col:128 * (col + 1)]

        u0 = ld(0, 0)
        p2 = ld(-1, 0) + u0
        p4 = p2 + ld(-2, 0) + ld(1, 0)
        pooled0 = jnp.where(low_group, p2 * inv_count(2), p4 * inv_count(4)) - u0
        u1 = ld(0, 1)
        p8 = u1
        for off in (-4, -3, -2, -1, 1, 2, 3):
            p8 = p8 + ld(off, 1)
        p16 = p8
        for off in (-8, -7, -6, -5, 4, 5, 6, 7):
            p16 = p16 + ld(off, 1)
        pooled1 = jnp.where(low_group, p8 * inv_count(8), p16 * inv_count(16)) - u1

        pooled = jnp.concatenate([pooled0, pooled1], axis=1).astype(BF16)
        mixed = _dot(pooled, pw_ref[...]) * ps_ref[...]
        rows = slice(r0, r0 + POOL_CHUNK)
        o_ref[rows, :] = (mixed * sb_ref[rows, :].astype(F32)).astype(BF16)


def _pool(b, sb, lw, seq):
    t = b.shape[0]
    row = lambda i: (i, 0)
    const = lambda i: (0, 0)
    return pl.pallas_call(
        functools.partial(_pool_kernel, seq=seq),
        grid=(t // seq,),
        in_specs=[pl.BlockSpec((seq, POOL_WIDTH), row), pl.BlockSpec((seq, POOL_WIDTH), row),
                  pl.BlockSpec((POOL_WIDTH, POOL_WIDTH), const), pl.BlockSpec((1, POOL_WIDTH), const)],
        out_specs=pl.BlockSpec((seq, POOL_WIDTH), row),
        out_shape=jax.ShapeDtypeStruct((t, POOL_WIDTH), BF16),
        scratch_shapes=[pltpu.VMEM((seq + 2 * POOL_HALO, POOL_WIDTH), F32)],
        compiler_params=_params(1),
        name="pool",
    )(b, sb, lw['pool_w'], lw['pool_scale'])


def _attn_kernel(q_ref, k_ref, vt_ref, sc_ref, o_ref, s_ref, m_ref, l_ref, acc_ref, *, heads, n_chunks, chunk):
    tq = q_ref.shape[0]
    groups = chunk // 8

    def key_rows(c):
        return pl.ds(c * chunk, chunk) if isinstance(c, int) else pl.ds(pl.multiple_of(c * chunk, chunk), chunk)

    def scores(h, c):
        sl = slice(HEAD_SLAB * h, HEAD_SLAB * (h + 1))
        rows = key_rows(c)
        s = lax.dot_general(k_ref[rows, sl], q_ref[:, sl], _NT, preferred_element_type=F32)
        s_ref[h % 2, rows, :] = s
        m_ref[h % 2] = jnp.maximum(m_ref[h % 2], jnp.max(s.reshape(groups, 8, tq), axis=0))

    def weigh(h, c, m):
        rows = key_rows(c)
        p = jnp.exp2(s_ref[h % 2, rows, :] - m)
        l_ref[...] += jnp.sum(p.reshape(groups, 8, tq), axis=0)
        acc_ref[...] += _dot(vt_ref[V_DIM * h:V_DIM * (h + 1), rows], p.astype(BF16))

    def over_chunks(body):
        for c in range(n_chunks):
            body(c)

    neg_inf = jnp.full((8, tq), -jnp.inf, F32)
    m_ref[0] = neg_inf
    over_chunks(lambda c: scores(0, c))
    outs = []
    for h in range(heads):
        m = jnp.max(m_ref[h % 2], axis=0, keepdims=True)
        l_ref[...] = jnp.zeros((8, tq), F32)
        acc_ref[...] = jnp.zeros((V_DIM, tq), F32)
        if h + 1 < heads:
            m_ref[(h + 1) % 2] = neg_inf

        def both(c, h=h, m=m):
            weigh(h, c, m)
            if h + 1 < heads:
                scores(h + 1, c)

        over_chunks(both)
        denom = jnp.sum(l_ref[...], axis=0, keepdims=True)
        outs.append(acc_ref[...] * (1.0 / denom))
        if h % 2 == 1:
            o_pair = jnp.concatenate(outs, axis=0).T
            outs = []
            sl = slice(HEAD_SLAB * (h // 2), HEAD_SLAB * (h // 2 + 1))
            o_ref[:, sl] = (o_pair * sc_ref[:, sl].astype(F32)).astype(BF16)


def _attention(q, k, vt, sc, batch, lq, lk, tq, heads_per_step, chunk):
    nq = lq // tq
    n_hp = N_HEADS // heads_per_step
    qw = heads_per_step * HEAD_SLAB
    ow = heads_per_step * V_DIM
    q_map = lambda b, g, i: (b * nq + i, g)
    return pl.pallas_call(
        functools.partial(_attn_kernel, heads=heads_per_step, n_chunks=lk // chunk, chunk=chunk),
        grid=(batch, n_hp, nq),
        in_specs=[pl.BlockSpec((tq, qw), q_map), pl.BlockSpec((lk, qw), lambda b, g, i: (b, g)),
                  pl.BlockSpec((ow, lk), lambda b, g, i: (g, b)), pl.BlockSpec((tq, ow), q_map)],
        out_specs=pl.BlockSpec((tq, ow), q_map),
        out_shape=jax.ShapeDtypeStruct((batch * lq, ATT_WIDTH), BF16),
        scratch_shapes=[pltpu.VMEM((2, lk, tq), F32), pltpu.VMEM((2, 8, tq), F32), pltpu.VMEM((8, tq), F32),
                        pltpu.VMEM((V_DIM, tq), F32)],
        compiler_params=_params(3),
        name=f"attention_{lk}",
    )(q, k, vt, sc)


def _out_kernel(*refs, final):
    if final:
        h_ref, mod_ref, ng_ref, xa_ref, xb_ref, xc_ref, wa_ref, wb_ref, wc_ref, wg_ref, wo_ref, fg_ref, o_ref = refs
    else:
        h_ref, mod_ref, ng_ref, xa_ref, xb_ref, xc_ref, wa_ref, wb_ref, wc_ref, wg_ref, wo_ref, o_ref = refs
    h = h_ref[...]
    xn = _modulated_norm(h, ng_ref[...], mod_ref).astype(BF16)
    y = None
    for i, (x_ref, w_ref) in enumerate(((xa_ref, wa_ref), (xb_ref, wb_ref), (xc_ref, wc_ref))):
        g = jax.nn.sigmoid(_dot(xn, wg_ref[:, i * D_MODEL:(i + 1) * D_MODEL]))
        term = g * _dot(x_ref[...], w_ref[...])
        y = term if y is None else y + term
    h_new = h + mod_ref[2:3, :] * _dot(y.astype(BF16), wo_ref[...])
    if final:
        o_ref[...] = _rms(h_new, fg_ref[...])
    else:
        o_ref[...] = h_new


def _out(h, mod3, mod_row, xa, xb, xc, lw, final_g, tm):
    t = h.shape[0]
    final = final_g is not None
    const = lambda i: (0, 0)
    row = lambda i: (i, 0)
    in_specs = [pl.BlockSpec((tm, D_MODEL), row),
                pl.BlockSpec((None, 3, D_MODEL), lambda i: (mod_row(i), 0, 0)),
                pl.BlockSpec((1, D_MODEL), const),
                pl.BlockSpec((tm, FN_WIDTH), row), pl.BlockSpec((tm, POOL_WIDTH), row),
                pl.BlockSpec((tm, ATT_WIDTH), row),
                pl.BlockSpec((FN_WIDTH, D_MODEL), const), pl.BlockSpec((POOL_WIDTH, D_MODEL), const),
                pl.BlockSpec((ATT_WIDTH, D_MODEL), const), pl.BlockSpec((D_MODEL, 3 * D_MODEL), const),
                pl.BlockSpec((D_MODEL, D_MODEL), const)]
    args = [h, mod3, lw['norm_g'], xa, xb, xc, lw['w_br_a'], lw['w_br_b'], lw['w_br_c'], lw['wg'], lw['w_out']]
    if final:
        in_specs.append(pl.BlockSpec((1, D_MODEL), const))
        args.append(final_g)
    return pl.pallas_call(
        functools.partial(_out_kernel, final=final),
        grid=(t // tm,),
        in_specs=in_specs,
        out_specs=pl.BlockSpec((tm, D_MODEL), row),
        out_shape=jax.ShapeDtypeStruct((t, D_MODEL), F32),
        compiler_params=_params(1),
        name="out_final" if final else "out",
    )(*args)


def _rot_cols(w):
    q = QK_ROPE // 4
    return jnp.concatenate([-w[..., q:2 * q], w[..., :q], -w[..., 3 * q:], w[..., 2 * q:3 * q]], axis=-1)


def _rope_slab(w):
    return jnp.pad(w, ((0, 0), (QK_NOPE, HEAD_SLAB - QK_NOPE - QK_ROPE)))


def _pack_layer(l, norm_g, w_in, pool_w, pool_scale, q_norm_g, w_q_up, kv_norm_g, w_kv_up, w_br_a, w_br_b, w_br_c,
                w_out):
    wi = w_in[l]
    w_kr = wi[:, _OFF_KR:_OFF_CZ]
    w1 = jnp.concatenate([wi[:, :_OFF_KV], wi[:, _OFF_CZ:_OFF_G], wi[:, _OFF_KV:_OFF_KR], _rope_slab(w_kr),
                          _rope_slab(_rot_cols(w_kr))], axis=1).astype(BF16)
    wq_h = w_q_up[l].reshape(Q_RANK, N_HEADS, QK_NOPE + QK_ROPE)
    pad_q = HEAD_SLAB - QK_NOPE - QK_ROPE
    wq = jnp.pad(wq_h, ((0, 0), (0, 0), (0, pad_q))).reshape(Q_RANK, N_HEADS * HEAD_SLAB).astype(BF16)
    wq_rot = jnp.pad(_rot_cols(wq_h[..., QK_NOPE:]), ((0, 0), (0, 0), (QK_NOPE, pad_q)))
    wq_rot = wq_rot.reshape(Q_RANK, N_HEADS * HEAD_SLAB).astype(BF16)
    wkv_h = w_kv_up[l].reshape(KV_RANK, N_HEADS, QK_NOPE + V_DIM)
    wk = jnp.pad(wkv_h[..., :QK_NOPE], ((0, 0), (0, 0), (0, HEAD_SLAB - QK_NOPE)))
    wk = wk.reshape(KV_RANK, N_HEADS * HEAD_SLAB).astype(BF16)
    wvt = wkv_h[..., QK_NOPE:].reshape(KV_RANK, ATT_WIDTH).T.astype(BF16)
    n_groups = len(POOL_WINDOWS)
    pw = jnp.zeros((POOL_WIDTH, POOL_WIDTH), F32)
    for g in range(n_groups):
        pw = pw.at[g * POOL_GC:(g + 1) * POOL_GC, g * POOL_GC:(g + 1) * POOL_GC].set(pool_w[l, g])
    return {
        'norm_g': norm_g[l].reshape(1, D_MODEL), 'w1': w1, 'wg': wi[:, _OFF_G:].astype(BF16),
        'q_norm_g': q_norm_g[l].reshape(1, Q_RANK), 'wq': wq, 'wq_rot': wq_rot,
        'kv_norm_g': kv_norm_g[l].reshape(1, KV_RANK), 'wk': wk, 'wvt': wvt,
        'pool_w': pw.astype(BF16), 'pool_scale': pool_scale[l].reshape(1, POOL_WIDTH),
        'w_br_a': w_br_a[l].astype(BF16), 'w_br_b': w_br_b[l].astype(BF16), 'w_br_c': w_br_c[l].astype(BF16),
        'w_out': w_out[l].astype(BF16),
    }


def _rope_tables(seq):
    rows = seq // GRID_W
    t = jnp.arange(seq)
    row = jnp.repeat(jnp.arange(rows), GRID_W).astype(F32)
    col = (t % GRID_W).astype(F32)
    half = QK_ROPE // 2
    freqs = ROPE_THETA ** (-jnp.arange(0, half, 2, dtype=F32) / half)
    ar = row[:, None] * freqs
    ac = col[:, None] * freqs
    cos = jnp.concatenate([jnp.cos(ar), jnp.cos(ar), jnp.cos(ac), jnp.cos(ac)], axis=-1)
    sin = jnp.concatenate([jnp.sin(ar), jnp.sin(ar), jnp.sin(ac), jnp.sin(ac)], axis=-1)
    pad = HEAD_SLAB - QK_NOPE - QK_ROPE
    cos = jnp.concatenate([jnp.ones((seq, QK_NOPE), F32), cos, jnp.ones((seq, pad), F32)], axis=-1)
    sin = jnp.concatenate([jnp.zeros((seq, QK_NOPE), F32), sin, jnp.zeros((seq, pad), F32)], axis=-1)
    return cos, sin


TOKEN_TILE = 512
SAMPLE_Q_TILE = 512
SAMPLE_HEADS_PER_STEP = 4
SAMPLE_KEY_CHUNK = 512


def kernel(x_prompt, x_sample, cache_ckv, cache_krope, c, c_ctx, norm_g, w_mod, b_mod, w_in, pool_w, pool_scale,
           q_norm_g, w_q_up, kv_norm_g, w_kv_up, w_br_a, w_br_b, w_br_c, w_out, final_norm_g):
    batch, seq, _ = x_prompt.shape
    dec_batch, dec_seq, _ = x_sample.shape
    past = cache_ckv.shape[2]
    tm = TOKEN_TILE

    mod_rows = 8
    cvec = jnp.concatenate([c_ctx[None, :], c, jnp.zeros((mod_rows - 1 - dec_batch, D_MODEL), F32)], axis=0)
    mod = _modulation(cvec, w_mod, b_mod).reshape(DEPTH, mod_rows, 3, D_MODEL)
    prompt_row = lambda i: 0
    tiles_per_sample = dec_seq // tm
    assert past == tm, "the cached context must fill exactly one key tile"
    sample_row = lambda i: 1 + i // tiles_per_sample

    cos, sin = _rope_tables(dec_seq)
    rope = (jnp.tile(cos, (dec_batch, 1)), jnp.tile(sin, (dec_batch, 1)))
    final_g = final_norm_g.reshape(1, D_MODEL)

    hp = x_prompt.reshape(batch * seq, D_MODEL)
    hs = x_sample.reshape(dec_batch * dec_seq, D_MODEL)
    ckv_list, kr_list = [], []
    for l in range(DEPTH):
        lw = _pack_layer(l, norm_g, w_in, pool_w, pool_scale, q_norm_g, w_q_up, kv_norm_g, w_kv_up, w_br_a, w_br_b,
                         w_br_c, w_out)
        last = final_g if l == DEPTH - 1 else None

        a_in, sa, b_in, sb, q, k, vt, ckv, kr, sc = _inproj(hp, mod[l], prompt_row, lw, None, tm)
        ckv_list.append(ckv.reshape(batch, seq, KV_RANK))
        kr_list.append(kr[:, QK_NOPE:QK_NOPE + QK_ROPE].reshape(batch, seq, QK_ROPE))
        xa = _fourier_direct(a_in, sa, seq)
        xb = _pool(b_in, sb, lw, seq)
        xc = _attention(q, k, vt, sc, batch, seq, seq, seq, N_HEADS, seq)
        hp = _out(hp, mod[l], prompt_row, xa, xb, xc, lw, last, tm)

        key_tiles = tiles_per_sample + 1
        a_in, sa, b_in, sb, q, k, vt, _, _, sc = _inproj(
            hs, mod[l], sample_row, lw, rope, tm,
            key_tile=lambda i: (i // tiles_per_sample) * key_tiles + i % tiles_per_sample,
            n_key_tiles=dec_batch * key_tiles)
        kr_cache = jnp.pad(cache_krope[:, l], ((0, 0), (0, 0), (QK_NOPE, HEAD_SLAB - QK_NOPE - QK_ROPE)))
        k, vt = _cache_kv(cache_ckv[:, l].reshape(dec_batch * past, KV_RANK),
                          kr_cache.reshape(dec_batch * past, HEAD_SLAB), lw, k, vt, past,
                          key_tile=lambda b: b * key_tiles + tiles_per_sample)
        xa = _fourier_fft(a_in, sa, dec_seq)
        xb = _pool(b_in, sb, lw, dec_seq)
        xc = _attention(q, k, vt, sc, dec_batch, dec_seq, dec_seq + past, SAMPLE_Q_TILE, SAMPLE_HEADS_PER_STEP,
                        SAMPLE_KEY_CHUNK)
        hs = _out(hs, mod[l], sample_row, xa, xb, xc, lw, last, tm)

    y_prompt = hp.reshape(batch, seq, D_MODEL)
    y_sample = hs.reshape(dec_batch, dec_seq, D_MODEL)
    return (y_prompt, y_sample, jnp.stack(ckv_list, axis=1), jnp.stack(kr_list, axis=1))
```

```python
import functools

import numpy as np
import jax
import jax.numpy as jnp
from jax import lax
from jax.experimental import pallas as pl
from jax.experimental.pallas import tpu as pltpu

D_MODEL = 1024
DEPTH = 2
GRID_W = 64
EPS = 1e-6
FN_WIDTH = 256
FN_GC = 64
POOL_WINDOWS = (2, 4, 8, 16)
POOL_WIDTH = 256
POOL_GC = 64
N_HEADS = 8
QK_NOPE = 64
QK_ROPE = 32
V_DIM = 64
Q_RANK = 256
KV_RANK = 128
ATT_WIDTH = 512
ROPE_THETA = 10000.0
HEAD_SLAB = 128
QK_SCALE = (QK_NOPE + QK_ROPE) ** -0.5
Q_PRESCALE = QK_SCALE * float(np.log2(np.e))

VMEM_LIMIT_BYTES = 56 * 1024 * 1024

F32 = jnp.float32
BF16 = jnp.bfloat16

_OFF_A, _OFF_B, _OFF_Q, _OFF_KV, _OFF_KR, _OFF_CZ, _OFF_G = 0, 512, 1024, 1280, 1408, 1440, 1952
_W1_A = (0, 512)
_W1_B = (512, 1024)
_W1_Q = (1024, 1280)
_W1_CZ = (1280, 1792)
_W1_KV = (1792, 2176)
W1_WIDTH = 2176


def _params(n_parallel):
    return pltpu.CompilerParams(dimension_semantics=("arbitrary",) * n_parallel,
                                vmem_limit_bytes=VMEM_LIMIT_BYTES)


def _dot(a, b):
    return jnp.dot(a, b, preferred_element_type=F32)


def _silu(x):
    return x * jax.nn.sigmoid(x)


def _rms(x, g):
    r = lax.rsqrt(jnp.mean(x * x, axis=-1, keepdims=True) + EPS)
    return (x * r) * g


def _modulated_norm(h, norm_g, mod_ref):
    shift = mod_ref[0:1, :]
    scale = mod_ref[1:2, :]
    return _rms(h, norm_g) * (1.0 + scale) + shift


def _split_bf16(x):
    hi = x.astype(BF16)
    lo = (x - hi.astype(F32)).astype(BF16)
    return hi, lo


def _dot3_right(x, m_hi, m_lo):
    x_hi, x_lo = _split_bf16(x)
    return _dot(x_hi, m_hi) + _dot(x_lo, m_hi) + _dot(x_hi, m_lo)


def _dot3_left(m_hi, m_lo, x):
    x_hi, x_lo = _split_bf16(x)
    return _dot(m_hi, x_hi) + _dot(m_hi, x_lo) + _dot(m_lo, x_hi)


def _mod_kernel(c_ref, w_ref, b_ref, o_ref):
    s = _silu(c_ref[...]).astype(BF16)
    o_ref[...] = _dot(s, w_ref[...].astype(BF16)) + b_ref[...]


def _modulation(cvec, w_mod, b_mod):
    rows = cvec.shape[0]
    tn = 768
    return pl.pallas_call(
        _mod_kernel,
        grid=(DEPTH, 3 * D_MODEL // tn),
        in_specs=[pl.BlockSpec((rows, D_MODEL), lambda l, j: (0, 0)),
                  pl.BlockSpec((None, D_MODEL, tn), lambda l, j: (l, 0, j)),
                  pl.BlockSpec((None, 1, tn), lambda l, j: (l, 0, j))],
        out_specs=pl.BlockSpec((None, rows, tn), lambda l, j: (l, 0, j)),
        out_shape=jax.ShapeDtypeStruct((DEPTH, rows, 3 * D_MODEL), F32),
        compiler_params=_params(2),
        name="modulation",
    )(cvec, w_mod, b_mod.reshape(DEPTH, 1, 3 * D_MODEL))


_NT = (((1,), (1,)), ((), ()))


def _key_value_heads(ckv_bf16, kr_slab, wk_ref, wvt_ref, k_ref, vt_ref):
    kn = _dot(ckv_bf16, wk_ref[...])
    for h in range(N_HEADS):
        sl = slice(HEAD_SLAB * h, HEAD_SLAB * (h + 1))
        k_ref[:, sl] = (kn[:, sl] + kr_slab).astype(BF16)
    vt_ref[...] = lax.dot_general(wvt_ref[...], ckv_bf16, _NT, preferred_element_type=F32).astype(BF16)


def _inproj_kernel(*refs, use_rope):
    if use_rope:
        (h_ref, mod_ref, ng_ref, w1_ref, qg_ref, wq_ref, wqr_ref, kvg_ref, wk_ref, wvt_ref, cos_ref, sin_ref,
         ain_ref, sa_ref, bin_ref, sb_ref, q_ref, k_ref, vt_ref, ckv_ref, kr_ref, sc_ref) = refs
    else:
        (h_ref, mod_ref, ng_ref, w1_ref, qg_ref, wq_ref, kvg_ref, wk_ref, wvt_ref,
         ain_ref, sa_ref, bin_ref, sb_ref, q_ref, k_ref, vt_ref, ckv_ref, kr_ref, sc_ref) = refs

    xn = _modulated_norm(h_ref[...], ng_ref[...], mod_ref).astype(BF16)

    def proj(cols):
        return _dot(xn, w1_ref[:, cols[0]:cols[1]])

    a = proj(_W1_A)
    ain_ref[...] = a[:, :FN_WIDTH]
    sa_ref[...] = _silu(a[:, FN_WIDTH:]).astype(BF16)
    b = proj(_W1_B)
    bin_ref[...] = b[:, :POOL_WIDTH]
    sb_ref[...] = _silu(b[:, POOL_WIDTH:]).astype(BF16)
    sc_ref[...] = _silu(proj(_W1_CZ)).astype(BF16)

    qn = _rms(proj(_W1_Q), qg_ref[...]).astype(BF16)
    q = _dot(qn, wq_ref[...])
    if use_rope:
        cos = cos_ref[...]
        sin = sin_ref[...]
        qr = _dot(qn, wqr_ref[...])
        for h in range(N_HEADS):
            sl = slice(HEAD_SLAB * h, HEAD_SLAB * (h + 1))
            q_ref[:, sl] = ((q[:, sl] * cos + qr[:, sl] * sin) * Q_PRESCALE).astype(BF16)
    else:
        q_ref[...] = (q * Q_PRESCALE).astype(BF16)

    kv = proj(_W1_KV)
    ckv = _rms(kv[:, :KV_RANK], kvg_ref[...])
    ckv_ref[...] = ckv
    kr = kv[:, KV_RANK:KV_RANK + HEAD_SLAB]
    kr_ref[...] = kr
    if use_rope:
        kr = kr * cos + kv[:, KV_RANK + HEAD_SLAB:] * sin
    _key_value_heads(ckv.astype(BF16), kr, wk_ref, wvt_ref, k_ref, vt_ref)


def _layer_spec(arr, l):
    return pl.BlockSpec((None,) + arr.shape[1:], lambda *_: (l,) + (0,) * (arr.ndim - 1))


def _mod_spec(l, mod_row):
    return pl.BlockSpec((None, None, 3, D_MODEL), lambda i: (l, mod_row(i), 0, 0))


def _inproj(h, mod, mod_row, pw, l, rope, tm):
    t = h.shape[0]
    use_rope = rope is not None
    row = lambda i: (i, 0)
    names = ['norm_g', 'w1', 'q_norm_g', 'wq'] + (['wq_rot'] if use_rope else []) + ['kv_norm_g', 'wk', 'wvt']
    in_specs = [pl.BlockSpec((tm, D_MODEL), row), _mod_spec(l, mod_row)] + [_layer_spec(pw[n], l) for n in names]
    args = [h, mod] + [pw[n] for n in names]
    if use_rope:
        cos, sin, rope_tile = rope
        in_specs += [pl.BlockSpec((tm, HEAD_SLAB), lambda i: (rope_tile(i), 0))] * 2
        args += [cos, sin]
    wide = N_HEADS * HEAD_SLAB
    token_outs = lambda ws: ([pl.BlockSpec((tm, w), row) for w, _ in ws],
                             [jax.ShapeDtypeStruct((t, w), dt) for w, dt in ws])
    specs_a, shapes_a = token_outs([(FN_WIDTH, F32), (FN_WIDTH, BF16), (POOL_WIDTH, F32), (POOL_WIDTH, BF16),
                                    (wide, BF16)])
    specs_b, shapes_b = token_outs([(KV_RANK, F32), (HEAD_SLAB, F32), (ATT_WIDTH, BF16)])
    out_specs = specs_a + [pl.BlockSpec((tm, wide), row), pl.BlockSpec((ATT_WIDTH, tm), lambda i: (0, i))] + specs_b
    out_shape = shapes_a + [jax.ShapeDtypeStruct((t, wide), BF16), jax.ShapeDtypeStruct((ATT_WIDTH, t), BF16)] + shapes_b
    return pl.pallas_call(
        functools.partial(_inproj_kernel, use_rope=use_rope),
        grid=(t // tm,),
        in_specs=in_specs,
        out_specs=out_specs,
        out_shape=out_shape,
        compiler_params=_params(1),
        name="inproj_rope" if use_rope else "inproj",
    )(*args)


def _cache_kv_kernel(ckv_ref, kr_ref, wk_ref, wvt_ref, k_ref, vt_ref):
    _key_value_heads(ckv_ref[...].astype(BF16), kr_ref[...], wk_ref, wvt_ref, k_ref, vt_ref)


def _cache_kv(cache_ckv, cache_kr_slab, pw, l):
    batch, _, past, _ = cache_ckv.shape
    wide = N_HEADS * HEAD_SLAB
    cache_map = lambda b: (b, l, 0, 0)
    return pl.pallas_call(
        _cache_kv_kernel,
        grid=(batch,),
        in_specs=[pl.BlockSpec((None, None, past, KV_RANK), cache_map),
                  pl.BlockSpec((None, None, past, HEAD_SLAB), cache_map),
                  _layer_spec(pw['wk'], l), _layer_spec(pw['wvt'], l)],
        out_specs=[pl.BlockSpec((past, wide), lambda b: (b, 0)), pl.BlockSpec((ATT_WIDTH, past), lambda b: (0, b))],
        out_shape=[jax.ShapeDtypeStruct((batch * past, wide), BF16),
                   jax.ShapeDtypeStruct((ATT_WIDTH, batch * past), BF16)],
        compiler_params=_params(1),
        name="cache_kv",
    )(cache_ckv, cache_kr_slab, pw['wk'], pw['wvt'])


def _hi_lo(m):
    m = np.asarray(m, np.float32)
    hi = m.astype(BF16)
    return hi, (m - hi.astype(np.float32)).astype(BF16)


def _dft_cos_sin(n):
    k = np.arange(n)
    ang = 2.0 * np.pi * ((k[:, None] * k[None, :]) % n) / n
    return np.cos(ang), np.sin(ang)


def _channel_dft_tables():
    c, s = _dft_cos_sin(FN_GC)
    eye = np.eye(FN_WIDTH // FN_GC)
    return np.kron(eye, c), np.kron(eye, s)


def _fourier_direct_kernel(a_ref, sa_ref, cc_hi, cc_lo, ss_hi, ss_lo, cl_hi, cl_lo, sl_hi, sl_lo, o_ref, *, norm):
    a = a_ref[...]
    tc = _dot3_right(a, cc_hi[...], cc_lo[...])
    ts = _dot3_right(a, ss_hi[...], ss_lo[...])
    f = _dot3_left(cl_hi[...], cl_lo[...], tc) - _dot3_left(sl_hi[...], sl_lo[...], ts)
    o_ref[...] = ((f * norm) * sa_ref[...].astype(F32)).astype(BF16)


def _fourier_direct(a, sa, seq):
    t = a.shape[0]
    cc, ss = _channel_dft_tables()
    cl, sl = _dft_cos_sin(seq)
    tables = [x for m in (cc, ss, cl, sl) for x in _hi_lo(m)]
    row = lambda i: (i, 0)
    const = lambda i: (0, 0)
    tspecs = [pl.BlockSpec((FN_WIDTH, FN_WIDTH), const)] * 4 + [pl.BlockSpec((seq, seq), const)] * 4
    return pl.pallas_call(
        functools.partial(_fourier_direct_kernel, norm=float((seq * FN_GC) ** -0.5)),
        grid=(t // seq,),
        in_specs=[pl.BlockSpec((seq, FN_WIDTH), row), pl.BlockSpec((seq, FN_WIDTH), row)] + tspecs,
        out_specs=pl.BlockSpec((seq, FN_WIDTH), row),
        out_shape=jax.ShapeDtypeStruct((t, FN_WIDTH), BF16),
        compiler_params=_params(1),
        name="fourier_direct",
    )(a, sa, *tables)


FFT_R = 64


def _fourier_fft_kernel(a_ref, sa_ref, cc_hi, cc_lo, ss_hi, ss_lo, m1_hi, m1_lo, m2_hi, m2_lo, twc_ref, tws_ref,
                        o_ref, zr0, zr1, zi0, zi1, yr0, yr1, yi0, yi1, *, seq, norm):
    r = FFT_R
    half = FN_WIDTH // 2
    chunk = 512

    def put(refs, rows, x):
        refs[0][rows, :] = x[:, :half]
        refs[1][rows, :] = x[:, half:]

    def get(refs, rows):
        return jnp.concatenate([refs[0][rows, :], refs[1][rows, :]], axis=1)

    zr, zi, yr_s, yi_s = (zr0, zr1), (zi0, zi1), (yr0, yr1), (yi0, yi1)
    for c in range(seq // chunk):
        rows = slice(c * chunk, (c + 1) * chunk)
        a = a_ref[rows, :]
        put(zr, rows, _dot3_right(a, cc_hi[...], cc_lo[...]))
        put(zi, rows, -_dot3_right(a, ss_hi[...], ss_lo[...]))
    for n2 in range(r):
        strided = pl.ds(n2, r, stride=r)
        z = jnp.concatenate([get(zr, strided), get(zi, strided)], axis=0)
        y = _dot3_left(m1_hi[...], m1_lo[...], z)
        yr, yi = y[:r], y[r:]
        blk = slice(n2 * r, (n2 + 1) * r)
        cos = jnp.concatenate([twc_ref[blk, :]] * 2, axis=1)
        sin = jnp.concatenate([tws_ref[blk, :]] * 2, axis=1)
        put(yr_s, blk, yr * cos + yi * sin)
        put(yi_s, blk, yi * cos - yr * sin)
    for k1 in range(r):
        strided = pl.ds(k1, r, stride=r)
        y = jnp.concatenate([get(yr_s, strided), get(yi_s, strided)], axis=0)
        put(zr, strided, _dot3_left(m2_hi[...], m2_lo[...], y))
    for c in range(seq // chunk):
        rows = slice(c * chunk, (c + 1) * chunk)
        o_ref[rows, :] = ((get(zr, rows) * norm) * sa_ref[rows, :].astype(F32)).astype(BF16)


def _fourier_fft(a, sa, seq):
    assert seq == FFT_R * FFT_R
    t = a.shape[0]
    cc, ss = _channel_dft_tables()
    c, s = _dft_cos_sin(FFT_R)
    m1 = np.block([[c, s], [-s, c]])
    m2 = np.concatenate([c, s], axis=1)
    n2 = np.arange(FFT_R)[:, None]
    k1 = np.arange(FFT_R)[None, :]
    ang = (2.0 * np.pi * (n2 * k1) / seq).reshape(seq, 1)
    twc = np.ascontiguousarray(np.broadcast_to(np.cos(ang).astype(np.float32), (seq, 128)))
    tws = np.ascontiguousarray(np.broadcast_to(np.sin(ang).astype(np.float32), (seq, 128)))
    tables = [x for m in (cc, ss, m1, m2) for x in _hi_lo(m)]
    row = lambda i: (i, 0)
    const = lambda i: (0, 0)
    tspecs = ([pl.BlockSpec((FN_WIDTH, FN_WIDTH), const)] * 4 + [pl.BlockSpec((2 * FFT_R, 2 * FFT_R), const)] * 2
              + [pl.BlockSpec((FFT_R, 2 * FFT_R), const)] * 2 + [pl.BlockSpec((seq, 128), const)] * 2)
    return pl.pallas_call(
        functools.partial(_fourier_fft_kernel, seq=seq, norm=float((seq * FN_GC) ** -0.5)),
        grid=(t // seq,),
        in_specs=[pl.BlockSpec((seq, FN_WIDTH), row), pl.BlockSpec((seq, FN_WIDTH), row)] + tspecs,
        out_specs=pl.BlockSpec((seq, FN_WIDTH), row),
        out_shape=jax.ShapeDtypeStruct((t, FN_WIDTH), BF16),
        scratch_shapes=[pltpu.VMEM((seq, FN_WIDTH // 2), F32)] * 8,
        compiler_params=_params(1),
        name="fourier_fft",
    )(a, sa, *tables, twc, tws)


POOL_HALO = 8
POOL_CHUNK = 256


def _pool_kernel(b_ref, sb_ref, pw_ref, ps_ref, o_ref, pad_ref, *, seq):
    zeros = jnp.zeros((POOL_HALO, POOL_WIDTH), F32)
    pad_ref[0:POOL_HALO, :] = zeros
    pad_ref[POOL_HALO + seq:, :] = zeros
    pad_ref[POOL_HALO:POOL_HALO + seq, :] = b_ref[...]
    lane = lax.broadcasted_iota(jnp.int32, (POOL_CHUNK, 128), 1)
    low_group = lane < POOL_GC

    for c in range(seq // POOL_CHUNK):
        r0 = c * POOL_CHUNK
        t = lax.broadcasted_iota(jnp.int32, (POOL_CHUNK, 128), 0) + r0

        def inv_count(w):
            left = w // 2
            right = w - 1 - left
            lo = jnp.maximum(t - left, 0)
            hi = jnp.minimum(t + right, seq - 1)
            return 1.0 / (hi - lo + 1).astype(F32)

        def ld(off, col):
            start = POOL_HALO + r0 + off
            return pad_ref[start:start + POOL_CHUNK, 128 * col:128 * (col + 1)]

        u0 = ld(0, 0)
        p2 = ld(-1, 0) + u0
        p4 = p2 + ld(-2, 0) + ld(1, 0)
        pooled0 = jnp.where(low_group, p2 * inv_count(2), p4 * inv_count(4)) - u0
        u1 = ld(0, 1)
        p8 = u1
        for off in (-4, -3, -2, -1, 1, 2, 3):
            p8 = p8 + ld(off, 1)
        p16 = p8
        for off in (-8, -7, -6, -5, 4, 5, 6, 7):
            p16 = p16 + ld(off, 1)
        pooled1 = jnp.where(low_group, p8 * inv_count(8), p16 * inv_count(16)) - u1

        pooled = jnp.concatenate([pooled0, pooled1], axis=1).astype(BF16)
        mixed = _dot(pooled, pw_ref[...]) * ps_ref[...]
        rows = slice(r0, r0 + POOL_CHUNK)
        o_ref[rows, :] = (mixed * sb_ref[rows, :].astype(F32)).astype(BF16)


def _pool(b, sb, pw, l, seq):
    t = b.shape[0]
    row = lambda i: (i, 0)
    return pl.pallas_call(
        functools.partial(_pool_kernel, seq=seq),
        grid=(t // seq,),
        in_specs=[pl.BlockSpec((seq, POOL_WIDTH), row), pl.BlockSpec((seq, POOL_WIDTH), row),
                  _layer_spec(pw['pool_w'], l), _layer_spec(pw['pool_scale'], l)],
        out_specs=pl.BlockSpec((seq, POOL_WIDTH), row),
        out_shape=jax.ShapeDtypeStruct((t, POOL_WIDTH), BF16),
        scratch_shapes=[pltpu.VMEM((seq + 2 * POOL_HALO, POOL_WIDTH), F32)],
        compiler_params=_params(1),
        name="pool",
    )(b, sb, pw['pool_w'], pw['pool_scale'])


def _attn_kernel(*refs, heads, chunk, use_cache, lookahead):
    if use_cache:
        q_ref, k_ref, vt_ref, kc_ref, vct_ref, sc_ref, o_ref, s_ref, m_ref, l_ref, acc_ref = refs
        sources = ((k_ref, vt_ref), (kc_ref, vct_ref))
    else:
        q_ref, k_ref, vt_ref, sc_ref, o_ref, s_ref, m_ref, l_ref, acc_ref = refs
        sources = ((k_ref, vt_ref),)
    tq = q_ref.shape[0]
    groups = chunk // 8
    chunks = []
    for keys, values in sources:
        for off in range(0, keys.shape[0], chunk):
            chunks.append((keys, values, off, len(chunks) * chunk))

    slots = s_ref.shape[0]

    def scores(h, c):
        keys, _, off, row = chunks[c]
        sl = slice(HEAD_SLAB * h, HEAD_SLAB * (h + 1))
        s = lax.dot_general(keys[off:off + chunk, sl], q_ref[:, sl], _NT, preferred_element_type=F32)
        s_ref[h % slots, row:row + chunk, :] = s
        m_ref[h % slots] = jnp.maximum(m_ref[h % slots], jnp.max(s.reshape(groups, 8, tq), axis=0))

    def weigh(h, c, m):
        _, values, off, row = chunks[c]
        p = jnp.exp2(s_ref[h % slots, row:row + chunk, :] - m)
        l_ref[h % slots] += jnp.sum(p.reshape(groups, 8, tq), axis=0)
        acc_ref[h % slots] += _dot(values[V_DIM * h:V_DIM * (h + 1), off:off + chunk], p.astype(BF16))

    outs = []
    for t in range(heads + lookahead):
        h_w, h_s = t - lookahead, t
        if h_s < heads:
            m_ref[h_s % slots] = jnp.full((8, tq), -jnp.inf, F32)
        if h_w >= 0:
            m = jnp.max(m_ref[h_w % slots], axis=0, keepdims=True)
            l_ref[h_w % slots] = jnp.zeros((8, tq), F32)
            acc_ref[h_w % slots] = jnp.zeros((V_DIM, tq), F32)
        for c in range(len(chunks)):
            if h_w >= 0:
                weigh(h_w, c, m)
            if h_s < heads:
                scores(h_s, c)
        if h_w >= 0:
            denom = jnp.sum(l_ref[h_w % slots], axis=0, keepdims=True)
            outs.append(acc_ref[h_w % slots] * (1.0 / denom))
            if h_w % 2 == 1:
                o_pair = jnp.concatenate(outs, axis=0).T
                outs = []
                sl = slice(HEAD_SLAB * (h_w // 2), HEAD_SLAB * (h_w // 2 + 1))
                o_ref[:, sl] = (o_pair * sc_ref[:, sl].astype(F32)).astype(BF16)


def _attention(q, k, vt, sc, cache, batch, lq, lk, tq, heads_per_step, chunk, lookahead):
    use_cache = cache is not None
    nq = lq // tq
    n_hp = N_HEADS // heads_per_step
    qw = heads_per_step * HEAD_SLAB
    ow = heads_per_step * V_DIM
    q_map = lambda b, g, i: (b * nq + i, g)
    k_map = lambda b, g, i: (b, g)
    vt_map = lambda b, g, i: (g, b)
    in_specs = [pl.BlockSpec((tq, qw), q_map), pl.BlockSpec((lk, qw), k_map), pl.BlockSpec((ow, lk), vt_map)]
    args = [q, k, vt]
    lc = 0
    if use_cache:
        lc = cache[0].shape[0] // batch
        in_specs += [pl.BlockSpec((lc, qw), k_map), pl.BlockSpec((ow, lc), vt_map)]
        args += list(cache)
    in_specs.append(pl.BlockSpec((tq, ow), q_map))
    args.append(sc)
    slots = min(heads_per_step, lookahead + 1)
    return pl.pallas_call(
        functools.partial(_attn_kernel, heads=heads_per_step, chunk=chunk, use_cache=use_cache, lookahead=lookahead),
        grid=(batch, n_hp, nq),
        in_specs=in_specs,
        out_specs=pl.BlockSpec((tq, ow), q_map),
        out_shape=jax.ShapeDtypeStruct((batch * lq, ATT_WIDTH), BF16),
        scratch_shapes=[pltpu.VMEM((slots, lk + lc, tq), F32), pltpu.VMEM((slots, 8, tq), F32),
                        pltpu.VMEM((slots, 8, tq), F32), pltpu.VMEM((slots, V_DIM, tq), F32)],
        compiler_params=_params(3),
        name="attention_cache" if use_cache else "attention",
    )(*args)


def _out_kernel(*refs, final):
    if final:
        h_ref, mod_ref, ng_ref, xa_ref, xb_ref, xc_ref, wa_ref, wb_ref, wc_ref, wg_ref, wo_ref, fg_ref, o_ref = refs
    else:
        h_ref, mod_ref, ng_ref, xa_ref, xb_ref, xc_ref, wa_ref, wb_ref, wc_ref, wg_ref, wo_ref, o_ref = refs
    h = h_ref[...]
    xn = _modulated_norm(h, ng_ref[...], mod_ref).astype(BF16)
    y = None
    for i, (x_ref, w_ref) in enumerate(((xa_ref, wa_ref), (xb_ref, wb_ref), (xc_ref, wc_ref))):
        g = jax.nn.sigmoid(_dot(xn, wg_ref[:, i * D_MODEL:(i + 1) * D_MODEL]))
        term = g * _dot(x_ref[...], w_ref[...])
        y = term if y is None else y + term
    h_new = h + mod_ref[2:3, :] * _dot(y.astype(BF16), wo_ref[...])
    if final:
        o_ref[...] = _rms(h_new, fg_ref[...])
    else:
        o_ref[...] = h_new


def _out(h, mod, mod_row, xa, xb, xc, pw, l, final_g, tm):
    t = h.shape[0]
    final = final_g is not None
    row = lambda i: (i, 0)
    names = ['w_br_a', 'w_br_b', 'w_br_c', 'wg', 'w_out']
    in_specs = ([pl.BlockSpec((tm, D_MODEL), row), _mod_spec(l, mod_row), _layer_spec(pw['norm_g'], l),
                 pl.BlockSpec((tm, FN_WIDTH), row), pl.BlockSpec((tm, POOL_WIDTH), row),
                 pl.BlockSpec((tm, ATT_WIDTH), row)] + [_layer_spec(pw[n], l) for n in names])
    args = [h, mod, pw['norm_g'], xa, xb, xc] + [pw[n] for n in names]
    if final:
        in_specs.append(pl.BlockSpec((1, D_MODEL), lambda i: (0, 0)))
        args.append(final_g)
    return pl.pallas_call(
        functools.partial(_out_kernel, final=final),
        grid=(t // tm,),
        in_specs=in_specs,
        out_specs=pl.BlockSpec((tm, D_MODEL), row),
        out_shape=jax.ShapeDtypeStruct((t, D_MODEL), F32),
        compiler_params=_params(1),
        name="out_final" if final else "out",
    )(*args)


def _rot_cols(w):
    q = QK_ROPE // 4
    return jnp.concatenate([-w[..., q:2 * q], w[..., :q], -w[..., 3 * q:], w[..., 2 * q:3 * q]], axis=-1)


_ROPE_PAD = ((QK_NOPE, HEAD_SLAB - QK_NOPE - QK_ROPE),)


def _pack_weights(norm_g, w_in, pool_w, pool_scale, q_norm_g, w_q_up, kv_norm_g, w_kv_up, w_br_a, w_br_b, w_br_c,
                  w_out):
    lead = ((0, 0), (0, 0))
    w_kr = w_in[:, :, _OFF_KR:_OFF_CZ]
    w1 = jnp.concatenate([w_in[:, :, :_OFF_KV], w_in[:, :, _OFF_CZ:_OFF_G], w_in[:, :, _OFF_KV:_OFF_KR],
                          jnp.pad(w_kr, lead + _ROPE_PAD), jnp.pad(_rot_cols(w_kr), lead + _ROPE_PAD)],
                         axis=2).astype(BF16)
    wide = N_HEADS * HEAD_SLAB
    wq_h = w_q_up.reshape(DEPTH, Q_RANK, N_HEADS, QK_NOPE + QK_ROPE)
    wq = jnp.pad(wq_h, lead + ((0, 0), (0, _ROPE_PAD[0][1]))).reshape(DEPTH, Q_RANK, wide).astype(BF16)
    wq_rot = jnp.pad(_rot_cols(wq_h[..., QK_NOPE:]), lead + ((0, 0),) + _ROPE_PAD)
    wq_rot = wq_rot.reshape(DEPTH, Q_RANK, wide).astype(BF16)
    wkv_h = w_kv_up.reshape(DEPTH, KV_RANK, N_HEADS, QK_NOPE + V_DIM)
    wk = jnp.pad(wkv_h[..., :QK_NOPE], lead + ((0, 0), (0, HEAD_SLAB - QK_NOPE))).reshape(DEPTH, KV_RANK, wide)
    wvt = wkv_h[..., QK_NOPE:].reshape(DEPTH, KV_RANK, ATT_WIDTH).transpose(0, 2, 1)
    groups = len(POOL_WINDOWS)
    eye = jnp.eye(groups, dtype=F32)
    pool_bd = (pool_w[:, :, :, None, :] * eye[None, :, None, :, None]).reshape(DEPTH, POOL_WIDTH, POOL_WIDTH)
    return {
        'norm_g': norm_g.reshape(DEPTH, 1, D_MODEL), 'w1': w1, 'wg': w_in[:, :, _OFF_G:].astype(BF16),
        'q_norm_g': q_norm_g.reshape(DEPTH, 1, Q_RANK), 'wq': wq, 'wq_rot': wq_rot,
        'kv_norm_g': kv_norm_g.reshape(DEPTH, 1, KV_RANK), 'wk': wk.astype(BF16), 'wvt': wvt.astype(BF16),
        'pool_w': pool_bd.astype(BF16), 'pool_scale': pool_scale.reshape(DEPTH, 1, POOL_WIDTH),
        'w_br_a': w_br_a.astype(BF16), 'w_br_b': w_br_b.astype(BF16), 'w_br_c': w_br_c.astype(BF16),
        'w_out': w_out.astype(BF16),
    }


def _rope_tables(seq):
    f32 = np.float32
    t = np.arange(seq)
    row = (t // GRID_W).astype(f32)
    col = (t % GRID_W).astype(f32)
    half = QK_ROPE // 2
    freqs = f32(ROPE_THETA) ** (-np.arange(0, half, 2, dtype=f32) / f32(half))
    ar = row[:, None] * freqs
    ac = col[:, None] * freqs
    cos = np.ones((seq, HEAD_SLAB), f32)
    sin = np.zeros((seq, HEAD_SLAB), f32)
    cos[:, QK_NOPE:QK_NOPE + QK_ROPE] = np.concatenate([np.cos(ar), np.cos(ar), np.cos(ac), np.cos(ac)], axis=-1)
    sin[:, QK_NOPE:QK_NOPE + QK_ROPE] = np.concatenate([np.sin(ar), np.sin(ar), np.sin(ac), np.sin(ac)], axis=-1)
    return cos, sin


TOKEN_TILE = 512
SAMPLE_Q_TILE = 512
SAMPLE_HEADS_PER_STEP = 4
SAMPLE_KEY_CHUNK = 512


def kernel(x_prompt, x_sample, cache_ckv, cache_krope, c, c_ctx, norm_g, w_mod, b_mod, w_in, pool_w, pool_scale,
           q_norm_g, w_q_up, kv_norm_g, w_kv_up, w_br_a, w_br_b, w_br_c, w_out, final_norm_g):
    batch, seq, _ = x_prompt.shape
    dec_batch, dec_seq, _ = x_sample.shape
    past = cache_ckv.shape[2]
    tm = TOKEN_TILE

    mod_rows = 8
    cvec = jnp.concatenate([c_ctx[None, :], c, jnp.zeros((mod_rows - 1 - dec_batch, D_MODEL), F32)], axis=0)
    mod = _modulation(cvec, w_mod, b_mod).reshape(DEPTH, mod_rows, 3, D_MODEL)
    prompt_row = lambda i: 0
    tiles_per_sample = dec_seq // tm
    assert past % SAMPLE_KEY_CHUNK == 0 and dec_seq % SAMPLE_KEY_CHUNK == 0
    sample_row = lambda i: 1 + i // tiles_per_sample

    cos, sin = _rope_tables(dec_seq)
    rope = (cos, sin, lambda i: i % tiles_per_sample)
    final_g = final_norm_g.reshape(1, D_MODEL)
    pw = _pack_weights(norm_g, w_in, pool_w, pool_scale, q_norm_g, w_q_up, kv_norm_g, w_kv_up, w_br_a, w_br_b,
                       w_br_c, w_out)
    cache_kr_slab = jnp.pad(cache_krope, ((0, 0), (0, 0), (0, 0)) + _ROPE_PAD)

    hp = x_prompt.reshape(batch * seq, D_MODEL)
    hs = x_sample.reshape(dec_batch * dec_seq, D_MODEL)
    ckv_list, kr_list = [], []
    for l in range(DEPTH):
        last = final_g if l == DEPTH - 1 else None

        a_in, sa, b_in, sb, q, k, vt, ckv, kr, sc = _inproj(hp, mod, prompt_row, pw, l, None, tm)
        ckv_list.append(ckv.reshape(batch, seq, KV_RANK))
        kr_list.append(kr.reshape(batch, seq, HEAD_SLAB))
        xa = _fourier_direct(a_in, sa, seq)
        xb = _pool(b_in, sb, pw, l, seq)
        xc = _attention(q, k, vt, sc, None, batch, seq, seq, seq, N_HEADS, seq, lookahead=N_HEADS)
        hp = _out(hp, mod, prompt_row, xa, xb, xc, pw, l, last, tm)

        a_in, sa, b_in, sb, q, k, vt, _, _, sc = _inproj(hs, mod, sample_row, pw, l, rope, tm)
        cache = _cache_kv(cache_ckv, cache_kr_slab, pw, l)
        xa = _fourier_fft(a_in, sa, dec_seq)
        xb = _pool(b_in, sb, pw, l, dec_seq)
        xc = _attention(q, k, vt, sc, cache, dec_batch, dec_seq, dec_seq, SAMPLE_Q_TILE, SAMPLE_HEADS_PER_STEP,
                        SAMPLE_KEY_CHUNK, lookahead=1)
        hs = _out(hs, mod, sample_row, xa, xb, xc, pw, l, last, tm)

    y_prompt = hp.reshape(batch, seq, D_MODEL)
    y_sample = hs.reshape(dec_batch, dec_seq, D_MODEL)
    new_krope = jnp.stack(kr_list, axis=1)[..., QK_NOPE:QK_NOPE + QK_ROPE]
    return (y_prompt, y_sample, jnp.stack(ckv_list, axis=1), new_krope)
```

```python
import functools

import numpy as np
import jax
import jax.numpy as jnp
from jax import lax
from jax.experimental import pallas as pl
from jax.experimental.pallas import tpu as pltpu

D_MODEL = 1024
DEPTH = 2
GRID_W = 64
EPS = 1e-6
FN_WIDTH = 256
FN_GC = 64
POOL_WINDOWS = (2, 4, 8, 16)
POOL_WIDTH = 256
POOL_GC = 64
N_HEADS = 8
QK_NOPE = 64
QK_ROPE = 32
V_DIM = 64
Q_RANK = 256
KV_RANK = 128
ATT_WIDTH = 512
ROPE_THETA = 10000.0
HEAD_SLAB = 128
QK_SCALE = (QK_NOPE + QK_ROPE) ** -0.5
Q_PRESCALE = QK_SCALE * float(np.log2(np.e))

VMEM_LIMIT_BYTES = 56 * 1024 * 1024

F32 = jnp.float32
BF16 = jnp.bfloat16

_OFF_A, _OFF_B, _OFF_Q, _OFF_KV, _OFF_KR, _OFF_CZ, _OFF_G = 0, 512, 1024, 1280, 1408, 1440, 1952
_W1_A = (0, 512)
_W1_B = (512, 1024)
_W1_Q = (1024, 1280)
_W1_CZ = (1280, 1792)
_W1_KV = (1792, 2176)
W1_WIDTH = 2176


def _params(n_parallel):
    return pltpu.CompilerParams(dimension_semantics=("arbitrary",) * n_parallel,
                                vmem_limit_bytes=VMEM_LIMIT_BYTES)


def _dot(a, b):
    return jnp.dot(a, b, preferred_element_type=F32)


def _silu(x):
    return x * jax.nn.sigmoid(x)


def _rms(x, g):
    r = lax.rsqrt(jnp.mean(x * x, axis=-1, keepdims=True) + EPS)
    return (x * r) * g


def _modulated_norm(h, norm_g, mod_ref):
    shift = mod_ref[0:1, :]
    scale = mod_ref[1:2, :]
    return _rms(h, norm_g) * (1.0 + scale) + shift


def _split_bf16(x):
    hi = x.astype(BF16)
    lo = (x - hi.astype(F32)).astype(BF16)
    return hi, lo


def _dot3_right(x, m_hi, m_lo):
    x_hi, x_lo = _split_bf16(x)
    return _dot(x_hi, m_hi) + _dot(x_lo, m_hi) + _dot(x_hi, m_lo)


def _dot3_left(m_hi, m_lo, x):
    x_hi, x_lo = _split_bf16(x)
    return _dot(m_hi, x_hi) + _dot(m_hi, x_lo) + _dot(m_lo, x_hi)


def _mod_kernel(c_ref, w_ref, b_ref, o_ref):
    s = _silu(c_ref[...]).astype(BF16)
    o_ref[...] = _dot(s, w_ref[...].astype(BF16)) + b_ref[...]


def _modulation(cvec, w_mod, b_mod):
    rows = cvec.shape[0]
    tn = 768
    return pl.pallas_call(
        _mod_kernel,
        grid=(DEPTH, 3 * D_MODEL // tn),
        in_specs=[pl.BlockSpec((rows, D_MODEL), lambda l, j: (0, 0)),
                  pl.BlockSpec((None, D_MODEL, tn), lambda l, j: (l, 0, j)),
                  pl.BlockSpec((None, 1, tn), lambda l, j: (l, 0, j))],
        out_specs=pl.BlockSpec((None, rows, tn), lambda l, j: (l, 0, j)),
        out_shape=jax.ShapeDtypeStruct((DEPTH, rows, 3 * D_MODEL), F32),
        compiler_params=_params(2),
        name="modulation",
    )(cvec, w_mod, b_mod.reshape(DEPTH, 1, 3 * D_MODEL))


_NT = (((1,), (1,)), ((), ()))


def _key_value_heads(ckv_bf16, kr_slab, wk_ref, wvt_ref, k_ref, vt_ref):
    kn = _dot(ckv_bf16, wk_ref[...])
    for h in range(N_HEADS):
        sl = slice(HEAD_SLAB * h, HEAD_SLAB * (h + 1))
        k_ref[:, sl] = (kn[:, sl] + kr_slab).astype(BF16)
    vt_ref[...] = lax.dot_general(wvt_ref[...], ckv_bf16, _NT, preferred_element_type=F32).astype(BF16)


def _inproj_kernel(*refs, use_rope):
    if use_rope:
        (h_ref, mod_ref, ng_ref, w1_ref, qg_ref, wq_ref, wqr_ref, kvg_ref, wk_ref, wvt_ref, cos_ref, sin_ref,
         ain_ref, sa_ref, bin_ref, sb_ref, q_ref, k_ref, vt_ref, ckv_ref, kr_ref, sc_ref) = refs
    else:
        (h_ref, mod_ref, ng_ref, w1_ref, qg_ref, wq_ref, kvg_ref, wk_ref, wvt_ref,
         ain_ref, sa_ref, bin_ref, sb_ref, q_ref, k_ref, vt_ref, ckv_ref, kr_ref, sc_ref) = refs

    xn = _modulated_norm(h_ref[...], ng_ref[...], mod_ref).astype(BF16)

    def proj(cols):
        return _dot(xn, w1_ref[:, cols[0]:cols[1]])

    a = proj(_W1_A)
    ain_ref[...] = a[:, :FN_WIDTH]
    sa_ref[...] = _silu(a[:, FN_WIDTH:]).astype(BF16)
    b = proj(_W1_B)
    bin_ref[...] = b[:, :POOL_WIDTH]
    sb_ref[...] = _silu(b[:, POOL_WIDTH:]).astype(BF16)
    sc_ref[...] = _silu(proj(_W1_CZ)).astype(BF16)

    qn = _rms(proj(_W1_Q), qg_ref[...]).astype(BF16)
    q = _dot(qn, wq_ref[...])
    if use_rope:
        cos = cos_ref[...]
        sin = sin_ref[...]
        qr = _dot(qn, wqr_ref[...])
        for h in range(N_HEADS):
            sl = slice(HEAD_SLAB * h, HEAD_SLAB * (h + 1))
            q_ref[:, sl] = ((q[:, sl] * cos + qr[:, sl] * sin) * Q_PRESCALE).astype(BF16)
    else:
        q_ref[...] = (q * Q_PRESCALE).astype(BF16)

    kv = proj(_W1_KV)
    ckv = _rms(kv[:, :KV_RANK], kvg_ref[...])
    ckv_ref[...] = ckv
    kr = kv[:, KV_RANK:KV_RANK + HEAD_SLAB]
    kr_ref[...] = kr
    if use_rope:
        kr = kr * cos + kv[:, KV_RANK + HEAD_SLAB:] * sin
    _key_value_heads(ckv.astype(BF16), kr, wk_ref, wvt_ref, k_ref, vt_ref)


def _layer_spec(arr, l):
    return pl.BlockSpec((None,) + arr.shape[1:], lambda *_: (l,) + (0,) * (arr.ndim - 1))


def _mod_spec(l, mod_row):
    return pl.BlockSpec((None, None, 3, D_MODEL), lambda i: (l, mod_row(i), 0, 0))


def _inproj(h, mod, mod_row, pw, l, rope, tm):
    t = h.shape[0]
    use_rope = rope is not None
    row = lambda i: (i, 0)
    names = ['norm_g', 'w1', 'q_norm_g', 'wq'] + (['wq_rot'] if use_rope else []) + ['kv_norm_g', 'wk', 'wvt']
    in_specs = [pl.BlockSpec((tm, D_MODEL), row), _mod_spec(l, mod_row)] + [_layer_spec(pw[n], l) for n in names]
    args = [h, mod] + [pw[n] for n in names]
    if use_rope:
        cos, sin, rope_tile = rope
        in_specs += [pl.BlockSpec((tm, HEAD_SLAB), lambda i: (rope_tile(i), 0))] * 2
        args += [cos, sin]
    wide = N_HEADS * HEAD_SLAB
    token_outs = lambda ws: ([pl.BlockSpec((tm, w), row) for w, _ in ws],
                             [jax.ShapeDtypeStruct((t, w), dt) for w, dt in ws])
    specs_a, shapes_a = token_outs([(FN_WIDTH, F32), (FN_WIDTH, BF16), (POOL_WIDTH, F32), (POOL_WIDTH, BF16),
                                    (wide, BF16)])
    specs_b, shapes_b = token_outs([(KV_RANK, F32), (HEAD_SLAB, F32), (ATT_WIDTH, BF16)])
    out_specs = specs_a + [pl.BlockSpec((tm, wide), row), pl.BlockSpec((ATT_WIDTH, tm), lambda i: (0, i))] + specs_b
    out_shape = shapes_a + [jax.ShapeDtypeStruct((t, wide), BF16), jax.ShapeDtypeStruct((ATT_WIDTH, t), BF16)] + shapes_b
    return pl.pallas_call(
        functools.partial(_inproj_kernel, use_rope=use_rope),
        grid=(t // tm,),
        in_specs=in_specs,
        out_specs=out_specs,
        out_shape=out_shape,
        compiler_params=_params(1),
        name="inproj_rope" if use_rope else "inproj",
    )(*args)


def _cache_kv_kernel(ckv_ref, kr_ref, wk_ref, wvt_ref, k_ref, vt_ref):
    _key_value_heads(ckv_ref[...].astype(BF16), kr_ref[...], wk_ref, wvt_ref, k_ref, vt_ref)


def _cache_kv(cache_ckv, cache_kr_slab, pw, l):
    batch, _, past, _ = cache_ckv.shape
    wide = N_HEADS * HEAD_SLAB
    cache_map = lambda b: (b, l, 0, 0)
    return pl.pallas_call(
        _cache_kv_kernel,
        grid=(batch,),
        in_specs=[pl.BlockSpec((None, None, past, KV_RANK), cache_map),
                  pl.BlockSpec((None, None, past, HEAD_SLAB), cache_map),
                  _layer_spec(pw['wk'], l), _layer_spec(pw['wvt'], l)],
        out_specs=[pl.BlockSpec((past, wide), lambda b: (b, 0)), pl.BlockSpec((ATT_WIDTH, past), lambda b: (0, b))],
        out_shape=[jax.ShapeDtypeStruct((batch * past, wide), BF16),
                   jax.ShapeDtypeStruct((ATT_WIDTH, batch * past), BF16)],
        compiler_params=_params(1),
        name="cache_kv",
    )(cache_ckv, cache_kr_slab, pw['wk'], pw['wvt'])


def _hi_lo(m):
    m = np.asarray(m, np.float32)
    hi = m.astype(BF16)
    return hi, (m - hi.astype(np.float32)).astype(BF16)


def _dft_cos_sin(n):
    k = np.arange(n)
    ang = 2.0 * np.pi * ((k[:, None] * k[None, :]) % n) / n
    return np.cos(ang), np.sin(ang)


def _channel_dft_tables():
    c, s = _dft_cos_sin(FN_GC)
    eye = np.eye(FN_WIDTH // FN_GC)
    return np.kron(eye, c), np.kron(eye, s)


def _fourier_direct_kernel(a_ref, sa_ref, cc_hi, cc_lo, ss_hi, ss_lo, m_hi, m_lo, o_ref, *, seq, norm):
    n = a_ref.shape[0] // seq
    a = a_ref[...]
    tc = _dot3_right(a, cc_hi[...], cc_lo[...])
    ts = _dot3_right(a, ss_hi[...], ss_lo[...])
    side_by_side = lambda x: jnp.concatenate([x[i * seq:(i + 1) * seq] for i in range(n)], axis=1)
    f = _dot3_left(m_hi[...], m_lo[...], jnp.concatenate([side_by_side(tc), side_by_side(ts)], axis=0))
    for i in range(n):
        rows = slice(i * seq, (i + 1) * seq)
        f_i = f[:, i * FN_WIDTH:(i + 1) * FN_WIDTH]
        o_ref[rows, :] = ((f_i * norm) * sa_ref[rows, :].astype(F32)).astype(BF16)


def _fourier_direct(a, sa, seq, seqs_per_step):
    t = a.shape[0]
    rows = seq * seqs_per_step
    cc, ss = _channel_dft_tables()
    cl, sl = _dft_cos_sin(seq)
    tables = [x for m in (cc, ss, np.concatenate([cl, -sl], axis=1)) for x in _hi_lo(m)]
    row = lambda i: (i, 0)
    const = lambda i: (0, 0)
    tspecs = [pl.BlockSpec((FN_WIDTH, FN_WIDTH), const)] * 4 + [pl.BlockSpec((seq, 2 * seq), const)] * 2
    return pl.pallas_call(
        functools.partial(_fourier_direct_kernel, seq=seq, norm=float((seq * FN_GC) ** -0.5)),
        grid=(t // rows,),
        in_specs=[pl.BlockSpec((rows, FN_WIDTH), row), pl.BlockSpec((rows, FN_WIDTH), row)] + tspecs,
        out_specs=pl.BlockSpec((rows, FN_WIDTH), row),
        out_shape=jax.ShapeDtypeStruct((t, FN_WIDTH), BF16),
        compiler_params=_params(1),
        name="fourier_direct",
    )(a, sa, *tables)


FFT_R = 64


def _fourier_fft_kernel(a_ref, sa_ref, cc_hi, cc_lo, ss_hi, ss_lo, m1_hi, m1_lo, m2_hi, m2_lo, twc_ref, tws_ref,
                        o_ref, zr0, zr1, zi0, zi1, yr0, yr1, yi0, yi1, *, seq, norm):
    r = FFT_R
    half = FN_WIDTH // 2
    chunk = 512

    def put(refs, rows, x):
        refs[0][rows, :] = x[:, :half]
        refs[1][rows, :] = x[:, half:]

    def get(refs, rows):
        return jnp.concatenate([refs[0][rows, :], refs[1][rows, :]], axis=1)

    zr, zi, yr_s, yi_s = (zr0, zr1), (zi0, zi1), (yr0, yr1), (yi0, yi1)
    for c in range(seq // chunk):
        rows = slice(c * chunk, (c + 1) * chunk)
        a = a_ref[rows, :]
        put(zr, rows, _dot3_right(a, cc_hi[...], cc_lo[...]))
        put(zi, rows, -_dot3_right(a, ss_hi[...], ss_lo[...]))
    for n2 in range(r):
        strided = pl.ds(n2, r, stride=r)
        z = jnp.concatenate([get(zr, strided), get(zi, strided)], axis=0)
        y = _dot3_left(m1_hi[...], m1_lo[...], z)
        yr, yi = y[:r], y[r:]
        blk = slice(n2 * r, (n2 + 1) * r)
        cos = jnp.concatenate([twc_ref[blk, :]] * 2, axis=1)
        sin = jnp.concatenate([tws_ref[blk, :]] * 2, axis=1)
        put(yr_s, blk, yr * cos + yi * sin)
        put(yi_s, blk, yi * cos - yr * sin)
    for k1 in range(r):
        strided = pl.ds(k1, r, stride=r)
        y = jnp.concatenate([get(yr_s, strided), get(yi_s, strided)], axis=0)
        put(zr, strided, _dot3_left(m2_hi[...], m2_lo[...], y))
    for c in range(seq // chunk):
        rows = slice(c * chunk, (c + 1) * chunk)
        o_ref[rows, :] = ((get(zr, rows) * norm) * sa_ref[rows, :].astype(F32)).astype(BF16)


def _fourier_fft(a, sa, seq):
    assert seq == FFT_R * FFT_R
    t = a.shape[0]
    cc, ss = _channel_dft_tables()
    c, s = _dft_cos_sin(FFT_R)
    m1 = np.block([[c, s], [-s, c]])
    m2 = np.concatenate([c, s], axis=1)
    n2 = np.arange(FFT_R)[:, None]
    k1 = np.arange(FFT_R)[None, :]
    ang = (2.0 * np.pi * (n2 * k1) / seq).reshape(seq, 1)
    twc = np.ascontiguousarray(np.broadcast_to(np.cos(ang).astype(np.float32), (seq, 128)))
    tws = np.ascontiguousarray(np.broadcast_to(np.sin(ang).astype(np.float32), (seq, 128)))
    tables = [x for m in (cc, ss, m1, m2) for x in _hi_lo(m)]
    row = lambda i: (i, 0)
    const = lambda i: (0, 0)
    tspecs = ([pl.BlockSpec((FN_WIDTH, FN_WIDTH), const)] * 4 + [pl.BlockSpec((2 * FFT_R, 2 * FFT_R), const)] * 2
              + [pl.BlockSpec((FFT_R, 2 * FFT_R), const)] * 2 + [pl.BlockSpec((seq, 128), const)] * 2)
    return pl.pallas_call(
        functools.partial(_fourier_fft_kernel, seq=seq, norm=float((seq * FN_GC) ** -0.5)),
        grid=(t // seq,),
        in_specs=[pl.BlockSpec((seq, FN_WIDTH), row), pl.BlockSpec((seq, FN_WIDTH), row)] + tspecs,
        out_specs=pl.BlockSpec((seq, FN_WIDTH), row),
        out_shape=jax.ShapeDtypeStruct((t, FN_WIDTH), BF16),
        scratch_shapes=[pltpu.VMEM((seq, FN_WIDTH // 2), F32)] * 8,
        compiler_params=_params(1),
        name="fourier_fft",
    )(a, sa, *tables, twc, tws)


POOL_HALO = 8
POOL_CHUNK = 256


def _pool_kernel(b_ref, sb_ref, pw_ref, ps_ref, o_ref, pad_ref, *, seq):
    zeros = jnp.zeros((POOL_HALO, POOL_WIDTH), F32)
    lane = lax.broadcasted_iota(jnp.int32, (POOL_CHUNK, 128), 1)
    low_group = lane < POOL_GC
    for i in range(pad_ref.shape[0]):
        pad_ref[i, 0:POOL_HALO, :] = zeros
        pad_ref[i, POOL_HALO + seq:, :] = zeros
        pad_ref[i, POOL_HALO:POOL_HALO + seq, :] = b_ref[i * seq:(i + 1) * seq, :]

    for i, c in [(i, c) for i in range(pad_ref.shape[0]) for c in range(seq // POOL_CHUNK)]:
        r0 = c * POOL_CHUNK
        t = lax.broadcasted_iota(jnp.int32, (POOL_CHUNK, 128), 0) + r0

        def inv_count(w):
            left = w // 2
            right = w - 1 - left
            lo = jnp.maximum(t - left, 0)
            hi = jnp.minimum(t + right, seq - 1)
            return 1.0 / (hi - lo + 1).astype(F32)

        def ld(off, col):
            start = POOL_HALO + r0 + off
            return pad_ref[i, start:start + POOL_CHUNK, 128 * col:128 * (col + 1)]

        u0 = ld(0, 0)
        p2 = ld(-1, 0) + u0
        p4 = p2 + ld(-2, 0) + ld(1, 0)
        pooled0 = jnp.where(low_group, p2 * inv_count(2), p4 * inv_count(4)) - u0
        u1 = ld(0, 1)
        p8 = u1
        for off in (-4, -3, -2, -1, 1, 2, 3):
            p8 = p8 + ld(off, 1)
        p16 = p8
        for off in (-8, -7, -6, -5, 4, 5, 6, 7):
            p16 = p16 + ld(off, 1)
        pooled1 = jnp.where(low_group, p8 * inv_count(8), p16 * inv_count(16)) - u1

        pooled = jnp.concatenate([pooled0, pooled1], axis=1).astype(BF16)
        mixed = _dot(pooled, pw_ref[...]) * ps_ref[...]
        rows = slice(i * seq + r0, i * seq + r0 + POOL_CHUNK)
        o_ref[rows, :] = (mixed * sb_ref[rows, :].astype(F32)).astype(BF16)


def _pool(b, sb, pw, l, seq, seqs_per_step):
    t = b.shape[0]
    rows = seq * seqs_per_step
    row = lambda i: (i, 0)
    return pl.pallas_call(
        functools.partial(_pool_kernel, seq=seq),
        grid=(t // rows,),
        in_specs=[pl.BlockSpec((rows, POOL_WIDTH), row), pl.BlockSpec((rows, POOL_WIDTH), row),
                  _layer_spec(pw['pool_w'], l), _layer_spec(pw['pool_scale'], l)],
        out_specs=pl.BlockSpec((rows, POOL_WIDTH), row),
        out_shape=jax.ShapeDtypeStruct((t, POOL_WIDTH), BF16),
        scratch_shapes=[pltpu.VMEM((seqs_per_step, seq + 2 * POOL_HALO, POOL_WIDTH), F32)],
        compiler_params=_params(1),
        name="pool",
    )(b, sb, pw['pool_w'], pw['pool_scale'])


def _attn_kernel(*refs, heads, chunk, use_cache, lookahead):
    if use_cache:
        q_ref, k_ref, vt_ref, kc_ref, vct_ref, sc_ref, o_ref, s_ref, m_ref, l_ref, acc_ref = refs
        sources = ((k_ref, vt_ref), (kc_ref, vct_ref))
    else:
        q_ref, k_ref, vt_ref, sc_ref, o_ref, s_ref, m_ref, l_ref, acc_ref = refs
        sources = ((k_ref, vt_ref),)
    tq = q_ref.shape[0]
    groups = chunk // 8
    chunks = []
    for keys, values in sources:
        for off in range(0, keys.shape[0], chunk):
            chunks.append((keys, values, off, len(chunks) * chunk))

    slots = s_ref.shape[0]

    def scores(h, c):
        keys, _, off, row = chunks[c]
        sl = slice(HEAD_SLAB * h, HEAD_SLAB * (h + 1))
        s = lax.dot_general(keys[off:off + chunk, sl], q_ref[:, sl], _NT, preferred_element_type=F32)
        s_ref[h % slots, row:row + chunk, :] = s
        m_ref[h % slots] = jnp.maximum(m_ref[h % slots], jnp.max(s.reshape(groups, 8, tq), axis=0))

    def weigh(h, c, m):
        _, values, off, row = chunks[c]
        p = jnp.exp2(s_ref[h % slots, row:row + chunk, :] - m)
        l_ref[h % slots] += jnp.sum(p.reshape(groups, 8, tq), axis=0)
        acc_ref[h % slots] += _dot(values[V_DIM * h:V_DIM * (h + 1), off:off + chunk], p.astype(BF16))

    outs = []
    for t in range(heads + lookahead):
        h_w, h_s = t - lookahead, t
        if h_s < heads:
            m_ref[h_s % slots] = jnp.full((8, tq), -jnp.inf, F32)
        if h_w >= 0:
            m = jnp.max(m_ref[h_w % slots], axis=0, keepdims=True)
            l_ref[h_w % slots] = jnp.zeros((8, tq), F32)
            acc_ref[h_w % slots] = jnp.zeros((V_DIM, tq), F32)
        for c in range(len(chunks)):
            if h_w >= 0:
                weigh(h_w, c, m)
            if h_s < heads:
                scores(h_s, c)
        if h_w >= 0:
            denom = jnp.sum(l_ref[h_w % slots], axis=0, keepdims=True)
            outs.append(acc_ref[h_w % slots] * (1.0 / denom))
            if h_w % 2 == 1:
                o_pair = jnp.concatenate(outs, axis=0).T
                outs = []
                sl = slice(HEAD_SLAB * (h_w // 2), HEAD_SLAB * (h_w // 2 + 1))
                o_ref[:, sl] = (o_pair * sc_ref[:, sl].astype(F32)).astype(BF16)


def _attention(q, k, vt, sc, cache, batch, lq, lk, tq, heads_per_step, chunk, lookahead):
    use_cache = cache is not None
    nq = lq // tq
    n_hp = N_HEADS // heads_per_step
    qw = heads_per_step * HEAD_SLAB
    ow = heads_per_step * V_DIM
    q_map = lambda b, g, i: (b * nq + i, g)
    k_map = lambda b, g, i: (b, g)
    vt_map = lambda b, g, i: (g, b)
    in_specs = [pl.BlockSpec((tq, qw), q_map), pl.BlockSpec((lk, qw), k_map), pl.BlockSpec((ow, lk), vt_map)]
    args = [q, k, vt]
    lc = 0
    if use_cache:
        lc = cache[0].shape[0] // batch
        in_specs += [pl.BlockSpec((lc, qw), k_map), pl.BlockSpec((ow, lc), vt_map)]
        args += list(cache)
    in_specs.append(pl.BlockSpec((tq, ow), q_map))
    args.append(sc)
    slots = min(heads_per_step, lookahead + 1)
    return pl.pallas_call(
        functools.partial(_attn_kernel, heads=heads_per_step, chunk=chunk, use_cache=use_cache, lookahead=lookahead),
        grid=(batch, n_hp, nq),
        in_specs=in_specs,
        out_specs=pl.BlockSpec((tq, ow), q_map),
        out_shape=jax.ShapeDtypeStruct((batch * lq, ATT_WIDTH), BF16),
        scratch_shapes=[pltpu.VMEM((slots, lk + lc, tq), F32), pltpu.VMEM((slots, 8, tq), F32),
                        pltpu.VMEM((slots, 8, tq), F32), pltpu.VMEM((slots, V_DIM, tq), F32)],
        compiler_params=_params(3),
        name="attention_cache" if use_cache else "attention",
    )(*args)


def _out_kernel(*refs, final):
    if final:
        h_ref, mod_ref, ng_ref, xa_ref, xb_ref, xc_ref, wa_ref, wb_ref, wc_ref, wg_ref, wo_ref, fg_ref, o_ref = refs
    else:
        h_ref, mod_ref, ng_ref, xa_ref, xb_ref, xc_ref, wa_ref, wb_ref, wc_ref, wg_ref, wo_ref, o_ref = refs
    h = h_ref[...]
    xn = _modulated_norm(h, ng_ref[...], mod_ref).astype(BF16)
    y = None
    for i, (x_ref, w_ref) in enumerate(((xa_ref, wa_ref), (xb_ref, wb_ref), (xc_ref, wc_ref))):
        g = jax.nn.sigmoid(_dot(xn, wg_ref[:, i * D_MODEL:(i + 1) * D_MODEL]))
        term = g * _dot(x_ref[...], w_ref[...])
        y = term if y is None else y + term
    h_new = h + mod_ref[2:3, :] * _dot(y.astype(BF16), wo_ref[...])
    if final:
        o_ref[...] = _rms(h_new, fg_ref[...])
    else:
        o_ref[...] = h_new


def _out(h, mod, mod_row, xa, xb, xc, pw, l, final_g, tm):
    t = h.shape[0]
    final = final_g is not None
    row = lambda i: (i, 0)
    names = ['w_br_a', 'w_br_b', 'w_br_c', 'wg', 'w_out']
    in_specs = ([pl.BlockSpec((tm, D_MODEL), row), _mod_spec(l, mod_row), _layer_spec(pw['norm_g'], l),
                 pl.BlockSpec((tm, FN_WIDTH), row), pl.BlockSpec((tm, POOL_WIDTH), row),
                 pl.BlockSpec((tm, ATT_WIDTH), row)] + [_layer_spec(pw[n], l) for n in names])
    args = [h, mod, pw['norm_g'], xa, xb, xc] + [pw[n] for n in names]
    if final:
        in_specs.append(pl.BlockSpec((1, D_MODEL), lambda i: (0, 0)))
        args.append(final_g)
    return pl.pallas_call(
        functools.partial(_out_kernel, final=final),
        grid=(t // tm,),
        in_specs=in_specs,
        out_specs=pl.BlockSpec((tm, D_MODEL), row),
        out_shape=jax.ShapeDtypeStruct((t, D_MODEL), F32),
        compiler_params=_params(1),
        name="out_final" if final else "out",
    )(*args)


def _rot_cols(w):
    q = QK_ROPE // 4
    return jnp.concatenate([-w[..., q:2 * q], w[..., :q], -w[..., 3 * q:], w[..., 2 * q:3 * q]], axis=-1)


_ROPE_PAD = ((QK_NOPE, HEAD_SLAB - QK_NOPE - QK_ROPE),)


def _pack_weights(norm_g, w_in, pool_w, pool_scale, q_norm_g, w_q_up, kv_norm_g, w_kv_up, w_br_a, w_br_b, w_br_c,
                  w_out):
    lead = ((0, 0), (0, 0))
    w_kr = w_in[:, :, _OFF_KR:_OFF_CZ]
    w1 = jnp.concatenate([w_in[:, :, :_OFF_KV], w_in[:, :, _OFF_CZ:_OFF_G], w_in[:, :, _OFF_KV:_OFF_KR],
                          jnp.pad(w_kr, lead + _ROPE_PAD), jnp.pad(_rot_cols(w_kr), lead + _ROPE_PAD)],
                         axis=2).astype(BF16)
    wide = N_HEADS * HEAD_SLAB
    wq_h = w_q_up.reshape(DEPTH, Q_RANK, N_HEADS, QK_NOPE + QK_ROPE)
    wq = jnp.pad(wq_h, lead + ((0, 0), (0, _ROPE_PAD[0][1]))).reshape(DEPTH, Q_RANK, wide).astype(BF16)
    wq_rot = jnp.pad(_rot_cols(wq_h[..., QK_NOPE:]), lead + ((0, 0),) + _ROPE_PAD)
    wq_rot = wq_rot.reshape(DEPTH, Q_RANK, wide).astype(BF16)
    wkv_h = w_kv_up.reshape(DEPTH, KV_RANK, N_HEADS, QK_NOPE + V_DIM)
    wk = jnp.pad(wkv_h[..., :QK_NOPE], lead + ((0, 0), (0, HEAD_SLAB - QK_NOPE))).reshape(DEPTH, KV_RANK, wide)
    wvt = wkv_h[..., QK_NOPE:].reshape(DEPTH, KV_RANK, ATT_WIDTH).transpose(0, 2, 1)
    groups = len(POOL_WINDOWS)
    eye = jnp.eye(groups, dtype=F32)
    pool_bd = (pool_w[:, :, :, None, :] * eye[None, :, None, :, None]).reshape(DEPTH, POOL_WIDTH, POOL_WIDTH)
    return {
        'norm_g': norm_g.reshape(DEPTH, 1, D_MODEL), 'w1': w1, 'wg': w_in[:, :, _OFF_G:].astype(BF16),
        'q_norm_g': q_norm_g.reshape(DEPTH, 1, Q_RANK), 'wq': wq, 'wq_rot': wq_rot,
        'kv_norm_g': kv_norm_g.reshape(DEPTH, 1, KV_RANK), 'wk': wk.astype(BF16), 'wvt': wvt.astype(BF16),
        'pool_w': pool_bd.astype(BF16), 'pool_scale': pool_scale.reshape(DEPTH, 1, POOL_WIDTH),
        'w_br_a': w_br_a.astype(BF16), 'w_br_b': w_br_b.astype(BF16), 'w_br_c': w_br_c.astype(BF16),
        'w_out': w_out.astype(BF16),
    }


def _rope_tables(seq):
    f32 = np.float32
    t = np.arange(seq)
    row = (t // GRID_W).astype(f32)
    col = (t % GRID_W).astype(f32)
    half = QK_ROPE // 2
    freqs = f32(ROPE_THETA) ** (-np.arange(0, half, 2, dtype=f32) / f32(half))
    ar = row[:, None] * freqs
    ac = col[:, None] * freqs
    cos = np.ones((seq, HEAD_SLAB), f32)
    sin = np.zeros((seq, HEAD_SLAB), f32)
    cos[:, QK_NOPE:QK_NOPE + QK_ROPE] = np.concatenate([np.cos(ar), np.cos(ar), np.cos(ac), np.cos(ac)], axis=-1)
    sin[:, QK_NOPE:QK_NOPE + QK_ROPE] = np.concatenate([np.sin(ar), np.sin(ar), np.sin(ac), np.sin(ac)], axis=-1)
    return cos, sin


TOKEN_TILE = 512
SAMPLE_Q_TILE = 512
SAMPLE_HEADS_PER_STEP = 4
SAMPLE_KEY_CHUNK = 512
PROMPT_SEQS_PER_STEP = 4


def kernel(x_prompt, x_sample, cache_ckv, cache_krope, c, c_ctx, norm_g, w_mod, b_mod, w_in, pool_w, pool_scale,
           q_norm_g, w_q_up, kv_norm_g, w_kv_up, w_br_a, w_br_b, w_br_c, w_out, final_norm_g):
    batch, seq, _ = x_prompt.shape
    dec_batch, dec_seq, _ = x_sample.shape
    past = cache_ckv.shape[2]
    tm = TOKEN_TILE

    mod_rows = 8
    cvec = jnp.concatenate([c_ctx[None, :], c, jnp.zeros((mod_rows - 1 - dec_batch, D_MODEL), F32)], axis=0)
    mod = _modulation(cvec, w_mod, b_mod).reshape(DEPTH, mod_rows, 3, D_MODEL)
    prompt_row = lambda i: 0
    tiles_per_sample = dec_seq // tm
    assert past % SAMPLE_KEY_CHUNK == 0 and dec_seq % SAMPLE_KEY_CHUNK == 0
    sample_row = lambda i: 1 + i // tiles_per_sample

    cos, sin = _rope_tables(dec_seq)
    rope = (cos, sin, lambda i: i % tiles_per_sample)
    final_g = final_norm_g.reshape(1, D_MODEL)
    pw = _pack_weights(norm_g, w_in, pool_w, pool_scale, q_norm_g, w_q_up, kv_norm_g, w_kv_up, w_br_a, w_br_b,
                       w_br_c, w_out)
    cache_kr_slab = jnp.pad(cache_krope, ((0, 0), (0, 0), (0, 0)) + _ROPE_PAD)

    hp = x_prompt.reshape(batch * seq, D_MODEL)
    hs = x_sample.reshape(dec_batch * dec_seq, D_MODEL)
    ckv_list, kr_list = [], []
    for l in range(DEPTH):
        last = final_g if l == DEPTH - 1 else None

        a_in, sa, b_in, sb, q, k, vt, ckv, kr, sc = _inproj(hp, mod, prompt_row, pw, l, None, tm)
        ckv_list.append(ckv.reshape(batch, seq, KV_RANK))
        kr_list.append(kr.reshape(batch, seq, HEAD_SLAB))
        xa = _fourier_direct(a_in, sa, seq, PROMPT_SEQS_PER_STEP)
        xb = _pool(b_in, sb, pw, l, seq, PROMPT_SEQS_PER_STEP)
        xc = _attention(q, k, vt, sc, None, batch, seq, seq, seq, N_HEADS, seq, lookahead=N_HEADS)
        hp = _out(hp, mod, prompt_row, xa, xb, xc, pw, l, last, tm)

        a_in, sa, b_in, sb, q, k, vt, _, _, sc = _inproj(hs, mod, sample_row, pw, l, rope, tm)
        cache = _cache_kv(cache_ckv, cache_kr_slab, pw, l)
        xa = _fourier_fft(a_in, sa, dec_seq)
        xb = _pool(b_in, sb, pw, l, dec_seq, 1)
        xc = _attention(q, k, vt, sc, cache, dec_batch, dec_seq, dec_seq, SAMPLE_Q_TILE, SAMPLE_HEADS_PER_STEP,
                        SAMPLE_KEY_CHUNK, lookahead=1)
        hs = _out(hs, mod, sample_row, xa, xb, xc, pw, l, last, tm)

    y_prompt = hp.reshape(batch, seq, D_MODEL)
    y_sample = hs.reshape(dec_batch, dec_seq, D_MODEL)
    new_krope = jnp.stack(kr_list, axis=1)[..., QK_NOPE:QK_NOPE + QK_ROPE]
    return (y_prompt, y_sample, jnp.stack(ckv_list, axis=1), new_krope)
```

```python
import functools

import numpy as np
import jax
import jax.numpy as jnp
from jax import lax
from jax.experimental import pallas as pl
from jax.experimental.pallas import tpu as pltpu

D_MODEL = 1024
DEPTH = 2
GRID_W = 64
EPS = 1e-6
FN_WIDTH = 256
FN_GC = 64
POOL_WINDOWS = (2, 4, 8, 16)
POOL_WIDTH = 256
POOL_GC = 64
N_HEADS = 8
QK_NOPE = 64
QK_ROPE = 32
V_DIM = 64
Q_RANK = 256
KV_RANK = 128
ATT_WIDTH = 512
ROPE_THETA = 10000.0
HEAD_SLAB = 128
QK_SCALE = (QK_NOPE + QK_ROPE) ** -0.5
Q_PRESCALE = QK_SCALE * float(np.log2(np.e))

VMEM_LIMIT_BYTES = 56 * 1024 * 1024

F32 = jnp.float32
BF16 = jnp.bfloat16

_OFF_A, _OFF_B, _OFF_Q, _OFF_KV, _OFF_KR, _OFF_CZ, _OFF_G = 0, 512, 1024, 1280, 1408, 1440, 1952
_W1_A = (0, 512)
_W1_B = (512, 1024)
_W1_Q = (1024, 1280)
_W1_CZ = (1280, 1792)
_W1_KV = (1792, 2176)
W1_WIDTH = 2176


def _params(n_parallel):
    return pltpu.CompilerParams(dimension_semantics=("arbitrary",) * n_parallel,
                                vmem_limit_bytes=VMEM_LIMIT_BYTES)


def _dot(a, b):
    return jnp.dot(a, b, preferred_element_type=F32)


def _silu(x):
    return x * jax.nn.sigmoid(x)


def _rms(x, g):
    r = lax.rsqrt(jnp.mean(x * x, axis=-1, keepdims=True) + EPS)
    return (x * r) * g


def _modulated_norm(h, norm_g, mod_ref):
    shift = mod_ref[0:1, :]
    scale = mod_ref[1:2, :]
    return _rms(h, norm_g) * (1.0 + scale) + shift


def _split_bf16(x):
    hi = x.astype(BF16)
    lo = (x - hi.astype(F32)).astype(BF16)
    return hi, lo


def _dot3_right(x, m_hi, m_lo):
    x_hi, x_lo = _split_bf16(x)
    return _dot(x_hi, m_hi) + _dot(x_lo, m_hi) + _dot(x_hi, m_lo)


def _dot3_left(m_hi, m_lo, x):
    x_hi, x_lo = _split_bf16(x)
    return _dot(m_hi, x_hi) + _dot(m_hi, x_lo) + _dot(m_lo, x_hi)


def _mod_kernel(c_ref, w_ref, b_ref, o_ref):
    s = _silu(c_ref[...]).astype(BF16)
    o_ref[...] = _dot(s, w_ref[...].astype(BF16)) + b_ref[...]


def _modulation(cvec, w_mod, b_mod):
    rows = cvec.shape[0]
    tn = 768
    return pl.pallas_call(
        _mod_kernel,
        grid=(DEPTH, 3 * D_MODEL // tn),
        in_specs=[pl.BlockSpec((rows, D_MODEL), lambda l, j: (0, 0)),
                  pl.BlockSpec((None, D_MODEL, tn), lambda l, j: (l, 0, j)),
                  pl.BlockSpec((None, 1, tn), lambda l, j: (l, 0, j))],
        out_specs=pl.BlockSpec((None, rows, tn), lambda l, j: (l, 0, j)),
        out_shape=jax.ShapeDtypeStruct((DEPTH, rows, 3 * D_MODEL), F32),
        compiler_params=_params(2),
        name="modulation",
    )(cvec, w_mod, b_mod.reshape(DEPTH, 1, 3 * D_MODEL))


_NT = (((1,), (1,)), ((), ()))


def _key_value_heads(ckv_bf16, kr_slab, wk_ref, wvt_ref, k_ref, vt_ref):
    kn = _dot(ckv_bf16, wk_ref[...])
    for h in range(N_HEADS):
        sl = slice(HEAD_SLAB * h, HEAD_SLAB * (h + 1))
        k_ref[:, sl] = (kn[:, sl] + kr_slab).astype(BF16)
    vt_ref[...] = lax.dot_general(wvt_ref[...], ckv_bf16, _NT, preferred_element_type=F32).astype(BF16)


def _inproj_kernel(*refs, use_rope):
    if use_rope:
        (h_ref, mod_ref, ng_ref, w1_ref, qg_ref, wq_ref, wqr_ref, kvg_ref, wk_ref, wvt_ref, cos_ref, sin_ref,
         ain_ref, sa_ref, bin_ref, sb_ref, q_ref, k_ref, vt_ref, ckv_ref, kr_ref, sc_ref) = refs
    else:
        (h_ref, mod_ref, ng_ref, w1_ref, qg_ref, wq_ref, kvg_ref, wk_ref, wvt_ref,
         ain_ref, sa_ref, bin_ref, sb_ref, q_ref, k_ref, vt_ref, ckv_ref, kr_ref, sc_ref) = refs

    xn = _modulated_norm(h_ref[...], ng_ref[...], mod_ref).astype(BF16)

    def proj(cols):
        return _dot(xn, w1_ref[:, cols[0]:cols[1]])

    a = proj(_W1_A)
    ain_ref[...] = a[:, :FN_WIDTH]
    sa_ref[...] = _silu(a[:, FN_WIDTH:]).astype(BF16)
    b = proj(_W1_B)
    bin_ref[...] = b[:, :POOL_WIDTH]
    sb_ref[...] = _silu(b[:, POOL_WIDTH:]).astype(BF16)
    sc_ref[...] = _silu(proj(_W1_CZ)).astype(BF16)

    qn = _rms(proj(_W1_Q), qg_ref[...]).astype(BF16)
    q = _dot(qn, wq_ref[...])
    if use_rope:
        cos = cos_ref[...]
        sin = sin_ref[...]
        qr = _dot(qn, wqr_ref[...])
        for h in range(N_HEADS):
            sl = slice(HEAD_SLAB * h, HEAD_SLAB * (h + 1))
            q_ref[:, sl] = ((q[:, sl] * cos + qr[:, sl] * sin) * Q_PRESCALE).astype(BF16)
    else:
        q_ref[...] = (q * Q_PRESCALE).astype(BF16)

    kv = proj(_W1_KV)
    ckv = _rms(kv[:, :KV_RANK], kvg_ref[...])
    ckv_ref[...] = ckv
    kr = kv[:, KV_RANK:KV_RANK + HEAD_SLAB]
    kr_ref[...] = kr[:, QK_NOPE:QK_NOPE + QK_ROPE]
    if use_rope:
        kr = kr * cos + kv[:, KV_RANK + HEAD_SLAB:] * sin
    _key_value_heads(ckv.astype(BF16), kr, wk_ref, wvt_ref, k_ref, vt_ref)


def _layer_spec(arr, l):
    return pl.BlockSpec((None,) + arr.shape[1:], lambda *_: (l,) + (0,) * (arr.ndim - 1))


def _mod_spec(l, mod_row):
    return pl.BlockSpec((None, None, 3, D_MODEL), lambda i: (l, mod_row(i), 0, 0))


def _inproj(h, mod, mod_row, pw, l, rope, tm):
    t = h.shape[0]
    use_rope = rope is not None
    row = lambda i: (i, 0)
    names = ['norm_g', 'w1', 'q_norm_g', 'wq'] + (['wq_rot'] if use_rope else []) + ['kv_norm_g', 'wk', 'wvt']
    in_specs = [pl.BlockSpec((tm, D_MODEL), row), _mod_spec(l, mod_row)] + [_layer_spec(pw[n], l) for n in names]
    args = [h, mod] + [pw[n] for n in names]
    if use_rope:
        cos, sin, rope_tile = rope
        in_specs += [pl.BlockSpec((tm, HEAD_SLAB), lambda i: (rope_tile(i), 0))] * 2
        args += [cos, sin]
    wide = N_HEADS * HEAD_SLAB
    token_outs = lambda ws: ([pl.BlockSpec((tm, w), row) for w, _ in ws],
                             [jax.ShapeDtypeStruct((t, w), dt) for w, dt in ws])
    specs_a, shapes_a = token_outs([(FN_WIDTH, F32), (FN_WIDTH, BF16), (POOL_WIDTH, F32), (POOL_WIDTH, BF16),
                                    (wide, BF16)])
    specs_b, shapes_b = token_outs([(KV_RANK, F32), (QK_ROPE, F32), (ATT_WIDTH, BF16)])
    out_specs = specs_a + [pl.BlockSpec((tm, wide), row), pl.BlockSpec((ATT_WIDTH, tm), lambda i: (0, i))] + specs_b
    out_shape = shapes_a + [jax.ShapeDtypeStruct((t, wide), BF16), jax.ShapeDtypeStruct((ATT_WIDTH, t), BF16)] + shapes_b
    return pl.pallas_call(
        functools.partial(_inproj_kernel, use_rope=use_rope),
        grid=(t // tm,),
        in_specs=in_specs,
        out_specs=out_specs,
        out_shape=out_shape,
        compiler_params=_params(1),
        name="inproj_rope" if use_rope else "inproj",
    )(*args)


def _cache_kv_kernel(ckv_ref, kr_ref, wk_ref, wvt_ref, k_ref, vt_ref):
    _key_value_heads(ckv_ref[...].astype(BF16), kr_ref[...], wk_ref, wvt_ref, k_ref, vt_ref)


def _cache_kv(cache_ckv, cache_kr_slab, pw, l):
    batch, _, past, _ = cache_ckv.shape
    wide = N_HEADS * HEAD_SLAB
    cache_map = lambda b: (b, l, 0, 0)
    return pl.pallas_call(
        _cache_kv_kernel,
        grid=(batch,),
        in_specs=[pl.BlockSpec((None, None, past, KV_RANK), cache_map),
                  pl.BlockSpec((None, None, past, HEAD_SLAB), cache_map),
                  _layer_spec(pw['wk'], l), _layer_spec(pw['wvt'], l)],
        out_specs=[pl.BlockSpec((past, wide), lambda b: (b, 0)), pl.BlockSpec((ATT_WIDTH, past), lambda b: (0, b))],
        out_shape=[jax.ShapeDtypeStruct((batch * past, wide), BF16),
                   jax.ShapeDtypeStruct((ATT_WIDTH, batch * past), BF16)],
        compiler_params=_params(1),
        name="cache_kv",
    )(cache_ckv, cache_kr_slab, pw['wk'], pw['wvt'])


def _hi_lo(m):
    m = np.asarray(m, np.float32)
    hi = m.astype(BF16)
    return hi, (m - hi.astype(np.float32)).astype(BF16)


def _dft_cos_sin(n):
    k = np.arange(n)
    ang = 2.0 * np.pi * ((k[:, None] * k[None, :]) % n) / n
    return np.cos(ang), np.sin(ang)


def _channel_dft_tables():
    c, s = _dft_cos_sin(FN_GC)
    eye = np.eye(FN_WIDTH // FN_GC)
    return np.kron(eye, c), np.kron(eye, s)


def _fourier_direct_kernel(a_ref, sa_ref, cc_hi, cc_lo, ss_hi, ss_lo, m_hi, m_lo, o_ref, *, seq, norm):
    n = a_ref.shape[0] // seq
    a = a_ref[...]
    tc = _dot3_right(a, cc_hi[...], cc_lo[...])
    ts = _dot3_right(a, ss_hi[...], ss_lo[...])
    side_by_side = lambda x: jnp.concatenate([x[i * seq:(i + 1) * seq] for i in range(n)], axis=1)
    f = _dot3_left(m_hi[...], m_lo[...], jnp.concatenate([side_by_side(tc), side_by_side(ts)], axis=0))
    for i in range(n):
        rows = slice(i * seq, (i + 1) * seq)
        f_i = f[:, i * FN_WIDTH:(i + 1) * FN_WIDTH]
        o_ref[rows, :] = ((f_i * norm) * sa_ref[rows, :].astype(F32)).astype(BF16)


def _fourier_direct(a, sa, seq, seqs_per_step):
    t = a.shape[0]
    rows = seq * seqs_per_step
    cc, ss = _channel_dft_tables()
    cl, sl = _dft_cos_sin(seq)
    tables = [x for m in (cc, ss, np.concatenate([cl, -sl], axis=1)) for x in _hi_lo(m)]
    row = lambda i: (i, 0)
    const = lambda i: (0, 0)
    tspecs = [pl.BlockSpec((FN_WIDTH, FN_WIDTH), const)] * 4 + [pl.BlockSpec((seq, 2 * seq), const)] * 2
    return pl.pallas_call(
        functools.partial(_fourier_direct_kernel, seq=seq, norm=float((seq * FN_GC) ** -0.5)),
        grid=(t // rows,),
        in_specs=[pl.BlockSpec((rows, FN_WIDTH), row), pl.BlockSpec((rows, FN_WIDTH), row)] + tspecs,
        out_specs=pl.BlockSpec((rows, FN_WIDTH), row),
        out_shape=jax.ShapeDtypeStruct((t, FN_WIDTH), BF16),
        compiler_params=_params(1),
        name="fourier_direct",
    )(a, sa, *tables)


FFT_R = 64


def _fourier_fft_kernel(a_ref, sa_ref, cc_hi, cc_lo, ss_hi, ss_lo, m1_hi, m1_lo, m2_hi, m2_lo, twc_ref, tws_ref,
                        o_ref, zr0, zr1, zi0, zi1, yr0, yr1, yi0, yi1, *, seq, norm):
    r = FFT_R
    half = FN_WIDTH // 2
    chunk = 512

    def put(refs, rows, x):
        refs[0][rows, :] = x[:, :half]
        refs[1][rows, :] = x[:, half:]

    def get(refs, rows):
        return jnp.concatenate([refs[0][rows, :], refs[1][rows, :]], axis=1)

    zr, zi, yr_s, yi_s = (zr0, zr1), (zi0, zi1), (yr0, yr1), (yi0, yi1)
    for c in range(seq // chunk):
        rows = slice(c * chunk, (c + 1) * chunk)
        a = a_ref[rows, :]
        put(zr, rows, _dot3_right(a, cc_hi[...], cc_lo[...]))
        put(zi, rows, -_dot3_right(a, ss_hi[...], ss_lo[...]))
    for n2 in range(r):
        strided = pl.ds(n2, r, stride=r)
        z = jnp.concatenate([get(zr, strided), get(zi, strided)], axis=0)
        y = _dot3_left(m1_hi[...], m1_lo[...], z)
        yr, yi = y[:r], y[r:]
        blk = slice(n2 * r, (n2 + 1) * r)
        cos = jnp.concatenate([twc_ref[blk, :]] * 2, axis=1)
        sin = jnp.concatenate([tws_ref[blk, :]] * 2, axis=1)
        put(yr_s, blk, yr * cos + yi * sin)
        put(yi_s, blk, yi * cos - yr * sin)
    for k1 in range(r):
        strided = pl.ds(k1, r, stride=r)
        y = jnp.concatenate([get(yr_s, strided), get(yi_s, strided)], axis=0)
        put(zr, strided, _dot3_left(m2_hi[...], m2_lo[...], y))
    for c in range(seq // chunk):
        rows = slice(c * chunk, (c + 1) * chunk)
        o_ref[rows, :] = ((get(zr, rows) * norm) * sa_ref[rows, :].astype(F32)).astype(BF16)


def _fourier_fft(a, sa, seq):
    assert seq == FFT_R * FFT_R
    t = a.shape[0]
    cc, ss = _channel_dft_tables()
    c, s = _dft_cos_sin(FFT_R)
    m1 = np.block([[c, s], [-s, c]])
    m2 = np.concatenate([c, s], axis=1)
    n2 = np.arange(FFT_R)[:, None]
    k1 = np.arange(FFT_R)[None, :]
    ang = (2.0 * np.pi * (n2 * k1) / seq).reshape(seq, 1)
    twc = np.ascontiguousarray(np.broadcast_to(np.cos(ang).astype(np.float32), (seq, 128)))
    tws = np.ascontiguousarray(np.broadcast_to(np.sin(ang).astype(np.float32), (seq, 128)))
    tables = [x for m in (cc, ss, m1, m2) for x in _hi_lo(m)]
    row = lambda i: (i, 0)
    const = lambda i: (0, 0)
    tspecs = ([pl.BlockSpec((FN_WIDTH, FN_WIDTH), const)] * 4 + [pl.BlockSpec((2 * FFT_R, 2 * FFT_R), const)] * 2
              + [pl.BlockSpec((FFT_R, 2 * FFT_R), const)] * 2 + [pl.BlockSpec((seq, 128), const)] * 2)
    return pl.pallas_call(
        functools.partial(_fourier_fft_kernel, seq=seq, norm=float((seq * FN_GC) ** -0.5)),
        grid=(t // seq,),
        in_specs=[pl.BlockSpec((seq, FN_WIDTH), row), pl.BlockSpec((seq, FN_WIDTH), row)] + tspecs,
        out_specs=pl.BlockSpec((seq, FN_WIDTH), row),
        out_shape=jax.ShapeDtypeStruct((t, FN_WIDTH), BF16),
        scratch_shapes=[pltpu.VMEM((seq, FN_WIDTH // 2), F32)] * 8,
        compiler_params=_params(1),
        name="fourier_fft",
    )(a, sa, *tables, twc, tws)


POOL_HALO = 8
POOL_CHUNK = 256


def _pool_kernel(b_ref, sb_ref, pw_ref, ps_ref, o_ref, pad_ref, *, seq):
    zeros = jnp.zeros((POOL_HALO, POOL_WIDTH), F32)
    lane = lax.broadcasted_iota(jnp.int32, (POOL_CHUNK, 128), 1)
    low_group = lane < POOL_GC
    for i in range(pad_ref.shape[0]):
        pad_ref[i, 0:POOL_HALO, :] = zeros
        pad_ref[i, POOL_HALO + seq:, :] = zeros
        pad_ref[i, POOL_HALO:POOL_HALO + seq, :] = b_ref[i * seq:(i + 1) * seq, :]

    for i, c in [(i, c) for i in range(pad_ref.shape[0]) for c in range(seq // POOL_CHUNK)]:
        r0 = c * POOL_CHUNK
        t = lax.broadcasted_iota(jnp.int32, (POOL_CHUNK, 128), 0) + r0

        def inv_count(w):
            left = w // 2
            right = w - 1 - left
            lo = jnp.maximum(t - left, 0)
            hi = jnp.minimum(t + right, seq - 1)
            return 1.0 / (hi - lo + 1).astype(F32)

        def ld(off, col):
            start = POOL_HALO + r0 + off
            return pad_ref[i, start:start + POOL_CHUNK, 128 * col:128 * (col + 1)]

        u0 = ld(0, 0)
        p2 = ld(-1, 0) + u0
        p4 = p2 + ld(-2, 0) + ld(1, 0)
        pooled0 = jnp.where(low_group, p2 * inv_count(2), p4 * inv_count(4)) - u0
        u1 = ld(0, 1)
        p8 = u1
        for off in (-4, -3, -2, -1, 1, 2, 3):
            p8 = p8 + ld(off, 1)
        p16 = p8
        for off in (-8, -7, -6, -5, 4, 5, 6, 7):
            p16 = p16 + ld(off, 1)
        pooled1 = jnp.where(low_group, p8 * inv_count(8), p16 * inv_count(16)) - u1

        pooled = jnp.concatenate([pooled0, pooled1], axis=1).astype(BF16)
        mixed = _dot(pooled, pw_ref[...]) * ps_ref[...]
        rows = slice(i * seq + r0, i * seq + r0 + POOL_CHUNK)
        o_ref[rows, :] = (mixed * sb_ref[rows, :].astype(F32)).astype(BF16)


def _pool(b, sb, pw, l, seq, seqs_per_step):
    t = b.shape[0]
    rows = seq * seqs_per_step
    row = lambda i: (i, 0)
    return pl.pallas_call(
        functools.partial(_pool_kernel, seq=seq),
        grid=(t // rows,),
        in_specs=[pl.BlockSpec((rows, POOL_WIDTH), row), pl.BlockSpec((rows, POOL_WIDTH), row),
                  _layer_spec(pw['pool_w'], l), _layer_spec(pw['pool_scale'], l)],
        out_specs=pl.BlockSpec((rows, POOL_WIDTH), row),
        out_shape=jax.ShapeDtypeStruct((t, POOL_WIDTH), BF16),
        scratch_shapes=[pltpu.VMEM((seqs_per_step, seq + 2 * POOL_HALO, POOL_WIDTH), F32)],
        compiler_params=_params(1),
        name="pool",
    )(b, sb, pw['pool_w'], pw['pool_scale'])


def _attn_kernel(*refs, heads, chunk, use_cache, lookahead):
    if use_cache:
        q_ref, k_ref, vt_ref, kc_ref, vct_ref, sc_ref, o_ref, s_ref, m_ref, l_ref, acc_ref = refs
        sources = ((k_ref, vt_ref), (kc_ref, vct_ref))
    else:
        q_ref, k_ref, vt_ref, sc_ref, o_ref, s_ref, m_ref, l_ref, acc_ref = refs
        sources = ((k_ref, vt_ref),)
    tq = q_ref.shape[0]
    groups = chunk // 8
    chunks = []
    for keys, values in sources:
        for off in range(0, keys.shape[0], chunk):
            chunks.append((keys, values, off, len(chunks) * chunk))

    slots = s_ref.shape[0]

    def scores(h, c):
        keys, _, off, row = chunks[c]
        sl = slice(HEAD_SLAB * h, HEAD_SLAB * (h + 1))
        s = lax.dot_general(keys[off:off + chunk, sl], q_ref[:, sl], _NT, preferred_element_type=F32)
        s_ref[h % slots, row:row + chunk, :] = s
        m_ref[h % slots] = jnp.maximum(m_ref[h % slots], jnp.max(s.reshape(groups, 8, tq), axis=0))

    def weigh(h, c, m):
        _, values, off, row = chunks[c]
        p = jnp.exp2(s_ref[h % slots, row:row + chunk, :] - m)
        l_ref[h % slots] += jnp.sum(p.reshape(groups, 8, tq), axis=0)
        acc_ref[h % slots] += _dot(values[V_DIM * h:V_DIM * (h + 1), off:off + chunk], p.astype(BF16))

    outs = []
    for t in range(heads + lookahead):
        h_w, h_s = t - lookahead, t
        if h_s < heads:
            m_ref[h_s % slots] = jnp.full((8, tq), -jnp.inf, F32)
        if h_w >= 0:
            m = jnp.max(m_ref[h_w % slots], axis=0, keepdims=True)
            l_ref[h_w % slots] = jnp.zeros((8, tq), F32)
            acc_ref[h_w % slots] = jnp.zeros((V_DIM, tq), F32)
        for c in range(len(chunks)):
            if h_w >= 0:
                weigh(h_w, c, m)
            if h_s < heads:
                scores(h_s, c)
        if h_w >= 0:
            denom = jnp.sum(l_ref[h_w % slots], axis=0, keepdims=True)
            outs.append(acc_ref[h_w % slots] * (1.0 / denom))
            if h_w % 2 == 1:
                o_pair = jnp.concatenate(outs, axis=0).T
                outs = []
                sl = slice(HEAD_SLAB * (h_w // 2), HEAD_SLAB * (h_w // 2 + 1))
                o_ref[:, sl] = (o_pair * sc_ref[:, sl].astype(F32)).astype(BF16)


def _attention(q, k, vt, sc, cache, batch, lq, lk, tq, heads_per_step, chunk, lookahead):
    use_cache = cache is not None
    nq = lq // tq
    n_hp = N_HEADS // heads_per_step
    qw = heads_per_step * HEAD_SLAB
    ow = heads_per_step * V_DIM
    q_map = lambda b, g, i: (b * nq + i, g)
    k_map = lambda b, g, i: (b, g)
    vt_map = lambda b, g, i: (g, b)
    in_specs = [pl.BlockSpec((tq, qw), q_map), pl.BlockSpec((lk, qw), k_map), pl.BlockSpec((ow, lk), vt_map)]
    args = [q, k, vt]
    lc = 0
    if use_cache:
        lc = cache[0].shape[0] // batch
        in_specs += [pl.BlockSpec((lc, qw), k_map), pl.BlockSpec((ow, lc), vt_map)]
        args += list(cache)
    in_specs.append(pl.BlockSpec((tq, ow), q_map))
    args.append(sc)
    slots = min(heads_per_step, lookahead + 1)
    return pl.pallas_call(
        functools.partial(_attn_kernel, heads=heads_per_step, chunk=chunk, use_cache=use_cache, lookahead=lookahead),
        grid=(batch, n_hp, nq),
        in_specs=in_specs,
        out_specs=pl.BlockSpec((tq, ow), q_map),
        out_shape=jax.ShapeDtypeStruct((batch * lq, ATT_WIDTH), BF16),
        scratch_shapes=[pltpu.VMEM((slots, lk + lc, tq), F32), pltpu.VMEM((slots, 8, tq), F32),
                        pltpu.VMEM((slots, 8, tq), F32), pltpu.VMEM((slots, V_DIM, tq), F32)],
        compiler_params=_params(3),
        name="attention_cache" if use_cache else "attention",
    )(*args)


def _out_kernel(*refs, final):
    if final:
        h_ref, mod_ref, ng_ref, xa_ref, xb_ref, xc_ref, wa_ref, wb_ref, wc_ref, wg_ref, wo_ref, fg_ref, o_ref = refs
    else:
        h_ref, mod_ref, ng_ref, xa_ref, xb_ref, xc_ref, wa_ref, wb_ref, wc_ref, wg_ref, wo_ref, o_ref = refs
    h = h_ref[...]
    xn = _modulated_norm(h, ng_ref[...], mod_ref).astype(BF16)
    y = None
    for i, (x_ref, w_ref) in enumerate(((xa_ref, wa_ref), (xb_ref, wb_ref), (xc_ref, wc_ref))):
        g = jax.nn.sigmoid(_dot(xn, wg_ref[:, i * D_MODEL:(i + 1) * D_MODEL]))
        term = g * _dot(x_ref[...], w_ref[...])
        y = term if y is None else y + term
    h_new = h + mod_ref[2:3, :] * _dot(y.astype(BF16), wo_ref[...])
    if final:
        o_ref[...] = _rms(h_new, fg_ref[...])
    else:
        o_ref[...] = h_new


def _out(h, mod, mod_row, xa, xb, xc, pw, l, final_g, tm):
    t = h.shape[0]
    final = final_g is not None
    row = lambda i: (i, 0)
    names = ['w_br_a', 'w_br_b', 'w_br_c', 'wg', 'w_out']
    in_specs = ([pl.BlockSpec((tm, D_MODEL), row), _mod_spec(l, mod_row), _layer_spec(pw['norm_g'], l),
                 pl.BlockSpec((tm, FN_WIDTH), row), pl.BlockSpec((tm, POOL_WIDTH), row),
                 pl.BlockSpec((tm, ATT_WIDTH), row)] + [_layer_spec(pw[n], l) for n in names])
    args = [h, mod, pw['norm_g'], xa, xb, xc] + [pw[n] for n in names]
    if final:
        in_specs.append(pl.BlockSpec((1, D_MODEL), lambda i: (0, 0)))
        args.append(final_g)
    return pl.pallas_call(
        functools.partial(_out_kernel, final=final),
        grid=(t // tm,),
        in_specs=in_specs,
        out_specs=pl.BlockSpec((tm, D_MODEL), row),
        out_shape=jax.ShapeDtypeStruct((t, D_MODEL), F32),
        compiler_params=_params(1),
        name="out_final" if final else "out",
    )(*args)


def _rot_cols(w):
    q = QK_ROPE // 4
    return jnp.concatenate([-w[..., q:2 * q], w[..., :q], -w[..., 3 * q:], w[..., 2 * q:3 * q]], axis=-1)


_ROPE_PAD = ((QK_NOPE, HEAD_SLAB - QK_NOPE - QK_ROPE),)
PACK_ROWS = 256


def _rope_placement():
    plain = np.zeros((HEAD_SLAB, HEAD_SLAB), np.float32)
    rot = np.zeros((HEAD_SLAB, HEAD_SLAB), np.float32)
    q = QK_ROPE // 4
    for j in range(QK_ROPE):
        plain[j, QK_NOPE + j] = 1.0
        src, sign = (j + q, -1.0) if (j % (2 * q)) < q else (j - q, 1.0)
        rot[src, QK_NOPE + j] = sign
    return plain.astype(BF16), rot.astype(BF16)


def _pack_w_in_kernel(w_ref, plain_ref, rot_ref, w1_ref, wg_ref):
    x = w_ref[...]
    wg_ref[...] = x[:, _OFF_G:].astype(BF16)
    w1_ref[:, :_OFF_KV] = x[:, :_OFF_KV].astype(BF16)
    w1_ref[:, _W1_CZ[0]:_W1_CZ[1]] = x[:, _OFF_CZ:_OFF_G].astype(BF16)
    w1_ref[:, _W1_KV[0]:_W1_KV[0] + KV_RANK] = x[:, _OFF_KV:_OFF_KR].astype(BF16)
    slab = x[:, _OFF_KR:_OFF_KR + HEAD_SLAB].astype(BF16)
    lo = _W1_KV[0] + KV_RANK
    w1_ref[:, lo:lo + HEAD_SLAB] = _dot(slab, plain_ref[...]).astype(BF16)
    w1_ref[:, lo + HEAD_SLAB:lo + 2 * HEAD_SLAB] = _dot(slab, rot_ref[...]).astype(BF16)


def _pack_w_in(w_in):
    plain, rot = _rope_placement()
    in_width = w_in.shape[2]
    g_width = in_width - _OFF_G
    const = lambda l, i: (0, 0)
    return pl.pallas_call(
        _pack_w_in_kernel,
        grid=(DEPTH, D_MODEL // PACK_ROWS),
        in_specs=[pl.BlockSpec((None, PACK_ROWS, in_width), lambda l, i: (l, i, 0)),
                  pl.BlockSpec((HEAD_SLAB, HEAD_SLAB), const), pl.BlockSpec((HEAD_SLAB, HEAD_SLAB), const)],
        out_specs=[pl.BlockSpec((None, PACK_ROWS, W1_WIDTH), lambda l, i: (l, i, 0)),
                   pl.BlockSpec((None, PACK_ROWS, g_width), lambda l, i: (l, i, 0))],
        out_shape=[jax.ShapeDtypeStruct((DEPTH, D_MODEL, W1_WIDTH), BF16),
                   jax.ShapeDtypeStruct((DEPTH, D_MODEL, g_width), BF16)],
        compiler_params=_params(2),
        name="pack_w_in",
    )(w_in, plain, rot)


def _pack_weights(norm_g, w_in, pool_w, pool_scale, q_norm_g, w_q_up, kv_norm_g, w_kv_up, w_br_a, w_br_b, w_br_c,
                  w_out):
    lead = ((0, 0), (0, 0))
    w1, wg = _pack_w_in(w_in)
    wide = N_HEADS * HEAD_SLAB
    wq_h = w_q_up.reshape(DEPTH, Q_RANK, N_HEADS, QK_NOPE + QK_ROPE)
    wq = jnp.pad(wq_h, lead + ((0, 0), (0, _ROPE_PAD[0][1]))).reshape(DEPTH, Q_RANK, wide).astype(BF16)
    wq_rot = jnp.pad(_rot_cols(wq_h[..., QK_NOPE:]), lead + ((0, 0),) + _ROPE_PAD)
    wq_rot = wq_rot.reshape(DEPTH, Q_RANK, wide).astype(BF16)
    wkv_h = w_kv_up.reshape(DEPTH, KV_RANK, N_HEADS, QK_NOPE + V_DIM)
    wk = jnp.pad(wkv_h[..., :QK_NOPE], lead + ((0, 0), (0, HEAD_SLAB - QK_NOPE))).reshape(DEPTH, KV_RANK, wide)
    wvt = wkv_h[..., QK_NOPE:].reshape(DEPTH, KV_RANK, ATT_WIDTH).transpose(0, 2, 1)
    groups = len(POOL_WINDOWS)
    eye = jnp.eye(groups, dtype=F32)
    pool_bd = (pool_w[:, :, :, None, :] * eye[None, :, None, :, None]).reshape(DEPTH, POOL_WIDTH, POOL_WIDTH)
    return {
        'norm_g': norm_g.reshape(DEPTH, 1, D_MODEL), 'w1': w1, 'wg': wg,
        'q_norm_g': q_norm_g.reshape(DEPTH, 1, Q_RANK), 'wq': wq, 'wq_rot': wq_rot,
        'kv_norm_g': kv_norm_g.reshape(DEPTH, 1, KV_RANK), 'wk': wk.astype(BF16), 'wvt': wvt.astype(BF16),
        'pool_w': pool_bd.astype(BF16), 'pool_scale': pool_scale.reshape(DEPTH, 1, POOL_WIDTH),
        'w_br_a': w_br_a.astype(BF16), 'w_br_b': w_br_b.astype(BF16), 'w_br_c': w_br_c.astype(BF16),
        'w_out': w_out.astype(BF16),
    }


def _rope_tables(seq):
    f32 = np.float32
    t = np.arange(seq)
    row = (t // GRID_W).astype(f32)
    col = (t % GRID_W).astype(f32)
    half = QK_ROPE // 2
    freqs = f32(ROPE_THETA) ** (-np.arange(0, half, 2, dtype=f32) / f32(half))
    ar = row[:, None] * freqs
    ac = col[:, None] * freqs
    cos = np.ones((seq, HEAD_SLAB), f32)
    sin = np.zeros((seq, HEAD_SLAB), f32)
    cos[:, QK_NOPE:QK_NOPE + QK_ROPE] = np.concatenate([np.cos(ar), np.cos(ar), np.cos(ac), np.cos(ac)], axis=-1)
    sin[:, QK_NOPE:QK_NOPE + QK_ROPE] = np.concatenate([np.sin(ar), np.sin(ar), np.sin(ac), np.sin(ac)], axis=-1)
    return cos, sin


TOKEN_TILE = 512
SAMPLE_Q_TILE = 512
SAMPLE_HEADS_PER_STEP = 4
SAMPLE_KEY_CHUNK = 512
PROMPT_SEQS_PER_STEP = 4


def kernel(x_prompt, x_sample, cache_ckv, cache_krope, c, c_ctx, norm_g, w_mod, b_mod, w_in, pool_w, pool_scale,
           q_norm_g, w_q_up, kv_norm_g, w_kv_up, w_br_a, w_br_b, w_br_c, w_out, final_norm_g):
    batch, seq, _ = x_prompt.shape
    dec_batch, dec_seq, _ = x_sample.shape
    past = cache_ckv.shape[2]
    tm = TOKEN_TILE

    mod_rows = 8
    cvec = jnp.concatenate([c_ctx[None, :], c, jnp.zeros((mod_rows - 1 - dec_batch, D_MODEL), F32)], axis=0)
    mod = _modulation(cvec, w_mod, b_mod).reshape(DEPTH, mod_rows, 3, D_MODEL)
    prompt_row = lambda i: 0
    tiles_per_sample = dec_seq // tm
    assert past % SAMPLE_KEY_CHUNK == 0 and dec_seq % SAMPLE_KEY_CHUNK == 0
    sample_row = lambda i: 1 + i // tiles_per_sample

    cos, sin = _rope_tables(dec_seq)
    rope = (cos, sin, lambda i: i % tiles_per_sample)
    final_g = final_norm_g.reshape(1, D_MODEL)
    pw = _pack_weights(norm_g, w_in, pool_w, pool_scale, q_norm_g, w_q_up, kv_norm_g, w_kv_up, w_br_a, w_br_b,
                       w_br_c, w_out)
    cache_kr_slab = jnp.pad(cache_krope, ((0, 0), (0, 0), (0, 0)) + _ROPE_PAD)

    hp = x_prompt.reshape(batch * seq, D_MODEL)
    hs = x_sample.reshape(dec_batch * dec_seq, D_MODEL)
    ckv_list, kr_list = [], []
    for l in range(DEPTH):
        last = final_g if l == DEPTH - 1 else None

        a_in, sa, b_in, sb, q, k, vt, ckv, kr, sc = _inproj(hp, mod, prompt_row, pw, l, None, tm)
        ckv_list.append(ckv.reshape(batch, seq, KV_RANK))
        kr_list.append(kr.reshape(batch, seq, QK_ROPE))
        xa = _fourier_direct(a_in, sa, seq, PROMPT_SEQS_PER_STEP)
        xb = _pool(b_in, sb, pw, l, seq, PROMPT_SEQS_PER_STEP)
        xc = _attention(q, k, vt, sc, None, batch, seq, seq, seq, N_HEADS, seq, lookahead=N_HEADS)
        hp = _out(hp, mod, prompt_row, xa, xb, xc, pw, l, last, tm)

        a_in, sa, b_in, sb, q, k, vt, _, _, sc = _inproj(hs, mod, sample_row, pw, l, rope, tm)
        cache = _cache_kv(cache_ckv, cache_kr_slab, pw, l)
        xa = _fourier_fft(a_in, sa, dec_seq)
        xb = _pool(b_in, sb, pw, l, dec_seq, 1)
        xc = _attention(q, k, vt, sc, cache, dec_batch, dec_seq, dec_seq, SAMPLE_Q_TILE, SAMPLE_HEADS_PER_STEP,
                        SAMPLE_KEY_CHUNK, lookahead=1)
        hs = _out(hs, mod, sample_row, xa, xb, xc, pw, l, last, tm)

    y_prompt = hp.reshape(batch, seq, D_MODEL)
    y_sample = hs.reshape(dec_batch, dec_seq, D_MODEL)
    return (y_prompt, y_sample, jnp.stack(ckv_list, axis=1), jnp.stack(kr_list, axis=1))
```

```python
import functools

import numpy as np
import jax
import jax.numpy as jnp
from jax import lax
from jax.experimental import pallas as pl
from jax.experimental.pallas import tpu as pltpu

D_MODEL = 1024
DEPTH = 2
GRID_W = 64
EPS = 1e-6
FN_WIDTH = 256
FN_GC = 64
POOL_WINDOWS = (2, 4, 8, 16)
POOL_WIDTH = 256
POOL_GC = 64
N_HEADS = 8
QK_NOPE = 64
QK_ROPE = 32
V_DIM = 64
Q_RANK = 256
KV_RANK = 128
ATT_WIDTH = 512
ROPE_THETA = 10000.0
HEAD_SLAB = 128
QK_SCALE = (QK_NOPE + QK_ROPE) ** -0.5
Q_PRESCALE = QK_SCALE * float(np.log2(np.e))

VMEM_LIMIT_BYTES = 56 * 1024 * 1024

F32 = jnp.float32
BF16 = jnp.bfloat16

_OFF_A, _OFF_B, _OFF_Q, _OFF_KV, _OFF_KR, _OFF_CZ, _OFF_G = 0, 512, 1024, 1280, 1408, 1440, 1952
_W1_A = (0, 512)
_W1_B = (512, 1024)
_W1_Q = (1024, 1280)
_W1_CZ = (1280, 1792)
_W1_KV = (1792, 2176)
W1_WIDTH = 2176


def _params(n_parallel):
    return pltpu.CompilerParams(dimension_semantics=("arbitrary",) * n_parallel,
                                vmem_limit_bytes=VMEM_LIMIT_BYTES)


def _dot(a, b):
    return jnp.dot(a, b, preferred_element_type=F32)


def _silu(x):
    return x * jax.nn.sigmoid(x)


def _rms(x, g):
    r = lax.rsqrt(jnp.mean(x * x, axis=-1, keepdims=True) + EPS)
    return (x * r) * g


def _modulated_norm(h, norm_g, mod_ref):
    shift = mod_ref[0:1, :]
    scale = mod_ref[1:2, :]
    return _rms(h, norm_g) * (1.0 + scale) + shift


def _split_bf16(x):
    hi = x.astype(BF16)
    lo = (x - hi.astype(F32)).astype(BF16)
    return hi, lo


def _dot3_right(x, m_hi, m_lo):
    x_hi, x_lo = _split_bf16(x)
    return _dot(x_hi, m_hi) + _dot(x_lo, m_hi) + _dot(x_hi, m_lo)


def _dot3_left(m_hi, m_lo, x):
    x_hi, x_lo = _split_bf16(x)
    return _dot(m_hi, x_hi) + _dot(m_hi, x_lo) + _dot(m_lo, x_hi)


def _mod_kernel(c_ref, w_ref, b_ref, o_ref):
    s = _silu(c_ref[...]).astype(BF16)
    o_ref[...] = _dot(s, w_ref[...].astype(BF16)) + b_ref[...]


def _modulation(cvec, w_mod, b_mod):
    rows = cvec.shape[0]
    tn = 768
    return pl.pallas_call(
        _mod_kernel,
        grid=(DEPTH, 3 * D_MODEL // tn),
        in_specs=[pl.BlockSpec((rows, D_MODEL), lambda l, j: (0, 0)),
                  pl.BlockSpec((None, D_MODEL, tn), lambda l, j: (l, 0, j)),
                  pl.BlockSpec((None, 1, tn), lambda l, j: (l, 0, j))],
        out_specs=pl.BlockSpec((None, rows, tn), lambda l, j: (l, 0, j)),
        out_shape=jax.ShapeDtypeStruct((DEPTH, rows, 3 * D_MODEL), F32),
        compiler_params=_params(2),
        name="modulation",
    )(cvec, w_mod, b_mod.reshape(DEPTH, 1, 3 * D_MODEL))


_NT = (((1,), (1,)), ((), ()))


def _key_value_heads(ckv_bf16, kr_slab, wk_ref, wvt_ref, k_ref, vt_ref):
    kn = _dot(ckv_bf16, wk_ref[...])
    for h in range(N_HEADS):
        sl = slice(HEAD_SLAB * h, HEAD_SLAB * (h + 1))
        k_ref[:, sl] = (kn[:, sl] + kr_slab).astype(BF16)
    vt_ref[...] = lax.dot_general(wvt_ref[...], ckv_bf16, _NT, preferred_element_type=F32).astype(BF16)


def _inproj_kernel(*refs, use_rope):
    if use_rope:
        (h_ref, mod_ref, ng_ref, w1_ref, qg_ref, wq_ref, wqr_ref, kvg_ref, wk_ref, wvt_ref, cos_ref, sin_ref,
         ain_ref, sa_ref, bin_ref, sb_ref, q_ref, k_ref, vt_ref, ckv_ref, kr_ref, sc_ref) = refs
    else:
        (h_ref, mod_ref, ng_ref, w1_ref, qg_ref, wq_ref, kvg_ref, wk_ref, wvt_ref,
         ain_ref, sa_ref, bin_ref, sb_ref, q_ref, k_ref, vt_ref, ckv_ref, kr_ref, sc_ref) = refs

    xn = _modulated_norm(h_ref[...], ng_ref[...], mod_ref).astype(BF16)

    def proj(cols):
        return lax.dot_general(xn, w1_ref[cols[0]:cols[1], :], _NT, preferred_element_type=F32)

    a = proj(_W1_A)
    ain_ref[...] = a[:, :FN_WIDTH]
    sa_ref[...] = _silu(a[:, FN_WIDTH:]).astype(BF16)
    b = proj(_W1_B)
    bin_ref[...] = b[:, :POOL_WIDTH]
    sb_ref[...] = _silu(b[:, POOL_WIDTH:]).astype(BF16)
    sc_ref[...] = _silu(proj(_W1_CZ)).astype(BF16)

    qn = _rms(proj(_W1_Q), qg_ref[...]).astype(BF16)
    q = _dot(qn, wq_ref[...])
    if use_rope:
        cos = cos_ref[...]
        sin = sin_ref[...]
        qr = _dot(qn, wqr_ref[...])
        for h in range(N_HEADS):
            sl = slice(HEAD_SLAB * h, HEAD_SLAB * (h + 1))
            q_ref[:, sl] = ((q[:, sl] * cos + qr[:, sl] * sin) * Q_PRESCALE).astype(BF16)
    else:
        q_ref[...] = (q * Q_PRESCALE).astype(BF16)

    kv = proj(_W1_KV)
    ckv = _rms(kv[:, :KV_RANK], kvg_ref[...])
    ckv_ref[...] = ckv
    kr = kv[:, KV_RANK:KV_RANK + HEAD_SLAB]
    kr_ref[...] = kr[:, QK_NOPE:QK_NOPE + QK_ROPE]
    if use_rope:
        kr = kr * cos + kv[:, KV_RANK + HEAD_SLAB:] * sin
    _key_value_heads(ckv.astype(BF16), kr, wk_ref, wvt_ref, k_ref, vt_ref)


def _layer_spec(arr, l):
    return pl.BlockSpec((None,) + arr.shape[1:], lambda *_: (l,) + (0,) * (arr.ndim - 1))


def _mod_spec(l, mod_row):
    return pl.BlockSpec((None, None, 3, D_MODEL), lambda i: (l, mod_row(i), 0, 0))


def _inproj(h, mod, mod_row, pw, l, rope, tm):
    t = h.shape[0]
    use_rope = rope is not None
    row = lambda i: (i, 0)
    names = ['norm_g', 'w1', 'q_norm_g', 'wq'] + (['wq_rot'] if use_rope else []) + ['kv_norm_g', 'wk', 'wvt']
    in_specs = [pl.BlockSpec((tm, D_MODEL), row), _mod_spec(l, mod_row)] + [_layer_spec(pw[n], l) for n in names]
    args = [h, mod] + [pw[n] for n in names]
    if use_rope:
        cos, sin, rope_tile = rope
        in_specs += [pl.BlockSpec((tm, HEAD_SLAB), lambda i: (rope_tile(i), 0))] * 2
        args += [cos, sin]
    wide = N_HEADS * HEAD_SLAB
    token_outs = lambda ws: ([pl.BlockSpec((tm, w), row) for w, _ in ws],
                             [jax.ShapeDtypeStruct((t, w), dt) for w, dt in ws])
    specs_a, shapes_a = token_outs([(FN_WIDTH, F32), (FN_WIDTH, BF16), (POOL_WIDTH, F32), (POOL_WIDTH, BF16),
                                    (wide, BF16)])
    specs_b, shapes_b = token_outs([(KV_RANK, F32), (QK_ROPE, F32), (ATT_WIDTH, BF16)])
    out_specs = specs_a + [pl.BlockSpec((tm, wide), row), pl.BlockSpec((ATT_WIDTH, tm), lambda i: (0, i))] + specs_b
    out_shape = shapes_a + [jax.ShapeDtypeStruct((t, wide), BF16), jax.ShapeDtypeStruct((ATT_WIDTH, t), BF16)] + shapes_b
    return pl.pallas_call(
        functools.partial(_inproj_kernel, use_rope=use_rope),
        grid=(t // tm,),
        in_specs=in_specs,
        out_specs=out_specs,
        out_shape=out_shape,
        compiler_params=_params(1),
        name="inproj_rope" if use_rope else "inproj",
    )(*args)


def _cache_kv_kernel(ckv_ref, kr_ref, wk_ref, wvt_ref, k_ref, vt_ref):
    _key_value_heads(ckv_ref[...].astype(BF16), kr_ref[...], wk_ref, wvt_ref, k_ref, vt_ref)


def _cache_kv(cache_ckv, cache_kr_slab, pw, l):
    batch, _, past, _ = cache_ckv.shape
    wide = N_HEADS * HEAD_SLAB
    cache_map = lambda b: (b, l, 0, 0)
    return pl.pallas_call(
        _cache_kv_kernel,
        grid=(batch,),
        in_specs=[pl.BlockSpec((None, None, past, KV_RANK), cache_map),
                  pl.BlockSpec((None, None, past, HEAD_SLAB), cache_map),
                  _layer_spec(pw['wk'], l), _layer_spec(pw['wvt'], l)],
        out_specs=[pl.BlockSpec((past, wide), lambda b: (b, 0)), pl.BlockSpec((ATT_WIDTH, past), lambda b: (0, b))],
        out_shape=[jax.ShapeDtypeStruct((batch * past, wide), BF16),
                   jax.ShapeDtypeStruct((ATT_WIDTH, batch * past), BF16)],
        compiler_params=_params(1),
        name="cache_kv",
    )(cache_ckv, cache_kr_slab, pw['wk'], pw['wvt'])


def _hi_lo(m):
    m = np.asarray(m, np.float32)
    hi = m.astype(BF16)
    return hi, (m - hi.astype(np.float32)).astype(BF16)


def _dft_cos_sin(n):
    k = np.arange(n)
    ang = 2.0 * np.pi * ((k[:, None] * k[None, :]) % n) / n
    return np.cos(ang), np.sin(ang)


def _channel_dft_tables():
    c, s = _dft_cos_sin(FN_GC)
    eye = np.eye(FN_WIDTH // FN_GC)
    return np.kron(eye, c), np.kron(eye, s)


def _fourier_direct_kernel(a_ref, sa_ref, cc_hi, cc_lo, ss_hi, ss_lo, m_hi, m_lo, o_ref, *, seq, norm):
    n = a_ref.shape[0] // seq
    a = a_ref[...]
    tc = _dot3_right(a, cc_hi[...], cc_lo[...])
    ts = _dot3_right(a, ss_hi[...], ss_lo[...])
    side_by_side = lambda x: jnp.concatenate([x[i * seq:(i + 1) * seq] for i in range(n)], axis=1)
    f = _dot3_left(m_hi[...], m_lo[...], jnp.concatenate([side_by_side(tc), side_by_side(ts)], axis=0))
    for i in range(n):
        rows = slice(i * seq, (i + 1) * seq)
        f_i = f[:, i * FN_WIDTH:(i + 1) * FN_WIDTH]
        o_ref[rows, :] = ((f_i * norm) * sa_ref[rows, :].astype(F32)).astype(BF16)


def _fourier_direct(a, sa, seq, seqs_per_step):
    t = a.shape[0]
    rows = seq * seqs_per_step
    cc, ss = _channel_dft_tables()
    cl, sl = _dft_cos_sin(seq)
    tables = [x for m in (cc, ss, np.concatenate([cl, -sl], axis=1)) for x in _hi_lo(m)]
    row = lambda i: (i, 0)
    const = lambda i: (0, 0)
    tspecs = [pl.BlockSpec((FN_WIDTH, FN_WIDTH), const)] * 4 + [pl.BlockSpec((seq, 2 * seq), const)] * 2
    return pl.pallas_call(
        functools.partial(_fourier_direct_kernel, seq=seq, norm=float((seq * FN_GC) ** -0.5)),
        grid=(t // rows,),
        in_specs=[pl.BlockSpec((rows, FN_WIDTH), row), pl.BlockSpec((rows, FN_WIDTH), row)] + tspecs,
        out_specs=pl.BlockSpec((rows, FN_WIDTH), row),
        out_shape=jax.ShapeDtypeStruct((t, FN_WIDTH), BF16),
        compiler_params=_params(1),
        name="fourier_direct",
    )(a, sa, *tables)


FFT_R = 64


def _fourier_fft_kernel(a_ref, sa_ref, cc_hi, cc_lo, ss_hi, ss_lo, m1_hi, m1_lo, m2_hi, m2_lo, twc_ref, tws_ref,
                        o_ref, zr0, zr1, zi0, zi1, yr0, yr1, yi0, yi1, *, seq, norm):
    r = FFT_R
    half = FN_WIDTH // 2
    chunk = 512

    def put(refs, rows, x):
        refs[0][rows, :] = x[:, :half]
        refs[1][rows, :] = x[:, half:]

    def get(refs, rows):
        return jnp.concatenate([refs[0][rows, :], refs[1][rows, :]], axis=1)

    zr, zi, yr_s, yi_s = (zr0, zr1), (zi0, zi1), (yr0, yr1), (yi0, yi1)
    for c in range(seq // chunk):
        rows = slice(c * chunk, (c + 1) * chunk)
        a = a_ref[rows, :]
        put(zr, rows, _dot3_right(a, cc_hi[...], cc_lo[...]))
        put(zi, rows, -_dot3_right(a, ss_hi[...], ss_lo[...]))
    for n2 in range(r):
        strided = pl.ds(n2, r, stride=r)
        z = jnp.concatenate([get(zr, strided), get(zi, strided)], axis=0)
        y = _dot3_left(m1_hi[...], m1_lo[...], z)
        yr, yi = y[:r], y[r:]
        blk = slice(n2 * r, (n2 + 1) * r)
        cos = jnp.concatenate([twc_ref[blk, :]] * 2, axis=1)
        sin = jnp.concatenate([tws_ref[blk, :]] * 2, axis=1)
        put(yr_s, blk, yr * cos + yi * sin)
        put(yi_s, blk, yi * cos - yr * sin)
    for k1 in range(r):
        strided = pl.ds(k1, r, stride=r)
        y = jnp.concatenate([get(yr_s, strided), get(yi_s, strided)], axis=0)
        put(zr, strided, _dot3_left(m2_hi[...], m2_lo[...], y))
    for c in range(seq // chunk):
        rows = slice(c * chunk, (c + 1) * chunk)
        o_ref[rows, :] = ((get(zr, rows) * norm) * sa_ref[rows, :].astype(F32)).astype(BF16)


def _fourier_fft(a, sa, seq):
    assert seq == FFT_R * FFT_R
    t = a.shape[0]
    cc, ss = _channel_dft_tables()
    c, s = _dft_cos_sin(FFT_R)
    m1 = np.block([[c, s], [-s, c]])
    m2 = np.concatenate([c, s], axis=1)
    n2 = np.arange(FFT_R)[:, None]
    k1 = np.arange(FFT_R)[None, :]
    ang = (2.0 * np.pi * (n2 * k1) / seq).reshape(seq, 1)
    twc = np.ascontiguousarray(np.broadcast_to(np.cos(ang).astype(np.float32), (seq, 128)))
    tws = np.ascontiguousarray(np.broadcast_to(np.sin(ang).astype(np.float32), (seq, 128)))
    tables = [x for m in (cc, ss, m1, m2) for x in _hi_lo(m)]
    row = lambda i: (i, 0)
    const = lambda i: (0, 0)
    tspecs = ([pl.BlockSpec((FN_WIDTH, FN_WIDTH), const)] * 4 + [pl.BlockSpec((2 * FFT_R, 2 * FFT_R), const)] * 2
              + [pl.BlockSpec((FFT_R, 2 * FFT_R), const)] * 2 + [pl.BlockSpec((seq, 128), const)] * 2)
    return pl.pallas_call(
        functools.partial(_fourier_fft_kernel, seq=seq, norm=float((seq * FN_GC) ** -0.5)),
        grid=(t // seq,),
        in_specs=[pl.BlockSpec((seq, FN_WIDTH), row), pl.BlockSpec((seq, FN_WIDTH), row)] + tspecs,
        out_specs=pl.BlockSpec((seq, FN_WIDTH), row),
        out_shape=jax.ShapeDtypeStruct((t, FN_WIDTH), BF16),
        scratch_shapes=[pltpu.VMEM((seq, FN_WIDTH // 2), F32)] * 8,
        compiler_params=_params(1),
        name="fourier_fft",
    )(a, sa, *tables, twc, tws)


POOL_HALO = 8
POOL_CHUNK = 256


def _pool_kernel(b_ref, sb_ref, pw_ref, ps_ref, o_ref, pad_ref, *, seq):
    zeros = jnp.zeros((POOL_HALO, POOL_WIDTH), F32)
    lane = lax.broadcasted_iota(jnp.int32, (POOL_CHUNK, 128), 1)
    low_group = lane < POOL_GC
    for i in range(pad_ref.shape[0]):
        pad_ref[i, 0:POOL_HALO, :] = zeros
        pad_ref[i, POOL_HALO + seq:, :] = zeros
        pad_ref[i, POOL_HALO:POOL_HALO + seq, :] = b_ref[i * seq:(i + 1) * seq, :]

    for i, c in [(i, c) for i in range(pad_ref.shape[0]) for c in range(seq // POOL_CHUNK)]:
        r0 = c * POOL_CHUNK
        t = lax.broadcasted_iota(jnp.int32, (POOL_CHUNK, 128), 0) + r0

        def inv_count(w):
            left = w // 2
            right = w - 1 - left
            lo = jnp.maximum(t - left, 0)
            hi = jnp.minimum(t + right, seq - 1)
            return 1.0 / (hi - lo + 1).astype(F32)

        def ld(off, col):
            start = POOL_HALO + r0 + off
            return pad_ref[i, start:start + POOL_CHUNK, 128 * col:128 * (col + 1)]

        u0 = ld(0, 0)
        p2 = ld(-1, 0) + u0
        p4 = p2 + ld(-2, 0) + ld(1, 0)
        pooled0 = jnp.where(low_group, p2 * inv_count(2), p4 * inv_count(4)) - u0
        u1 = ld(0, 1)
        p8 = u1
        for off in (-4, -3, -2, -1, 1, 2, 3):
            p8 = p8 + ld(off, 1)
        p16 = p8
        for off in (-8, -7, -6, -5, 4, 5, 6, 7):
            p16 = p16 + ld(off, 1)
        pooled1 = jnp.where(low_group, p8 * inv_count(8), p16 * inv_count(16)) - u1

        pooled = jnp.concatenate([pooled0, pooled1], axis=1).astype(BF16)
        mixed = _dot(pooled, pw_ref[...]) * ps_ref[...]
        rows = slice(i * seq + r0, i * seq + r0 + POOL_CHUNK)
        o_ref[rows, :] = (mixed * sb_ref[rows, :].astype(F32)).astype(BF16)


def _pool(b, sb, pw, l, seq, seqs_per_step):
    t = b.shape[0]
    rows = seq * seqs_per_step
    row = lambda i: (i, 0)
    return pl.pallas_call(
        functools.partial(_pool_kernel, seq=seq),
        grid=(t // rows,),
        in_specs=[pl.BlockSpec((rows, POOL_WIDTH), row), pl.BlockSpec((rows, POOL_WIDTH), row),
                  _layer_spec(pw['pool_w'], l), _layer_spec(pw['pool_scale'], l)],
        out_specs=pl.BlockSpec((rows, POOL_WIDTH), row),
        out_shape=jax.ShapeDtypeStruct((t, POOL_WIDTH), BF16),
        scratch_shapes=[pltpu.VMEM((seqs_per_step, seq + 2 * POOL_HALO, POOL_WIDTH), F32)],
        compiler_params=_params(1),
        name="pool",
    )(b, sb, pw['pool_w'], pw['pool_scale'])


def _attn_kernel(*refs, heads, chunk, use_cache, lookahead):
    if use_cache:
        q_ref, k_ref, vt_ref, kc_ref, vct_ref, sc_ref, o_ref, s_ref, m_ref, l_ref, acc_ref = refs
        sources = ((k_ref, vt_ref), (kc_ref, vct_ref))
    else:
        q_ref, k_ref, vt_ref, sc_ref, o_ref, s_ref, m_ref, l_ref, acc_ref = refs
        sources = ((k_ref, vt_ref),)
    tq = q_ref.shape[0]
    groups = chunk // 8
    chunks = []
    for keys, values in sources:
        for off in range(0, keys.shape[0], chunk):
            chunks.append((keys, values, off, len(chunks) * chunk))

    slots = s_ref.shape[0]

    def scores(h, c):
        keys, _, off, row = chunks[c]
        sl = slice(HEAD_SLAB * h, HEAD_SLAB * (h + 1))
        s = lax.dot_general(keys[off:off + chunk, sl], q_ref[:, sl], _NT, preferred_element_type=F32)
        s_ref[h % slots, row:row + chunk, :] = s
        m_ref[h % slots] = jnp.maximum(m_ref[h % slots], jnp.max(s.reshape(groups, 8, tq), axis=0))

    def weigh(h, c, m):
        _, values, off, row = chunks[c]
        p = jnp.exp2(s_ref[h % slots, row:row + chunk, :] - m)
        l_ref[h % slots] += jnp.sum(p.reshape(groups, 8, tq), axis=0)
        acc_ref[h % slots] += _dot(values[V_DIM * h:V_DIM * (h + 1), off:off + chunk], p.astype(BF16))

    outs = []
    for t in range(heads + lookahead):
        h_w, h_s = t - lookahead, t
        if h_s < heads:
            m_ref[h_s % slots] = jnp.full((8, tq), -jnp.inf, F32)
        if h_w >= 0:
            m = jnp.max(m_ref[h_w % slots], axis=0, keepdims=True)
            l_ref[h_w % slots] = jnp.zeros((8, tq), F32)
            acc_ref[h_w % slots] = jnp.zeros((V_DIM, tq), F32)
        for c in range(len(chunks)):
            if h_w >= 0:
                weigh(h_w, c, m)
            if h_s < heads:
                scores(h_s, c)
        if h_w >= 0:
            denom = jnp.sum(l_ref[h_w % slots], axis=0, keepdims=True)
            outs.append(acc_ref[h_w % slots] * (1.0 / denom))
            if h_w % 2 == 1:
                o_pair = jnp.concatenate(outs, axis=0).T
                outs = []
                sl = slice(HEAD_SLAB * (h_w // 2), HEAD_SLAB * (h_w // 2 + 1))
                o_ref[:, sl] = (o_pair * sc_ref[:, sl].astype(F32)).astype(BF16)


def _attention(q, k, vt, sc, cache, batch, lq, lk, tq, heads_per_step, chunk, lookahead):
    use_cache = cache is not None
    nq = lq // tq
    n_hp = N_HEADS // heads_per_step
    qw = heads_per_step * HEAD_SLAB
    ow = heads_per_step * V_DIM
    q_map = lambda b, g, i: (b * nq + i, g)
    k_map = lambda b, g, i: (b, g)
    vt_map = lambda b, g, i: (g, b)
    in_specs = [pl.BlockSpec((tq, qw), q_map), pl.BlockSpec((lk, qw), k_map), pl.BlockSpec((ow, lk), vt_map)]
    args = [q, k, vt]
    lc = 0
    if use_cache:
        lc = cache[0].shape[0] // batch
        in_specs += [pl.BlockSpec((lc, qw), k_map), pl.BlockSpec((ow, lc), vt_map)]
        args += list(cache)
    in_specs.append(pl.BlockSpec((tq, ow), q_map))
    args.append(sc)
    slots = min(heads_per_step, lookahead + 1)
    return pl.pallas_call(
        functools.partial(_attn_kernel, heads=heads_per_step, chunk=chunk, use_cache=use_cache, lookahead=lookahead),
        grid=(batch, n_hp, nq),
        in_specs=in_specs,
        out_specs=pl.BlockSpec((tq, ow), q_map),
        out_shape=jax.ShapeDtypeStruct((batch * lq, ATT_WIDTH), BF16),
        scratch_shapes=[pltpu.VMEM((slots, lk + lc, tq), F32), pltpu.VMEM((slots, 8, tq), F32),
                        pltpu.VMEM((slots, 8, tq), F32), pltpu.VMEM((slots, V_DIM, tq), F32)],
        compiler_params=_params(3),
        name="attention_cache" if use_cache else "attention",
    )(*args)


def _out_kernel(*refs, final):
    if final:
        h_ref, mod_ref, ng_ref, xa_ref, xb_ref, xc_ref, wa_ref, wb_ref, wc_ref, wg_ref, wo_ref, fg_ref, o_ref = refs
    else:
        h_ref, mod_ref, ng_ref, xa_ref, xb_ref, xc_ref, wa_ref, wb_ref, wc_ref, wg_ref, wo_ref, o_ref = refs
    h = h_ref[...]
    xn = _modulated_norm(h, ng_ref[...], mod_ref).astype(BF16)
    y = None
    for i, (x_ref, w_ref) in enumerate(((xa_ref, wa_ref), (xb_ref, wb_ref), (xc_ref, wc_ref))):
        wg_i = wg_ref[i * D_MODEL:(i + 1) * D_MODEL, :]
        g = jax.nn.sigmoid(lax.dot_general(xn, wg_i, _NT, preferred_element_type=F32))
        term = g * _dot(x_ref[...], w_ref[...])
        y = term if y is None else y + term
    h_new = h + mod_ref[2:3, :] * _dot(y.astype(BF16), wo_ref[...])
    if final:
        o_ref[...] = _rms(h_new, fg_ref[...])
    else:
        o_ref[...] = h_new


def _out(h, mod, mod_row, xa, xb, xc, pw, l, final_g, tm):
    t = h.shape[0]
    final = final_g is not None
    row = lambda i: (i, 0)
    names = ['w_br_a', 'w_br_b', 'w_br_c', 'wg', 'w_out']
    in_specs = ([pl.BlockSpec((tm, D_MODEL), row), _mod_spec(l, mod_row), _layer_spec(pw['norm_g'], l),
                 pl.BlockSpec((tm, FN_WIDTH), row), pl.BlockSpec((tm, POOL_WIDTH), row),
                 pl.BlockSpec((tm, ATT_WIDTH), row)] + [_layer_spec(pw[n], l) for n in names])
    args = [h, mod, pw['norm_g'], xa, xb, xc] + [pw[n] for n in names]
    if final:
        in_specs.append(pl.BlockSpec((1, D_MODEL), lambda i: (0, 0)))
        args.append(final_g)
    return pl.pallas_call(
        functools.partial(_out_kernel, final=final),
        grid=(t // tm,),
        in_specs=in_specs,
        out_specs=pl.BlockSpec((tm, D_MODEL), row),
        out_shape=jax.ShapeDtypeStruct((t, D_MODEL), F32),
        compiler_params=_params(1),
        name="out_final" if final else "out",
    )(*args)


def _rot_cols(w):
    q = QK_ROPE // 4
    return jnp.concatenate([-w[..., q:2 * q], w[..., :q], -w[..., 3 * q:], w[..., 2 * q:3 * q]], axis=-1)


_ROPE_PAD = ((QK_NOPE, HEAD_SLAB - QK_NOPE - QK_ROPE),)
W1_ROW_TILE = 128
WG_ROW_TILE = 512


def _w1_source_row(k):
    direct = _OFF_KV // W1_ROW_TILE
    n_cz = (_OFF_G - _OFF_CZ) // W1_ROW_TILE
    return jnp.where(k < direct, k * W1_ROW_TILE,
                     jnp.where(k < direct + n_cz, _OFF_CZ + (k - direct) * W1_ROW_TILE,
                               jnp.where(k == direct + n_cz, _OFF_KV, _OFF_KR - QK_NOPE)))


def _pack_w1_kernel(w_ref, o_ref):
    k = pl.program_id(1)
    n_plain = W1_WIDTH // W1_ROW_TILE - 2
    x = w_ref[0]

    @pl.when(k < n_plain)
    def _():
        o_ref[...] = x.astype(BF16)

    zeros_lo = jnp.zeros((QK_NOPE, D_MODEL), F32)
    zeros_hi = jnp.zeros((HEAD_SLAB - QK_NOPE - QK_ROPE, D_MODEL), F32)
    kr = x[QK_NOPE:QK_NOPE + QK_ROPE, :]

    @pl.when(k == n_plain)
    def _():
        o_ref[...] = jnp.concatenate([zeros_lo, kr, zeros_hi], axis=0).astype(BF16)

    @pl.when(k == n_plain + 1)
    def _():
        q = QK_ROPE // 4
        rot = jnp.concatenate([-kr[q:2 * q], kr[:q], -kr[3 * q:], kr[2 * q:3 * q]], axis=0)
        o_ref[...] = jnp.concatenate([zeros_lo, rot, zeros_hi], axis=0).astype(BF16)


def _pack_cast_kernel(w_ref, o_ref):
    o_ref[...] = w_ref[0].astype(BF16)


def _pack_w_in(w_in):
    wt = jnp.swapaxes(w_in, 1, 2)
    g_width = w_in.shape[2] - _OFF_G
    w1t = pl.pallas_call(
        _pack_w1_kernel,
        grid=(DEPTH, W1_WIDTH // W1_ROW_TILE),
        in_specs=[pl.BlockSpec((pl.Element(1), pl.Element(W1_ROW_TILE), pl.Element(D_MODEL)),
                               lambda l, k: (l, pl.multiple_of(_w1_source_row(k), 32), 0))],
        out_specs=pl.BlockSpec((None, W1_ROW_TILE, D_MODEL), lambda l, k: (l, k, 0)),
        out_shape=jax.ShapeDtypeStruct((DEPTH, W1_WIDTH, D_MODEL), BF16),
        compiler_params=_params(2),
        name="pack_w1",
    )(wt)
    wgt = pl.pallas_call(
        _pack_cast_kernel,
        grid=(DEPTH, g_width // WG_ROW_TILE),
        in_specs=[pl.BlockSpec((pl.Element(1), pl.Element(WG_ROW_TILE), pl.Element(D_MODEL)),
                               lambda l, k: (l, pl.multiple_of(_OFF_G + k * WG_ROW_TILE, 32), 0))],
        out_specs=pl.BlockSpec((None, WG_ROW_TILE, D_MODEL), lambda l, k: (l, k, 0)),
        out_shape=jax.ShapeDtypeStruct((DEPTH, g_width, D_MODEL), BF16),
        compiler_params=_params(2),
        name="pack_wg",
    )(wt)
    return w1t, wgt


def _pack_weights(norm_g, w_in, pool_w, pool_scale, q_norm_g, w_q_up, kv_norm_g, w_kv_up, w_br_a, w_br_b, w_br_c,
                  w_out):
    lead = ((0, 0), (0, 0))
    w1, wg = _pack_w_in(w_in)
    wide = N_HEADS * HEAD_SLAB
    wq_h = w_q_up.reshape(DEPTH, Q_RANK, N_HEADS, QK_NOPE + QK_ROPE)
    wq = jnp.pad(wq_h, lead + ((0, 0), (0, _ROPE_PAD[0][1]))).reshape(DEPTH, Q_RANK, wide).astype(BF16)
    wq_rot = jnp.pad(_rot_cols(wq_h[..., QK_NOPE:]), lead + ((0, 0),) + _ROPE_PAD)
    wq_rot = wq_rot.reshape(DEPTH, Q_RANK, wide).astype(BF16)
    wkv_h = w_kv_up.reshape(DEPTH, KV_RANK, N_HEADS, QK_NOPE + V_DIM)
    wk = jnp.pad(wkv_h[..., :QK_NOPE], lead + ((0, 0), (0, HEAD_SLAB - QK_NOPE))).reshape(DEPTH, KV_RANK, wide)
    wvt = wkv_h[..., QK_NOPE:].reshape(DEPTH, KV_RANK, ATT_WIDTH).transpose(0, 2, 1)
    groups = len(POOL_WINDOWS)
    eye = jnp.eye(groups, dtype=F32)
    pool_bd = (pool_w[:, :, :, None, :] * eye[None, :, None, :, None]).reshape(DEPTH, POOL_WIDTH, POOL_WIDTH)
    return {
        'norm_g': norm_g.reshape(DEPTH, 1, D_MODEL), 'w1': w1, 'wg': wg,
        'q_norm_g': q_norm_g.reshape(DEPTH, 1, Q_RANK), 'wq': wq, 'wq_rot': wq_rot,
        'kv_norm_g': kv_norm_g.reshape(DEPTH, 1, KV_RANK), 'wk': wk.astype(BF16), 'wvt': wvt.astype(BF16),
        'pool_w': pool_bd.astype(BF16), 'pool_scale': pool_scale.reshape(DEPTH, 1, POOL_WIDTH),
        'w_br_a': w_br_a.astype(BF16), 'w_br_b': w_br_b.astype(BF16), 'w_br_c': w_br_c.astype(BF16),
        'w_out': w_out.astype(BF16),
    }


def _rope_tables(seq):
    f32 = np.float32
    t = np.arange(seq)
    row = (t // GRID_W).astype(f32)
    col = (t % GRID_W).astype(f32)
    half = QK_ROPE // 2
    freqs = f32(ROPE_THETA) ** (-np.arange(0, half, 2, dtype=f32) / f32(half))
    ar = row[:, None] * freqs
    ac = col[:, None] * freqs
    cos = np.ones((seq, HEAD_SLAB), f32)
    sin = np.zeros((seq, HEAD_SLAB), f32)
    cos[:, QK_NOPE:QK_NOPE + QK_ROPE] = np.concatenate([np.cos(ar), np.cos(ar), np.cos(ac), np.cos(ac)], axis=-1)
    sin[:, QK_NOPE:QK_NOPE + QK_ROPE] = np.concatenate([np.sin(ar), np.sin(ar), np.sin(ac), np.sin(ac)], axis=-1)
    return cos, sin


TOKEN_TILE = 512
SAMPLE_Q_TILE = 512
SAMPLE_HEADS_PER_STEP = 4
SAMPLE_KEY_CHUNK = 512
PROMPT_SEQS_PER_STEP = 4


def kernel(x_prompt, x_sample, cache_ckv, cache_krope, c, c_ctx, norm_g, w_mod, b_mod, w_in, pool_w, pool_scale,
           q_norm_g, w_q_up, kv_norm_g, w_kv_up, w_br_a, w_br_b, w_br_c, w_out, final_norm_g):
    batch, seq, _ = x_prompt.shape
    dec_batch, dec_seq, _ = x_sample.shape
    past = cache_ckv.shape[2]
    tm = TOKEN_TILE

    mod_rows = 8
    cvec = jnp.concatenate([c_ctx[None, :], c, jnp.zeros((mod_rows - 1 - dec_batch, D_MODEL), F32)], axis=0)
    mod = _modulation(cvec, w_mod, b_mod).reshape(DEPTH, mod_rows, 3, D_MODEL)
    prompt_row = lambda i: 0
    tiles_per_sample = dec_seq // tm
    assert past % SAMPLE_KEY_CHUNK == 0 and dec_seq % SAMPLE_KEY_CHUNK == 0
    sample_row = lambda i: 1 + i // tiles_per_sample

    cos, sin = _rope_tables(dec_seq)
    rope = (cos, sin, lambda i: i % tiles_per_sample)
    final_g = final_norm_g.reshape(1, D_MODEL)
    pw = _pack_weights(norm_g, w_in, pool_w, pool_scale, q_norm_g, w_q_up, kv_norm_g, w_kv_up, w_br_a, w_br_b,
                       w_br_c, w_out)
    cache_kr_slab = jnp.pad(cache_krope, ((0, 0), (0, 0), (0, 0)) + _ROPE_PAD)

    hp = x_prompt.reshape(batch * seq, D_MODEL)
    hs = x_sample.reshape(dec_batch * dec_seq, D_MODEL)
    ckv_list, kr_list = [], []
    for l in range(DEPTH):
        last = final_g if l == DEPTH - 1 else None

        a_in, sa, b_in, sb, q, k, vt, ckv, kr, sc = _inproj(hp, mod, prompt_row, pw, l, None, tm)
        ckv_list.append(ckv.reshape(batch, seq, KV_RANK))
        kr_list.append(kr.reshape(batch, seq, QK_ROPE))
        xa = _fourier_direct(a_in, sa, seq, PROMPT_SEQS_PER_STEP)
        xb = _pool(b_in, sb, pw, l, seq, PROMPT_SEQS_PER_STEP)
        xc = _attention(q, k, vt, sc, None, batch, seq, seq, seq, N_HEADS, seq, lookahead=N_HEADS)
        hp = _out(hp, mod, prompt_row, xa, xb, xc, pw, l, last, tm)

        a_in, sa, b_in, sb, q, k, vt, _, _, sc = _inproj(hs, mod, sample_row, pw, l, rope, tm)
        cache = _cache_kv(cache_ckv, cache_kr_slab, pw, l)
        xa = _fourier_fft(a_in, sa, dec_seq)
        xb = _pool(b_in, sb, pw, l, dec_seq, 1)
        xc = _attention(q, k, vt, sc, cache, dec_batch, dec_seq, dec_seq, SAMPLE_Q_TILE, SAMPLE_HEADS_PER_STEP,
                        SAMPLE_KEY_CHUNK, lookahead=1)
        hs = _out(hs, mod, sample_row, xa, xb, xc, pw, l, last, tm)

    y_prompt = hp.reshape(batch, seq, D_MODEL)
    y_sample = hs.reshape(dec_batch, dec_seq, D_MODEL)
    return (y_prompt, y_sample, jnp.stack(ckv_list, axis=1), jnp.stack(kr_list, axis=1))
```

```python
import functools

import numpy as np
import jax
import jax.numpy as jnp
from jax import lax
from jax.experimental import pallas as pl
from jax.experimental.pallas import tpu as pltpu

D_MODEL = 1024
DEPTH = 2
GRID_W = 64
EPS = 1e-6
FN_WIDTH = 256
FN_GC = 64
POOL_WINDOWS = (2, 4, 8, 16)
POOL_WIDTH = 256
POOL_GC = 64
N_HEADS = 8
QK_NOPE = 64
QK_ROPE = 32
V_DIM = 64
Q_RANK = 256
KV_RANK = 128
ATT_WIDTH = 512
ROPE_THETA = 10000.0
HEAD_SLAB = 128
QK_SCALE = (QK_NOPE + QK_ROPE) ** -0.5
Q_PRESCALE = QK_SCALE * float(np.log2(np.e))

VMEM_LIMIT_BYTES = 56 * 1024 * 1024

F32 = jnp.float32
BF16 = jnp.bfloat16

_OFF_A, _OFF_B, _OFF_Q, _OFF_KV, _OFF_KR, _OFF_CZ, _OFF_G = 0, 512, 1024, 1280, 1408, 1440, 1952
_W1_A = (0, 512)
_W1_B = (512, 1024)
_W1_Q = (1024, 1280)
_W1_CZ = (1280, 1792)
_W1_KV = (1792, 2176)
W1_WIDTH = 2176


def _params(n_parallel):
    return pltpu.CompilerParams(dimension_semantics=("arbitrary",) * n_parallel,
                                vmem_limit_bytes=VMEM_LIMIT_BYTES)


def _dot(a, b):
    return jnp.dot(a, b, preferred_element_type=F32)


def _silu(x):
    return x * jax.nn.sigmoid(x)


def _rms(x, g):
    r = lax.rsqrt(jnp.mean(x * x, axis=-1, keepdims=True) + EPS)
    return (x * r) * g


def _modulated_norm(h, norm_g, mod_ref):
    shift = mod_ref[0:1, :]
    scale = mod_ref[1:2, :]
    return _rms(h, norm_g) * (1.0 + scale) + shift


def _split_bf16(x):
    hi = x.astype(BF16)
    lo = (x - hi.astype(F32)).astype(BF16)
    return hi, lo


def _dot3_right(x, m_hi, m_lo):
    x_hi, x_lo = _split_bf16(x)
    return _dot(x_hi, m_hi) + _dot(x_lo, m_hi) + _dot(x_hi, m_lo)


def _dot3_left(m_hi, m_lo, x):
    x_hi, x_lo = _split_bf16(x)
    return _dot(m_hi, x_hi) + _dot(m_hi, x_lo) + _dot(m_lo, x_hi)


def _mod_kernel(c_ref, w_ref, b_ref, o_ref):
    s = _silu(c_ref[...]).astype(BF16)
    o_ref[...] = _dot(s, w_ref[...].astype(BF16)) + b_ref[...]


def _modulation(cvec, w_mod, b_mod):
    rows = cvec.shape[0]
    tn = 768
    return pl.pallas_call(
        _mod_kernel,
        grid=(DEPTH, 3 * D_MODEL // tn),
        in_specs=[pl.BlockSpec((rows, D_MODEL), lambda l, j: (0, 0)),
                  pl.BlockSpec((None, D_MODEL, tn), lambda l, j: (l, 0, j)),
                  pl.BlockSpec((None, 1, tn), lambda l, j: (l, 0, j))],
        out_specs=pl.BlockSpec((None, rows, tn), lambda l, j: (l, 0, j)),
        out_shape=jax.ShapeDtypeStruct((DEPTH, rows, 3 * D_MODEL), F32),
        compiler_params=_params(2),
        name="modulation",
    )(cvec, w_mod, b_mod.reshape(DEPTH, 1, 3 * D_MODEL))


_NT = (((1,), (1,)), ((), ()))


def _key_value_heads(ckv_bf16, kr_slab, wk_ref, wvt_ref, k_ref, vt_ref):
    kn = _dot(ckv_bf16, wk_ref[...])
    for h in range(N_HEADS):
        sl = slice(HEAD_SLAB * h, HEAD_SLAB * (h + 1))
        k_ref[:, sl] = (kn[:, sl] + kr_slab).astype(BF16)
    vt_ref[...] = lax.dot_general(wvt_ref[...], ckv_bf16, _NT, preferred_element_type=F32).astype(BF16)


def _inproj_kernel(*refs, use_rope):
    if use_rope:
        (h_ref, mod_ref, ng_ref, w1_ref, qg_ref, wq_ref, wqr_ref, kvg_ref, wk_ref, wvt_ref, cos_ref, sin_ref,
         ain_ref, sa_ref, bin_ref, sb_ref, q_ref, k_ref, vt_ref, ckv_ref, kr_ref, sc_ref) = refs
    else:
        (h_ref, mod_ref, ng_ref, w1_ref, qg_ref, wq_ref, kvg_ref, wk_ref, wvt_ref,
         ain_ref, sa_ref, bin_ref, sb_ref, q_ref, k_ref, vt_ref, ckv_ref, kr_ref, sc_ref) = refs

    xn = _modulated_norm(h_ref[...], ng_ref[...], mod_ref).astype(BF16)

    def proj(cols):
        return lax.dot_general(xn, w1_ref[cols[0]:cols[1], :], _NT, preferred_element_type=F32)

    a = proj(_W1_A)
    ain_ref[...] = a[:, :FN_WIDTH]
    sa_ref[...] = _silu(a[:, FN_WIDTH:]).astype(BF16)
    b = proj(_W1_B)
    bin_ref[...] = b[:, :POOL_WIDTH]
    sb_ref[...] = _silu(b[:, POOL_WIDTH:]).astype(BF16)
    sc_ref[...] = _silu(proj(_W1_CZ)).astype(BF16)

    qn = _rms(proj(_W1_Q), qg_ref[...]).astype(BF16)
    q = _dot(qn, wq_ref[...])
    if use_rope:
        cos = cos_ref[...]
        sin = sin_ref[...]
        qr = _dot(qn, wqr_ref[...])
        for h in range(N_HEADS):
            sl = slice(HEAD_SLAB * h, HEAD_SLAB * (h + 1))
            q_ref[:, sl] = ((q[:, sl] * cos + qr[:, sl] * sin) * Q_PRESCALE).astype(BF16)
    else:
        q_ref[...] = (q * Q_PRESCALE).astype(BF16)

    kv = proj(_W1_KV)
    ckv = _rms(kv[:, :KV_RANK], kvg_ref[...])
    ckv_ref[...] = ckv
    kr = kv[:, KV_RANK:KV_RANK + HEAD_SLAB]
    kr_ref[...] = kr[:, QK_NOPE:QK_NOPE + QK_ROPE]
    if use_rope:
        kr = kr * cos + kv[:, KV_RANK + HEAD_SLAB:] * sin
    _key_value_heads(ckv.astype(BF16), kr, wk_ref, wvt_ref, k_ref, vt_ref)


def _layer_spec(arr, l):
    return pl.BlockSpec((None,) + arr.shape[1:], lambda *_: (l,) + (0,) * (arr.ndim - 1),
                        pipeline_mode=pl.Buffered(1))


def _mod_spec(l, mod_row):
    return pl.BlockSpec((None, None, 3, D_MODEL), lambda i: (l, mod_row(i), 0, 0))


def _inproj(h, mod, mod_row, pw, l, rope, tm):
    t = h.shape[0]
    use_rope = rope is not None
    row = lambda i: (i, 0)
    names = ['norm_g', 'w1', 'q_norm_g', 'wq'] + (['wq_rot'] if use_rope else []) + ['kv_norm_g', 'wk', 'wvt']
    in_specs = [pl.BlockSpec((tm, D_MODEL), row), _mod_spec(l, mod_row)] + [_layer_spec(pw[n], l) for n in names]
    args = [h, mod] + [pw[n] for n in names]
    if use_rope:
        cos, sin, rope_tile = rope
        in_specs += [pl.BlockSpec((tm, HEAD_SLAB), lambda i: (rope_tile(i), 0))] * 2
        args += [cos, sin]
    wide = N_HEADS * HEAD_SLAB
    token_outs = lambda ws: ([pl.BlockSpec((tm, w), row) for w, _ in ws],
                             [jax.ShapeDtypeStruct((t, w), dt) for w, dt in ws])
    specs_a, shapes_a = token_outs([(FN_WIDTH, F32), (FN_WIDTH, BF16), (POOL_WIDTH, F32), (POOL_WIDTH, BF16),
                                    (wide, BF16)])
    specs_b, shapes_b = token_outs([(KV_RANK, F32), (QK_ROPE, F32), (ATT_WIDTH, BF16)])
    out_specs = specs_a + [pl.BlockSpec((tm, wide), row), pl.BlockSpec((ATT_WIDTH, tm), lambda i: (0, i))] + specs_b
    out_shape = shapes_a + [jax.ShapeDtypeStruct((t, wide), BF16), jax.ShapeDtypeStruct((ATT_WIDTH, t), BF16)] + shapes_b
    return pl.pallas_call(
        functools.partial(_inproj_kernel, use_rope=use_rope),
        grid=(t // tm,),
        in_specs=in_specs,
        out_specs=out_specs,
        out_shape=out_shape,
        compiler_params=_params(1),
        name="inproj_rope" if use_rope else "inproj",
    )(*args)


def _cache_kv_kernel(ckv_ref, kr_ref, wk_ref, wvt_ref, k_ref, vt_ref):
    _key_value_heads(ckv_ref[...].astype(BF16), kr_ref[...], wk_ref, wvt_ref, k_ref, vt_ref)


def _cache_kv(cache_ckv, cache_kr_slab, pw, l):
    batch, _, past, _ = cache_ckv.shape
    wide = N_HEADS * HEAD_SLAB
    cache_map = lambda b: (b, l, 0, 0)
    return pl.pallas_call(
        _cache_kv_kernel,
        grid=(batch,),
        in_specs=[pl.BlockSpec((None, None, past, KV_RANK), cache_map),
                  pl.BlockSpec((None, None, past, HEAD_SLAB), cache_map),
                  _layer_spec(pw['wk'], l), _layer_spec(pw['wvt'], l)],
        out_specs=[pl.BlockSpec((past, wide), lambda b: (b, 0)), pl.BlockSpec((ATT_WIDTH, past), lambda b: (0, b))],
        out_shape=[jax.ShapeDtypeStruct((batch * past, wide), BF16),
                   jax.ShapeDtypeStruct((ATT_WIDTH, batch * past), BF16)],
        compiler_params=_params(1),
        name="cache_kv",
    )(cache_ckv, cache_kr_slab, pw['wk'], pw['wvt'])


def _hi_lo(m):
    m = np.asarray(m, np.float32)
    hi = m.astype(BF16)
    return hi, (m - hi.astype(np.float32)).astype(BF16)


def _dft_cos_sin(n):
    k = np.arange(n)
    ang = 2.0 * np.pi * ((k[:, None] * k[None, :]) % n) / n
    return np.cos(ang), np.sin(ang)


def _channel_dft_tables():
    c, s = _dft_cos_sin(FN_GC)
    eye = np.eye(FN_WIDTH // FN_GC)
    return np.kron(eye, c), np.kron(eye, s)


def _fourier_direct_kernel(a_ref, sa_ref, cc_hi, cc_lo, ss_hi, ss_lo, m_hi, m_lo, o_ref, *, seq, norm):
    n = a_ref.shape[0] // seq
    a = a_ref[...]
    tc = _dot3_right(a, cc_hi[...], cc_lo[...])
    ts = _dot3_right(a, ss_hi[...], ss_lo[...])
    side_by_side = lambda x: jnp.concatenate([x[i * seq:(i + 1) * seq] for i in range(n)], axis=1)
    f = _dot3_left(m_hi[...], m_lo[...], jnp.concatenate([side_by_side(tc), side_by_side(ts)], axis=0))
    for i in range(n):
        rows = slice(i * seq, (i + 1) * seq)
        f_i = f[:, i * FN_WIDTH:(i + 1) * FN_WIDTH]
        o_ref[rows, :] = ((f_i * norm) * sa_ref[rows, :].astype(F32)).astype(BF16)


def _fourier_direct(a, sa, seq, seqs_per_step):
    t = a.shape[0]
    rows = seq * seqs_per_step
    cc, ss = _channel_dft_tables()
    cl, sl = _dft_cos_sin(seq)
    tables = [x for m in (cc, ss, np.concatenate([cl, -sl], axis=1)) for x in _hi_lo(m)]
    row = lambda i: (i, 0)
    const = lambda i: (0, 0)
    tspecs = [pl.BlockSpec((FN_WIDTH, FN_WIDTH), const)] * 4 + [pl.BlockSpec((seq, 2 * seq), const)] * 2
    return pl.pallas_call(
        functools.partial(_fourier_direct_kernel, seq=seq, norm=float((seq * FN_GC) ** -0.5)),
        grid=(t // rows,),
        in_specs=[pl.BlockSpec((rows, FN_WIDTH), row), pl.BlockSpec((rows, FN_WIDTH), row)] + tspecs,
        out_specs=pl.BlockSpec((rows, FN_WIDTH), row),
        out_shape=jax.ShapeDtypeStruct((t, FN_WIDTH), BF16),
        compiler_params=_params(1),
        name="fourier_direct",
    )(a, sa, *tables)


FFT_R = 64


def _fourier_fft_kernel(a_ref, sa_ref, cc_hi, cc_lo, ss_hi, ss_lo, m1_hi, m1_lo, m2_hi, m2_lo, twc_ref, tws_ref,
                        o_ref, zr0, zr1, zi0, zi1, yr0, yr1, yi0, yi1, *, seq, norm):
    r = FFT_R
    half = FN_WIDTH // 2
    chunk = 512

    def put(refs, rows, x):
        refs[0][rows, :] = x[:, :half]
        refs[1][rows, :] = x[:, half:]

    def get(refs, rows):
        return jnp.concatenate([refs[0][rows, :], refs[1][rows, :]], axis=1)

    zr, zi, yr_s, yi_s = (zr0, zr1), (zi0, zi1), (yr0, yr1), (yi0, yi1)
    for c in range(seq // chunk):
        rows = slice(c * chunk, (c + 1) * chunk)
        a = a_ref[rows, :]
        put(zr, rows, _dot3_right(a, cc_hi[...], cc_lo[...]))
        put(zi, rows, -_dot3_right(a, ss_hi[...], ss_lo[...]))
    for n2 in range(r):
        strided = pl.ds(n2, r, stride=r)
        z = jnp.concatenate([get(zr, strided), get(zi, strided)], axis=0)
        y = _dot3_left(m1_hi[...], m1_lo[...], z)
        yr, yi = y[:r], y[r:]
        blk = slice(n2 * r, (n2 + 1) * r)
        cos = jnp.concatenate([twc_ref[blk, :]] * 2, axis=1)
        sin = jnp.concatenate([tws_ref[blk, :]] * 2, axis=1)
        put(yr_s, blk, yr * cos + yi * sin)
        put(yi_s, blk, yi * cos - yr * sin)
    for k1 in range(r):
        strided = pl.ds(k1, r, stride=r)
        y = jnp.concatenate([get(yr_s, strided), get(yi_s, strided)], axis=0)
        put(zr, strided, _dot3_left(m2_hi[...], m2_lo[...], y))
    for c in range(seq // chunk):
        rows = slice(c * chunk, (c + 1) * chunk)
        o_ref[rows, :] = ((get(zr, rows) * norm) * sa_ref[rows, :].astype(F32)).astype(BF16)


def _fourier_fft(a, sa, seq):
    assert seq == FFT_R * FFT_R
    t = a.shape[0]
    cc, ss = _channel_dft_tables()
    c, s = _dft_cos_sin(FFT_R)
    m1 = np.block([[c, s], [-s, c]])
    m2 = np.concatenate([c, s], axis=1)
    n2 = np.arange(FFT_R)[:, None]
    k1 = np.arange(FFT_R)[None, :]
    ang = (2.0 * np.pi * (n2 * k1) / seq).reshape(seq, 1)
    twc = np.ascontiguousarray(np.broadcast_to(np.cos(ang).astype(np.float32), (seq, 128)))
    tws = np.ascontiguousarray(np.broadcast_to(np.sin(ang).astype(np.float32), (seq, 128)))
    tables = [x for m in (cc, ss, m1, m2) for x in _hi_lo(m)]
    row = lambda i: (i, 0)
    const = lambda i: (0, 0)
    tspecs = ([pl.BlockSpec((FN_WIDTH, FN_WIDTH), const)] * 4 + [pl.BlockSpec((2 * FFT_R, 2 * FFT_R), const)] * 2
              + [pl.BlockSpec((FFT_R, 2 * FFT_R), const)] * 2 + [pl.BlockSpec((seq, 128), const)] * 2)
    return pl.pallas_call(
        functools.partial(_fourier_fft_kernel, seq=seq, norm=float((seq * FN_GC) ** -0.5)),
        grid=(t // seq,),
        in_specs=[pl.BlockSpec((seq, FN_WIDTH), row), pl.BlockSpec((seq, FN_WIDTH), row)] + tspecs,
        out_specs=pl.BlockSpec((seq, FN_WIDTH), row),
        out_shape=jax.ShapeDtypeStruct((t, FN_WIDTH), BF16),
        scratch_shapes=[pltpu.VMEM((seq, FN_WIDTH // 2), F32)] * 8,
        compiler_params=_params(1),
        name="fourier_fft",
    )(a, sa, *tables, twc, tws)


POOL_HALO = 8
POOL_CHUNK = 256


def _pool_kernel(b_ref, sb_ref, pw_ref, ps_ref, o_ref, pad_ref, *, seq):
    zeros = jnp.zeros((POOL_HALO, POOL_WIDTH), F32)
    lane = lax.broadcasted_iota(jnp.int32, (POOL_CHUNK, 128), 1)
    low_group = lane < POOL_GC
    for i in range(pad_ref.shape[0]):
        pad_ref[i, 0:POOL_HALO, :] = zeros
        pad_ref[i, POOL_HALO + seq:, :] = zeros
        pad_ref[i, POOL_HALO:POOL_HALO + seq, :] = b_ref[i * seq:(i + 1) * seq, :]

    for i, c in [(i, c) for i in range(pad_ref.shape[0]) for c in range(seq // POOL_CHUNK)]:
        r0 = c * POOL_CHUNK
        t = lax.broadcasted_iota(jnp.int32, (POOL_CHUNK, 128), 0) + r0

        def inv_count(w):
            left = w // 2
            right = w - 1 - left
            lo = jnp.maximum(t - left, 0)
            hi = jnp.minimum(t + right, seq - 1)
            return 1.0 / (hi - lo + 1).astype(F32)

        def ld(off, col):
            start = POOL_HALO + r0 + off
            return pad_ref[i, start:start + POOL_CHUNK, 128 * col:128 * (col + 1)]

        u0 = ld(0, 0)
        p2 = ld(-1, 0) + u0
        p4 = p2 + ld(-2, 0) + ld(1, 0)
        pooled0 = jnp.where(low_group, p2 * inv_count(2), p4 * inv_count(4)) - u0
        u1 = ld(0, 1)
        p8 = u1
        for off in (-4, -3, -2, -1, 1, 2, 3):
            p8 = p8 + ld(off, 1)
        p16 = p8
        for off in (-8, -7, -6, -5, 4, 5, 6, 7):
            p16 = p16 + ld(off, 1)
        pooled1 = jnp.where(low_group, p8 * inv_count(8), p16 * inv_count(16)) - u1

        pooled = jnp.concatenate([pooled0, pooled1], axis=1).astype(BF16)
        mixed = _dot(pooled, pw_ref[...]) * ps_ref[...]
        rows = slice(i * seq + r0, i * seq + r0 + POOL_CHUNK)
        o_ref[rows, :] = (mixed * sb_ref[rows, :].astype(F32)).astype(BF16)


def _pool(b, sb, pw, l, seq, seqs_per_step):
    t = b.shape[0]
    rows = seq * seqs_per_step
    row = lambda i: (i, 0)
    return pl.pallas_call(
        functools.partial(_pool_kernel, seq=seq),
        grid=(t // rows,),
        in_specs=[pl.BlockSpec((rows, POOL_WIDTH), row), pl.BlockSpec((rows, POOL_WIDTH), row),
                  _layer_spec(pw['pool_w'], l), _layer_spec(pw['pool_scale'], l)],
        out_specs=pl.BlockSpec((rows, POOL_WIDTH), row),
        out_shape=jax.ShapeDtypeStruct((t, POOL_WIDTH), BF16),
        scratch_shapes=[pltpu.VMEM((seqs_per_step, seq + 2 * POOL_HALO, POOL_WIDTH), F32)],
        compiler_params=_params(1),
        name="pool",
    )(b, sb, pw['pool_w'], pw['pool_scale'])


def _attn_kernel(*refs, heads, chunk, use_cache, lookahead):
    if use_cache:
        q_ref, k_ref, vt_ref, kc_ref, vct_ref, sc_ref, o_ref, s_ref, m_ref, l_ref, acc_ref = refs
        sources = ((k_ref, vt_ref), (kc_ref, vct_ref))
    else:
        q_ref, k_ref, vt_ref, sc_ref, o_ref, s_ref, m_ref, l_ref, acc_ref = refs
        sources = ((k_ref, vt_ref),)
    tq = q_ref.shape[0]
    groups = chunk // 8
    chunks = []
    for keys, values in sources:
        for off in range(0, keys.shape[0], chunk):
            chunks.append((keys, values, off, len(chunks) * chunk))

    slots = s_ref.shape[0]

    def scores(h, c):
        keys, _, off, row = chunks[c]
        sl = slice(HEAD_SLAB * h, HEAD_SLAB * (h + 1))
        s = lax.dot_general(keys[off:off + chunk, sl], q_ref[:, sl], _NT, preferred_element_type=F32)
        s_ref[h % slots, row:row + chunk, :] = s
        m_ref[h % slots] = jnp.maximum(m_ref[h % slots], jnp.max(s.reshape(groups, 8, tq), axis=0))

    def weigh(h, c, m):
        _, values, off, row = chunks[c]
        p = jnp.exp2(s_ref[h % slots, row:row + chunk, :] - m)
        l_ref[h % slots] += jnp.sum(p.reshape(groups, 8, tq), axis=0)
        acc_ref[h % slots] += _dot(values[V_DIM * h:V_DIM * (h + 1), off:off + chunk], p.astype(BF16))

    outs = []
    for t in range(heads + lookahead):
        h_w, h_s = t - lookahead, t
        if h_s < heads:
            m_ref[h_s % slots] = jnp.full((8, tq), -jnp.inf, F32)
        if h_w >= 0:
            m = jnp.max(m_ref[h_w % slots], axis=0, keepdims=True)
            l_ref[h_w % slots] = jnp.zeros((8, tq), F32)
            acc_ref[h_w % slots] = jnp.zeros((V_DIM, tq), F32)
        for c in range(len(chunks)):
            if h_w >= 0:
                weigh(h_w, c, m)
            if h_s < heads:
                scores(h_s, c)
        if h_w >= 0:
            denom = jnp.sum(l_ref[h_w % slots], axis=0, keepdims=True)
            outs.append(acc_ref[h_w % slots] * (1.0 / denom))
            if h_w % 2 == 1:
                o_pair = jnp.concatenate(outs, axis=0).T
                outs = []
                sl = slice(HEAD_SLAB * (h_w // 2), HEAD_SLAB * (h_w // 2 + 1))
                o_ref[:, sl] = (o_pair * sc_ref[:, sl].astype(F32)).astype(BF16)


def _attention(q, k, vt, sc, cache, batch, lq, lk, tq, heads_per_step, chunk, lookahead):
    use_cache = cache is not None
    nq = lq // tq
    n_hp = N_HEADS // heads_per_step
    qw = heads_per_step * HEAD_SLAB
    ow = heads_per_step * V_DIM
    q_map = lambda b, g, i: (b * nq + i, g)
    k_map = lambda b, g, i: (b, g)
    vt_map = lambda b, g, i: (g, b)
    in_specs = [pl.BlockSpec((tq, qw), q_map), pl.BlockSpec((lk, qw), k_map), pl.BlockSpec((ow, lk), vt_map)]
    args = [q, k, vt]
    lc = 0
    if use_cache:
        lc = cache[0].shape[0] // batch
        in_specs += [pl.BlockSpec((lc, qw), k_map), pl.BlockSpec((ow, lc), vt_map)]
        args += list(cache)
    in_specs.append(pl.BlockSpec((tq, ow), q_map))
    args.append(sc)
    slots = min(heads_per_step, lookahead + 1)
    return pl.pallas_call(
        functools.partial(_attn_kernel, heads=heads_per_step, chunk=chunk, use_cache=use_cache, lookahead=lookahead),
        grid=(batch, n_hp, nq),
        in_specs=in_specs,
        out_specs=pl.BlockSpec((tq, ow), q_map),
        out_shape=jax.ShapeDtypeStruct((batch * lq, ATT_WIDTH), BF16),
        scratch_shapes=[pltpu.VMEM((slots, lk + lc, tq), F32), pltpu.VMEM((slots, 8, tq), F32),
                        pltpu.VMEM((slots, 8, tq), F32), pltpu.VMEM((slots, V_DIM, tq), F32)],
        compiler_params=_params(3),
        name="attention_cache" if use_cache else "attention",
    )(*args)


def _out_kernel(*refs, final):
    if final:
        h_ref, mod_ref, ng_ref, xa_ref, xb_ref, xc_ref, wa_ref, wb_ref, wc_ref, wg_ref, wo_ref, fg_ref, o_ref = refs
    else:
        h_ref, mod_ref, ng_ref, xa_ref, xb_ref, xc_ref, wa_ref, wb_ref, wc_ref, wg_ref, wo_ref, o_ref = refs
    h = h_ref[...]
    xn = _modulated_norm(h, ng_ref[...], mod_ref).astype(BF16)
    y = None
    for i, (x_ref, w_ref) in enumerate(((xa_ref, wa_ref), (xb_ref, wb_ref), (xc_ref, wc_ref))):
        wg_i = wg_ref[i * D_MODEL:(i + 1) * D_MODEL, :]
        g = jax.nn.sigmoid(lax.dot_general(xn, wg_i, _NT, preferred_element_type=F32))
        term = g * _dot(x_ref[...], w_ref[...])
        y = term if y is None else y + term
    h_new = h + mod_ref[2:3, :] * _dot(y.astype(BF16), wo_ref[...])
    if final:
        o_ref[...] = _rms(h_new, fg_ref[...])
    else:
        o_ref[...] = h_new


def _out(h, mod, mod_row, xa, xb, xc, pw, l, final_g, tm):
    t = h.shape[0]
    final = final_g is not None
    row = lambda i: (i, 0)
    names = ['w_br_a', 'w_br_b', 'w_br_c', 'wg', 'w_out']
    in_specs = ([pl.BlockSpec((tm, D_MODEL), row), _mod_spec(l, mod_row), _layer_spec(pw['norm_g'], l),
                 pl.BlockSpec((tm, FN_WIDTH), row), pl.BlockSpec((tm, POOL_WIDTH), row),
                 pl.BlockSpec((tm, ATT_WIDTH), row)] + [_layer_spec(pw[n], l) for n in names])
    args = [h, mod, pw['norm_g'], xa, xb, xc] + [pw[n] for n in names]
    if final:
        in_specs.append(pl.BlockSpec((1, D_MODEL), lambda i: (0, 0)))
        args.append(final_g)
    return pl.pallas_call(
        functools.partial(_out_kernel, final=final),
        grid=(t // tm,),
        in_specs=in_specs,
        out_specs=pl.BlockSpec((tm, D_MODEL), row),
        out_shape=jax.ShapeDtypeStruct((t, D_MODEL), F32),
        compiler_params=_params(1),
        name="out_final" if final else "out",
    )(*args)


def _rot_cols(w):
    q = QK_ROPE // 4
    return jnp.concatenate([-w[..., q:2 * q], w[..., :q], -w[..., 3 * q:], w[..., 2 * q:3 * q]], axis=-1)


_ROPE_PAD = ((QK_NOPE, HEAD_SLAB - QK_NOPE - QK_ROPE),)
W1_ROW_TILE = 256
W1_PLAIN_TILES = _W1_KV[0] // W1_ROW_TILE
W1_PACKED_ROWS = (W1_PLAIN_TILES + 2) * W1_ROW_TILE
WG_ROW_TILE = 512


def _w1_source_row(k):
    direct = _OFF_KV // W1_ROW_TILE
    return jnp.where(k < direct, k * W1_ROW_TILE,
                     jnp.where(k < W1_PLAIN_TILES, _OFF_CZ + (k - direct) * W1_ROW_TILE, _OFF_KV))


def _pack_w1_kernel(w_ref, o_ref):
    k = pl.program_id(1)
    x = w_ref[0]

    @pl.when(k < W1_PLAIN_TILES)
    def _():
        o_ref[...] = x.astype(BF16)

    zeros_lo = jnp.zeros((QK_NOPE, D_MODEL), F32)
    zeros_hi = jnp.zeros((HEAD_SLAB - QK_NOPE - QK_ROPE, D_MODEL), F32)
    kr = x[KV_RANK:KV_RANK + QK_ROPE, :]

    @pl.when(k == W1_PLAIN_TILES)
    def _():
        o_ref[...] = jnp.concatenate([x[:KV_RANK], zeros_lo, kr, zeros_hi], axis=0).astype(BF16)

    @pl.when(k == W1_PLAIN_TILES + 1)
    def _():
        q = QK_ROPE // 4
        rot = jnp.concatenate([-kr[q:2 * q], kr[:q], -kr[3 * q:], kr[2 * q:3 * q]], axis=0)
        filler = jnp.zeros((W1_ROW_TILE - HEAD_SLAB, D_MODEL), F32)
        o_ref[...] = jnp.concatenate([zeros_lo, rot, zeros_hi, filler], axis=0).astype(BF16)


def _pack_cast_kernel(w_ref, o_ref):
    o_ref[...] = w_ref[0].astype(BF16)


def _pack_w_in(w_in):
    wt = jnp.swapaxes(w_in, 1, 2)
    g_width = w_in.shape[2] - _OFF_G
    w1t = pl.pallas_call(
        _pack_w1_kernel,
        grid=(DEPTH, W1_PACKED_ROWS // W1_ROW_TILE),
        in_specs=[pl.BlockSpec((pl.Element(1), pl.Element(W1_ROW_TILE), pl.Element(D_MODEL)),
                               lambda l, k: (l, pl.multiple_of(_w1_source_row(k), 32), 0))],
        out_specs=pl.BlockSpec((None, W1_ROW_TILE, D_MODEL), lambda l, k: (l, k, 0)),
        out_shape=jax.ShapeDtypeStruct((DEPTH, W1_PACKED_ROWS, D_MODEL), BF16),
        compiler_params=_params(2),
        name="pack_w1",
    )(wt)
    wgt = pl.pallas_call(
        _pack_cast_kernel,
        grid=(DEPTH, g_width // WG_ROW_TILE),
        in_specs=[pl.BlockSpec((pl.Element(1), pl.Element(WG_ROW_TILE), pl.Element(D_MODEL)),
                               lambda l, k: (l, pl.multiple_of(_OFF_G + k * WG_ROW_TILE, 32), 0))],
        out_specs=pl.BlockSpec((None, WG_ROW_TILE, D_MODEL), lambda l, k: (l, k, 0)),
        out_shape=jax.ShapeDtypeStruct((DEPTH, g_width, D_MODEL), BF16),
        compiler_params=_params(2),
        name="pack_wg",
    )(wt)
    return w1t, wgt


def _pack_weights(norm_g, w_in, pool_w, pool_scale, q_norm_g, w_q_up, kv_norm_g, w_kv_up, w_br_a, w_br_b, w_br_c,
                  w_out):
    lead = ((0, 0), (0, 0))
    w1, wg = _pack_w_in(w_in)
    wide = N_HEADS * HEAD_SLAB
    wq_h = w_q_up.reshape(DEPTH, Q_RANK, N_HEADS, QK_NOPE + QK_ROPE)
    wq = jnp.pad(wq_h, lead + ((0, 0), (0, _ROPE_PAD[0][1]))).reshape(DEPTH, Q_RANK, wide).astype(BF16)
    wq_rot = jnp.pad(_rot_cols(wq_h[..., QK_NOPE:]), lead + ((0, 0),) + _ROPE_PAD)
    wq_rot = wq_rot.reshape(DEPTH, Q_RANK, wide).astype(BF16)
    wkv_h = w_kv_up.reshape(DEPTH, KV_RANK, N_HEADS, QK_NOPE + V_DIM)
    wk = jnp.pad(wkv_h[..., :QK_NOPE], lead + ((0, 0), (0, HEAD_SLAB - QK_NOPE))).reshape(DEPTH, KV_RANK, wide)
    wvt = wkv_h[..., QK_NOPE:].reshape(DEPTH, KV_RANK, ATT_WIDTH).transpose(0, 2, 1)
    groups = len(POOL_WINDOWS)
    eye = jnp.eye(groups, dtype=F32)
    pool_bd = (pool_w[:, :, :, None, :] * eye[None, :, None, :, None]).reshape(DEPTH, POOL_WIDTH, POOL_WIDTH)
    return {
        'norm_g': norm_g.reshape(DEPTH, 1, D_MODEL), 'w1': w1, 'wg': wg,
        'q_norm_g': q_norm_g.reshape(DEPTH, 1, Q_RANK), 'wq': wq, 'wq_rot': wq_rot,
        'kv_norm_g': kv_norm_g.reshape(DEPTH, 1, KV_RANK), 'wk': wk.astype(BF16), 'wvt': wvt.astype(BF16),
        'pool_w': pool_bd.astype(BF16), 'pool_scale': pool_scale.reshape(DEPTH, 1, POOL_WIDTH),
        'w_br_a': w_br_a.astype(BF16), 'w_br_b': w_br_b.astype(BF16), 'w_br_c': w_br_c.astype(BF16),
        'w_out': w_out.astype(BF16),
    }


def _rope_tables(seq):
    f32 = np.float32
    t = np.arange(seq)
    row = (t // GRID_W).astype(f32)
    col = (t % GRID_W).astype(f32)
    half = QK_ROPE // 2
    freqs = f32(ROPE_THETA) ** (-np.arange(0, half, 2, dtype=f32) / f32(half))
    ar = row[:, None] * freqs
    ac = col[:, None] * freqs
    cos = np.ones((seq, HEAD_SLAB), f32)
    sin = np.zeros((seq, HEAD_SLAB), f32)
    cos[:, QK_NOPE:QK_NOPE + QK_ROPE] = np.concatenate([np.cos(ar), np.cos(ar), np.cos(ac), np.cos(ac)], axis=-1)
    sin[:, QK_NOPE:QK_NOPE + QK_ROPE] = np.concatenate([np.sin(ar), np.sin(ar), np.sin(ac), np.sin(ac)], axis=-1)
    return cos, sin


TOKEN_TILE = 1024
SAMPLE_Q_TILE = 512
SAMPLE_HEADS_PER_STEP = 4
SAMPLE_KEY_CHUNK = 512
PROMPT_SEQS_PER_STEP = 4


def kernel(x_prompt, x_sample, cache_ckv, cache_krope, c, c_ctx, norm_g, w_mod, b_mod, w_in, pool_w, pool_scale,
           q_norm_g, w_q_up, kv_norm_g, w_kv_up, w_br_a, w_br_b, w_br_c, w_out, final_norm_g):
    batch, seq, _ = x_prompt.shape
    dec_batch, dec_seq, _ = x_sample.shape
    past = cache_ckv.shape[2]
    tm = TOKEN_TILE

    mod_rows = 8
    cvec = jnp.concatenate([c_ctx[None, :], c, jnp.zeros((mod_rows - 1 - dec_batch, D_MODEL), F32)], axis=0)
    mod = _modulation(cvec, w_mod, b_mod).reshape(DEPTH, mod_rows, 3, D_MODEL)
    prompt_row = lambda i: 0
    tiles_per_sample = dec_seq // tm
    assert past % SAMPLE_KEY_CHUNK == 0 and dec_seq % SAMPLE_KEY_CHUNK == 0
    sample_row = lambda i: 1 + i // tiles_per_sample

    cos, sin = _rope_tables(dec_seq)
    rope = (cos, sin, lambda i: i % tiles_per_sample)
    final_g = final_norm_g.reshape(1, D_MODEL)
    pw = _pack_weights(norm_g, w_in, pool_w, pool_scale, q_norm_g, w_q_up, kv_norm_g, w_kv_up, w_br_a, w_br_b,
                       w_br_c, w_out)
    cache_kr_slab = jnp.pad(cache_krope, ((0, 0), (0, 0), (0, 0)) + _ROPE_PAD)

    hp = x_prompt.reshape(batch * seq, D_MODEL)
    hs = x_sample.reshape(dec_batch * dec_seq, D_MODEL)
    ckv_list, kr_list = [], []
    for l in range(DEPTH):
        last = final_g if l == DEPTH - 1 else None

        a_in, sa, b_in, sb, q, k, vt, ckv, kr, sc = _inproj(hp, mod, prompt_row, pw, l, None, tm)
        ckv_list.append(ckv.reshape(batch, seq, KV_RANK))
        kr_list.append(kr.reshape(batch, seq, QK_ROPE))
        xa = _fourier_direct(a_in, sa, seq, PROMPT_SEQS_PER_STEP)
        xb = _pool(b_in, sb, pw, l, seq, PROMPT_SEQS_PER_STEP)
        xc = _attention(q, k, vt, sc, None, batch, seq, seq, seq, N_HEADS, seq, lookahead=N_HEADS)
        hp = _out(hp, mod, prompt_row, xa, xb, xc, pw, l, last, tm)

        a_in, sa, b_in, sb, q, k, vt, _, _, sc = _inproj(hs, mod, sample_row, pw, l, rope, tm)
        cache = _cache_kv(cache_ckv, cache_kr_slab, pw, l)
        xa = _fourier_fft(a_in, sa, dec_seq)
        xb = _pool(b_in, sb, pw, l, dec_seq, 1)
        xc = _attention(q, k, vt, sc, cache, dec_batch, dec_seq, dec_seq, SAMPLE_Q_TILE, SAMPLE_HEADS_PER_STEP,
                        SAMPLE_KEY_CHUNK, lookahead=1)
        hs = _out(hs, mod, sample_row, xa, xb, xc, pw, l, last, tm)

    y_prompt = hp.reshape(batch, seq, D_MODEL)
    y_sample = hs.reshape(dec_batch, dec_seq, D_MODEL)
    return (y_prompt, y_sample, jnp.stack(ckv_list, axis=1), jnp.stack(kr_list, axis=1))
```

```python
import functools

import numpy as np
import jax
import jax.numpy as jnp
from jax import lax
from jax.experimental import pallas as pl
from jax.experimental.pallas import tpu as pltpu

D_MODEL = 1024
DEPTH = 2
GRID_W = 64
EPS = 1e-6
FN_WIDTH = 256
FN_GC = 64
POOL_WINDOWS = (2, 4, 8, 16)
POOL_WIDTH = 256
POOL_GC = 64
N_HEADS = 8
QK_NOPE = 64
QK_ROPE = 32
V_DIM = 64
Q_RANK = 256
KV_RANK = 128
ATT_WIDTH = 512
ROPE_THETA = 10000.0
HEAD_SLAB = 128
QK_SCALE = (QK_NOPE + QK_ROPE) ** -0.5
Q_PRESCALE = QK_SCALE * float(np.log2(np.e))

VMEM_LIMIT_BYTES = 56 * 1024 * 1024

F32 = jnp.float32
BF16 = jnp.bfloat16

_OFF_A, _OFF_B, _OFF_Q, _OFF_KV, _OFF_KR, _OFF_CZ, _OFF_G = 0, 512, 1024, 1280, 1408, 1440, 1952
_W1_A = (0, 512)
_W1_B = (512, 1024)
_W1_Q = (1024, 1280)
_W1_CZ = (1280, 1792)
_W1_KV = (1792, 2176)
W1_WIDTH = 2176


def _params(n_parallel):
    return pltpu.CompilerParams(dimension_semantics=("arbitrary",) * n_parallel,
                                vmem_limit_bytes=VMEM_LIMIT_BYTES)


def _dot(a, b):
    return jnp.dot(a, b, preferred_element_type=F32)


def _silu(x):
    return x * jax.nn.sigmoid(x)


def _rms(x, g):
    r = lax.rsqrt(jnp.mean(x * x, axis=-1, keepdims=True) + EPS)
    return (x * r) * g


def _modulated_norm(h, norm_g, mod_ref):
    shift = mod_ref[0:1, :]
    scale = mod_ref[1:2, :]
    return _rms(h, norm_g) * (1.0 + scale) + shift


def _split_bf16(x):
    hi = x.astype(BF16)
    lo = (x - hi.astype(F32)).astype(BF16)
    return hi, lo


def _dot3_right(x, m_hi, m_lo):
    x_hi, x_lo = _split_bf16(x)
    return _dot(x_hi, m_hi) + _dot(x_lo, m_hi) + _dot(x_hi, m_lo)


def _dot3_left(m_hi, m_lo, x):
    x_hi, x_lo = _split_bf16(x)
    return _dot(m_hi, x_hi) + _dot(m_hi, x_lo) + _dot(m_lo, x_hi)


def _mod_kernel(c_ref, w_ref, b_ref, o_ref):
    s = _silu(c_ref[...]).astype(BF16)
    o_ref[...] = _dot(s, w_ref[...].astype(BF16)) + b_ref[...]


def _modulation(cvec, w_mod, b_mod):
    rows = cvec.shape[0]
    tn = 768
    return pl.pallas_call(
        _mod_kernel,
        grid=(DEPTH, 3 * D_MODEL // tn),
        in_specs=[pl.BlockSpec((rows, D_MODEL), lambda l, j: (0, 0)),
                  pl.BlockSpec((None, D_MODEL, tn), lambda l, j: (l, 0, j)),
                  pl.BlockSpec((None, 1, tn), lambda l, j: (l, 0, j))],
        out_specs=pl.BlockSpec((None, rows, tn), lambda l, j: (l, 0, j)),
        out_shape=jax.ShapeDtypeStruct((DEPTH, rows, 3 * D_MODEL), F32),
        compiler_params=_params(2),
        name="modulation",
    )(cvec, w_mod, b_mod.reshape(DEPTH, 1, 3 * D_MODEL))


_NT = (((1,), (1,)), ((), ()))


def _key_value_heads(ckv_bf16, kr_slab, wk_ref, wvt_ref, k_ref, vt_ref):
    kn = _dot(ckv_bf16, wk_ref[...])
    for h in range(N_HEADS):
        sl = slice(HEAD_SLAB * h, HEAD_SLAB * (h + 1))
        k_ref[:, sl] = (kn[:, sl] + kr_slab).astype(BF16)
    vt_ref[...] = lax.dot_general(wvt_ref[...], ckv_bf16, _NT, preferred_element_type=F32).astype(BF16)


def _inproj_kernel(*refs, use_rope):
    if use_rope:
        (h_ref, mod_ref, ng_ref, w1_ref, qg_ref, wq_ref, wqr_ref, kvg_ref, wk_ref, wvt_ref, cos_ref, sin_ref,
         ain_ref, sa_ref, bin_ref, sb_ref, q_ref, k_ref, vt_ref, ckv_ref, kr_ref, sc_ref) = refs
    else:
        (h_ref, mod_ref, ng_ref, w1_ref, qg_ref, wq_ref, kvg_ref, wk_ref, wvt_ref,
         ain_ref, sa_ref, bin_ref, sb_ref, q_ref, k_ref, vt_ref, ckv_ref, kr_ref, sc_ref) = refs

    xn = _modulated_norm(h_ref[...], ng_ref[...], mod_ref).astype(BF16)

    def proj(cols):
        return lax.dot_general(xn, w1_ref[cols[0]:cols[1], :], _NT, preferred_element_type=F32)

    a = proj(_W1_A)
    ain_ref[...] = a[:, :FN_WIDTH]
    sa_ref[...] = _silu(a[:, FN_WIDTH:]).astype(BF16)
    b = proj(_W1_B)
    bin_ref[...] = b[:, :POOL_WIDTH]
    sb_ref[...] = _silu(b[:, POOL_WIDTH:]).astype(BF16)
    sc_ref[...] = _silu(proj(_W1_CZ)).astype(BF16)

    qn = _rms(proj(_W1_Q), qg_ref[...]).astype(BF16)
    q = _dot(qn, wq_ref[...])
    if use_rope:
        cos = cos_ref[...]
        sin = sin_ref[...]
        qr = _dot(qn, wqr_ref[...])
        for h in range(N_HEADS):
            sl = slice(HEAD_SLAB * h, HEAD_SLAB * (h + 1))
            q_ref[:, sl] = ((q[:, sl] * cos + qr[:, sl] * sin) * Q_PRESCALE).astype(BF16)
    else:
        q_ref[...] = (q * Q_PRESCALE).astype(BF16)

    kv = proj(_W1_KV)
    ckv = _rms(kv[:, :KV_RANK], kvg_ref[...])
    ckv_ref[...] = ckv
    kr = kv[:, KV_RANK:KV_RANK + HEAD_SLAB]
    kr_ref[...] = kr[:, QK_NOPE:QK_NOPE + QK_ROPE]
    if use_rope:
        kr = kr * cos + kv[:, KV_RANK + HEAD_SLAB:] * sin
    _key_value_heads(ckv.astype(BF16), kr, wk_ref, wvt_ref, k_ref, vt_ref)


def _layer_spec(arr, l):
    return pl.BlockSpec((None,) + arr.shape[1:], lambda *_: (l,) + (0,) * (arr.ndim - 1),
                        pipeline_mode=pl.Buffered(1))


def _mod_spec(l, mod_row):
    return pl.BlockSpec((None, None, 3, D_MODEL), lambda i: (l, mod_row(i), 0, 0))


def _inproj(h, mod, mod_row, pw, l, rope, tm):
    t = h.shape[0]
    use_rope = rope is not None
    row = lambda i: (i, 0)
    names = ['norm_g', 'w1', 'q_norm_g', 'wq'] + (['wq_rot'] if use_rope else []) + ['kv_norm_g', 'wk', 'wvt']
    in_specs = [pl.BlockSpec((tm, D_MODEL), row), _mod_spec(l, mod_row)] + [_layer_spec(pw[n], l) for n in names]
    args = [h, mod] + [pw[n] for n in names]
    if use_rope:
        cos, sin, rope_tile = rope
        in_specs += [pl.BlockSpec((tm, HEAD_SLAB), lambda i: (rope_tile(i), 0))] * 2
        args += [cos, sin]
    wide = N_HEADS * HEAD_SLAB
    token_outs = lambda ws: ([pl.BlockSpec((tm, w), row) for w, _ in ws],
                             [jax.ShapeDtypeStruct((t, w), dt) for w, dt in ws])
    specs_a, shapes_a = token_outs([(FN_WIDTH, F32), (FN_WIDTH, BF16), (POOL_WIDTH, F32), (POOL_WIDTH, BF16),
                                    (wide, BF16)])
    specs_b, shapes_b = token_outs([(KV_RANK, F32), (QK_ROPE, F32), (ATT_WIDTH, BF16)])
    out_specs = specs_a + [pl.BlockSpec((tm, wide), row), pl.BlockSpec((ATT_WIDTH, tm), lambda i: (0, i))] + specs_b
    out_shape = shapes_a + [jax.ShapeDtypeStruct((t, wide), BF16), jax.ShapeDtypeStruct((ATT_WIDTH, t), BF16)] + shapes_b
    return pl.pallas_call(
        functools.partial(_inproj_kernel, use_rope=use_rope),
        grid=(t // tm,),
        in_specs=in_specs,
        out_specs=out_specs,
        out_shape=out_shape,
        compiler_params=_params(1),
        name="inproj_rope" if use_rope else "inproj",
    )(*args)


def _cache_kv_kernel(ckv_ref, kr_ref, wk_ref, wvt_ref, k_ref, vt_ref):
    _key_value_heads(ckv_ref[...].astype(BF16), kr_ref[...], wk_ref, wvt_ref, k_ref, vt_ref)


def _cache_kv(cache_ckv, cache_kr_slab, pw, l):
    batch, _, past, _ = cache_ckv.shape
    wide = N_HEADS * HEAD_SLAB
    cache_map = lambda b: (b, l, 0, 0)
    return pl.pallas_call(
        _cache_kv_kernel,
        grid=(batch,),
        in_specs=[pl.BlockSpec((None, None, past, KV_RANK), cache_map),
                  pl.BlockSpec((None, None, past, HEAD_SLAB), cache_map),
                  _layer_spec(pw['wk'], l), _layer_spec(pw['wvt'], l)],
        out_specs=[pl.BlockSpec((past, wide), lambda b: (b, 0)), pl.BlockSpec((ATT_WIDTH, past), lambda b: (0, b))],
        out_shape=[jax.ShapeDtypeStruct((batch * past, wide), BF16),
                   jax.ShapeDtypeStruct((ATT_WIDTH, batch * past), BF16)],
        compiler_params=_params(1),
        name="cache_kv",
    )(cache_ckv, cache_kr_slab, pw['wk'], pw['wvt'])


def _hi_lo(m):
    m = np.asarray(m, np.float32)
    hi = m.astype(BF16)
    return hi, (m - hi.astype(np.float32)).astype(BF16)


def _dft_cos_sin(n):
    k = np.arange(n)
    ang = 2.0 * np.pi * ((k[:, None] * k[None, :]) % n) / n
    return np.cos(ang), np.sin(ang)


def _channel_dft_tables():
    c, s = _dft_cos_sin(FN_GC)
    eye = np.eye(FN_WIDTH // FN_GC)
    return np.kron(eye, c), np.kron(eye, s)


def _fourier_direct_kernel(a_ref, sa_ref, cc_hi, cc_lo, ss_hi, ss_lo, m_hi, m_lo, o_ref, *, seq, norm):
    n = a_ref.shape[0] // seq
    a = a_ref[...]
    tc = _dot3_right(a, cc_hi[...], cc_lo[...])
    ts = _dot3_right(a, ss_hi[...], ss_lo[...])
    side_by_side = lambda x: jnp.concatenate([x[i * seq:(i + 1) * seq] for i in range(n)], axis=1)
    f = _dot3_left(m_hi[...], m_lo[...], jnp.concatenate([side_by_side(tc), side_by_side(ts)], axis=0))
    for i in range(n):
        rows = slice(i * seq, (i + 1) * seq)
        f_i = f[:, i * FN_WIDTH:(i + 1) * FN_WIDTH]
        o_ref[rows, :] = ((f_i * norm) * sa_ref[rows, :].astype(F32)).astype(BF16)


def _fourier_direct(a, sa, seq, seqs_per_step):
    t = a.shape[0]
    rows = seq * seqs_per_step
    cc, ss = _channel_dft_tables()
    cl, sl = _dft_cos_sin(seq)
    tables = [x for m in (cc, ss, np.concatenate([cl, -sl], axis=1)) for x in _hi_lo(m)]
    row = lambda i: (i, 0)
    const = lambda i: (0, 0)
    tspecs = [pl.BlockSpec((FN_WIDTH, FN_WIDTH), const)] * 4 + [pl.BlockSpec((seq, 2 * seq), const)] * 2
    return pl.pallas_call(
        functools.partial(_fourier_direct_kernel, seq=seq, norm=float((seq * FN_GC) ** -0.5)),
        grid=(t // rows,),
        in_specs=[pl.BlockSpec((rows, FN_WIDTH), row), pl.BlockSpec((rows, FN_WIDTH), row)] + tspecs,
        out_specs=pl.BlockSpec((rows, FN_WIDTH), row),
        out_shape=jax.ShapeDtypeStruct((t, FN_WIDTH), BF16),
        compiler_params=_params(1),
        name="fourier_direct",
    )(a, sa, *tables)


FFT_R = 64
FFT_PITCH = FFT_R + 4


def _fourier_fft_kernel(a_ref, sa_ref, cc_hi, cc_lo, ss_hi, ss_lo, m1_hi, m1_lo, m2_hi, m2_lo, twc_ref, tws_ref,
                        o_ref, zr0, zr1, zi0, zi1, yr0, yr1, yi0, yi1, *, seq, norm):
    r = FFT_R
    half = FN_WIDTH // 2
    chunk = 512

    def put(refs, rows, x):
        refs[0][rows, :] = x[:, :half]
        refs[1][rows, :] = x[:, half:]

    def get(refs, rows):
        return jnp.concatenate([refs[0][rows, :], refs[1][rows, :]], axis=1)

    block = lambda j: slice(FFT_PITCH * j, FFT_PITCH * j + r)
    across = lambda i: pl.ds(i, r, stride=FFT_PITCH)

    zr, zi, yr_s, yi_s = (zr0, zr1), (zi0, zi1), (yr0, yr1), (yi0, yi1)
    for c in range(seq // chunk):
        a = a_ref[c * chunk:(c + 1) * chunk, :]
        zr_c = _dot3_right(a, cc_hi[...], cc_lo[...])
        zi_c = -_dot3_right(a, ss_hi[...], ss_lo[...])
        for j in range(chunk // r):
            n1 = c * (chunk // r) + j
            put(zr, block(n1), zr_c[j * r:(j + 1) * r])
            put(zi, block(n1), zi_c[j * r:(j + 1) * r])
    for n2 in range(r):
        z = jnp.concatenate([get(zr, across(n2)), get(zi, across(n2))], axis=0)
        y = _dot3_left(m1_hi[...], m1_lo[...], z)
        yr, yi = y[:r], y[r:]
        tw = slice(n2 * r, (n2 + 1) * r)
        cos = jnp.concatenate([twc_ref[tw, :]] * 2, axis=1)
        sin = jnp.concatenate([tws_ref[tw, :]] * 2, axis=1)
        put(yr_s, block(n2), yr * cos + yi * sin)
        put(yi_s, block(n2), yi * cos - yr * sin)
    for k1 in range(r):
        y = jnp.concatenate([get(yr_s, across(k1)), get(yi_s, across(k1))], axis=0)
        put(zr, across(k1), _dot3_left(m2_hi[...], m2_lo[...], y))
    for k2 in range(r):
        rows = slice(k2 * r, (k2 + 1) * r)
        o_ref[rows, :] = ((get(zr, block(k2)) * norm) * sa_ref[rows, :].astype(F32)).astype(BF16)


def _fourier_fft(a, sa, seq):
    assert seq == FFT_R * FFT_R
    t = a.shape[0]
    cc, ss = _channel_dft_tables()
    c, s = _dft_cos_sin(FFT_R)
    m1 = np.block([[c, s], [-s, c]])
    m2 = np.concatenate([c, s], axis=1)
    n2 = np.arange(FFT_R)[:, None]
    k1 = np.arange(FFT_R)[None, :]
    ang = (2.0 * np.pi * (n2 * k1) / seq).reshape(seq, 1)
    twc = np.ascontiguousarray(np.broadcast_to(np.cos(ang).astype(np.float32), (seq, 128)))
    tws = np.ascontiguousarray(np.broadcast_to(np.sin(ang).astype(np.float32), (seq, 128)))
    tables = [x for m in (cc, ss, m1, m2) for x in _hi_lo(m)]
    row = lambda i: (i, 0)
    const = lambda i: (0, 0)
    tspecs = ([pl.BlockSpec((FN_WIDTH, FN_WIDTH), const)] * 4 + [pl.BlockSpec((2 * FFT_R, 2 * FFT_R), const)] * 2
              + [pl.BlockSpec((FFT_R, 2 * FFT_R), const)] * 2 + [pl.BlockSpec((seq, 128), const)] * 2)
    return pl.pallas_call(
        functools.partial(_fourier_fft_kernel, seq=seq, norm=float((seq * FN_GC) ** -0.5)),
        grid=(t // seq,),
        in_specs=[pl.BlockSpec((seq, FN_WIDTH), row), pl.BlockSpec((seq, FN_WIDTH), row)] + tspecs,
        out_specs=pl.BlockSpec((seq, FN_WIDTH), row),
        out_shape=jax.ShapeDtypeStruct((t, FN_WIDTH), BF16),
        scratch_shapes=[pltpu.VMEM((FFT_PITCH * FFT_R, FN_WIDTH // 2), F32)] * 8,
        compiler_params=_params(1),
        name="fourier_fft",
    )(a, sa, *tables, twc, tws)


POOL_HALO = 8
POOL_CHUNK = 256


def _pool_kernel(b_ref, sb_ref, pw_ref, ps_ref, o_ref, pad_ref, *, seq):
    zeros = jnp.zeros((POOL_HALO, POOL_WIDTH), F32)
    lane = lax.broadcasted_iota(jnp.int32, (POOL_CHUNK, 128), 1)
    low_group = lane < POOL_GC
    for i in range(pad_ref.shape[0]):
        pad_ref[i, 0:POOL_HALO, :] = zeros
        pad_ref[i, POOL_HALO + seq:, :] = zeros
        pad_ref[i, POOL_HALO:POOL_HALO + seq, :] = b_ref[i * seq:(i + 1) * seq, :]

    for i, c in [(i, c) for i in range(pad_ref.shape[0]) for c in range(seq // POOL_CHUNK)]:
        r0 = c * POOL_CHUNK
        t = lax.broadcasted_iota(jnp.int32, (POOL_CHUNK, 128), 0) + r0

        def inv_count(w):
            left = w // 2
            right = w - 1 - left
            lo = jnp.maximum(t - left, 0)
            hi = jnp.minimum(t + right, seq - 1)
            return 1.0 / (hi - lo + 1).astype(F32)

        def ld(off, col):
            start = POOL_HALO + r0 + off
            return pad_ref[i, start:start + POOL_CHUNK, 128 * col:128 * (col + 1)]

        u0 = ld(0, 0)
        p2 = ld(-1, 0) + u0
        p4 = p2 + ld(-2, 0) + ld(1, 0)
        pooled0 = jnp.where(low_group, p2 * inv_count(2), p4 * inv_count(4)) - u0
        u1 = ld(0, 1)
        p8 = u1
        for off in (-4, -3, -2, -1, 1, 2, 3):
            p8 = p8 + ld(off, 1)
        p16 = p8
        for off in (-8, -7, -6, -5, 4, 5, 6, 7):
            p16 = p16 + ld(off, 1)
        pooled1 = jnp.where(low_group, p8 * inv_count(8), p16 * inv_count(16)) - u1

        pooled = jnp.concatenate([pooled0, pooled1], axis=1).astype(BF16)
        mixed = _dot(pooled, pw_ref[...]) * ps_ref[...]
        rows = slice(i * seq + r0, i * seq + r0 + POOL_CHUNK)
        o_ref[rows, :] = (mixed * sb_ref[rows, :].astype(F32)).astype(BF16)


def _pool(b, sb, pw, l, seq, seqs_per_step):
    t = b.shape[0]
    rows = seq * seqs_per_step
    row = lambda i: (i, 0)
    return pl.pallas_call(
        functools.partial(_pool_kernel, seq=seq),
        grid=(t // rows,),
        in_specs=[pl.BlockSpec((rows, POOL_WIDTH), row), pl.BlockSpec((rows, POOL_WIDTH), row),
                  _layer_spec(pw['pool_w'], l), _layer_spec(pw['pool_scale'], l)],
        out_specs=pl.BlockSpec((rows, POOL_WIDTH), row),
        out_shape=jax.ShapeDtypeStruct((t, POOL_WIDTH), BF16),
        scratch_shapes=[pltpu.VMEM((seqs_per_step, seq + 2 * POOL_HALO, POOL_WIDTH), F32)],
        compiler_params=_params(1),
        name="pool",
    )(b, sb, pw['pool_w'], pw['pool_scale'])


def _attn_kernel(*refs, heads, chunk, use_cache, lookahead):
    if use_cache:
        q_ref, k_ref, vt_ref, kc_ref, vct_ref, sc_ref, o_ref, s_ref, m_ref, l_ref, acc_ref = refs
        sources = ((k_ref, vt_ref), (kc_ref, vct_ref))
    else:
        q_ref, k_ref, vt_ref, sc_ref, o_ref, s_ref, m_ref, l_ref, acc_ref = refs
        sources = ((k_ref, vt_ref),)
    tq = q_ref.shape[0]
    groups = chunk // 8
    chunks = []
    for keys, values in sources:
        for off in range(0, keys.shape[0], chunk):
            chunks.append((keys, values, off, len(chunks) * chunk))

    slots = s_ref.shape[0]

    def scores(h, c):
        keys, _, off, row = chunks[c]
        sl = slice(HEAD_SLAB * h, HEAD_SLAB * (h + 1))
        s = lax.dot_general(keys[off:off + chunk, sl], q_ref[:, sl], _NT, preferred_element_type=F32)
        s_ref[h % slots, row:row + chunk, :] = s
        m_ref[h % slots] = jnp.maximum(m_ref[h % slots], jnp.max(s.reshape(groups, 8, tq), axis=0))

    def weigh(h, c, m):
        _, values, off, row = chunks[c]
        p = jnp.exp2(s_ref[h % slots, row:row + chunk, :] - m)
        l_ref[h % slots] += jnp.sum(p.reshape(groups, 8, tq), axis=0)
        acc_ref[h % slots] += _dot(values[V_DIM * h:V_DIM * (h + 1), off:off + chunk], p.astype(BF16))

    outs = []
    for t in range(heads + lookahead):
        h_w, h_s = t - lookahead, t
        if h_s < heads:
            m_ref[h_s % slots] = jnp.full((8, tq), -jnp.inf, F32)
        if h_w >= 0:
            m = jnp.max(m_ref[h_w % slots], axis=0, keepdims=True)
            l_ref[h_w % slots] = jnp.zeros((8, tq), F32)
            acc_ref[h_w % slots] = jnp.zeros((V_DIM, tq), F32)
        for c in range(len(chunks)):
            if h_w >= 0:
                weigh(h_w, c, m)
            if h_s < heads:
                scores(h_s, c)
        if h_w >= 0:
            denom = jnp.sum(l_ref[h_w % slots], axis=0, keepdims=True)
            outs.append(acc_ref[h_w % slots] * (1.0 / denom))
            if h_w % 2 == 1:
                o_pair = jnp.concatenate(outs, axis=0).T
                outs = []
                sl = slice(HEAD_SLAB * (h_w // 2), HEAD_SLAB * (h_w // 2 + 1))
                o_ref[:, sl] = (o_pair * sc_ref[:, sl].astype(F32)).astype(BF16)


def _attention(q, k, vt, sc, cache, batch, lq, lk, tq, heads_per_step, chunk, lookahead):
    use_cache = cache is not None
    nq = lq // tq
    n_hp = N_HEADS // heads_per_step
    qw = heads_per_step * HEAD_SLAB
    ow = heads_per_step * V_DIM
    q_map = lambda b, g, i: (b * nq + i, g)
    k_map = lambda b, g, i: (b, g)
    vt_map = lambda b, g, i: (g, b)
    in_specs = [pl.BlockSpec((tq, qw), q_map), pl.BlockSpec((lk, qw), k_map), pl.BlockSpec((ow, lk), vt_map)]
    args = [q, k, vt]
    lc = 0
    if use_cache:
        lc = cache[0].shape[0] // batch
        in_specs += [pl.BlockSpec((lc, qw), k_map), pl.BlockSpec((ow, lc), vt_map)]
        args += list(cache)
    in_specs.append(pl.BlockSpec((tq, ow), q_map))
    args.append(sc)
    slots = min(heads_per_step, lookahead + 1)
    return pl.pallas_call(
        functools.partial(_attn_kernel, heads=heads_per_step, chunk=chunk, use_cache=use_cache, lookahead=lookahead),
        grid=(batch, n_hp, nq),
        in_specs=in_specs,
        out_specs=pl.BlockSpec((tq, ow), q_map),
        out_shape=jax.ShapeDtypeStruct((batch * lq, ATT_WIDTH), BF16),
        scratch_shapes=[pltpu.VMEM((slots, lk + lc, tq), F32), pltpu.VMEM((slots, 8, tq), F32),
                        pltpu.VMEM((slots, 8, tq), F32), pltpu.VMEM((slots, V_DIM, tq), F32)],
        compiler_params=_params(3),
        name="attention_cache" if use_cache else "attention",
    )(*args)


def _out_kernel(*refs, final):
    if final:
        h_ref, mod_ref, ng_ref, xa_ref, xb_ref, xc_ref, wa_ref, wb_ref, wc_ref, wg_ref, wo_ref, fg_ref, o_ref = refs
    else:
        h_ref, mod_ref, ng_ref, xa_ref, xb_ref, xc_ref, wa_ref, wb_ref, wc_ref, wg_ref, wo_ref, o_ref = refs
    h = h_ref[...]
    xn = _modulated_norm(h, ng_ref[...], mod_ref).astype(BF16)
    y = None
    for i, (x_ref, w_ref) in enumerate(((xa_ref, wa_ref), (xb_ref, wb_ref), (xc_ref, wc_ref))):
        wg_i = wg_ref[i * D_MODEL:(i + 1) * D_MODEL, :]
        g = jax.nn.sigmoid(lax.dot_general(xn, wg_i, _NT, preferred_element_type=F32))
        term = g * _dot(x_ref[...], w_ref[...])
        y = term if y is None else y + term
    h_new = h + mod_ref[2:3, :] * _dot(y.astype(BF16), wo_ref[...])
    if final:
        o_ref[...] = _rms(h_new, fg_ref[...])
    else:
        o_ref[...] = h_new


def _out(h, mod, mod_row, xa, xb, xc, pw, l, final_g, tm):
    t = h.shape[0]
    final = final_g is not None
    row = lambda i: (i, 0)
    names = ['w_br_a', 'w_br_b', 'w_br_c', 'wg', 'w_out']
    in_specs = ([pl.BlockSpec((tm, D_MODEL), row), _mod_spec(l, mod_row), _layer_spec(pw['norm_g'], l),
                 pl.BlockSpec((tm, FN_WIDTH), row), pl.BlockSpec((tm, POOL_WIDTH), row),
                 pl.BlockSpec((tm, ATT_WIDTH), row)] + [_layer_spec(pw[n], l) for n in names])
    args = [h, mod, pw['norm_g'], xa, xb, xc] + [pw[n] for n in names]
    if final:
        in_specs.append(pl.BlockSpec((1, D_MODEL), lambda i: (0, 0)))
        args.append(final_g)
    return pl.pallas_call(
        functools.partial(_out_kernel, final=final),
        grid=(t // tm,),
        in_specs=in_specs,
        out_specs=pl.BlockSpec((tm, D_MODEL), row),
        out_shape=jax.ShapeDtypeStruct((t, D_MODEL), F32),
        compiler_params=_params(1),
        name="out_final" if final else "out",
    )(*args)


def _rot_cols(w):
    q = QK_ROPE // 4
    return jnp.concatenate([-w[..., q:2 * q], w[..., :q], -w[..., 3 * q:], w[..., 2 * q:3 * q]], axis=-1)


_ROPE_PAD = ((QK_NOPE, HEAD_SLAB - QK_NOPE - QK_ROPE),)
W1_ROW_TILE = 256
W1_PLAIN_TILES = _W1_KV[0] // W1_ROW_TILE
W1_PACKED_ROWS = (W1_PLAIN_TILES + 2) * W1_ROW_TILE
WG_ROW_TILE = 512


def _w1_source_row(k):
    direct = _OFF_KV // W1_ROW_TILE
    return jnp.where(k < direct, k * W1_ROW_TILE,
                     jnp.where(k < W1_PLAIN_TILES, _OFF_CZ + (k - direct) * W1_ROW_TILE, _OFF_KV))


def _pack_w1_kernel(w_ref, o_ref):
    k = pl.program_id(1)
    x = w_ref[0]

    @pl.when(k < W1_PLAIN_TILES)
    def _():
        o_ref[...] = x.astype(BF16)

    zeros_lo = jnp.zeros((QK_NOPE, D_MODEL), F32)
    zeros_hi = jnp.zeros((HEAD_SLAB - QK_NOPE - QK_ROPE, D_MODEL), F32)
    kr = x[KV_RANK:KV_RANK + QK_ROPE, :]

    @pl.when(k == W1_PLAIN_TILES)
    def _():
        o_ref[...] = jnp.concatenate([x[:KV_RANK], zeros_lo, kr, zeros_hi], axis=0).astype(BF16)

    @pl.when(k == W1_PLAIN_TILES + 1)
    def _():
        q = QK_ROPE // 4
        rot = jnp.concatenate([-kr[q:2 * q], kr[:q], -kr[3 * q:], kr[2 * q:3 * q]], axis=0)
        filler = jnp.zeros((W1_ROW_TILE - HEAD_SLAB, D_MODEL), F32)
        o_ref[...] = jnp.concatenate([zeros_lo, rot, zeros_hi, filler], axis=0).astype(BF16)


def _pack_cast_kernel(w_ref, o_ref):
    o_ref[...] = w_ref[0].astype(BF16)


def _pack_w_in(w_in):
    wt = jnp.swapaxes(w_in, 1, 2)
    g_width = w_in.shape[2] - _OFF_G
    w1t = pl.pallas_call(
        _pack_w1_kernel,
        grid=(DEPTH, W1_PACKED_ROWS // W1_ROW_TILE),
        in_specs=[pl.BlockSpec((pl.Element(1), pl.Element(W1_ROW_TILE), pl.Element(D_MODEL)),
                               lambda l, k: (l, pl.multiple_of(_w1_source_row(k), 32), 0))],
        out_specs=pl.BlockSpec((None, W1_ROW_TILE, D_MODEL), lambda l, k: (l, k, 0)),
        out_shape=jax.ShapeDtypeStruct((DEPTH, W1_PACKED_ROWS, D_MODEL), BF16),
        compiler_params=_params(2),
        name="pack_w1",
    )(wt)
    wgt = pl.pallas_call(
        _pack_cast_kernel,
        grid=(DEPTH, g_width // WG_ROW_TILE),
        in_specs=[pl.BlockSpec((pl.Element(1), pl.Element(WG_ROW_TILE), pl.Element(D_MODEL)),
                               lambda l, k: (l, pl.multiple_of(_OFF_G + k * WG_ROW_TILE, 32), 0))],
        out_specs=pl.BlockSpec((None, WG_ROW_TILE, D_MODEL), lambda l, k: (l, k, 0)),
        out_shape=jax.ShapeDtypeStruct((DEPTH, g_width, D_MODEL), BF16),
        compiler_params=_params(2),
        name="pack_wg",
    )(wt)
    return w1t, wgt


def _pack_weights(norm_g, w_in, pool_w, pool_scale, q_norm_g, w_q_up, kv_norm_g, w_kv_up, w_br_a, w_br_b, w_br_c,
                  w_out):
    lead = ((0, 0), (0, 0))
    w1, wg = _pack_w_in(w_in)
    wide = N_HEADS * HEAD_SLAB
    wq_h = w_q_up.reshape(DEPTH, Q_RANK, N_HEADS, QK_NOPE + QK_ROPE)
    wq = jnp.pad(wq_h, lead + ((0, 0), (0, _ROPE_PAD[0][1]))).reshape(DEPTH, Q_RANK, wide).astype(BF16)
    wq_rot = jnp.pad(_rot_cols(wq_h[..., QK_NOPE:]), lead + ((0, 0),) + _ROPE_PAD)
    wq_rot = wq_rot.reshape(DEPTH, Q_RANK, wide).astype(BF16)
    wkv_h = w_kv_up.reshape(DEPTH, KV_RANK, N_HEADS, QK_NOPE + V_DIM)
    wk = jnp.pad(wkv_h[..., :QK_NOPE], lead + ((0, 0), (0, HEAD_SLAB - QK_NOPE))).reshape(DEPTH, KV_RANK, wide)
    wvt = wkv_h[..., QK_NOPE:].reshape(DEPTH, KV_RANK, ATT_WIDTH).transpose(0, 2, 1)
    groups = len(POOL_WINDOWS)
    eye = jnp.eye(groups, dtype=F32)
    pool_bd = (pool_w[:, :, :, None, :] * eye[None, :, None, :, None]).reshape(DEPTH, POOL_WIDTH, POOL_WIDTH)
    return {
        'norm_g': norm_g.reshape(DEPTH, 1, D_MODEL), 'w1': w1, 'wg': wg,
        'q_norm_g': q_norm_g.reshape(DEPTH, 1, Q_RANK), 'wq': wq, 'wq_rot': wq_rot,
        'kv_norm_g': kv_norm_g.reshape(DEPTH, 1, KV_RANK), 'wk': wk.astype(BF16), 'wvt': wvt.astype(BF16),
        'pool_w': pool_bd.astype(BF16), 'pool_scale': pool_scale.reshape(DEPTH, 1, POOL_WIDTH),
        'w_br_a': w_br_a.astype(BF16), 'w_br_b': w_br_b.astype(BF16), 'w_br_c': w_br_c.astype(BF16),
        'w_out': w_out.astype(BF16),
    }


def _rope_tables(seq):
    f32 = np.float32
    t = np.arange(seq)
    row = (t // GRID_W).astype(f32)
    col = (t % GRID_W).astype(f32)
    half = QK_ROPE // 2
    freqs = f32(ROPE_THETA) ** (-np.arange(0, half, 2, dtype=f32) / f32(half))
    ar = row[:, None] * freqs
    ac = col[:, None] * freqs
    cos = np.ones((seq, HEAD_SLAB), f32)
    sin = np.zeros((seq, HEAD_SLAB), f32)
    cos[:, QK_NOPE:QK_NOPE + QK_ROPE] = np.concatenate([np.cos(ar), np.cos(ar), np.cos(ac), np.cos(ac)], axis=-1)
    sin[:, QK_NOPE:QK_NOPE + QK_ROPE] = np.concatenate([np.sin(ar), np.sin(ar), np.sin(ac), np.sin(ac)], axis=-1)
    return cos, sin


TOKEN_TILE = 1024
SAMPLE_Q_TILE = 512
SAMPLE_HEADS_PER_STEP = 4
SAMPLE_KEY_CHUNK = 512
PROMPT_SEQS_PER_STEP = 4


def kernel(x_prompt, x_sample, cache_ckv, cache_krope, c, c_ctx, norm_g, w_mod, b_mod, w_in, pool_w, pool_scale,
           q_norm_g, w_q_up, kv_norm_g, w_kv_up, w_br_a, w_br_b, w_br_c, w_out, final_norm_g):
    batch, seq, _ = x_prompt.shape
    dec_batch, dec_seq, _ = x_sample.shape
    past = cache_ckv.shape[2]
    tm = TOKEN_TILE

    mod_rows = 8
    cvec = jnp.concatenate([c_ctx[None, :], c, jnp.zeros((mod_rows - 1 - dec_batch, D_MODEL), F32)], axis=0)
    mod = _modulation(cvec, w_mod, b_mod).reshape(DEPTH, mod_rows, 3, D_MODEL)
    prompt_row = lambda i: 0
    tiles_per_sample = dec_seq // tm
    assert past % SAMPLE_KEY_CHUNK == 0 and dec_seq % SAMPLE_KEY_CHUNK == 0
    sample_row = lambda i: 1 + i // tiles_per_sample

    cos, sin = _rope_tables(dec_seq)
    rope = (cos, sin, lambda i: i % tiles_per_sample)
    final_g = final_norm_g.reshape(1, D_MODEL)
    pw = _pack_weights(norm_g, w_in, pool_w, pool_scale, q_norm_g, w_q_up, kv_norm_g, w_kv_up, w_br_a, w_br_b,
                       w_br_c, w_out)
    cache_kr_slab = jnp.pad(cache_krope, ((0, 0), (0, 0), (0, 0)) + _ROPE_PAD)

    hp = x_prompt.reshape(batch * seq, D_MODEL)
    hs = x_sample.reshape(dec_batch * dec_seq, D_MODEL)
    ckv_list, kr_list = [], []
    for l in range(DEPTH):
        last = final_g if l == DEPTH - 1 else None

        a_in, sa, b_in, sb, q, k, vt, ckv, kr, sc = _inproj(hp, mod, prompt_row, pw, l, None, tm)
        ckv_list.append(ckv.reshape(batch, seq, KV_RANK))
        kr_list.append(kr.reshape(batch, seq, QK_ROPE))
        xa = _fourier_direct(a_in, sa, seq, PROMPT_SEQS_PER_STEP)
        xb = _pool(b_in, sb, pw, l, seq, PROMPT_SEQS_PER_STEP)
        xc = _attention(q, k, vt, sc, None, batch, seq, seq, seq, N_HEADS, seq, lookahead=N_HEADS)
        hp = _out(hp, mod, prompt_row, xa, xb, xc, pw, l, last, tm)

        a_in, sa, b_in, sb, q, k, vt, _, _, sc = _inproj(hs, mod, sample_row, pw, l, rope, tm)
        cache = _cache_kv(cache_ckv, cache_kr_slab, pw, l)
        xa = _fourier_fft(a_in, sa, dec_seq)
        xb = _pool(b_in, sb, pw, l, dec_seq, 1)
        xc = _attention(q, k, vt, sc, cache, dec_batch, dec_seq, dec_seq, SAMPLE_Q_TILE, SAMPLE_HEADS_PER_STEP,
                        SAMPLE_KEY_CHUNK, lookahead=1)
        hs = _out(hs, mod, sample_row, xa, xb, xc, pw, l, last, tm)

    y_prompt = hp.reshape(batch, seq, D_MODEL)
    y_sample = hs.reshape(dec_batch, dec_seq, D_MODEL)
    return (y_prompt, y_sample, jnp.stack(ckv_list, axis=1), jnp.stack(kr_list, axis=1))
```

```python
import functools

import numpy as np
import jax
import jax.numpy as jnp
from jax import lax
from jax.experimental import pallas as pl
from jax.experimental.pallas import tpu as pltpu

D_MODEL = 1024
DEPTH = 2
GRID_W = 64
EPS = 1e-6
FN_WIDTH = 256
FN_GC = 64
POOL_WINDOWS = (2, 4, 8, 16)
POOL_WIDTH = 256
POOL_GC = 64
N_HEADS = 8
QK_NOPE = 64
QK_ROPE = 32
V_DIM = 64
Q_RANK = 256
KV_RANK = 128
ATT_WIDTH = 512
ROPE_THETA = 10000.0
HEAD_SLAB = 128
QK_SCALE = (QK_NOPE + QK_ROPE) ** -0.5
Q_PRESCALE = QK_SCALE * float(np.log2(np.e))

VMEM_LIMIT_BYTES = 56 * 1024 * 1024

F32 = jnp.float32
BF16 = jnp.bfloat16

_OFF_A, _OFF_B, _OFF_Q, _OFF_KV, _OFF_KR, _OFF_CZ, _OFF_G = 0, 512, 1024, 1280, 1408, 1440, 1952
_W1_A = (0, 512)
_W1_B = (512, 1024)
_W1_Q = (1024, 1280)
_W1_CZ = (1280, 1792)
_W1_KV = (1792, 2048)
W1_WIDTH = 2048


def _params(n_parallel):
    return pltpu.CompilerParams(dimension_semantics=("arbitrary",) * n_parallel,
                                vmem_limit_bytes=VMEM_LIMIT_BYTES)


def _dot(a, b):
    return jnp.dot(a, b, preferred_element_type=F32)


def _silu(x):
    return x * jax.nn.sigmoid(x)


def _rms(x, g):
    r = lax.rsqrt(jnp.mean(x * x, axis=-1, keepdims=True) + EPS)
    return (x * r) * g


def _modulated_norm(h, norm_g, mod_ref):
    shift = mod_ref[0:1, :]
    scale = mod_ref[1:2, :]
    return _rms(h, norm_g) * (1.0 + scale) + shift


def _split_bf16(x):
    hi = x.astype(BF16)
    lo = (x - hi.astype(F32)).astype(BF16)
    return hi, lo


def _dot3_right(x, m_hi, m_lo):
    x_hi, x_lo = _split_bf16(x)
    return _dot(x_hi, m_hi) + _dot(x_lo, m_hi) + _dot(x_hi, m_lo)


def _dot3_left(m_hi, m_lo, x):
    x_hi, x_lo = _split_bf16(x)
    return _dot(m_hi, x_hi) + _dot(m_hi, x_lo) + _dot(m_lo, x_hi)


def _mod_kernel(c_ref, w_ref, b_ref, o_ref):
    s = _silu(c_ref[...]).astype(BF16)
    o_ref[...] = _dot(s, w_ref[...].astype(BF16)) + b_ref[...]


def _modulation(cvec, w_mod, b_mod):
    rows = cvec.shape[0]
    tn = 768
    return pl.pallas_call(
        _mod_kernel,
        grid=(DEPTH, 3 * D_MODEL // tn),
        in_specs=[pl.BlockSpec((rows, D_MODEL), lambda l, j: (0, 0)),
                  pl.BlockSpec((None, D_MODEL, tn), lambda l, j: (l, 0, j)),
                  pl.BlockSpec((None, 1, tn), lambda l, j: (l, 0, j))],
        out_specs=pl.BlockSpec((None, rows, tn), lambda l, j: (l, 0, j)),
        out_shape=jax.ShapeDtypeStruct((DEPTH, rows, 3 * D_MODEL), F32),
        compiler_params=_params(2),
        name="modulation",
    )(cvec, w_mod, b_mod.reshape(DEPTH, 1, 3 * D_MODEL))


_NT = (((1,), (1,)), ((), ()))


def _key_value_heads(ckv_bf16, kr_slab, wk_ref, wvt_ref, k_ref, vt_ref):
    kn = _dot(ckv_bf16, wk_ref[...])
    for h in range(N_HEADS):
        sl = slice(HEAD_SLAB * h, HEAD_SLAB * (h + 1))
        k_ref[:, sl] = (kn[:, sl] + kr_slab).astype(BF16)
    vt_ref[...] = lax.dot_general(wvt_ref[...], ckv_bf16, _NT, preferred_element_type=F32).astype(BF16)


def _inproj_kernel(*refs, use_rope):
    if use_rope:
        (h_ref, mod_ref, ng_ref, w1_ref, qg_ref, wq_ref, kvg_ref, wk_ref, wvt_ref, cos_ref, sin_ref,
         ain_ref, sa_ref, bin_ref, sb_ref, q_ref, k_ref, vt_ref, ckv_ref, kr_ref, sc_ref) = refs
    else:
        (h_ref, mod_ref, ng_ref, w1_ref, qg_ref, wq_ref, kvg_ref, wk_ref, wvt_ref,
         ain_ref, sa_ref, bin_ref, sb_ref, q_ref, k_ref, vt_ref, ckv_ref, kr_ref, sc_ref) = refs

    xn = _modulated_norm(h_ref[...], ng_ref[...], mod_ref).astype(BF16)

    def proj(cols):
        return lax.dot_general(xn, w1_ref[cols[0]:cols[1], :], _NT, preferred_element_type=F32)

    a = proj(_W1_A)
    ain_ref[...] = a[:, :FN_WIDTH]
    sa_ref[...] = _silu(a[:, FN_WIDTH:]).astype(BF16)
    b = proj(_W1_B)
    bin_ref[...] = b[:, :POOL_WIDTH]
    sb_ref[...] = _silu(b[:, POOL_WIDTH:]).astype(BF16)
    sc_ref[...] = _silu(proj(_W1_CZ)).astype(BF16)

    qn = _rms(proj(_W1_Q), qg_ref[...]).astype(BF16)
    q = _dot(qn, wq_ref[...])
    lane = lax.broadcasted_iota(jnp.int32, (q.shape[0], HEAD_SLAB), 1)
    if use_rope:
        cos = cos_ref[...]
        sin = sin_ref[...]
        takes_upper = (lane % (QK_ROPE // 2)) < (QK_ROPE // 4)
        for h in range(N_HEADS):
            sl = slice(HEAD_SLAB * h, HEAD_SLAB * (h + 1))
            q_h = q[:, sl]
            upper = pltpu.roll(q_h, HEAD_SLAB - QK_ROPE // 4, axis=1)
            lower = pltpu.roll(q_h, QK_ROPE // 4, axis=1)
            q_rot = jnp.where(takes_upper, -upper, lower)
            q_ref[:, sl] = ((q_h * cos + q_rot * sin) * Q_PRESCALE).astype(BF16)
    else:
        q_ref[...] = (q * Q_PRESCALE).astype(BF16)

    kv = proj(_W1_KV)
    ckv = _rms(kv[:, :KV_RANK], kvg_ref[...])
    ckv_ref[...] = ckv
    slab = kv[:, KV_RANK:]
    kr_ref[...] = slab[:, QK_NOPE:QK_NOPE + QK_ROPE]
    kr = jnp.where(lane < QK_NOPE + QK_ROPE, slab, 0.0)
    if use_rope:
        kr_rot = pltpu.roll(slab, HEAD_SLAB - QK_ROPE, axis=1)
        kr = kr * cos + kr_rot * sin
    _key_value_heads(ckv.astype(BF16), kr, wk_ref, wvt_ref, k_ref, vt_ref)


def _layer_spec(arr, l):
    return pl.BlockSpec((None,) + arr.shape[1:], lambda *_: (l,) + (0,) * (arr.ndim - 1),
                        pipeline_mode=pl.Buffered(1))


def _mod_spec(l, mod_row):
    return pl.BlockSpec((None, None, 3, D_MODEL), lambda i: (l, mod_row(i), 0, 0))


def _inproj(h, mod, mod_row, pw, l, rope, tm):
    t = h.shape[0]
    use_rope = rope is not None
    row = lambda i: (i, 0)
    names = ['norm_g', 'w1', 'q_norm_g', 'wq', 'kv_norm_g', 'wk', 'wvt']
    in_specs = [pl.BlockSpec((tm, D_MODEL), row), _mod_spec(l, mod_row)] + [_layer_spec(pw[n], l) for n in names]
    args = [h, mod] + [pw[n] for n in names]
    if use_rope:
        cos, sin, rope_tile = rope
        in_specs += [pl.BlockSpec((tm, HEAD_SLAB), lambda i: (rope_tile(i), 0))] * 2
        args += [cos, sin]
    wide = N_HEADS * HEAD_SLAB
    token_outs = lambda ws: ([pl.BlockSpec((tm, w), row) for w, _ in ws],
                             [jax.ShapeDtypeStruct((t, w), dt) for w, dt in ws])
    specs_a, shapes_a = token_outs([(FN_WIDTH, F32), (FN_WIDTH, BF16), (POOL_WIDTH, F32), (POOL_WIDTH, BF16),
                                    (wide, BF16)])
    specs_b, shapes_b = token_outs([(KV_RANK, F32), (QK_ROPE, F32), (ATT_WIDTH, BF16)])
    out_specs = specs_a + [pl.BlockSpec((tm, wide), row), pl.BlockSpec((ATT_WIDTH, tm), lambda i: (0, i))] + specs_b
    out_shape = shapes_a + [jax.ShapeDtypeStruct((t, wide), BF16), jax.ShapeDtypeStruct((ATT_WIDTH, t), BF16)] + shapes_b
    return pl.pallas_call(
        functools.partial(_inproj_kernel, use_rope=use_rope),
        grid=(t // tm,),
        in_specs=in_specs,
        out_specs=out_specs,
        out_shape=out_shape,
        compiler_params=_params(1),
        name="inproj_rope" if use_rope else "inproj",
    )(*args)


def _cache_kv_kernel(ckv_ref, kr_ref, wk_ref, wvt_ref, k_ref, vt_ref):
    _key_value_heads(ckv_ref[...].astype(BF16), kr_ref[...], wk_ref, wvt_ref, k_ref, vt_ref)


def _cache_kv(cache_ckv, cache_kr_slab, pw, l):
    batch, _, past, _ = cache_ckv.shape
    wide = N_HEADS * HEAD_SLAB
    cache_map = lambda b: (b, l, 0, 0)
    return pl.pallas_call(
        _cache_kv_kernel,
        grid=(batch,),
        in_specs=[pl.BlockSpec((None, None, past, KV_RANK), cache_map),
                  pl.BlockSpec((None, None, past, HEAD_SLAB), cache_map),
                  _layer_spec(pw['wk'], l), _layer_spec(pw['wvt'], l)],
        out_specs=[pl.BlockSpec((past, wide), lambda b: (b, 0)), pl.BlockSpec((ATT_WIDTH, past), lambda b: (0, b))],
        out_shape=[jax.ShapeDtypeStruct((batch * past, wide), BF16),
                   jax.ShapeDtypeStruct((ATT_WIDTH, batch * past), BF16)],
        compiler_params=_params(1),
        name="cache_kv",
    )(cache_ckv, cache_kr_slab, pw['wk'], pw['wvt'])


def _hi_lo(m):
    m = np.asarray(m, np.float32)
    hi = m.astype(BF16)
    return hi, (m - hi.astype(np.float32)).astype(BF16)


def _dft_cos_sin(n):
    k = np.arange(n)
    ang = 2.0 * np.pi * ((k[:, None] * k[None, :]) % n) / n
    return np.cos(ang), np.sin(ang)


def _channel_dft_tables():
    c, s = _dft_cos_sin(FN_GC)
    eye = np.eye(FN_WIDTH // FN_GC)
    return np.kron(eye, c), np.kron(eye, s)


def _fourier_direct_kernel(a_ref, sa_ref, cc_hi, cc_lo, ss_hi, ss_lo, m_hi, m_lo, o_ref, *, seq, norm):
    n = a_ref.shape[0] // seq
    a = a_ref[...]
    tc = _dot3_right(a, cc_hi[...], cc_lo[...])
    ts = _dot3_right(a, ss_hi[...], ss_lo[...])
    side_by_side = lambda x: jnp.concatenate([x[i * seq:(i + 1) * seq] for i in range(n)], axis=1)
    f = _dot3_left(m_hi[...], m_lo[...], jnp.concatenate([side_by_side(tc), side_by_side(ts)], axis=0))
    for i in range(n):
        rows = slice(i * seq, (i + 1) * seq)
        f_i = f[:, i * FN_WIDTH:(i + 1) * FN_WIDTH]
        o_ref[rows, :] = ((f_i * norm) * sa_ref[rows, :].astype(F32)).astype(BF16)


def _fourier_direct(a, sa, seq, seqs_per_step):
    t = a.shape[0]
    rows = seq * seqs_per_step
    cc, ss = _channel_dft_tables()
    cl, sl = _dft_cos_sin(seq)
    tables = [x for m in (cc, ss, np.concatenate([cl, -sl], axis=1)) for x in _hi_lo(m)]
    row = lambda i: (i, 0)
    const = lambda i: (0, 0)
    tspecs = [pl.BlockSpec((FN_WIDTH, FN_WIDTH), const)] * 4 + [pl.BlockSpec((seq, 2 * seq), const)] * 2
    return pl.pallas_call(
        functools.partial(_fourier_direct_kernel, seq=seq, norm=float((seq * FN_GC) ** -0.5)),
        grid=(t // rows,),
        in_specs=[pl.BlockSpec((rows, FN_WIDTH), row), pl.BlockSpec((rows, FN_WIDTH), row)] + tspecs,
        out_specs=pl.BlockSpec((rows, FN_WIDTH), row),
        out_shape=jax.ShapeDtypeStruct((t, FN_WIDTH), BF16),
        compiler_params=_params(1),
        name="fourier_direct",
    )(a, sa, *tables)


FFT_R = 64
FFT_PITCH = FFT_R + 4


def _fourier_fft_kernel(a_ref, sa_ref, cc_hi, cc_lo, ss_hi, ss_lo, m1_hi, m1_lo, m2_hi, m2_lo, twc_ref, tws_ref,
                        o_ref, zr0, zr1, zi0, zi1, yr0, yr1, yi0, yi1, *, seq, norm):
    r = FFT_R
    half = FN_WIDTH // 2
    chunk = 512

    def put(refs, rows, x):
        refs[0][rows, :] = x[:, :half]
        refs[1][rows, :] = x[:, half:]

    def get(refs, rows):
        return jnp.concatenate([refs[0][rows, :], refs[1][rows, :]], axis=1)

    block = lambda j: slice(FFT_PITCH * j, FFT_PITCH * j + r)
    across = lambda i: pl.ds(i, r, stride=FFT_PITCH)

    zr, zi, yr_s, yi_s = (zr0, zr1), (zi0, zi1), (yr0, yr1), (yi0, yi1)
    for c in range(seq // chunk):
        a = a_ref[c * chunk:(c + 1) * chunk, :]
        zr_c = _dot3_right(a, cc_hi[...], cc_lo[...])
        zi_c = -_dot3_right(a, ss_hi[...], ss_lo[...])
        for j in range(chunk // r):
            n1 = c * (chunk // r) + j
            put(zr, block(n1), zr_c[j * r:(j + 1) * r])
            put(zi, block(n1), zi_c[j * r:(j + 1) * r])
    for n2 in range(r):
        z = jnp.concatenate([get(zr, across(n2)), get(zi, across(n2))], axis=0)
        y = _dot3_left(m1_hi[...], m1_lo[...], z)
        yr, yi = y[:r], y[r:]
        tw = slice(n2 * r, (n2 + 1) * r)
        cos = jnp.concatenate([twc_ref[tw, :]] * 2, axis=1)
        sin = jnp.concatenate([tws_ref[tw, :]] * 2, axis=1)
        put(yr_s, block(n2), yr * cos + yi * sin)
        put(yi_s, block(n2), yi * cos - yr * sin)
    for k1 in range(r):
        y = jnp.concatenate([get(yr_s, across(k1)), get(yi_s, across(k1))], axis=0)
        put(zr, across(k1), _dot3_left(m2_hi[...], m2_lo[...], y))
    for k2 in range(r):
        rows = slice(k2 * r, (k2 + 1) * r)
        o_ref[rows, :] = ((get(zr, block(k2)) * norm) * sa_ref[rows, :].astype(F32)).astype(BF16)


def _fourier_fft(a, sa, seq):
    assert seq == FFT_R * FFT_R
    t = a.shape[0]
    cc, ss = _channel_dft_tables()
    c, s = _dft_cos_sin(FFT_R)
    m1 = np.block([[c, s], [-s, c]])
    m2 = np.concatenate([c, s], axis=1)
    n2 = np.arange(FFT_R)[:, None]
    k1 = np.arange(FFT_R)[None, :]
    ang = (2.0 * np.pi * (n2 * k1) / seq).reshape(seq, 1)
    twc = np.ascontiguousarray(np.broadcast_to(np.cos(ang).astype(np.float32), (seq, 128)))
    tws = np.ascontiguousarray(np.broadcast_to(np.sin(ang).astype(np.float32), (seq, 128)))
    tables = [x for m in (cc, ss, m1, m2) for x in _hi_lo(m)]
    row = lambda i: (i, 0)
    const = lambda i: (0, 0)
    tspecs = ([pl.BlockSpec((FN_WIDTH, FN_WIDTH), const)] * 4 + [pl.BlockSpec((2 * FFT_R, 2 * FFT_R), const)] * 2
              + [pl.BlockSpec((FFT_R, 2 * FFT_R), const)] * 2 + [pl.BlockSpec((seq, 128), const)] * 2)
    return pl.pallas_call(
        functools.partial(_fourier_fft_kernel, seq=seq, norm=float((seq * FN_GC) ** -0.5)),
        grid=(t // seq,),
        in_specs=[pl.BlockSpec((seq, FN_WIDTH), row), pl.BlockSpec((seq, FN_WIDTH), row)] + tspecs,
        out_specs=pl.BlockSpec((seq, FN_WIDTH), row),
        out_shape=jax.ShapeDtypeStruct((t, FN_WIDTH), BF16),
        scratch_shapes=[pltpu.VMEM((FFT_PITCH * FFT_R, FN_WIDTH // 2), F32)] * 8,
        compiler_params=_params(1),
        name="fourier_fft",
    )(a, sa, *tables, twc, tws)


POOL_HALO = 8
POOL_CHUNK = 256


def _pool_kernel(b_ref, sb_ref, pw_ref, ps_ref, o_ref, pad_ref, *, seq):
    zeros = jnp.zeros((POOL_HALO, POOL_WIDTH), F32)
    lane = lax.broadcasted_iota(jnp.int32, (POOL_CHUNK, 128), 1)
    low_group = lane < POOL_GC
    for i in range(pad_ref.shape[0]):
        pad_ref[i, 0:POOL_HALO, :] = zeros
        pad_ref[i, POOL_HALO + seq:, :] = zeros
        pad_ref[i, POOL_HALO:POOL_HALO + seq, :] = b_ref[i * seq:(i + 1) * seq, :]

    for i, c in [(i, c) for i in range(pad_ref.shape[0]) for c in range(seq // POOL_CHUNK)]:
        r0 = c * POOL_CHUNK
        t = lax.broadcasted_iota(jnp.int32, (POOL_CHUNK, 128), 0) + r0

        def inv_count(w):
            left = w // 2
            right = w - 1 - left
            lo = jnp.maximum(t - left, 0)
            hi = jnp.minimum(t + right, seq - 1)
            return 1.0 / (hi - lo + 1).astype(F32)

        def ld(off, col):
            start = POOL_HALO + r0 + off
            return pad_ref[i, start:start + POOL_CHUNK, 128 * col:128 * (col + 1)]

        u0 = ld(0, 0)
        p2 = ld(-1, 0) + u0
        p4 = p2 + ld(-2, 0) + ld(1, 0)
        pooled0 = jnp.where(low_group, p2 * inv_count(2), p4 * inv_count(4)) - u0
        u1 = ld(0, 1)
        p8 = u1
        for off in (-4, -3, -2, -1, 1, 2, 3):
            p8 = p8 + ld(off, 1)
        p16 = p8
        for off in (-8, -7, -6, -5, 4, 5, 6, 7):
            p16 = p16 + ld(off, 1)
        pooled1 = jnp.where(low_group, p8 * inv_count(8), p16 * inv_count(16)) - u1

        pooled = jnp.concatenate([pooled0, pooled1], axis=1).astype(BF16)
        mixed = _dot(pooled, pw_ref[...]) * ps_ref[...]
        rows = slice(i * seq + r0, i * seq + r0 + POOL_CHUNK)
        o_ref[rows, :] = (mixed * sb_ref[rows, :].astype(F32)).astype(BF16)


def _pool(b, sb, pw, l, seq, seqs_per_step):
    t = b.shape[0]
    rows = seq * seqs_per_step
    row = lambda i: (i, 0)
    return pl.pallas_call(
        functools.partial(_pool_kernel, seq=seq),
        grid=(t // rows,),
        in_specs=[pl.BlockSpec((rows, POOL_WIDTH), row), pl.BlockSpec((rows, POOL_WIDTH), row),
                  _layer_spec(pw['pool_w'], l), _layer_spec(pw['pool_scale'], l)],
        out_specs=pl.BlockSpec((rows, POOL_WIDTH), row),
        out_shape=jax.ShapeDtypeStruct((t, POOL_WIDTH), BF16),
        scratch_shapes=[pltpu.VMEM((seqs_per_step, seq + 2 * POOL_HALO, POOL_WIDTH), F32)],
        compiler_params=_params(1),
        name="pool",
    )(b, sb, pw['pool_w'], pw['pool_scale'])


def _attn_kernel(*refs, heads, chunk, use_cache, lookahead):
    if use_cache:
        q_ref, k_ref, vt_ref, kc_ref, vct_ref, sc_ref, o_ref, s_ref, m_ref, l_ref, acc_ref = refs
        sources = ((k_ref, vt_ref), (kc_ref, vct_ref))
    else:
        q_ref, k_ref, vt_ref, sc_ref, o_ref, s_ref, m_ref, l_ref, acc_ref = refs
        sources = ((k_ref, vt_ref),)
    tq = q_ref.shape[0]
    groups = chunk // 8
    chunks = []
    for keys, values in sources:
        for off in range(0, keys.shape[0], chunk):
            chunks.append((keys, values, off, len(chunks) * chunk))

    slots = s_ref.shape[0]

    def scores(h, c):
        keys, _, off, row = chunks[c]
        sl = slice(HEAD_SLAB * h, HEAD_SLAB * (h + 1))
        s = lax.dot_general(keys[off:off + chunk, sl], q_ref[:, sl], _NT, preferred_element_type=F32)
        s_ref[h % slots, row:row + chunk, :] = s
        m_ref[h % slots] = jnp.maximum(m_ref[h % slots], jnp.max(s.reshape(groups, 8, tq), axis=0))

    def weigh(h, c, m):
        _, values, off, row = chunks[c]
        p = jnp.exp2(s_ref[h % slots, row:row + chunk, :] - m)
        l_ref[h % slots] += jnp.sum(p.reshape(groups, 8, tq), axis=0)
        acc_ref[h % slots] += _dot(values[V_DIM * h:V_DIM * (h + 1), off:off + chunk], p.astype(BF16))

    outs = []
    for t in range(heads + lookahead):
        h_w, h_s = t - lookahead, t
        if h_s < heads:
            m_ref[h_s % slots] = jnp.full((8, tq), -jnp.inf, F32)
        if h_w >= 0:
            m = jnp.max(m_ref[h_w % slots], axis=0, keepdims=True)
            l_ref[h_w % slots] = jnp.zeros((8, tq), F32)
            acc_ref[h_w % slots] = jnp.zeros((V_DIM, tq), F32)
        for c in range(len(chunks)):
            if h_w >= 0:
                weigh(h_w, c, m)
            if h_s < heads:
                scores(h_s, c)
        if h_w >= 0:
            denom = jnp.sum(l_ref[h_w % slots], axis=0, keepdims=True)
            outs.append(acc_ref[h_w % slots] * (1.0 / denom))
            if h_w % 2 == 1:
                o_pair = jnp.concatenate(outs, axis=0).T
                outs = []
                sl = slice(HEAD_SLAB * (h_w // 2), HEAD_SLAB * (h_w // 2 + 1))
                o_ref[:, sl] = (o_pair * sc_ref[:, sl].astype(F32)).astype(BF16)


def _attention(q, k, vt, sc, cache, batch, lq, lk, tq, heads_per_step, chunk, lookahead):
    use_cache = cache is not None
    nq = lq // tq
    n_hp = N_HEADS // heads_per_step
    qw = heads_per_step * HEAD_SLAB
    ow = heads_per_step * V_DIM
    q_map = lambda b, g, i: (b * nq + i, g)
    k_map = lambda b, g, i: (b, g)
    vt_map = lambda b, g, i: (g, b)
    in_specs = [pl.BlockSpec((tq, qw), q_map), pl.BlockSpec((lk, qw), k_map), pl.BlockSpec((ow, lk), vt_map)]
    args = [q, k, vt]
    lc = 0
    if use_cache:
        lc = cache[0].shape[0] // batch
        in_specs += [pl.BlockSpec((lc, qw), k_map), pl.BlockSpec((ow, lc), vt_map)]
        args += list(cache)
    in_specs.append(pl.BlockSpec((tq, ow), q_map))
    args.append(sc)
    slots = min(heads_per_step, lookahead + 1)
    return pl.pallas_call(
        functools.partial(_attn_kernel, heads=heads_per_step, chunk=chunk, use_cache=use_cache, lookahead=lookahead),
        grid=(batch, n_hp, nq),
        in_specs=in_specs,
        out_specs=pl.BlockSpec((tq, ow), q_map),
        out_shape=jax.ShapeDtypeStruct((batch * lq, ATT_WIDTH), BF16),
        scratch_shapes=[pltpu.VMEM((slots, lk + lc, tq), F32), pltpu.VMEM((slots, 8, tq), F32),
                        pltpu.VMEM((slots, 8, tq), F32), pltpu.VMEM((slots, V_DIM, tq), F32)],
        compiler_params=_params(3),
        name="attention_cache" if use_cache else "attention",
    )(*args)


def _out_kernel(*refs, final):
    if final:
        h_ref, mod_ref, ng_ref, xa_ref, xb_ref, xc_ref, wa_ref, wb_ref, wc_ref, wg_ref, wo_ref, fg_ref, o_ref = refs
    else:
        h_ref, mod_ref, ng_ref, xa_ref, xb_ref, xc_ref, wa_ref, wb_ref, wc_ref, wg_ref, wo_ref, o_ref = refs
    h = h_ref[...]
    xn = _modulated_norm(h, ng_ref[...], mod_ref).astype(BF16)
    y = None
    for i, (x_ref, w_ref) in enumerate(((xa_ref, wa_ref), (xb_ref, wb_ref), (xc_ref, wc_ref))):
        wg_i = wg_ref[i * D_MODEL:(i + 1) * D_MODEL, :]
        g = jax.nn.sigmoid(lax.dot_general(xn, wg_i, _NT, preferred_element_type=F32))
        term = g * _dot(x_ref[...], w_ref[...])
        y = term if y is None else y + term
    h_new = h + mod_ref[2:3, :] * _dot(y.astype(BF16), wo_ref[...])
    if final:
        o_ref[...] = _rms(h_new, fg_ref[...])
    else:
        o_ref[...] = h_new


def _out(h, mod, mod_row, xa, xb, xc, pw, l, final_g, tm):
    t = h.shape[0]
    final = final_g is not None
    row = lambda i: (i, 0)
    names = ['w_br_a', 'w_br_b', 'w_br_c', 'wg', 'w_out']
    in_specs = ([pl.BlockSpec((tm, D_MODEL), row), _mod_spec(l, mod_row), _layer_spec(pw['norm_g'], l),
                 pl.BlockSpec((tm, FN_WIDTH), row), pl.BlockSpec((tm, POOL_WIDTH), row),
                 pl.BlockSpec((tm, ATT_WIDTH), row)] + [_layer_spec(pw[n], l) for n in names])
    args = [h, mod, pw['norm_g'], xa, xb, xc] + [pw[n] for n in names]
    if final:
        in_specs.append(pl.BlockSpec((1, D_MODEL), lambda i: (0, 0)))
        args.append(final_g)
    return pl.pallas_call(
        functools.partial(_out_kernel, final=final),
        grid=(t // tm,),
        in_specs=in_specs,
        out_specs=pl.BlockSpec((tm, D_MODEL), row),
        out_shape=jax.ShapeDtypeStruct((t, D_MODEL), F32),
        compiler_params=_params(1),
        name="out_final" if final else "out",
    )(*args)


_ROPE_PAD = ((QK_NOPE, HEAD_SLAB - QK_NOPE - QK_ROPE),)
W1_ROW_TILE = 256
W1_PLAIN_TILES = _W1_KV[0] // W1_ROW_TILE
WG_ROW_TILE = 512


def _w1_source_row(k):
    direct = _OFF_KV // W1_ROW_TILE
    return jnp.where(k < direct, k * W1_ROW_TILE,
                     jnp.where(k < W1_PLAIN_TILES, _OFF_CZ + (k - direct) * W1_ROW_TILE, _OFF_KV))


def _pack_w1_kernel(w_ref, o_ref):
    k = pl.program_id(1)
    x = w_ref[0]

    @pl.when(k < W1_PLAIN_TILES)
    def _():
        o_ref[...] = x.astype(BF16)

    @pl.when(k == W1_PLAIN_TILES)
    def _():
        kr = x[KV_RANK:KV_RANK + QK_ROPE, :]
        q = QK_ROPE // 4
        rot = jnp.concatenate([-kr[q:2 * q], kr[:q], -kr[3 * q:], kr[2 * q:3 * q]], axis=0)
        zeros = jnp.zeros((QK_NOPE, D_MODEL), F32)
        o_ref[...] = jnp.concatenate([x[:KV_RANK], zeros, kr, rot], axis=0).astype(BF16)


def _pack_cast_kernel(w_ref, o_ref):
    o_ref[...] = w_ref[0].astype(BF16)


def _pack_w_in(w_in):
    wt = jnp.swapaxes(w_in, 1, 2)
    g_width = w_in.shape[2] - _OFF_G
    w1t = pl.pallas_call(
        _pack_w1_kernel,
        grid=(DEPTH, W1_WIDTH // W1_ROW_TILE),
        in_specs=[pl.BlockSpec((pl.Element(1), pl.Element(W1_ROW_TILE), pl.Element(D_MODEL)),
                               lambda l, k: (l, pl.multiple_of(_w1_source_row(k), 32), 0))],
        out_specs=pl.BlockSpec((None, W1_ROW_TILE, D_MODEL), lambda l, k: (l, k, 0)),
        out_shape=jax.ShapeDtypeStruct((DEPTH, W1_WIDTH, D_MODEL), BF16),
        compiler_params=_params(2),
        name="pack_w1",
    )(wt)
    wgt = pl.pallas_call(
        _pack_cast_kernel,
        grid=(DEPTH, g_width // WG_ROW_TILE),
        in_specs=[pl.BlockSpec((pl.Element(1), pl.Element(WG_ROW_TILE), pl.Element(D_MODEL)),
                               lambda l, k: (l, pl.multiple_of(_OFF_G + k * WG_ROW_TILE, 32), 0))],
        out_specs=pl.BlockSpec((None, WG_ROW_TILE, D_MODEL), lambda l, k: (l, k, 0)),
        out_shape=jax.ShapeDtypeStruct((DEPTH, g_width, D_MODEL), BF16),
        compiler_params=_params(2),
        name="pack_wg",
    )(wt)
    return w1t, wgt


def _pack_weights(norm_g, w_in, pool_w, pool_scale, q_norm_g, w_q_up, kv_norm_g, w_kv_up, w_br_a, w_br_b, w_br_c,
                  w_out):
    lead = ((0, 0), (0, 0))
    w1, wg = _pack_w_in(w_in)
    wide = N_HEADS * HEAD_SLAB
    wq_h = w_q_up.reshape(DEPTH, Q_RANK, N_HEADS, QK_NOPE + QK_ROPE)
    wq = jnp.pad(wq_h, lead + ((0, 0), (0, _ROPE_PAD[0][1]))).reshape(DEPTH, Q_RANK, wide).astype(BF16)
    wkv_h = w_kv_up.reshape(DEPTH, KV_RANK, N_HEADS, QK_NOPE + V_DIM)
    wk = jnp.pad(wkv_h[..., :QK_NOPE], lead + ((0, 0), (0, HEAD_SLAB - QK_NOPE))).reshape(DEPTH, KV_RANK, wide)
    wvt = wkv_h[..., QK_NOPE:].reshape(DEPTH, KV_RANK, ATT_WIDTH).transpose(0, 2, 1)
    groups = len(POOL_WINDOWS)
    eye = jnp.eye(groups, dtype=F32)
    pool_bd = (pool_w[:, :, :, None, :] * eye[None, :, None, :, None]).reshape(DEPTH, POOL_WIDTH, POOL_WIDTH)
    return {
        'norm_g': norm_g.reshape(DEPTH, 1, D_MODEL), 'w1': w1, 'wg': wg,
        'q_norm_g': q_norm_g.reshape(DEPTH, 1, Q_RANK), 'wq': wq,
        'kv_norm_g': kv_norm_g.reshape(DEPTH, 1, KV_RANK), 'wk': wk.astype(BF16), 'wvt': wvt.astype(BF16),
        'pool_w': pool_bd.astype(BF16), 'pool_scale': pool_scale.reshape(DEPTH, 1, POOL_WIDTH),
        'w_br_a': w_br_a.astype(BF16), 'w_br_b': w_br_b.astype(BF16), 'w_br_c': w_br_c.astype(BF16),
        'w_out': w_out.astype(BF16),
    }


def _rope_tables(seq):
    f32 = np.float32
    t = np.arange(seq)
    row = (t // GRID_W).astype(f32)
    col = (t % GRID_W).astype(f32)
    half = QK_ROPE // 2
    freqs = f32(ROPE_THETA) ** (-np.arange(0, half, 2, dtype=f32) / f32(half))
    ar = row[:, None] * freqs
    ac = col[:, None] * freqs
    cos = np.ones((seq, HEAD_SLAB), f32)
    sin = np.zeros((seq, HEAD_SLAB), f32)
    cos[:, QK_NOPE:QK_NOPE + QK_ROPE] = np.concatenate([np.cos(ar), np.cos(ar), np.cos(ac), np.cos(ac)], axis=-1)
    sin[:, QK_NOPE:QK_NOPE + QK_ROPE] = np.concatenate([np.sin(ar), np.sin(ar), np.sin(ac), np.sin(ac)], axis=-1)
    return cos, sin


TOKEN_TILE = 1024
SAMPLE_Q_TILE = 512
SAMPLE_HEADS_PER_STEP = 4
SAMPLE_KEY_CHUNK = 512
PROMPT_SEQS_PER_STEP = 4


def kernel(x_prompt, x_sample, cache_ckv, cache_krope, c, c_ctx, norm_g, w_mod, b_mod, w_in, pool_w, pool_scale,
           q_norm_g, w_q_up, kv_norm_g, w_kv_up, w_br_a, w_br_b, w_br_c, w_out, final_norm_g):
    batch, seq, _ = x_prompt.shape
    dec_batch, dec_seq, _ = x_sample.shape
    past = cache_ckv.shape[2]
    tm = TOKEN_TILE

    mod_rows = 8
    cvec = jnp.concatenate([c_ctx[None, :], c, jnp.zeros((mod_rows - 1 - dec_batch, D_MODEL), F32)], axis=0)
    mod = _modulation(cvec, w_mod, b_mod).reshape(DEPTH, mod_rows, 3, D_MODEL)
    prompt_row = lambda i: 0
    tiles_per_sample = dec_seq // tm
    assert past % SAMPLE_KEY_CHUNK == 0 and dec_seq % SAMPLE_KEY_CHUNK == 0
    sample_row = lambda i: 1 + i // tiles_per_sample

    cos, sin = _rope_tables(dec_seq)
    rope = (cos, sin, lambda i: i % tiles_per_sample)
    final_g = final_norm_g.reshape(1, D_MODEL)
    pw = _pack_weights(norm_g, w_in, pool_w, pool_scale, q_norm_g, w_q_up, kv_norm_g, w_kv_up, w_br_a, w_br_b,
                       w_br_c, w_out)
    cache_kr_slab = jnp.pad(cache_krope, ((0, 0), (0, 0), (0, 0)) + _ROPE_PAD)

    hp = x_prompt.reshape(batch * seq, D_MODEL)
    hs = x_sample.reshape(dec_batch * dec_seq, D_MODEL)
    ckv_list, kr_list = [], []
    for l in range(DEPTH):
        last = final_g if l == DEPTH - 1 else None

        a_in, sa, b_in, sb, q, k, vt, ckv, kr, sc = _inproj(hp, mod, prompt_row, pw, l, None, tm)
        ckv_list.append(ckv.reshape(batch, seq, KV_RANK))
        kr_list.append(kr.reshape(batch, seq, QK_ROPE))
        xa = _fourier_direct(a_in, sa, seq, PROMPT_SEQS_PER_STEP)
        xb = _pool(b_in, sb, pw, l, seq, PROMPT_SEQS_PER_STEP)
        xc = _attention(q, k, vt, sc, None, batch, seq, seq, seq, N_HEADS, seq, lookahead=N_HEADS)
        hp = _out(hp, mod, prompt_row, xa, xb, xc, pw, l, last, tm)

        a_in, sa, b_in, sb, q, k, vt, _, _, sc = _inproj(hs, mod, sample_row, pw, l, rope, tm)
        cache = _cache_kv(cache_ckv, cache_kr_slab, pw, l)
        xa = _fourier_fft(a_in, sa, dec_seq)
        xb = _pool(b_in, sb, pw, l, dec_seq, 1)
        xc = _attention(q, k, vt, sc, cache, dec_batch, dec_seq, dec_seq, SAMPLE_Q_TILE, SAMPLE_HEADS_PER_STEP,
                        SAMPLE_KEY_CHUNK, lookahead=1)
        hs = _out(hs, mod, sample_row, xa, xb, xc, pw, l, last, tm)

    y_prompt = hp.reshape(batch, seq, D_MODEL)
    y_sample = hs.reshape(dec_batch, dec_seq, D_MODEL)
    return (y_prompt, y_sample, jnp.stack(ckv_list, axis=1), jnp.stack(kr_list, axis=1))
```

```python
import functools

import numpy as np
import jax
import jax.numpy as jnp
from jax import lax
from jax.experimental import pallas as pl
from jax.experimental.pallas import tpu as pltpu

D_MODEL = 1024
DEPTH = 2
GRID_W = 64
EPS = 1e-6
FN_WIDTH = 256
FN_GC = 64
POOL_WINDOWS = (2, 4, 8, 16)
POOL_WIDTH = 256
POOL_GC = 64
N_HEADS = 8
QK_NOPE = 64
QK_ROPE = 32
V_DIM = 64
Q_RANK = 256
KV_RANK = 128
ATT_WIDTH = 512
ROPE_THETA = 10000.0
HEAD_SLAB = 128
QK_SCALE = (QK_NOPE + QK_ROPE) ** -0.5
Q_PRESCALE = QK_SCALE * float(np.log2(np.e))

VMEM_LIMIT_BYTES = 56 * 1024 * 1024

F32 = jnp.float32
BF16 = jnp.bfloat16

_OFF_A, _OFF_B, _OFF_Q, _OFF_KV, _OFF_KR, _OFF_CZ, _OFF_G = 0, 512, 1024, 1280, 1408, 1440, 1952
_W1_A = (0, 512)
_W1_B = (512, 1024)
_W1_Q = (1024, 1280)
_W1_CZ = (1280, 1792)
_W1_KV = (1792, 2048)
W1_WIDTH = 2048


def _params(n_parallel):
    return pltpu.CompilerParams(dimension_semantics=("arbitrary",) * n_parallel,
                                vmem_limit_bytes=VMEM_LIMIT_BYTES)


def _dot(a, b):
    return jnp.dot(a, b, preferred_element_type=F32)


def _silu(x):
    return x * jax.nn.sigmoid(x)


def _rms(x, g):
    r = lax.rsqrt(jnp.mean(x * x, axis=-1, keepdims=True) + EPS)
    return (x * r) * g


def _modulated_norm(h, norm_g, mod_ref):
    shift = mod_ref[0:1, :]
    scale = mod_ref[1:2, :]
    return _rms(h, norm_g) * (1.0 + scale) + shift


def _split_bf16(x):
    hi = x.astype(BF16)
    lo = (x - hi.astype(F32)).astype(BF16)
    return hi, lo


def _dot3_right(x, m_hi, m_lo):
    x_hi, x_lo = _split_bf16(x)
    return _dot(x_hi, m_hi) + _dot(x_lo, m_hi) + _dot(x_hi, m_lo)


def _dot3_left(m_hi, m_lo, x):
    x_hi, x_lo = _split_bf16(x)
    return _dot(m_hi, x_hi) + _dot(m_hi, x_lo) + _dot(m_lo, x_hi)


def _mod_kernel(c_ref, w_ref, b_ref, o_ref):
    s = _silu(c_ref[...]).astype(BF16)
    o_ref[...] = _dot(s, w_ref[...].astype(BF16)) + b_ref[...]


def _modulation(cvec, w_mod, b_mod):
    rows = cvec.shape[0]
    tn = 768
    return pl.pallas_call(
        _mod_kernel,
        grid=(DEPTH, 3 * D_MODEL // tn),
        in_specs=[pl.BlockSpec((rows, D_MODEL), lambda l, j: (0, 0)),
                  pl.BlockSpec((None, D_MODEL, tn), lambda l, j: (l, 0, j)),
                  pl.BlockSpec((None, 1, tn), lambda l, j: (l, 0, j))],
        out_specs=pl.BlockSpec((None, rows, tn), lambda l, j: (l, 0, j)),
        out_shape=jax.ShapeDtypeStruct((DEPTH, rows, 3 * D_MODEL), F32),
        compiler_params=_params(2),
        name="modulation",
    )(cvec, w_mod, b_mod.reshape(DEPTH, 1, 3 * D_MODEL))


_NT = (((1,), (1,)), ((), ()))


def _key_value_heads(ckv_bf16, kr_slab, wk_ref, wvt_ref, k_ref, vt_ref):
    kn = _dot(ckv_bf16, wk_ref[...])
    for h in range(N_HEADS):
        sl = slice(HEAD_SLAB * h, HEAD_SLAB * (h + 1))
        k_ref[:, sl] = (kn[:, sl] + kr_slab).astype(BF16)
    vt_ref[...] = lax.dot_general(wvt_ref[...], ckv_bf16, _NT, preferred_element_type=F32).astype(BF16)


def _inproj_kernel(*refs, use_rope):
    if use_rope:
        (h_ref, mod_ref, ng_ref, w1_ref, qg_ref, wq_ref, kvg_ref, wk_ref, wvt_ref, cos_ref, sin_ref,
         ain_ref, sa_ref, bin_ref, sb_ref, q_ref, k_ref, vt_ref, ckv_ref, kr_ref, sc_ref) = refs
    else:
        (h_ref, mod_ref, ng_ref, w1_ref, qg_ref, wq_ref, kvg_ref, wk_ref, wvt_ref,
         ain_ref, sa_ref, bin_ref, sb_ref, q_ref, k_ref, vt_ref, ckv_ref, kr_ref, sc_ref) = refs

    xn = _modulated_norm(h_ref[...], ng_ref[...], mod_ref).astype(BF16)

    def proj(cols):
        return lax.dot_general(xn, w1_ref[cols[0]:cols[1], :], _NT, preferred_element_type=F32)

    a = proj(_W1_A)
    ain_ref[...] = a[:, :FN_WIDTH]
    sa_ref[...] = _silu(a[:, FN_WIDTH:]).astype(BF16)
    b = proj(_W1_B)
    bin_ref[...] = b[:, :POOL_WIDTH]
    sb_ref[...] = _silu(b[:, POOL_WIDTH:]).astype(BF16)
    sc_ref[...] = _silu(proj(_W1_CZ)).astype(BF16)

    qn = _rms(proj(_W1_Q), qg_ref[...]).astype(BF16)
    q = _dot(qn, wq_ref[...])
    lane = lax.broadcasted_iota(jnp.int32, (q.shape[0], HEAD_SLAB), 1)
    if use_rope:
        cos = cos_ref[...]
        sin = sin_ref[...]
        takes_upper = (lane % (QK_ROPE // 2)) < (QK_ROPE // 4)
        for h in range(N_HEADS):
            sl = slice(HEAD_SLAB * h, HEAD_SLAB * (h + 1))
            q_h = q[:, sl]
            upper = pltpu.roll(q_h, HEAD_SLAB - QK_ROPE // 4, axis=1)
            lower = pltpu.roll(q_h, QK_ROPE // 4, axis=1)
            q_rot = jnp.where(takes_upper, -upper, lower)
            q_ref[:, sl] = ((q_h * cos + q_rot * sin) * Q_PRESCALE).astype(BF16)
    else:
        q_ref[...] = (q * Q_PRESCALE).astype(BF16)

    kv = proj(_W1_KV)
    ckv = _rms(kv[:, :KV_RANK], kvg_ref[...])
    ckv_ref[...] = ckv
    slab = kv[:, KV_RANK:]
    kr_ref[...] = slab[:, QK_NOPE:QK_NOPE + QK_ROPE]
    kr = jnp.where(lane < QK_NOPE + QK_ROPE, slab, 0.0)
    if use_rope:
        kr_rot = pltpu.roll(slab, HEAD_SLAB - QK_ROPE, axis=1)
        kr = kr * cos + kr_rot * sin
    _key_value_heads(ckv.astype(BF16), kr, wk_ref, wvt_ref, k_ref, vt_ref)


def _layer_spec(arr, l):
    return pl.BlockSpec((None,) + arr.shape[1:], lambda *_: (l,) + (0,) * (arr.ndim - 1),
                        pipeline_mode=pl.Buffered(1))


def _mod_spec(l, mod_row):
    return pl.BlockSpec((None, None, 3, D_MODEL), lambda i: (l, mod_row(i), 0, 0))


def _inproj(h, mod, mod_row, pw, l, rope, tm):
    t = h.shape[0]
    use_rope = rope is not None
    row = lambda i: (i, 0)
    names = ['norm_g', 'w1', 'q_norm_g', 'wq', 'kv_norm_g', 'wk', 'wvt']
    in_specs = [pl.BlockSpec((tm, D_MODEL), row), _mod_spec(l, mod_row)] + [_layer_spec(pw[n], l) for n in names]
    args = [h, mod] + [pw[n] for n in names]
    if use_rope:
        cos, sin, rope_tile = rope
        in_specs += [pl.BlockSpec((tm, HEAD_SLAB), lambda i: (rope_tile(i), 0))] * 2
        args += [cos, sin]
    wide = N_HEADS * HEAD_SLAB
    token_outs = lambda ws: ([pl.BlockSpec((tm, w), row) for w, _ in ws],
                             [jax.ShapeDtypeStruct((t, w), dt) for w, dt in ws])
    specs_a, shapes_a = token_outs([(FN_WIDTH, F32), (FN_WIDTH, BF16), (POOL_WIDTH, F32), (POOL_WIDTH, BF16),
                                    (wide, BF16)])
    specs_b, shapes_b = token_outs([(KV_RANK, F32), (QK_ROPE, F32), (ATT_WIDTH, BF16)])
    out_specs = specs_a + [pl.BlockSpec((tm, wide), row), pl.BlockSpec((ATT_WIDTH, tm), lambda i: (0, i))] + specs_b
    out_shape = shapes_a + [jax.ShapeDtypeStruct((t, wide), BF16), jax.ShapeDtypeStruct((ATT_WIDTH, t), BF16)] + shapes_b
    return pl.pallas_call(
        functools.partial(_inproj_kernel, use_rope=use_rope),
        grid=(t // tm,),
        in_specs=in_specs,
        out_specs=out_specs,
        out_shape=out_shape,
        compiler_params=_params(1),
        name="inproj_rope" if use_rope else "inproj",
    )(*args)


def _cache_kv_kernel(ckv_ref, kr_ref, wk_ref, wvt_ref, k_ref, vt_ref):
    _key_value_heads(ckv_ref[...].astype(BF16), kr_ref[...], wk_ref, wvt_ref, k_ref, vt_ref)


def _cache_kv(cache_ckv, cache_kr_slab, pw, l):
    batch, _, past, _ = cache_ckv.shape
    wide = N_HEADS * HEAD_SLAB
    cache_map = lambda b: (b, l, 0, 0)
    return pl.pallas_call(
        _cache_kv_kernel,
        grid=(batch,),
        in_specs=[pl.BlockSpec((None, None, past, KV_RANK), cache_map),
                  pl.BlockSpec((None, None, past, HEAD_SLAB), cache_map),
                  _layer_spec(pw['wk'], l), _layer_spec(pw['wvt'], l)],
        out_specs=[pl.BlockSpec((past, wide), lambda b: (b, 0)), pl.BlockSpec((ATT_WIDTH, past), lambda b: (0, b))],
        out_shape=[jax.ShapeDtypeStruct((batch * past, wide), BF16),
                   jax.ShapeDtypeStruct((ATT_WIDTH, batch * past), BF16)],
        compiler_params=_params(1),
        name="cache_kv",
    )(cache_ckv, cache_kr_slab, pw['wk'], pw['wvt'])


def _hi_lo(m):
    m = np.asarray(m, np.float32)
    hi = m.astype(BF16)
    return hi, (m - hi.astype(np.float32)).astype(BF16)


def _dft_cos_sin(n):
    k = np.arange(n)
    ang = 2.0 * np.pi * ((k[:, None] * k[None, :]) % n) / n
    return np.cos(ang), np.sin(ang)


def _channel_dft_tables():
    c, s = _dft_cos_sin(FN_GC)
    eye = np.eye(FN_WIDTH // FN_GC)
    return np.kron(eye, c), np.kron(eye, s)


def _fourier_direct_kernel(a_ref, sa_ref, cc_hi, cc_lo, ss_hi, ss_lo, m_hi, m_lo, o_ref, *, seq, norm):
    n = a_ref.shape[0] // seq
    a = a_ref[...]
    tc = _dot3_right(a, cc_hi[...], cc_lo[...])
    ts = _dot3_right(a, ss_hi[...], ss_lo[...])
    side_by_side = lambda x: jnp.concatenate([x[i * seq:(i + 1) * seq] for i in range(n)], axis=1)
    f = _dot3_left(m_hi[...], m_lo[...], jnp.concatenate([side_by_side(tc), side_by_side(ts)], axis=0))
    for i in range(n):
        rows = slice(i * seq, (i + 1) * seq)
        f_i = f[:, i * FN_WIDTH:(i + 1) * FN_WIDTH]
        o_ref[rows, :] = ((f_i * norm) * sa_ref[rows, :].astype(F32)).astype(BF16)


def _fourier_direct(a, sa, seq, seqs_per_step):
    t = a.shape[0]
    rows = seq * seqs_per_step
    cc, ss = _channel_dft_tables()
    cl, sl = _dft_cos_sin(seq)
    tables = [x for m in (cc, ss, np.concatenate([cl, -sl], axis=1)) for x in _hi_lo(m)]
    row = lambda i: (i, 0)
    const = lambda i: (0, 0)
    tspecs = [pl.BlockSpec((FN_WIDTH, FN_WIDTH), const)] * 4 + [pl.BlockSpec((seq, 2 * seq), const)] * 2
    return pl.pallas_call(
        functools.partial(_fourier_direct_kernel, seq=seq, norm=float((seq * FN_GC) ** -0.5)),
        grid=(t // rows,),
        in_specs=[pl.BlockSpec((rows, FN_WIDTH), row), pl.BlockSpec((rows, FN_WIDTH), row)] + tspecs,
        out_specs=pl.BlockSpec((rows, FN_WIDTH), row),
        out_shape=jax.ShapeDtypeStruct((t, FN_WIDTH), BF16),
        compiler_params=_params(1),
        name="fourier_direct",
    )(a, sa, *tables)


FFT_R = 64
FFT_PITCH = FFT_R + 4


def _fourier_fft_kernel(a_ref, sa_ref, cc_hi, cc_lo, ss_hi, ss_lo, m1_hi, m1_lo, m2_hi, m2_lo, twc_ref, tws_ref,
                        o_ref, zr0, zr1, zi0, zi1, yr0, yr1, yi0, yi1, *, seq, norm):
    r = FFT_R
    half = FN_WIDTH // 2
    chunk = 512

    def put(refs, rows, x):
        refs[0][rows, :] = x[:, :half]
        refs[1][rows, :] = x[:, half:]

    def get(refs, rows):
        return jnp.concatenate([refs[0][rows, :], refs[1][rows, :]], axis=1)

    block = lambda j: slice(FFT_PITCH * j, FFT_PITCH * j + r)
    across = lambda i: pl.ds(i, r, stride=FFT_PITCH)

    zr, zi, yr_s, yi_s = (zr0, zr1), (zi0, zi1), (yr0, yr1), (yi0, yi1)
    for c in range(seq // chunk):
        a = a_ref[c * chunk:(c + 1) * chunk, :]
        zr_c = _dot3_right(a, cc_hi[...], cc_lo[...])
        zi_c = -_dot3_right(a, ss_hi[...], ss_lo[...])
        for j in range(chunk // r):
            n1 = c * (chunk // r) + j
            put(zr, block(n1), zr_c[j * r:(j + 1) * r])
            put(zi, block(n1), zi_c[j * r:(j + 1) * r])
    for n2 in range(r):
        z = jnp.concatenate([get(zr, across(n2)), get(zi, across(n2))], axis=0)
        y = _dot3_left(m1_hi[...], m1_lo[...], z)
        yr, yi = y[:r], y[r:]
        tw = slice(n2 * r, (n2 + 1) * r)
        cos = jnp.concatenate([twc_ref[tw, :]] * 2, axis=1)
        sin = jnp.concatenate([tws_ref[tw, :]] * 2, axis=1)
        put(yr_s, block(n2), yr * cos + yi * sin)
        put(yi_s, block(n2), yi * cos - yr * sin)
    for k1 in range(r):
        y = jnp.concatenate([get(yr_s, across(k1)), get(yi_s, across(k1))], axis=0)
        put(zr, across(k1), _dot3_left(m2_hi[...], m2_lo[...], y))
    for k2 in range(r):
        rows = slice(k2 * r, (k2 + 1) * r)
        o_ref[rows, :] = ((get(zr, block(k2)) * norm) * sa_ref[rows, :].astype(F32)).astype(BF16)


def _fourier_fft(a, sa, seq):
    assert seq == FFT_R * FFT_R
    t = a.shape[0]
    cc, ss = _channel_dft_tables()
    c, s = _dft_cos_sin(FFT_R)
    m1 = np.block([[c, s], [-s, c]])
    m2 = np.concatenate([c, s], axis=1)
    n2 = np.arange(FFT_R)[:, None]
    k1 = np.arange(FFT_R)[None, :]
    ang = (2.0 * np.pi * (n2 * k1) / seq).reshape(seq, 1)
    twc = np.ascontiguousarray(np.broadcast_to(np.cos(ang).astype(np.float32), (seq, 128)))
    tws = np.ascontiguousarray(np.broadcast_to(np.sin(ang).astype(np.float32), (seq, 128)))
    tables = [x for m in (cc, ss, m1, m2) for x in _hi_lo(m)]
    row = lambda i: (i, 0)
    const = lambda i: (0, 0)
    tspecs = ([pl.BlockSpec((FN_WIDTH, FN_WIDTH), const)] * 4 + [pl.BlockSpec((2 * FFT_R, 2 * FFT_R), const)] * 2
              + [pl.BlockSpec((FFT_R, 2 * FFT_R), const)] * 2 + [pl.BlockSpec((seq, 128), const)] * 2)
    return pl.pallas_call(
        functools.partial(_fourier_fft_kernel, seq=seq, norm=float((seq * FN_GC) ** -0.5)),
        grid=(t // seq,),
        in_specs=[pl.BlockSpec((seq, FN_WIDTH), row), pl.BlockSpec((seq, FN_WIDTH), row)] + tspecs,
        out_specs=pl.BlockSpec((seq, FN_WIDTH), row),
        out_shape=jax.ShapeDtypeStruct((t, FN_WIDTH), BF16),
        scratch_shapes=[pltpu.VMEM((FFT_PITCH * FFT_R, FN_WIDTH // 2), F32)] * 8,
        compiler_params=_params(1),
        name="fourier_fft",
    )(a, sa, *tables, twc, tws)


POOL_HALO = 8
POOL_CHUNK = 256


def _pool_kernel(b_ref, sb_ref, pw_ref, ps_ref, o_ref, pad_ref, *, seq):
    zeros = jnp.zeros((POOL_HALO, POOL_WIDTH), F32)
    lane = lax.broadcasted_iota(jnp.int32, (POOL_CHUNK, 128), 1)
    low_group = lane < POOL_GC
    for i in range(pad_ref.shape[0]):
        pad_ref[i, 0:POOL_HALO, :] = zeros
        pad_ref[i, POOL_HALO + seq:, :] = zeros
        pad_ref[i, POOL_HALO:POOL_HALO + seq, :] = b_ref[i * seq:(i + 1) * seq, :]

    for i, c in [(i, c) for i in range(pad_ref.shape[0]) for c in range(seq // POOL_CHUNK)]:
        r0 = c * POOL_CHUNK
        t = lax.broadcasted_iota(jnp.int32, (POOL_CHUNK, 128), 0) + r0

        def inv_count(w):
            left = w // 2
            right = w - 1 - left
            lo = jnp.maximum(t - left, 0)
            hi = jnp.minimum(t + right, seq - 1)
            return 1.0 / (hi - lo + 1).astype(F32)

        def ld(off, col):
            start = POOL_HALO + r0 + off
            return pad_ref[i, start:start + POOL_CHUNK, 128 * col:128 * (col + 1)]

        u0 = ld(0, 0)
        p2 = ld(-1, 0) + u0
        p4 = p2 + ld(-2, 0) + ld(1, 0)
        pooled0 = jnp.where(low_group, p2 * inv_count(2), p4 * inv_count(4)) - u0
        u1 = ld(0, 1)
        p8 = u1
        for off in (-4, -3, -2, -1, 1, 2, 3):
            p8 = p8 + ld(off, 1)
        p16 = p8
        for off in (-8, -7, -6, -5, 4, 5, 6, 7):
            p16 = p16 + ld(off, 1)
        pooled1 = jnp.where(low_group, p8 * inv_count(8), p16 * inv_count(16)) - u1

        pooled = jnp.concatenate([pooled0, pooled1], axis=1).astype(BF16)
        mixed = _dot(pooled, pw_ref[...]) * ps_ref[...]
        rows = slice(i * seq + r0, i * seq + r0 + POOL_CHUNK)
        o_ref[rows, :] = (mixed * sb_ref[rows, :].astype(F32)).astype(BF16)


def _pool(b, sb, pw, l, seq, seqs_per_step):
    t = b.shape[0]
    rows = seq * seqs_per_step
    row = lambda i: (i, 0)
    return pl.pallas_call(
        functools.partial(_pool_kernel, seq=seq),
        grid=(t // rows,),
        in_specs=[pl.BlockSpec((rows, POOL_WIDTH), row), pl.BlockSpec((rows, POOL_WIDTH), row),
                  _layer_spec(pw['pool_w'], l), _layer_spec(pw['pool_scale'], l)],
        out_specs=pl.BlockSpec((rows, POOL_WIDTH), row),
        out_shape=jax.ShapeDtypeStruct((t, POOL_WIDTH), BF16),
        scratch_shapes=[pltpu.VMEM((seqs_per_step, seq + 2 * POOL_HALO, POOL_WIDTH), F32)],
        compiler_params=_params(1),
        name="pool",
    )(b, sb, pw['pool_w'], pw['pool_scale'])


def _attn_kernel(*refs, heads, chunk, use_cache, lookahead):
    if use_cache:
        q_ref, k_ref, vt_ref, kc_ref, vct_ref, sc_ref, o_ref, s_ref, m_ref, l_ref, acc_ref = refs
        sources = ((k_ref, vt_ref), (kc_ref, vct_ref))
    else:
        q_ref, k_ref, vt_ref, sc_ref, o_ref, s_ref, m_ref, l_ref, acc_ref = refs
        sources = ((k_ref, vt_ref),)
    tq = q_ref.shape[0]
    chunks, row = [], 0
    for keys, values in sources:
        for off in range(0, keys.shape[0], chunk):
            size = min(chunk, keys.shape[0] - off)
            chunks.append((keys, values, off, size, row))
            row += size

    slots = s_ref.shape[0]

    def scores(h, c):
        keys, _, off, size, row = chunks[c]
        sl = slice(HEAD_SLAB * h, HEAD_SLAB * (h + 1))
        s = lax.dot_general(keys[off:off + size, sl], q_ref[:, sl], _NT, preferred_element_type=F32)
        s_ref[h % slots, row:row + size, :] = s
        m_ref[h % slots] = jnp.maximum(m_ref[h % slots], jnp.max(s.reshape(size // 8, 8, tq), axis=0))

    def weigh(h, c, m):
        _, values, off, size, row = chunks[c]
        p = jnp.exp2(s_ref[h % slots, row:row + size, :] - m)
        l_ref[h % slots] += jnp.sum(p.reshape(size // 8, 8, tq), axis=0)
        acc_ref[h % slots] += _dot(values[V_DIM * h:V_DIM * (h + 1), off:off + size], p.astype(BF16))

    outs = []
    for t in range(heads + lookahead):
        h_w, h_s = t - lookahead, t
        if h_s < heads:
            m_ref[h_s % slots] = jnp.full((8, tq), -jnp.inf, F32)
        if h_w >= 0:
            m = jnp.max(m_ref[h_w % slots], axis=0, keepdims=True)
            l_ref[h_w % slots] = jnp.zeros((8, tq), F32)
            acc_ref[h_w % slots] = jnp.zeros((V_DIM, tq), F32)
        for c in range(len(chunks)):
            if h_w >= 0:
                weigh(h_w, c, m)
            if h_s < heads:
                scores(h_s, c)
        if h_w >= 0:
            denom = jnp.sum(l_ref[h_w % slots], axis=0, keepdims=True)
            outs.append(acc_ref[h_w % slots] * (1.0 / denom))
            if h_w % 2 == 1:
                o_pair = jnp.concatenate(outs, axis=0).T
                outs = []
                sl = slice(HEAD_SLAB * (h_w // 2), HEAD_SLAB * (h_w // 2 + 1))
                o_ref[:, sl] = (o_pair * sc_ref[:, sl].astype(F32)).astype(BF16)


def _attention(q, k, vt, sc, cache, batch, lq, lk, tq, heads_per_step, chunk, lookahead):
    use_cache = cache is not None
    nq = lq // tq
    n_hp = N_HEADS // heads_per_step
    qw = heads_per_step * HEAD_SLAB
    ow = heads_per_step * V_DIM
    q_map = lambda b, g, i: (b * nq + i, g)
    k_map = lambda b, g, i: (b, g)
    vt_map = lambda b, g, i: (g, b)
    in_specs = [pl.BlockSpec((tq, qw), q_map), pl.BlockSpec((lk, qw), k_map), pl.BlockSpec((ow, lk), vt_map)]
    args = [q, k, vt]
    lc = 0
    if use_cache:
        lc = cache[0].shape[0] // batch
        in_specs += [pl.BlockSpec((lc, qw), k_map), pl.BlockSpec((ow, lc), vt_map)]
        args += list(cache)
    in_specs.append(pl.BlockSpec((tq, ow), q_map))
    args.append(sc)
    slots = min(heads_per_step, lookahead + 1)
    return pl.pallas_call(
        functools.partial(_attn_kernel, heads=heads_per_step, chunk=chunk, use_cache=use_cache, lookahead=lookahead),
        grid=(batch, n_hp, nq),
        in_specs=in_specs,
        out_specs=pl.BlockSpec((tq, ow), q_map),
        out_shape=jax.ShapeDtypeStruct((batch * lq, ATT_WIDTH), BF16),
        scratch_shapes=[pltpu.VMEM((slots, lk + lc, tq), F32), pltpu.VMEM((slots, 8, tq), F32),
                        pltpu.VMEM((slots, 8, tq), F32), pltpu.VMEM((slots, V_DIM, tq), F32)],
        compiler_params=_params(3),
        name="attention_cache" if use_cache else "attention",
    )(*args)


def _out_kernel(*refs, final):
    if final:
        h_ref, mod_ref, ng_ref, xa_ref, xb_ref, xc_ref, wa_ref, wb_ref, wc_ref, wg_ref, wo_ref, fg_ref, o_ref = refs
    else:
        h_ref, mod_ref, ng_ref, xa_ref, xb_ref, xc_ref, wa_ref, wb_ref, wc_ref, wg_ref, wo_ref, o_ref = refs
    h = h_ref[...]
    xn = _modulated_norm(h, ng_ref[...], mod_ref).astype(BF16)
    y = None
    for i, (x_ref, w_ref) in enumerate(((xa_ref, wa_ref), (xb_ref, wb_ref), (xc_ref, wc_ref))):
        wg_i = wg_ref[i * D_MODEL:(i + 1) * D_MODEL, :]
        g = jax.nn.sigmoid(lax.dot_general(xn, wg_i, _NT, preferred_element_type=F32))
        term = g * _dot(x_ref[...], w_ref[...])
        y = term if y is None else y + term
    h_new = h + mod_ref[2:3, :] * _dot(y.astype(BF16), wo_ref[...])
    if final:
        o_ref[...] = _rms(h_new, fg_ref[...])
    else:
        o_ref[...] = h_new


def _out(h, mod, mod_row, xa, xb, xc, pw, l, final_g, tm):
    t = h.shape[0]
    final = final_g is not None
    row = lambda i: (i, 0)
    names = ['w_br_a', 'w_br_b', 'w_br_c', 'wg', 'w_out']
    in_specs = ([pl.BlockSpec((tm, D_MODEL), row), _mod_spec(l, mod_row), _layer_spec(pw['norm_g'], l),
                 pl.BlockSpec((tm, FN_WIDTH), row), pl.BlockSpec((tm, POOL_WIDTH), row),
                 pl.BlockSpec((tm, ATT_WIDTH), row)] + [_layer_spec(pw[n], l) for n in names])
    args = [h, mod, pw['norm_g'], xa, xb, xc] + [pw[n] for n in names]
    if final:
        in_specs.append(pl.BlockSpec((1, D_MODEL), lambda i: (0, 0)))
        args.append(final_g)
    return pl.pallas_call(
        functools.partial(_out_kernel, final=final),
        grid=(t // tm,),
        in_specs=in_specs,
        out_specs=pl.BlockSpec((tm, D_MODEL), row),
        out_shape=jax.ShapeDtypeStruct((t, D_MODEL), F32),
        compiler_params=_params(1),
        name="out_final" if final else "out",
    )(*args)


_ROPE_PAD = ((QK_NOPE, HEAD_SLAB - QK_NOPE - QK_ROPE),)
W1_ROW_TILE = 256
W1_PLAIN_TILES = _W1_KV[0] // W1_ROW_TILE
WG_ROW_TILE = 512


def _w1_source_row(k):
    direct = _OFF_KV // W1_ROW_TILE
    return jnp.where(k < direct, k * W1_ROW_TILE,
                     jnp.where(k < W1_PLAIN_TILES, _OFF_CZ + (k - direct) * W1_ROW_TILE, _OFF_KV))


def _pack_w1_kernel(w_ref, o_ref):
    k = pl.program_id(1)
    x = w_ref[0]

    @pl.when(k < W1_PLAIN_TILES)
    def _():
        o_ref[...] = x.astype(BF16)

    @pl.when(k == W1_PLAIN_TILES)
    def _():
        kr = x[KV_RANK:KV_RANK + QK_ROPE, :]
        q = QK_ROPE // 4
        rot = jnp.concatenate([-kr[q:2 * q], kr[:q], -kr[3 * q:], kr[2 * q:3 * q]], axis=0)
        zeros = jnp.zeros((QK_NOPE, D_MODEL), F32)
        o_ref[...] = jnp.concatenate([x[:KV_RANK], zeros, kr, rot], axis=0).astype(BF16)


def _pack_cast_kernel(w_ref, o_ref):
    o_ref[...] = w_ref[0].astype(BF16)


def _pack_w_in(w_in):
    wt = jnp.swapaxes(w_in, 1, 2)
    g_width = w_in.shape[2] - _OFF_G
    w1t = pl.pallas_call(
        _pack_w1_kernel,
        grid=(DEPTH, W1_WIDTH // W1_ROW_TILE),
        in_specs=[pl.BlockSpec((pl.Element(1), pl.Element(W1_ROW_TILE), pl.Element(D_MODEL)),
                               lambda l, k: (l, pl.multiple_of(_w1_source_row(k), 32), 0))],
        out_specs=pl.BlockSpec((None, W1_ROW_TILE, D_MODEL), lambda l, k: (l, k, 0)),
        out_shape=jax.ShapeDtypeStruct((DEPTH, W1_WIDTH, D_MODEL), BF16),
        compiler_params=_params(2),
        name="pack_w1",
    )(wt)
    wgt = pl.pallas_call(
        _pack_cast_kernel,
        grid=(DEPTH, g_width // WG_ROW_TILE),
        in_specs=[pl.BlockSpec((pl.Element(1), pl.Element(WG_ROW_TILE), pl.Element(D_MODEL)),
                               lambda l, k: (l, pl.multiple_of(_OFF_G + k * WG_ROW_TILE, 32), 0))],
        out_specs=pl.BlockSpec((None, WG_ROW_TILE, D_MODEL), lambda l, k: (l, k, 0)),
        out_shape=jax.ShapeDtypeStruct((DEPTH, g_width, D_MODEL), BF16),
        compiler_params=_params(2),
        name="pack_wg",
    )(wt)
    return w1t, wgt


def _pack_weights(norm_g, w_in, pool_w, pool_scale, q_norm_g, w_q_up, kv_norm_g, w_kv_up, w_br_a, w_br_b, w_br_c,
                  w_out):
    lead = ((0, 0), (0, 0))
    w1, wg = _pack_w_in(w_in)
    wide = N_HEADS * HEAD_SLAB
    wq_h = w_q_up.reshape(DEPTH, Q_RANK, N_HEADS, QK_NOPE + QK_ROPE)
    wq = jnp.pad(wq_h, lead + ((0, 0), (0, _ROPE_PAD[0][1]))).reshape(DEPTH, Q_RANK, wide).astype(BF16)
    wkv_h = w_kv_up.reshape(DEPTH, KV_RANK, N_HEADS, QK_NOPE + V_DIM)
    wk = jnp.pad(wkv_h[..., :QK_NOPE], lead + ((0, 0), (0, HEAD_SLAB - QK_NOPE))).reshape(DEPTH, KV_RANK, wide)
    wvt = wkv_h[..., QK_NOPE:].reshape(DEPTH, KV_RANK, ATT_WIDTH).transpose(0, 2, 1)
    groups = len(POOL_WINDOWS)
    eye = jnp.eye(groups, dtype=F32)
    pool_bd = (pool_w[:, :, :, None, :] * eye[None, :, None, :, None]).reshape(DEPTH, POOL_WIDTH, POOL_WIDTH)
    return {
        'norm_g': norm_g.reshape(DEPTH, 1, D_MODEL), 'w1': w1, 'wg': wg,
        'q_norm_g': q_norm_g.reshape(DEPTH, 1, Q_RANK), 'wq': wq,
        'kv_norm_g': kv_norm_g.reshape(DEPTH, 1, KV_RANK), 'wk': wk.astype(BF16), 'wvt': wvt.astype(BF16),
        'pool_w': pool_bd.astype(BF16), 'pool_scale': pool_scale.reshape(DEPTH, 1, POOL_WIDTH),
        'w_br_a': w_br_a.astype(BF16), 'w_br_b': w_br_b.astype(BF16), 'w_br_c': w_br_c.astype(BF16),
        'w_out': w_out.astype(BF16),
    }


def _rope_tables(seq):
    f32 = np.float32
    t = np.arange(seq)
    row = (t // GRID_W).astype(f32)
    col = (t % GRID_W).astype(f32)
    half = QK_ROPE // 2
    freqs = f32(ROPE_THETA) ** (-np.arange(0, half, 2, dtype=f32) / f32(half))
    ar = row[:, None] * freqs
    ac = col[:, None] * freqs
    cos = np.ones((seq, HEAD_SLAB), f32)
    sin = np.zeros((seq, HEAD_SLAB), f32)
    cos[:, QK_NOPE:QK_NOPE + QK_ROPE] = np.concatenate([np.cos(ar), np.cos(ar), np.cos(ac), np.cos(ac)], axis=-1)
    sin[:, QK_NOPE:QK_NOPE + QK_ROPE] = np.concatenate([np.sin(ar), np.sin(ar), np.sin(ac), np.sin(ac)], axis=-1)
    return cos, sin


TOKEN_TILE = 1024
SAMPLE_Q_TILE = 512
SAMPLE_HEADS_PER_STEP = 4
SAMPLE_KEY_CHUNK = 2048
PROMPT_SEQS_PER_STEP = 4


def kernel(x_prompt, x_sample, cache_ckv, cache_krope, c, c_ctx, norm_g, w_mod, b_mod, w_in, pool_w, pool_scale,
           q_norm_g, w_q_up, kv_norm_g, w_kv_up, w_br_a, w_br_b, w_br_c, w_out, final_norm_g):
    batch, seq, _ = x_prompt.shape
    dec_batch, dec_seq, _ = x_sample.shape
    past = cache_ckv.shape[2]
    tm = TOKEN_TILE

    mod_rows = 8
    cvec = jnp.concatenate([c_ctx[None, :], c, jnp.zeros((mod_rows - 1 - dec_batch, D_MODEL), F32)], axis=0)
    mod = _modulation(cvec, w_mod, b_mod).reshape(DEPTH, mod_rows, 3, D_MODEL)
    prompt_row = lambda i: 0
    tiles_per_sample = dec_seq // tm
    assert past % HEAD_SLAB == 0 and dec_seq % SAMPLE_KEY_CHUNK == 0
    sample_row = lambda i: 1 + i // tiles_per_sample

    cos, sin = _rope_tables(dec_seq)
    rope = (cos, sin, lambda i: i % tiles_per_sample)
    final_g = final_norm_g.reshape(1, D_MODEL)
    pw = _pack_weights(norm_g, w_in, pool_w, pool_scale, q_norm_g, w_q_up, kv_norm_g, w_kv_up, w_br_a, w_br_b,
                       w_br_c, w_out)
    cache_kr_slab = jnp.pad(cache_krope, ((0, 0), (0, 0), (0, 0)) + _ROPE_PAD)

    hp = x_prompt.reshape(batch * seq, D_MODEL)
    hs = x_sample.reshape(dec_batch * dec_seq, D_MODEL)
    ckv_list, kr_list = [], []
    for l in range(DEPTH):
        last = final_g if l == DEPTH - 1 else None

        a_in, sa, b_in, sb, q, k, vt, ckv, kr, sc = _inproj(hp, mod, prompt_row, pw, l, None, tm)
        ckv_list.append(ckv.reshape(batch, seq, KV_RANK))
        kr_list.append(kr.reshape(batch, seq, QK_ROPE))
        xa = _fourier_direct(a_in, sa, seq, PROMPT_SEQS_PER_STEP)
        xb = _pool(b_in, sb, pw, l, seq, PROMPT_SEQS_PER_STEP)
        xc = _attention(q, k, vt, sc, None, batch, seq, seq, seq, N_HEADS, seq, lookahead=N_HEADS)
        hp = _out(hp, mod, prompt_row, xa, xb, xc, pw, l, last, tm)

        a_in, sa, b_in, sb, q, k, vt, _, _, sc = _inproj(hs, mod, sample_row, pw, l, rope, tm)
        cache = _cache_kv(cache_ckv, cache_kr_slab, pw, l)
        xa = _fourier_fft(a_in, sa, dec_seq)
        xb = _pool(b_in, sb, pw, l, dec_seq, 1)
        xc = _attention(q, k, vt, sc, cache, dec_batch, dec_seq, dec_seq, SAMPLE_Q_TILE, SAMPLE_HEADS_PER_STEP,
                        SAMPLE_KEY_CHUNK, lookahead=1)
        hs = _out(hs, mod, sample_row, xa, xb, xc, pw, l, last, tm)

    y_prompt = hp.reshape(batch, seq, D_MODEL)
    y_sample = hs.reshape(dec_batch, dec_seq, D_MODEL)
    return (y_prompt, y_sample, jnp.stack(ckv_list, axis=1), jnp.stack(kr_list, axis=1))
```

```python
import functools

import numpy as np
import jax
import jax.numpy as jnp
from jax import lax
from jax.experimental import pallas as pl
from jax.experimental.pallas import tpu as pltpu

D_MODEL = 1024
DEPTH = 2
GRID_W = 64
EPS = 1e-6
FN_WIDTH = 256
FN_GC = 64
POOL_WINDOWS = (2, 4, 8, 16)
POOL_WIDTH = 256
POOL_GC = 64
N_HEADS = 8
QK_NOPE = 64
QK_ROPE = 32
V_DIM = 64
Q_RANK = 256
KV_RANK = 128
ATT_WIDTH = 512
ROPE_THETA = 10000.0
HEAD_SLAB = 128
QK_SCALE = (QK_NOPE + QK_ROPE) ** -0.5
Q_PRESCALE = QK_SCALE * float(np.log2(np.e))

VMEM_LIMIT_BYTES = 56 * 1024 * 1024

F32 = jnp.float32
BF16 = jnp.bfloat16

_OFF_A, _OFF_B, _OFF_Q, _OFF_KV, _OFF_KR, _OFF_CZ, _OFF_G = 0, 512, 1024, 1280, 1408, 1440, 1952
_W1_A = (0, 512)
_W1_B = (512, 1024)
_W1_Q = (1024, 1280)
_W1_CZ = (1280, 1792)
_W1_KV = (1792, 2048)
W1_WIDTH = 2048


def _params(n_parallel):
    return pltpu.CompilerParams(dimension_semantics=("arbitrary",) * n_parallel,
                                vmem_limit_bytes=VMEM_LIMIT_BYTES)


def _dot(a, b):
    return jnp.dot(a, b, preferred_element_type=F32)


def _silu(x):
    return x * jax.nn.sigmoid(x)


def _rms(x, g):
    r = lax.rsqrt(jnp.mean(x * x, axis=-1, keepdims=True) + EPS)
    return (x * r) * g


def _modulated_norm(h, norm_g, mod_ref):
    shift = mod_ref[0:1, :]
    scale = mod_ref[1:2, :]
    return _rms(h, norm_g) * (1.0 + scale) + shift


def _split_bf16(x):
    hi = x.astype(BF16)
    lo = (x - hi.astype(F32)).astype(BF16)
    return hi, lo


def _dot3_right(x, m_hi, m_lo):
    x_hi, x_lo = _split_bf16(x)
    return _dot(x_hi, m_hi) + _dot(x_lo, m_hi) + _dot(x_hi, m_lo)


def _dot3_left(m_hi, m_lo, x):
    x_hi, x_lo = _split_bf16(x)
    return _dot(m_hi, x_hi) + _dot(m_hi, x_lo) + _dot(m_lo, x_hi)


def _mod_kernel(c_ref, w_ref, b_ref, o_ref):
    s = _silu(c_ref[...]).astype(BF16)
    o_ref[...] = _dot(s, w_ref[...].astype(BF16)) + b_ref[...]


def _modulation(cvec, w_mod, b_mod):
    rows = cvec.shape[0]
    tn = 768
    return pl.pallas_call(
        _mod_kernel,
        grid=(DEPTH, 3 * D_MODEL // tn),
        in_specs=[pl.BlockSpec((rows, D_MODEL), lambda l, j: (0, 0)),
                  pl.BlockSpec((None, D_MODEL, tn), lambda l, j: (l, 0, j)),
                  pl.BlockSpec((None, 1, tn), lambda l, j: (l, 0, j))],
        out_specs=pl.BlockSpec((None, rows, tn), lambda l, j: (l, 0, j)),
        out_shape=jax.ShapeDtypeStruct((DEPTH, rows, 3 * D_MODEL), F32),
        compiler_params=_params(2),
        name="modulation",
    )(cvec, w_mod, b_mod.reshape(DEPTH, 1, 3 * D_MODEL))


_NT = (((1,), (1,)), ((), ()))


def _key_value_heads(ckv_bf16, kr_slab, wk_ref, wvt_ref, k_ref, vt_ref):
    kn = _dot(ckv_bf16, wk_ref[...])
    for h in range(N_HEADS):
        sl = slice(HEAD_SLAB * h, HEAD_SLAB * (h + 1))
        k_ref[:, sl] = (kn[:, sl] + kr_slab).astype(BF16)
    vt_ref[...] = lax.dot_general(wvt_ref[...], ckv_bf16, _NT, preferred_element_type=F32).astype(BF16)


def _inproj_kernel(*refs, use_rope):
    if use_rope:
        (h_ref, mod_ref, ng_ref, w1_ref, qg_ref, wq_ref, kvg_ref, wk_ref, wvt_ref, cos_ref, sin_ref,
         ain_ref, sa_ref, bin_ref, sb_ref, q_ref, k_ref, vt_ref, ckv_ref, kr_ref, sc_ref) = refs
    else:
        (h_ref, mod_ref, ng_ref, w1_ref, qg_ref, wq_ref, kvg_ref, wk_ref, wvt_ref,
         ain_ref, sa_ref, bin_ref, sb_ref, q_ref, k_ref, vt_ref, ckv_ref, kr_ref, sc_ref) = refs

    xn = _modulated_norm(h_ref[...], ng_ref[...], mod_ref).astype(BF16)

    def proj(cols):
        return lax.dot_general(xn, w1_ref[cols[0]:cols[1], :], _NT, preferred_element_type=F32)

    a = proj(_W1_A)
    ain_ref[...] = a[:, :FN_WIDTH]
    sa_ref[...] = _silu(a[:, FN_WIDTH:]).astype(BF16)
    b = proj(_W1_B)
    bin_ref[...] = b[:, :POOL_WIDTH]
    sb_ref[...] = _silu(b[:, POOL_WIDTH:]).astype(BF16)
    sc_ref[...] = _silu(proj(_W1_CZ)).astype(BF16)

    qn = _rms(proj(_W1_Q), qg_ref[...]).astype(BF16)
    q = _dot(qn, wq_ref[...])
    lane = lax.broadcasted_iota(jnp.int32, (q.shape[0], HEAD_SLAB), 1)
    if use_rope:
        cos = cos_ref[...]
        sin = sin_ref[...]
        takes_upper = (lane % (QK_ROPE // 2)) < (QK_ROPE // 4)
        for h in range(N_HEADS):
            sl = slice(HEAD_SLAB * h, HEAD_SLAB * (h + 1))
            q_h = q[:, sl]
            upper = pltpu.roll(q_h, HEAD_SLAB - QK_ROPE // 4, axis=1)
            lower = pltpu.roll(q_h, QK_ROPE // 4, axis=1)
            q_rot = jnp.where(takes_upper, -upper, lower)
            q_ref[:, sl] = ((q_h * cos + q_rot * sin) * Q_PRESCALE).astype(BF16)
    else:
        q_ref[...] = (q * Q_PRESCALE).astype(BF16)

    kv = proj(_W1_KV)
    ckv = _rms(kv[:, :KV_RANK], kvg_ref[...])
    ckv_ref[...] = ckv
    slab = kv[:, KV_RANK:]
    kr_ref[...] = slab[:, QK_NOPE:QK_NOPE + QK_ROPE]
    kr = jnp.where(lane < QK_NOPE + QK_ROPE, slab, 0.0)
    if use_rope:
        kr_rot = pltpu.roll(slab, HEAD_SLAB - QK_ROPE, axis=1)
        kr = kr * cos + kr_rot * sin
    _key_value_heads(ckv.astype(BF16), kr, wk_ref, wvt_ref, k_ref, vt_ref)


def _layer_spec(arr, l):
    return pl.BlockSpec((None,) + arr.shape[1:], lambda *_: (l,) + (0,) * (arr.ndim - 1),
                        pipeline_mode=pl.Buffered(1))


def _mod_spec(l, mod_row):
    return pl.BlockSpec((None, None, 3, D_MODEL), lambda i: (l, mod_row(i), 0, 0))


def _inproj(h, mod, mod_row, pw, l, rope, tm):
    t = h.shape[0]
    use_rope = rope is not None
    row = lambda i: (i, 0)
    names = ['norm_g', 'w1', 'q_norm_g', 'wq', 'kv_norm_g', 'wk', 'wvt']
    in_specs = [pl.BlockSpec((tm, D_MODEL), row), _mod_spec(l, mod_row)] + [_layer_spec(pw[n], l) for n in names]
    args = [h, mod] + [pw[n] for n in names]
    if use_rope:
        cos, sin, rope_tile = rope
        in_specs += [pl.BlockSpec((tm, HEAD_SLAB), lambda i: (rope_tile(i), 0))] * 2
        args += [cos, sin]
    wide = N_HEADS * HEAD_SLAB
    token_outs = lambda ws: ([pl.BlockSpec((tm, w), row) for w, _ in ws],
                             [jax.ShapeDtypeStruct((t, w), dt) for w, dt in ws])
    specs_a, shapes_a = token_outs([(FN_WIDTH, F32), (FN_WIDTH, BF16), (POOL_WIDTH, F32), (POOL_WIDTH, BF16),
                                    (wide, BF16)])
    specs_b, shapes_b = token_outs([(KV_RANK, F32), (QK_ROPE, F32), (ATT_WIDTH, BF16)])
    out_specs = specs_a + [pl.BlockSpec((tm, wide), row), pl.BlockSpec((ATT_WIDTH, tm), lambda i: (0, i))] + specs_b
    out_shape = shapes_a + [jax.ShapeDtypeStruct((t, wide), BF16), jax.ShapeDtypeStruct((ATT_WIDTH, t), BF16)] + shapes_b
    return pl.pallas_call(
        functools.partial(_inproj_kernel, use_rope=use_rope),
        grid=(t // tm,),
        in_specs=in_specs,
        out_specs=out_specs,
        out_shape=out_shape,
        compiler_params=_params(1),
        name="inproj_rope" if use_rope else "inproj",
    )(*args)


def _cache_kv_kernel(ckv_ref, kr_ref, wk_ref, wvt_ref, k_ref, vt_ref):
    _key_value_heads(ckv_ref[...].astype(BF16), kr_ref[...], wk_ref, wvt_ref, k_ref, vt_ref)


def _cache_kv(cache_ckv, cache_kr_slab, pw, l):
    batch, _, past, _ = cache_ckv.shape
    wide = N_HEADS * HEAD_SLAB
    cache_map = lambda b: (b, l, 0, 0)
    return pl.pallas_call(
        _cache_kv_kernel,
        grid=(batch,),
        in_specs=[pl.BlockSpec((None, None, past, KV_RANK), cache_map),
                  pl.BlockSpec((None, None, past, HEAD_SLAB), cache_map),
                  _layer_spec(pw['wk'], l), _layer_spec(pw['wvt'], l)],
        out_specs=[pl.BlockSpec((past, wide), lambda b: (b, 0)), pl.BlockSpec((ATT_WIDTH, past), lambda b: (0, b))],
        out_shape=[jax.ShapeDtypeStruct((batch * past, wide), BF16),
                   jax.ShapeDtypeStruct((ATT_WIDTH, batch * past), BF16)],
        compiler_params=_params(1),
        name="cache_kv",
    )(cache_ckv, cache_kr_slab, pw['wk'], pw['wvt'])


def _hi_lo(m):
    m = np.asarray(m, np.float32)
    hi = m.astype(BF16)
    return hi, (m - hi.astype(np.float32)).astype(BF16)


def _dft_cos_sin(n):
    k = np.arange(n)
    ang = 2.0 * np.pi * ((k[:, None] * k[None, :]) % n) / n
    return np.cos(ang), np.sin(ang)


def _channel_dft_tables():
    c, s = _dft_cos_sin(FN_GC)
    eye = np.eye(FN_WIDTH // FN_GC)
    return np.kron(eye, c), np.kron(eye, s)


def _fourier_direct_kernel(a_ref, sa_ref, cc_hi, cc_lo, ss_hi, ss_lo, m_hi, m_lo, o_ref, *, seq, norm):
    n = a_ref.shape[0] // seq
    a = a_ref[...]
    tc = _dot3_right(a, cc_hi[...], cc_lo[...])
    ts = _dot3_right(a, ss_hi[...], ss_lo[...])
    side_by_side = lambda x: jnp.concatenate([x[i * seq:(i + 1) * seq] for i in range(n)], axis=1)
    f = _dot3_left(m_hi[...], m_lo[...], jnp.concatenate([side_by_side(tc), side_by_side(ts)], axis=0))
    for i in range(n):
        rows = slice(i * seq, (i + 1) * seq)
        f_i = f[:, i * FN_WIDTH:(i + 1) * FN_WIDTH]
        o_ref[rows, :] = ((f_i * norm) * sa_ref[rows, :].astype(F32)).astype(BF16)


def _fourier_direct(a, sa, seq, seqs_per_step):
    t = a.shape[0]
    rows = seq * seqs_per_step
    cc, ss = _channel_dft_tables()
    cl, sl = _dft_cos_sin(seq)
    tables = [x for m in (cc, ss, np.concatenate([cl, -sl], axis=1)) for x in _hi_lo(m)]
    row = lambda i: (i, 0)
    const = lambda i: (0, 0)
    tspecs = [pl.BlockSpec((FN_WIDTH, FN_WIDTH), const)] * 4 + [pl.BlockSpec((seq, 2 * seq), const)] * 2
    return pl.pallas_call(
        functools.partial(_fourier_direct_kernel, seq=seq, norm=float((seq * FN_GC) ** -0.5)),
        grid=(t // rows,),
        in_specs=[pl.BlockSpec((rows, FN_WIDTH), row), pl.BlockSpec((rows, FN_WIDTH), row)] + tspecs,
        out_specs=pl.BlockSpec((rows, FN_WIDTH), row),
        out_shape=jax.ShapeDtypeStruct((t, FN_WIDTH), BF16),
        compiler_params=_params(1),
        name="fourier_direct",
    )(a, sa, *tables)


FFT_R = 64
FFT_PITCH = FFT_R + 4


def _fourier_fft_kernel(a_ref, sa_ref, cc_hi, cc_lo, ss_hi, ss_lo, m1_hi, m1_lo, m2_hi, m2_lo, twc_ref, tws_ref,
                        o_ref, zr0, zr1, zi0, zi1, yr0, yr1, yi0, yi1, *, seq, norm):
    r = FFT_R
    half = FN_WIDTH // 2
    chunk = 512

    def put(refs, rows, x):
        refs[0][rows, :] = x[:, :half]
        refs[1][rows, :] = x[:, half:]

    def get(refs, rows):
        return jnp.concatenate([refs[0][rows, :], refs[1][rows, :]], axis=1)

    block = lambda j: slice(FFT_PITCH * j, FFT_PITCH * j + r)
    across = lambda i: pl.ds(i, r, stride=FFT_PITCH)

    zr, zi, yr_s, yi_s = (zr0, zr1), (zi0, zi1), (yr0, yr1), (yi0, yi1)
    for c in range(seq // chunk):
        a = a_ref[c * chunk:(c + 1) * chunk, :]
        zr_c = _dot3_right(a, cc_hi[...], cc_lo[...])
        zi_c = -_dot3_right(a, ss_hi[...], ss_lo[...])
        for j in range(chunk // r):
            n1 = c * (chunk // r) + j
            put(zr, block(n1), zr_c[j * r:(j + 1) * r])
            put(zi, block(n1), zi_c[j * r:(j + 1) * r])
    for n2 in range(r):
        z = jnp.concatenate([get(zr, across(n2)), get(zi, across(n2))], axis=0)
        y = _dot3_left(m1_hi[...], m1_lo[...], z)
        yr, yi = y[:r], y[r:]
        tw = slice(n2 * r, (n2 + 1) * r)
        cos = jnp.concatenate([twc_ref[tw, :]] * 2, axis=1)
        sin = jnp.concatenate([tws_ref[tw, :]] * 2, axis=1)
        put(yr_s, block(n2), yr * cos + yi * sin)
        put(yi_s, block(n2), yi * cos - yr * sin)
    for k1 in range(r):
        y = jnp.concatenate([get(yr_s, across(k1)), get(yi_s, across(k1))], axis=0)
        put(zr, across(k1), _dot3_left(m2_hi[...], m2_lo[...], y))
    for k2 in range(r):
        rows = slice(k2 * r, (k2 + 1) * r)
        o_ref[rows, :] = ((get(zr, block(k2)) * norm) * sa_ref[rows, :].astype(F32)).astype(BF16)


def _fourier_fft(a, sa, seq):
    assert seq == FFT_R * FFT_R
    t = a.shape[0]
    cc, ss = _channel_dft_tables()
    c, s = _dft_cos_sin(FFT_R)
    m1 = np.block([[c, s], [-s, c]])
    m2 = np.concatenate([c, s], axis=1)
    n2 = np.arange(FFT_R)[:, None]
    k1 = np.arange(FFT_R)[None, :]
    ang = (2.0 * np.pi * (n2 * k1) / seq).reshape(seq, 1)
    twc = np.ascontiguousarray(np.broadcast_to(np.cos(ang).astype(np.float32), (seq, 128)))
    tws = np.ascontiguousarray(np.broadcast_to(np.sin(ang).astype(np.float32), (seq, 128)))
    tables = [x for m in (cc, ss, m1, m2) for x in _hi_lo(m)]
    row = lambda i: (i, 0)
    const = lambda i: (0, 0)
    tspecs = ([pl.BlockSpec((FN_WIDTH, FN_WIDTH), const)] * 4 + [pl.BlockSpec((2 * FFT_R, 2 * FFT_R), const)] * 2
              + [pl.BlockSpec((FFT_R, 2 * FFT_R), const)] * 2 + [pl.BlockSpec((seq, 128), const)] * 2)
    return pl.pallas_call(
        functools.partial(_fourier_fft_kernel, seq=seq, norm=float((seq * FN_GC) ** -0.5)),
        grid=(t // seq,),
        in_specs=[pl.BlockSpec((seq, FN_WIDTH), row), pl.BlockSpec((seq, FN_WIDTH), row)] + tspecs,
        out_specs=pl.BlockSpec((seq, FN_WIDTH), row),
        out_shape=jax.ShapeDtypeStruct((t, FN_WIDTH), BF16),
        scratch_shapes=[pltpu.VMEM((FFT_PITCH * FFT_R, FN_WIDTH // 2), F32)] * 8,
        compiler_params=_params(1),
        name="fourier_fft",
    )(a, sa, *tables, twc, tws)


POOL_HALO = 8
POOL_CHUNK = 256


def _pool_kernel(b_ref, sb_ref, pw_ref, ps_ref, o_ref, pad_ref, *, seq):
    zeros = jnp.zeros((POOL_HALO, POOL_WIDTH), F32)
    lane = lax.broadcasted_iota(jnp.int32, (POOL_CHUNK, 128), 1)
    low_group = lane < POOL_GC
    for i in range(pad_ref.shape[0]):
        pad_ref[i, 0:POOL_HALO, :] = zeros
        pad_ref[i, POOL_HALO + seq:, :] = zeros
        pad_ref[i, POOL_HALO:POOL_HALO + seq, :] = b_ref[i * seq:(i + 1) * seq, :]

    for i, c in [(i, c) for i in range(pad_ref.shape[0]) for c in range(seq // POOL_CHUNK)]:
        r0 = c * POOL_CHUNK
        t = lax.broadcasted_iota(jnp.int32, (POOL_CHUNK, 128), 0) + r0

        def inv_count(w):
            left = w // 2
            right = w - 1 - left
            lo = jnp.maximum(t - left, 0)
            hi = jnp.minimum(t + right, seq - 1)
            return 1.0 / (hi - lo + 1).astype(F32)

        def ld(off, col):
            start = POOL_HALO + r0 + off
            return pad_ref[i, start:start + POOL_CHUNK, 128 * col:128 * (col + 1)]

        u0 = ld(0, 0)
        p2 = ld(-1, 0) + u0
        p4 = p2 + ld(-2, 0) + ld(1, 0)
        pooled0 = jnp.where(low_group, p2 * inv_count(2), p4 * inv_count(4)) - u0
        u1 = ld(0, 1)
        p8 = u1
        for off in (-4, -3, -2, -1, 1, 2, 3):
            p8 = p8 + ld(off, 1)
        p16 = p8
        for off in (-8, -7, -6, -5, 4, 5, 6, 7):
            p16 = p16 + ld(off, 1)
        pooled1 = jnp.where(low_group, p8 * inv_count(8), p16 * inv_count(16)) - u1

        pooled = jnp.concatenate([pooled0, pooled1], axis=1).astype(BF16)
        mixed = _dot(pooled, pw_ref[...]) * ps_ref[...]
        rows = slice(i * seq + r0, i * seq + r0 + POOL_CHUNK)
        o_ref[rows, :] = (mixed * sb_ref[rows, :].astype(F32)).astype(BF16)


def _pool(b, sb, pw, l, seq, seqs_per_step):
    t = b.shape[0]
    rows = seq * seqs_per_step
    row = lambda i: (i, 0)
    return pl.pallas_call(
        functools.partial(_pool_kernel, seq=seq),
        grid=(t // rows,),
        in_specs=[pl.BlockSpec((rows, POOL_WIDTH), row), pl.BlockSpec((rows, POOL_WIDTH), row),
                  _layer_spec(pw['pool_w'], l), _layer_spec(pw['pool_scale'], l)],
        out_specs=pl.BlockSpec((rows, POOL_WIDTH), row),
        out_shape=jax.ShapeDtypeStruct((t, POOL_WIDTH), BF16),
        scratch_shapes=[pltpu.VMEM((seqs_per_step, seq + 2 * POOL_HALO, POOL_WIDTH), F32)],
        compiler_params=_params(1),
        name="pool",
    )(b, sb, pw['pool_w'], pw['pool_scale'])


def _attn_kernel(*refs, heads, chunk, use_cache, lookahead):
    if use_cache:
        q_ref, k_ref, vt_ref, kc_ref, vct_ref, sc_ref, o_ref, s_ref, m_ref, l_ref, acc_ref = refs
        sources = ((k_ref, vt_ref), (kc_ref, vct_ref))
    else:
        q_ref, k_ref, vt_ref, sc_ref, o_ref, s_ref, m_ref, l_ref, acc_ref = refs
        sources = ((k_ref, vt_ref),)
    tq = q_ref.shape[0]
    chunks, row = [], 0
    for keys, values in sources:
        for off in range(0, keys.shape[0], chunk):
            size = min(chunk, keys.shape[0] - off)
            chunks.append((keys, values, off, size, row))
            row += size

    slots = s_ref.shape[0]

    def scores(h, c):
        keys, _, off, size, row = chunks[c]
        sl = slice(HEAD_SLAB * h, HEAD_SLAB * (h + 1))
        s = lax.dot_general(keys[off:off + size, sl], q_ref[:, sl], _NT, preferred_element_type=F32)
        s_ref[h % slots, row:row + size, :] = s
        m_ref[h % slots] = jnp.maximum(m_ref[h % slots], jnp.max(s.reshape(size // 8, 8, tq), axis=0))

    def weigh(h, c, m):
        _, values, off, size, row = chunks[c]
        p = jnp.exp2(s_ref[h % slots, row:row + size, :] - m)
        l_ref[h % slots] += jnp.sum(p.reshape(size // 8, 8, tq), axis=0)
        acc_ref[h % slots] += _dot(values[V_DIM * h:V_DIM * (h + 1), off:off + size], p.astype(BF16))

    outs = []
    for t in range(heads + lookahead):
        h_w, h_s = t - lookahead, t
        if h_s < heads:
            m_ref[h_s % slots] = jnp.full((8, tq), -jnp.inf, F32)
        if h_w >= 0:
            m = jnp.max(m_ref[h_w % slots], axis=0, keepdims=True)
            l_ref[h_w % slots] = jnp.zeros((8, tq), F32)
            acc_ref[h_w % slots] = jnp.zeros((V_DIM, tq), F32)
        for c in range(len(chunks)):
            if h_w >= 0:
                weigh(h_w, c, m)
            if h_s < heads:
                scores(h_s, c)
        if h_w >= 0:
            denom = jnp.sum(l_ref[h_w % slots], axis=0, keepdims=True)
            outs.append(acc_ref[h_w % slots] * (1.0 / denom))
            if h_w % 2 == 1:
                o_pair = jnp.concatenate(outs, axis=0).T
                outs = []
                sl = slice(HEAD_SLAB * (h_w // 2), HEAD_SLAB * (h_w // 2 + 1))
                o_ref[:, sl] = (o_pair * sc_ref[:, sl].astype(F32)).astype(BF16)


def _attention(q, k, vt, sc, cache, batch, lq, lk, tq, heads_per_step, chunk, lookahead):
    use_cache = cache is not None
    nq = lq // tq
    n_hp = N_HEADS // heads_per_step
    qw = heads_per_step * HEAD_SLAB
    ow = heads_per_step * V_DIM
    q_map = lambda b, g, i: (b * nq + i, g)
    k_map = lambda b, g, i: (b, g)
    vt_map = lambda b, g, i: (g, b)
    in_specs = [pl.BlockSpec((tq, qw), q_map), pl.BlockSpec((lk, qw), k_map), pl.BlockSpec((ow, lk), vt_map)]
    args = [q, k, vt]
    lc = 0
    if use_cache:
        lc = cache[0].shape[0] // batch
        in_specs += [pl.BlockSpec((lc, qw), k_map), pl.BlockSpec((ow, lc), vt_map)]
        args += list(cache)
    in_specs.append(pl.BlockSpec((tq, ow), q_map))
    args.append(sc)
    slots = min(heads_per_step, lookahead + 1)
    return pl.pallas_call(
        functools.partial(_attn_kernel, heads=heads_per_step, chunk=chunk, use_cache=use_cache, lookahead=lookahead),
        grid=(batch, n_hp, nq),
        in_specs=in_specs,
        out_specs=pl.BlockSpec((tq, ow), q_map),
        out_shape=jax.ShapeDtypeStruct((batch * lq, ATT_WIDTH), BF16),
        scratch_shapes=[pltpu.VMEM((slots, lk + lc, tq), F32), pltpu.VMEM((slots, 8, tq), F32),
                        pltpu.VMEM((slots, 8, tq), F32), pltpu.VMEM((slots, V_DIM, tq), F32)],
        compiler_params=_params(3),
        name="attention_cache" if use_cache else "attention",
    )(*args)


def _out_kernel(*refs, final):
    if final:
        h_ref, mod_ref, ng_ref, xa_ref, xb_ref, xc_ref, wa_ref, wb_ref, wc_ref, wg_ref, wo_ref, fg_ref, o_ref = refs
    else:
        h_ref, mod_ref, ng_ref, xa_ref, xb_ref, xc_ref, wa_ref, wb_ref, wc_ref, wg_ref, wo_ref, o_ref = refs
    h = h_ref[...]
    xn = _modulated_norm(h, ng_ref[...], mod_ref).astype(BF16)
    y = None
    for i, (x_ref, w_ref) in enumerate(((xa_ref, wa_ref), (xb_ref, wb_ref), (xc_ref, wc_ref))):
        wg_i = wg_ref[i * D_MODEL:(i + 1) * D_MODEL, :]
        g = jax.nn.sigmoid(lax.dot_general(xn, wg_i, _NT, preferred_element_type=F32))
        term = g * _dot(x_ref[...], w_ref[...])
        y = term if y is None else y + term
    h_new = h + mod_ref[2:3, :] * _dot(y.astype(BF16), wo_ref[...])
    if final:
        o_ref[...] = _rms(h_new, fg_ref[...])
    else:
        o_ref[...] = h_new


def _out(h, mod, mod_row, xa, xb, xc, pw, l, final_g, tm):
    t = h.shape[0]
    final = final_g is not None
    row = lambda i: (i, 0)
    names = ['w_br_a', 'w_br_b', 'w_br_c', 'wg', 'w_out']
    in_specs = ([pl.BlockSpec((tm, D_MODEL), row), _mod_spec(l, mod_row), _layer_spec(pw['norm_g'], l),
                 pl.BlockSpec((tm, FN_WIDTH), row), pl.BlockSpec((tm, POOL_WIDTH), row),
                 pl.BlockSpec((tm, ATT_WIDTH), row)] + [_layer_spec(pw[n], l) for n in names])
    args = [h, mod, pw['norm_g'], xa, xb, xc] + [pw[n] for n in names]
    if final:
        in_specs.append(pl.BlockSpec((1, D_MODEL), lambda i: (0, 0)))
        args.append(final_g)
    return pl.pallas_call(
        functools.partial(_out_kernel, final=final),
        grid=(t // tm,),
        in_specs=in_specs,
        out_specs=pl.BlockSpec((tm, D_MODEL), row),
        out_shape=jax.ShapeDtypeStruct((t, D_MODEL), F32),
        compiler_params=_params(1),
        name="out_final" if final else "out",
    )(*args)


_ROPE_PAD = ((QK_NOPE, HEAD_SLAB - QK_NOPE - QK_ROPE),)
W1_ROW_TILE = 256
W1_PLAIN_TILES = _W1_KV[0] // W1_ROW_TILE
WG_ROW_TILE = 512


def _w1_source_row(k):
    direct = _OFF_KV // W1_ROW_TILE
    return jnp.where(k < direct, k * W1_ROW_TILE,
                     jnp.where(k < W1_PLAIN_TILES, _OFF_CZ + (k - direct) * W1_ROW_TILE, _OFF_KV))


def _pack_w1_kernel(w_ref, o_ref):
    k = pl.program_id(1)
    x = w_ref[0]

    @pl.when(k < W1_PLAIN_TILES)
    def _():
        o_ref[...] = x.astype(BF16)

    @pl.when(k == W1_PLAIN_TILES)
    def _():
        kr = x[KV_RANK:KV_RANK + QK_ROPE, :]
        q = QK_ROPE // 4
        rot = jnp.concatenate([-kr[q:2 * q], kr[:q], -kr[3 * q:], kr[2 * q:3 * q]], axis=0)
        zeros = jnp.zeros((QK_NOPE, D_MODEL), F32)
        o_ref[...] = jnp.concatenate([x[:KV_RANK], zeros, kr, rot], axis=0).astype(BF16)


def _pack_cast_kernel(w_ref, o_ref):
    o_ref[...] = w_ref[0].astype(BF16)


def _pack_w_in(w_in):
    wt = jnp.swapaxes(w_in, 1, 2)
    g_width = w_in.shape[2] - _OFF_G
    w1t = pl.pallas_call(
        _pack_w1_kernel,
        grid=(DEPTH, W1_WIDTH // W1_ROW_TILE),
        in_specs=[pl.BlockSpec((pl.Element(1), pl.Element(W1_ROW_TILE), pl.Element(D_MODEL)),
                               lambda l, k: (l, pl.multiple_of(_w1_source_row(k), 32), 0))],
        out_specs=pl.BlockSpec((None, W1_ROW_TILE, D_MODEL), lambda l, k: (l, k, 0)),
        out_shape=jax.ShapeDtypeStruct((DEPTH, W1_WIDTH, D_MODEL), BF16),
        compiler_params=_params(2),
        name="pack_w1",
    )(wt)
    wgt = pl.pallas_call(
        _pack_cast_kernel,
        grid=(DEPTH, g_width // WG_ROW_TILE),
        in_specs=[pl.BlockSpec((pl.Element(1), pl.Element(WG_ROW_TILE), pl.Element(D_MODEL)),
                               lambda l, k: (l, pl.multiple_of(_OFF_G + k * WG_ROW_TILE, 32), 0))],
        out_specs=pl.BlockSpec((None, WG_ROW_TILE, D_MODEL), lambda l, k: (l, k, 0)),
        out_shape=jax.ShapeDtypeStruct((DEPTH, g_width, D_MODEL), BF16),
        compiler_params=_params(2),
        name="pack_wg",
    )(wt)
    return w1t, wgt


def _pack_weights(norm_g, w_in, pool_w, pool_scale, q_norm_g, w_q_up, kv_norm_g, w_kv_up, w_br_a, w_br_b, w_br_c,
                  w_out):
    lead = ((0, 0), (0, 0))
    w1, wg = _pack_w_in(w_in)
    wide = N_HEADS * HEAD_SLAB
    wq_h = w_q_up.reshape(DEPTH, Q_RANK, N_HEADS, QK_NOPE + QK_ROPE)
    wq = jnp.pad(wq_h, lead + ((0, 0), (0, _ROPE_PAD[0][1]))).reshape(DEPTH, Q_RANK, wide).astype(BF16)
    wkv_h = w_kv_up.reshape(DEPTH, KV_RANK, N_HEADS, QK_NOPE + V_DIM)
    wk = jnp.pad(wkv_h[..., :QK_NOPE], lead + ((0, 0), (0, HEAD_SLAB - QK_NOPE))).reshape(DEPTH, KV_RANK, wide)
    wvt = wkv_h[..., QK_NOPE:].reshape(DEPTH, KV_RANK, ATT_WIDTH).transpose(0, 2, 1)
    groups = len(POOL_WINDOWS)
    eye = jnp.eye(groups, dtype=F32)
    pool_bd = (pool_w[:, :, :, None, :] * eye[None, :, None, :, None]).reshape(DEPTH, POOL_WIDTH, POOL_WIDTH)
    return {
        'norm_g': norm_g.reshape(DEPTH, 1, D_MODEL), 'w1': w1, 'wg': wg,
        'q_norm_g': q_norm_g.reshape(DEPTH, 1, Q_RANK), 'wq': wq,
        'kv_norm_g': kv_norm_g.reshape(DEPTH, 1, KV_RANK), 'wk': wk.astype(BF16), 'wvt': wvt.astype(BF16),
        'pool_w': pool_bd.astype(BF16), 'pool_scale': pool_scale.reshape(DEPTH, 1, POOL_WIDTH),
        'w_br_a': w_br_a.astype(BF16), 'w_br_b': w_br_b.astype(BF16), 'w_br_c': w_br_c.astype(BF16),
        'w_out': w_out.astype(BF16),
    }


def _rope_tables(seq):
    f32 = np.float32
    t = np.arange(seq)
    row = (t // GRID_W).astype(f32)
    col = (t % GRID_W).astype(f32)
    half = QK_ROPE // 2
    freqs = f32(ROPE_THETA) ** (-np.arange(0, half, 2, dtype=f32) / f32(half))
    ar = row[:, None] * freqs
    ac = col[:, None] * freqs
    cos = np.ones((seq, HEAD_SLAB), f32)
    sin = np.zeros((seq, HEAD_SLAB), f32)
    cos[:, QK_NOPE:QK_NOPE + QK_ROPE] = np.concatenate([np.cos(ar), np.cos(ar), np.cos(ac), np.cos(ac)], axis=-1)
    sin[:, QK_NOPE:QK_NOPE + QK_ROPE] = np.concatenate([np.sin(ar), np.sin(ar), np.sin(ac), np.sin(ac)], axis=-1)
    return cos, sin


TOKEN_TILE = 1024
SAMPLE_Q_TILE = 512
SAMPLE_HEADS_PER_STEP = 8
SAMPLE_KEY_CHUNK = 2048
PROMPT_SEQS_PER_STEP = 4


def kernel(x_prompt, x_sample, cache_ckv, cache_krope, c, c_ctx, norm_g, w_mod, b_mod, w_in, pool_w, pool_scale,
           q_norm_g, w_q_up, kv_norm_g, w_kv_up, w_br_a, w_br_b, w_br_c, w_out, final_norm_g):
    batch, seq, _ = x_prompt.shape
    dec_batch, dec_seq, _ = x_sample.shape
    past = cache_ckv.shape[2]
    tm = TOKEN_TILE

    mod_rows = 8
    cvec = jnp.concatenate([c_ctx[None, :], c, jnp.zeros((mod_rows - 1 - dec_batch, D_MODEL), F32)], axis=0)
    mod = _modulation(cvec, w_mod, b_mod).reshape(DEPTH, mod_rows, 3, D_MODEL)
    prompt_row = lambda i: 0
    tiles_per_sample = dec_seq // tm
    assert past % HEAD_SLAB == 0 and dec_seq % SAMPLE_KEY_CHUNK == 0
    sample_row = lambda i: 1 + i // tiles_per_sample

    cos, sin = _rope_tables(dec_seq)
    rope = (cos, sin, lambda i: i % tiles_per_sample)
    final_g = final_norm_g.reshape(1, D_MODEL)
    pw = _pack_weights(norm_g, w_in, pool_w, pool_scale, q_norm_g, w_q_up, kv_norm_g, w_kv_up, w_br_a, w_br_b,
                       w_br_c, w_out)
    cache_kr_slab = jnp.pad(cache_krope, ((0, 0), (0, 0), (0, 0)) + _ROPE_PAD)

    hp = x_prompt.reshape(batch * seq, D_MODEL)
    hs = x_sample.reshape(dec_batch * dec_seq, D_MODEL)
    ckv_list, kr_list = [], []
    for l in range(DEPTH):
        last = final_g if l == DEPTH - 1 else None

        a_in, sa, b_in, sb, q, k, vt, ckv, kr, sc = _inproj(hp, mod, prompt_row, pw, l, None, tm)
        ckv_list.append(ckv.reshape(batch, seq, KV_RANK))
        kr_list.append(kr.reshape(batch, seq, QK_ROPE))
        xa = _fourier_direct(a_in, sa, seq, PROMPT_SEQS_PER_STEP)
        xb = _pool(b_in, sb, pw, l, seq, PROMPT_SEQS_PER_STEP)
        xc = _attention(q, k, vt, sc, None, batch, seq, seq, seq, N_HEADS, seq, lookahead=N_HEADS)
        hp = _out(hp, mod, prompt_row, xa, xb, xc, pw, l, last, tm)

        a_in, sa, b_in, sb, q, k, vt, _, _, sc = _inproj(hs, mod, sample_row, pw, l, rope, tm)
        cache = _cache_kv(cache_ckv, cache_kr_slab, pw, l)
        xa = _fourier_fft(a_in, sa, dec_seq)
        xb = _pool(b_in, sb, pw, l, dec_seq, 1)
        xc = _attention(q, k, vt, sc, cache, dec_batch, dec_seq, dec_seq, SAMPLE_Q_TILE, SAMPLE_HEADS_PER_STEP,
                        SAMPLE_KEY_CHUNK, lookahead=1)
        hs = _out(hs, mod, sample_row, xa, xb, xc, pw, l, last, tm)

    y_prompt = hp.reshape(batch, seq, D_MODEL)
    y_sample = hs.reshape(dec_batch, dec_seq, D_MODEL)
    return (y_prompt, y_sample, jnp.stack(ckv_list, axis=1), jnp.stack(kr_list, axis=1))
```

```python
import functools

import numpy as np
import jax
import jax.numpy as jnp
from jax import lax
from jax.experimental import pallas as pl
from jax.experimental.pallas import tpu as pltpu

D_MODEL = 1024
DEPTH = 2
GRID_W = 64
EPS = 1e-6
FN_WIDTH = 256
FN_GC = 64
POOL_WINDOWS = (2, 4, 8, 16)
POOL_WIDTH = 256
POOL_GC = 64
N_HEADS = 8
QK_NOPE = 64
QK_ROPE = 32
V_DIM = 64
Q_RANK = 256
KV_RANK = 128
ATT_WIDTH = 512
ROPE_THETA = 10000.0
HEAD_SLAB = 128
QK_SCALE = (QK_NOPE + QK_ROPE) ** -0.5
Q_PRESCALE = QK_SCALE * float(np.log2(np.e))

VMEM_LIMIT_BYTES = 56 * 1024 * 1024

F32 = jnp.float32
BF16 = jnp.bfloat16

_OFF_A, _OFF_B, _OFF_Q, _OFF_KV, _OFF_KR, _OFF_CZ, _OFF_G = 0, 512, 1024, 1280, 1408, 1440, 1952
_W1_A = (0, 512)
_W1_B = (512, 1024)
_W1_Q = (1024, 1280)
_W1_CZ = (1280, 1792)
_W1_KV = (1792, 2048)
W1_WIDTH = 2048


def _params(n_parallel):
    return pltpu.CompilerParams(dimension_semantics=("arbitrary",) * n_parallel,
                                vmem_limit_bytes=VMEM_LIMIT_BYTES)


def _dot(a, b):
    return jnp.dot(a, b, preferred_element_type=F32)


def _silu(x):
    return x * jax.nn.sigmoid(x)


def _rms(x, g):
    r = lax.rsqrt(jnp.mean(x * x, axis=-1, keepdims=True) + EPS)
    return (x * r) * g


def _modulated_norm(h, norm_g, mod_ref):
    shift = mod_ref[0:1, :]
    scale = mod_ref[1:2, :]
    return _rms(h, norm_g) * (1.0 + scale) + shift


def _split_bf16(x):
    hi = x.astype(BF16)
    lo = (x - hi.astype(F32)).astype(BF16)
    return hi, lo


def _dot3_right(x, m_hi, m_lo):
    x_hi, x_lo = _split_bf16(x)
    return _dot(x_hi, m_hi) + _dot(x_lo, m_hi) + _dot(x_hi, m_lo)


def _dot3_left(m_hi, m_lo, x):
    x_hi, x_lo = _split_bf16(x)
    return _dot(m_hi, x_hi) + _dot(m_hi, x_lo) + _dot(m_lo, x_hi)


def _mod_kernel(c_ref, w_ref, b_ref, o_ref):
    s = _silu(c_ref[...]).astype(BF16)
    o_ref[...] = _dot(s, w_ref[...].astype(BF16)) + b_ref[...]


def _modulation(cvec, w_mod, b_mod):
    rows = cvec.shape[0]
    tn = 768
    return pl.pallas_call(
        _mod_kernel,
        grid=(DEPTH, 3 * D_MODEL // tn),
        in_specs=[pl.BlockSpec((rows, D_MODEL), lambda l, j: (0, 0)),
                  pl.BlockSpec((None, D_MODEL, tn), lambda l, j: (l, 0, j)),
                  pl.BlockSpec((None, 1, tn), lambda l, j: (l, 0, j))],
        out_specs=pl.BlockSpec((None, rows, tn), lambda l, j: (l, 0, j)),
        out_shape=jax.ShapeDtypeStruct((DEPTH, rows, 3 * D_MODEL), F32),
        compiler_params=_params(2),
        name="modulation",
    )(cvec, w_mod, b_mod.reshape(DEPTH, 1, 3 * D_MODEL))


_NT = (((1,), (1,)), ((), ()))


def _key_value_heads(ckv_bf16, kr_slab, wk_ref, wvt_ref, k_ref, vt_ref):
    kn = _dot(ckv_bf16, wk_ref[...])
    for h in range(N_HEADS):
        sl = slice(HEAD_SLAB * h, HEAD_SLAB * (h + 1))
        k_ref[:, sl] = (kn[:, sl] + kr_slab).astype(BF16)
    vt_ref[...] = lax.dot_general(wvt_ref[...], ckv_bf16, _NT, preferred_element_type=F32).astype(BF16)


def _inproj_kernel(*refs, use_rope):
    if use_rope:
        (h_ref, mod_ref, ng_ref, w1_ref, qg_ref, wq_ref, kvg_ref, wk_ref, wvt_ref, cos_ref, sin_ref,
         ain_ref, sa_ref, bin_ref, sb_ref, q_ref, k_ref, vt_ref, ckv_ref, kr_ref, sc_ref) = refs
    else:
        (h_ref, mod_ref, ng_ref, w1_ref, qg_ref, wq_ref, kvg_ref, wk_ref, wvt_ref,
         ain_ref, sa_ref, bin_ref, sb_ref, q_ref, k_ref, vt_ref, ckv_ref, kr_ref, sc_ref) = refs

    xn = _modulated_norm(h_ref[...], ng_ref[...], mod_ref).astype(BF16)

    def proj(cols):
        return lax.dot_general(xn, w1_ref[cols[0]:cols[1], :], _NT, preferred_element_type=F32)

    a = proj(_W1_A)
    ain_ref[...] = a[:, :FN_WIDTH]
    sa_ref[...] = _silu(a[:, FN_WIDTH:]).astype(BF16)
    b = proj(_W1_B)
    bin_ref[...] = b[:, :POOL_WIDTH]
    sb_ref[...] = _silu(b[:, POOL_WIDTH:]).astype(BF16)
    sc_ref[...] = _silu(proj(_W1_CZ)).astype(BF16)

    qn = _rms(proj(_W1_Q), qg_ref[...]).astype(BF16)
    q = _dot(qn, wq_ref[...])
    lane = lax.broadcasted_iota(jnp.int32, (q.shape[0], HEAD_SLAB), 1)
    if use_rope:
        cos = cos_ref[...]
        sin = sin_ref[...]
        takes_upper = (lane % (QK_ROPE // 2)) < (QK_ROPE // 4)
        for h in range(N_HEADS):
            sl = slice(HEAD_SLAB * h, HEAD_SLAB * (h + 1))
            q_h = q[:, sl]
            upper = pltpu.roll(q_h, HEAD_SLAB - QK_ROPE // 4, axis=1)
            lower = pltpu.roll(q_h, QK_ROPE // 4, axis=1)
            q_rot = jnp.where(takes_upper, -upper, lower)
            q_ref[:, sl] = ((q_h * cos + q_rot * sin) * Q_PRESCALE).astype(BF16)
    else:
        q_ref[...] = (q * Q_PRESCALE).astype(BF16)

    kv = proj(_W1_KV)
    ckv = _rms(kv[:, :KV_RANK], kvg_ref[...])
    ckv_ref[...] = ckv
    slab = kv[:, KV_RANK:]
    kr_ref[...] = slab[:, QK_NOPE:QK_NOPE + QK_ROPE]
    kr = jnp.where(lane < QK_NOPE + QK_ROPE, slab, 0.0)
    if use_rope:
        kr_rot = pltpu.roll(slab, HEAD_SLAB - QK_ROPE, axis=1)
        kr = kr * cos + kr_rot * sin
    _key_value_heads(ckv.astype(BF16), kr, wk_ref, wvt_ref, k_ref, vt_ref)


def _layer_spec(arr, l):
    return pl.BlockSpec((None,) + arr.shape[1:], lambda *_: (l,) + (0,) * (arr.ndim - 1),
                        pipeline_mode=pl.Buffered(1))


def _mod_spec(l, mod_row):
    return pl.BlockSpec((None, None, 3, D_MODEL), lambda i: (l, mod_row(i), 0, 0))


def _inproj(h, mod, mod_row, pw, l, rope, tm):
    t = h.shape[0]
    use_rope = rope is not None
    row = lambda i: (i, 0)
    names = ['norm_g', 'w1', 'q_norm_g', 'wq', 'kv_norm_g', 'wk', 'wvt']
    in_specs = [pl.BlockSpec((tm, D_MODEL), row), _mod_spec(l, mod_row)] + [_layer_spec(pw[n], l) for n in names]
    args = [h, mod] + [pw[n] for n in names]
    if use_rope:
        cos, sin, rope_tile = rope
        in_specs += [pl.BlockSpec((tm, HEAD_SLAB), lambda i: (rope_tile(i), 0))] * 2
        args += [cos, sin]
    wide = N_HEADS * HEAD_SLAB
    token_outs = lambda ws: ([pl.BlockSpec((tm, w), row) for w, _ in ws],
                             [jax.ShapeDtypeStruct((t, w), dt) for w, dt in ws])
    specs_a, shapes_a = token_outs([(FN_WIDTH, F32), (FN_WIDTH, BF16), (POOL_WIDTH, F32), (POOL_WIDTH, BF16),
                                    (wide, BF16)])
    specs_b, shapes_b = token_outs([(KV_RANK, F32), (QK_ROPE, F32), (ATT_WIDTH, BF16)])
    out_specs = specs_a + [pl.BlockSpec((tm, wide), row), pl.BlockSpec((ATT_WIDTH, tm), lambda i: (0, i))] + specs_b
    out_shape = shapes_a + [jax.ShapeDtypeStruct((t, wide), BF16), jax.ShapeDtypeStruct((ATT_WIDTH, t), BF16)] + shapes_b
    return pl.pallas_call(
        functools.partial(_inproj_kernel, use_rope=use_rope),
        grid=(t // tm,),
        in_specs=in_specs,
        out_specs=out_specs,
        out_shape=out_shape,
        compiler_params=_params(1),
        name="inproj_rope" if use_rope else "inproj",
    )(*args)


def _cache_kv_kernel(ckv_ref, kr_ref, wk_ref, wvt_ref, k_ref, vt_ref):
    _key_value_heads(ckv_ref[...].astype(BF16), kr_ref[...], wk_ref, wvt_ref, k_ref, vt_ref)


def _cache_kv(cache_ckv, cache_kr_slab, pw, l):
    batch, _, past, _ = cache_ckv.shape
    wide = N_HEADS * HEAD_SLAB
    cache_map = lambda b: (b, l, 0, 0)
    return pl.pallas_call(
        _cache_kv_kernel,
        grid=(batch,),
        in_specs=[pl.BlockSpec((None, None, past, KV_RANK), cache_map),
                  pl.BlockSpec((None, None, past, HEAD_SLAB), cache_map),
                  _layer_spec(pw['wk'], l), _layer_spec(pw['wvt'], l)],
        out_specs=[pl.BlockSpec((past, wide), lambda b: (b, 0)), pl.BlockSpec((ATT_WIDTH, past), lambda b: (0, b))],
        out_shape=[jax.ShapeDtypeStruct((batch * past, wide), BF16),
                   jax.ShapeDtypeStruct((ATT_WIDTH, batch * past), BF16)],
        compiler_params=_params(1),
        name="cache_kv",
    )(cache_ckv, cache_kr_slab, pw['wk'], pw['wvt'])


def _hi_lo(m):
    m = np.asarray(m, np.float32)
    hi = m.astype(BF16)
    return hi, (m - hi.astype(np.float32)).astype(BF16)


def _dft_cos_sin(n):
    k = np.arange(n)
    ang = 2.0 * np.pi * ((k[:, None] * k[None, :]) % n) / n
    return np.cos(ang), np.sin(ang)


def _channel_dft_tables():
    c, s = _dft_cos_sin(FN_GC)
    eye = np.eye(FN_WIDTH // FN_GC)
    return np.kron(eye, c), np.kron(eye, s)


def _fourier_direct_kernel(a_ref, sa_ref, cc_hi, cc_lo, ss_hi, ss_lo, m_hi, m_lo, o_ref, *, seq, norm):
    n = a_ref.shape[0] // seq
    a = a_ref[...]
    tc = _dot3_right(a, cc_hi[...], cc_lo[...])
    ts = _dot3_right(a, ss_hi[...], ss_lo[...])
    side_by_side = lambda x: jnp.concatenate([x[i * seq:(i + 1) * seq] for i in range(n)], axis=1)
    f = _dot3_left(m_hi[...], m_lo[...], jnp.concatenate([side_by_side(tc), side_by_side(ts)], axis=0))
    for i in range(n):
        rows = slice(i * seq, (i + 1) * seq)
        f_i = f[:, i * FN_WIDTH:(i + 1) * FN_WIDTH]
        o_ref[rows, :] = ((f_i * norm) * sa_ref[rows, :].astype(F32)).astype(BF16)


def _fourier_direct(a, sa, seq, seqs_per_step):
    t = a.shape[0]
    rows = seq * seqs_per_step
    cc, ss = _channel_dft_tables()
    cl, sl = _dft_cos_sin(seq)
    tables = [x for m in (cc, ss, np.concatenate([cl, -sl], axis=1)) for x in _hi_lo(m)]
    row = lambda i: (i, 0)
    const = lambda i: (0, 0)
    tspecs = [pl.BlockSpec((FN_WIDTH, FN_WIDTH), const)] * 4 + [pl.BlockSpec((seq, 2 * seq), const)] * 2
    return pl.pallas_call(
        functools.partial(_fourier_direct_kernel, seq=seq, norm=float((seq * FN_GC) ** -0.5)),
        grid=(t // rows,),
        in_specs=[pl.BlockSpec((rows, FN_WIDTH), row), pl.BlockSpec((rows, FN_WIDTH), row)] + tspecs,
        out_specs=pl.BlockSpec((rows, FN_WIDTH), row),
        out_shape=jax.ShapeDtypeStruct((t, FN_WIDTH), BF16),
        compiler_params=_params(1),
        name="fourier_direct",
    )(a, sa, *tables)


FFT_R = 64
FFT_PITCH = FFT_R + 4


def _fourier_fft_kernel(a_ref, sa_ref, cc_hi, cc_lo, ss_hi, ss_lo, m1_hi, m1_lo, m2_hi, m2_lo, twc_ref, tws_ref,
                        o_ref, zr0, zr1, zi0, zi1, yr0, yr1, yi0, yi1, *, seq, norm):
    r = FFT_R
    half = FN_WIDTH // 2
    chunk = 512

    def put(refs, rows, x):
        refs[0][rows, :] = x[:, :half]
        refs[1][rows, :] = x[:, half:]

    def get(refs, rows):
        return jnp.concatenate([refs[0][rows, :], refs[1][rows, :]], axis=1)

    block = lambda j: slice(FFT_PITCH * j, FFT_PITCH * j + r)
    across = lambda i: pl.ds(i, r, stride=FFT_PITCH)

    zr, zi, yr_s, yi_s = (zr0, zr1), (zi0, zi1), (yr0, yr1), (yi0, yi1)
    for c in range(seq // chunk):
        a = a_ref[c * chunk:(c + 1) * chunk, :]
        zr_c = _dot3_right(a, cc_hi[...], cc_lo[...])
        zi_c = -_dot3_right(a, ss_hi[...], ss_lo[...])
        for j in range(chunk // r):
            n1 = c * (chunk // r) + j
            put(zr, block(n1), zr_c[j * r:(j + 1) * r])
            put(zi, block(n1), zi_c[j * r:(j + 1) * r])
    for n2 in range(r):
        z = jnp.concatenate([get(zr, across(n2)), get(zi, across(n2))], axis=0)
        y = _dot3_left(m1_hi[...], m1_lo[...], z)
        yr, yi = y[:r], y[r:]
        tw = slice(n2 * r, (n2 + 1) * r)
        cos = jnp.concatenate([twc_ref[tw, :]] * 2, axis=1)
        sin = jnp.concatenate([tws_ref[tw, :]] * 2, axis=1)
        put(yr_s, block(n2), yr * cos + yi * sin)
        put(yi_s, block(n2), yi * cos - yr * sin)
    for k1 in range(r):
        y = jnp.concatenate([get(yr_s, across(k1)), get(yi_s, across(k1))], axis=0)
        put(zr, across(k1), _dot3_left(m2_hi[...], m2_lo[...], y))
    for k2 in range(r):
        rows = slice(k2 * r, (k2 + 1) * r)
        o_ref[rows, :] = ((get(zr, block(k2)) * norm) * sa_ref[rows, :].astype(F32)).astype(BF16)


def _fourier_fft(a, sa, seq):
    assert seq == FFT_R * FFT_R
    t = a.shape[0]
    cc, ss = _channel_dft_tables()
    c, s = _dft_cos_sin(FFT_R)
    m1 = np.block([[c, s], [-s, c]])
    m2 = np.concatenate([c, s], axis=1)
    n2 = np.arange(FFT_R)[:, None]
    k1 = np.arange(FFT_R)[None, :]
    ang = (2.0 * np.pi * (n2 * k1) / seq).reshape(seq, 1)
    twc = np.ascontiguousarray(np.broadcast_to(np.cos(ang).astype(np.float32), (seq, 128)))
    tws = np.ascontiguousarray(np.broadcast_to(np.sin(ang).astype(np.float32), (seq, 128)))
    tables = [x for m in (cc, ss, m1, m2) for x in _hi_lo(m)]
    row = lambda i: (i, 0)
    const = lambda i: (0, 0)
    tspecs = ([pl.BlockSpec((FN_WIDTH, FN_WIDTH), const)] * 4 + [pl.BlockSpec((2 * FFT_R, 2 * FFT_R), const)] * 2
              + [pl.BlockSpec((FFT_R, 2 * FFT_R), const)] * 2 + [pl.BlockSpec((seq, 128), const)] * 2)
    return pl.pallas_call(
        functools.partial(_fourier_fft_kernel, seq=seq, norm=float((seq * FN_GC) ** -0.5)),
        grid=(t // seq,),
        in_specs=[pl.BlockSpec((seq, FN_WIDTH), row), pl.BlockSpec((seq, FN_WIDTH), row)] + tspecs,
        out_specs=pl.BlockSpec((seq, FN_WIDTH), row),
        out_shape=jax.ShapeDtypeStruct((t, FN_WIDTH), BF16),
        scratch_shapes=[pltpu.VMEM((FFT_PITCH * FFT_R, FN_WIDTH // 2), F32)] * 8,
        compiler_params=_params(1),
        name="fourier_fft",
    )(a, sa, *tables, twc, tws)


POOL_HALO = 8
POOL_CHUNK = 256


def _pool_kernel(b_ref, sb_ref, pw_ref, ps_ref, o_ref, pad_ref, *, seq):
    zeros = jnp.zeros((POOL_HALO, POOL_WIDTH), F32)
    lane = lax.broadcasted_iota(jnp.int32, (POOL_CHUNK, 128), 1)
    low_group = lane < POOL_GC
    for i in range(pad_ref.shape[0]):
        pad_ref[i, 0:POOL_HALO, :] = zeros
        pad_ref[i, POOL_HALO + seq:, :] = zeros
        pad_ref[i, POOL_HALO:POOL_HALO + seq, :] = b_ref[i * seq:(i + 1) * seq, :]

    for i, c in [(i, c) for i in range(pad_ref.shape[0]) for c in range(seq // POOL_CHUNK)]:
        r0 = c * POOL_CHUNK
        t = lax.broadcasted_iota(jnp.int32, (POOL_CHUNK, 128), 0) + r0

        interior = r0 >= POOL_HALO and r0 + POOL_CHUNK + POOL_HALO <= seq

        def inv_count(w):
            if interior:
                return 1.0 / w
            left = w // 2
            right = w - 1 - left
            lo = jnp.maximum(t - left, 0)
            hi = jnp.minimum(t + right, seq - 1)
            return 1.0 / (hi - lo + 1).astype(F32)

        def ld(off, col):
            start = POOL_HALO + r0 + off
            return pad_ref[i, start:start + POOL_CHUNK, 128 * col:128 * (col + 1)]

        u0 = ld(0, 0)
        p2 = ld(-1, 0) + u0
        p4 = p2 + ld(-2, 0) + ld(1, 0)
        pooled0 = jnp.where(low_group, p2 * inv_count(2), p4 * inv_count(4)) - u0
        u1 = ld(0, 1)
        p8 = u1
        for off in (-4, -3, -2, -1, 1, 2, 3):
            p8 = p8 + ld(off, 1)
        p16 = p8
        for off in (-8, -7, -6, -5, 4, 5, 6, 7):
            p16 = p16 + ld(off, 1)
        pooled1 = jnp.where(low_group, p8 * inv_count(8), p16 * inv_count(16)) - u1

        pooled = jnp.concatenate([pooled0, pooled1], axis=1).astype(BF16)
        mixed = _dot(pooled, pw_ref[...]) * ps_ref[...]
        rows = slice(i * seq + r0, i * seq + r0 + POOL_CHUNK)
        o_ref[rows, :] = (mixed * sb_ref[rows, :].astype(F32)).astype(BF16)


def _pool(b, sb, pw, l, seq, seqs_per_step):
    t = b.shape[0]
    rows = seq * seqs_per_step
    row = lambda i: (i, 0)
    return pl.pallas_call(
        functools.partial(_pool_kernel, seq=seq),
        grid=(t // rows,),
        in_specs=[pl.BlockSpec((rows, POOL_WIDTH), row), pl.BlockSpec((rows, POOL_WIDTH), row),
                  _layer_spec(pw['pool_w'], l), _layer_spec(pw['pool_scale'], l)],
        out_specs=pl.BlockSpec((rows, POOL_WIDTH), row),
        out_shape=jax.ShapeDtypeStruct((t, POOL_WIDTH), BF16),
        scratch_shapes=[pltpu.VMEM((seqs_per_step, seq + 2 * POOL_HALO, POOL_WIDTH), F32)],
        compiler_params=_params(1),
        name="pool",
    )(b, sb, pw['pool_w'], pw['pool_scale'])


def _attn_kernel(*refs, heads, chunk, use_cache, lookahead, seqs):
    if use_cache:
        q_ref, k_ref, vt_ref, kc_ref, vct_ref, sc_ref, o_ref, s_ref, m_ref, l_ref, acc_ref = refs
        sources = ((k_ref, vt_ref), (kc_ref, vct_ref))
    else:
        q_ref, k_ref, vt_ref, sc_ref, o_ref, s_ref, m_ref, l_ref, acc_ref = refs
        sources = ((k_ref, vt_ref),)
    tq = q_ref.shape[0] // seqs
    slots = s_ref.shape[0]
    problems = [(i, h) for i in range(seqs) for h in range(heads)]

    def chunks_of(i):
        out, row = [], 0
        for keys, values in sources:
            n = keys.shape[0] // seqs
            for off in range(0, n, chunk):
                size = min(chunk, n - off)
                out.append((keys, values, i * n + off, size, row))
                row += size
        return out

    n_chunks = len(chunks_of(0))

    def scores(j, c):
        i, h = problems[j]
        keys, _, off, size, row = chunks_of(i)[c]
        sl = slice(HEAD_SLAB * h, HEAD_SLAB * (h + 1))
        q = q_ref[i * tq:(i + 1) * tq, sl]
        s = lax.dot_general(keys[off:off + size, sl], q, _NT, preferred_element_type=F32)
        s_ref[j % slots, row:row + size, :] = s
        m_ref[j % slots] = jnp.maximum(m_ref[j % slots], jnp.max(s.reshape(size // 8, 8, tq), axis=0))

    def weigh(j, c, m):
        i, h = problems[j]
        _, values, off, size, row = chunks_of(i)[c]
        p = jnp.exp2(s_ref[j % slots, row:row + size, :] - m)
        l_ref[j % slots] += jnp.sum(p.reshape(size // 8, 8, tq), axis=0)
        acc_ref[j % slots] += _dot(values[V_DIM * h:V_DIM * (h + 1), off:off + size], p.astype(BF16))

    outs = []
    for t in range(len(problems) + lookahead):
        j_w, j_s = t - lookahead, t
        if j_s < len(problems):
            m_ref[j_s % slots] = jnp.full((8, tq), -jnp.inf, F32)
        if j_w >= 0:
            m = jnp.max(m_ref[j_w % slots], axis=0, keepdims=True)
            l_ref[j_w % slots] = jnp.zeros((8, tq), F32)
            acc_ref[j_w % slots] = jnp.zeros((V_DIM, tq), F32)
        for c in range(n_chunks):
            if j_w >= 0:
                weigh(j_w, c, m)
            if j_s < len(problems):
                scores(j_s, c)
        if j_w >= 0:
            i, h = problems[j_w]
            denom = jnp.sum(l_ref[j_w % slots], axis=0, keepdims=True)
            outs.append(acc_ref[j_w % slots] * (1.0 / denom))
            if h % 2 == 1:
                o_pair = jnp.concatenate(outs, axis=0).T
                outs = []
                rows = slice(i * tq, (i + 1) * tq)
                sl = slice(HEAD_SLAB * (h // 2), HEAD_SLAB * (h // 2 + 1))
                o_ref[rows, sl] = (o_pair * sc_ref[rows, sl].astype(F32)).astype(BF16)


def _attention(q, k, vt, sc, cache, batch, lq, lk, tq, heads_per_step, chunk, lookahead, seqs_per_step=1):
    use_cache = cache is not None
    assert seqs_per_step == 1 or (tq == lq and not use_cache)
    n = seqs_per_step
    nq = lq // tq
    n_hp = N_HEADS // heads_per_step
    qw = heads_per_step * HEAD_SLAB
    ow = heads_per_step * V_DIM
    q_map = lambda b, g, i: (b * nq + i, g)
    k_map = lambda b, g, i: (b, g)
    vt_map = lambda b, g, i: (g, b)
    in_specs = [pl.BlockSpec((n * tq, qw), q_map), pl.BlockSpec((n * lk, qw), k_map),
                pl.BlockSpec((ow, n * lk), vt_map)]
    args = [q, k, vt]
    lc = 0
    if use_cache:
        lc = cache[0].shape[0] // batch
        in_specs += [pl.BlockSpec((lc, qw), k_map), pl.BlockSpec((ow, lc), vt_map)]
        args += list(cache)
    in_specs.append(pl.BlockSpec((n * tq, ow), q_map))
    args.append(sc)
    slots = min(n * heads_per_step, lookahead + 1)
    return pl.pallas_call(
        functools.partial(_attn_kernel, heads=heads_per_step, chunk=chunk, use_cache=use_cache, lookahead=lookahead,
                          seqs=n),
        grid=(batch // n, n_hp, nq),
        in_specs=in_specs,
        out_specs=pl.BlockSpec((n * tq, ow), q_map),
        out_shape=jax.ShapeDtypeStruct((batch * lq, ATT_WIDTH), BF16),
        scratch_shapes=[pltpu.VMEM((slots, lk + lc, tq), F32), pltpu.VMEM((slots, 8, tq), F32),
                        pltpu.VMEM((slots, 8, tq), F32), pltpu.VMEM((slots, V_DIM, tq), F32)],
        compiler_params=_params(3),
        name="attention_cache" if use_cache else "attention",
    )(*args)


def _out_kernel(*refs, final):
    if final:
        h_ref, mod_ref, ng_ref, xa_ref, xb_ref, xc_ref, wa_ref, wb_ref, wc_ref, wg_ref, wo_ref, fg_ref, o_ref = refs
    else:
        h_ref, mod_ref, ng_ref, xa_ref, xb_ref, xc_ref, wa_ref, wb_ref, wc_ref, wg_ref, wo_ref, o_ref = refs
    h = h_ref[...]
    xn = _modulated_norm(h, ng_ref[...], mod_ref).astype(BF16)
    y = None
    for i, (x_ref, w_ref) in enumerate(((xa_ref, wa_ref), (xb_ref, wb_ref), (xc_ref, wc_ref))):
        wg_i = wg_ref[i * D_MODEL:(i + 1) * D_MODEL, :]
        g = jax.nn.sigmoid(lax.dot_general(xn, wg_i, _NT, preferred_element_type=F32))
        term = g * _dot(x_ref[...], w_ref[...])
        y = term if y is None else y + term
    h_new = h + mod_ref[2:3, :] * _dot(y.astype(BF16), wo_ref[...])
    if final:
        o_ref[...] = _rms(h_new, fg_ref[...])
    else:
        o_ref[...] = h_new


def _out(h, mod, mod_row, xa, xb, xc, pw, l, final_g, tm):
    t = h.shape[0]
    final = final_g is not None
    row = lambda i: (i, 0)
    names = ['w_br_a', 'w_br_b', 'w_br_c', 'wg', 'w_out']
    in_specs = ([pl.BlockSpec((tm, D_MODEL), row), _mod_spec(l, mod_row), _layer_spec(pw['norm_g'], l),
                 pl.BlockSpec((tm, FN_WIDTH), row), pl.BlockSpec((tm, POOL_WIDTH), row),
                 pl.BlockSpec((tm, ATT_WIDTH), row)] + [_layer_spec(pw[n], l) for n in names])
    args = [h, mod, pw['norm_g'], xa, xb, xc] + [pw[n] for n in names]
    if final:
        in_specs.append(pl.BlockSpec((1, D_MODEL), lambda i: (0, 0)))
        args.append(final_g)
    return pl.pallas_call(
        functools.partial(_out_kernel, final=final),
        grid=(t // tm,),
        in_specs=in_specs,
        out_specs=pl.BlockSpec((tm, D_MODEL), row),
        out_shape=jax.ShapeDtypeStruct((t, D_MODEL), F32),
        compiler_params=_params(1),
        name="out_final" if final else "out",
    )(*args)


_ROPE_PAD = ((QK_NOPE, HEAD_SLAB - QK_NOPE - QK_ROPE),)
W1_ROW_TILE = 256
W1_PLAIN_TILES = _W1_KV[0] // W1_ROW_TILE
WG_ROW_TILE = 512


def _w1_source_row(k):
    direct = _OFF_KV // W1_ROW_TILE
    return jnp.where(k < direct, k * W1_ROW_TILE,
                     jnp.where(k < W1_PLAIN_TILES, _OFF_CZ + (k - direct) * W1_ROW_TILE, _OFF_KV))


def _pack_w1_kernel(w_ref, o_ref):
    k = pl.program_id(1)
    x = w_ref[0]

    @pl.when(k < W1_PLAIN_TILES)
    def _():
        o_ref[...] = x.astype(BF16)

    @pl.when(k == W1_PLAIN_TILES)
    def _():
        kr = x[KV_RANK:KV_RANK + QK_ROPE, :]
        q = QK_ROPE // 4
        rot = jnp.concatenate([-kr[q:2 * q], kr[:q], -kr[3 * q:], kr[2 * q:3 * q]], axis=0)
        zeros = jnp.zeros((QK_NOPE, D_MODEL), F32)
        o_ref[...] = jnp.concatenate([x[:KV_RANK], zeros, kr, rot], axis=0).astype(BF16)


def _pack_cast_kernel(w_ref, o_ref):
    o_ref[...] = w_ref[0].astype(BF16)


def _pack_w_in(w_in):
    wt = jnp.swapaxes(w_in, 1, 2)
    g_width = w_in.shape[2] - _OFF_G
    w1t = pl.pallas_call(
        _pack_w1_kernel,
        grid=(DEPTH, W1_WIDTH // W1_ROW_TILE),
        in_specs=[pl.BlockSpec((pl.Element(1), pl.Element(W1_ROW_TILE), pl.Element(D_MODEL)),
                               lambda l, k: (l, pl.multiple_of(_w1_source_row(k), 32), 0))],
        out_specs=pl.BlockSpec((None, W1_ROW_TILE, D_MODEL), lambda l, k: (l, k, 0)),
        out_shape=jax.ShapeDtypeStruct((DEPTH, W1_WIDTH, D_MODEL), BF16),
        compiler_params=_params(2),
        name="pack_w1",
    )(wt)
    wgt = pl.pallas_call(
        _pack_cast_kernel,
        grid=(DEPTH, g_width // WG_ROW_TILE),
        in_specs=[pl.BlockSpec((pl.Element(1), pl.Element(WG_ROW_TILE), pl.Element(D_MODEL)),
                               lambda l, k: (l, pl.multiple_of(_OFF_G + k * WG_ROW_TILE, 32), 0))],
        out_specs=pl.BlockSpec((None, WG_ROW_TILE, D_MODEL), lambda l, k: (l, k, 0)),
        out_shape=jax.ShapeDtypeStruct((DEPTH, g_width, D_MODEL), BF16),
        compiler_params=_params(2),
        name="pack_wg",
    )(wt)
    return w1t, wgt


def _pack_weights(norm_g, w_in, pool_w, pool_scale, q_norm_g, w_q_up, kv_norm_g, w_kv_up, w_br_a, w_br_b, w_br_c,
                  w_out):
    lead = ((0, 0), (0, 0))
    w1, wg = _pack_w_in(w_in)
    wide = N_HEADS * HEAD_SLAB
    wq_h = w_q_up.reshape(DEPTH, Q_RANK, N_HEADS, QK_NOPE + QK_ROPE)
    wq = jnp.pad(wq_h, lead + ((0, 0), (0, _ROPE_PAD[0][1]))).reshape(DEPTH, Q_RANK, wide).astype(BF16)
    wkv_h = w_kv_up.reshape(DEPTH, KV_RANK, N_HEADS, QK_NOPE + V_DIM)
    wk = jnp.pad(wkv_h[..., :QK_NOPE], lead + ((0, 0), (0, HEAD_SLAB - QK_NOPE))).reshape(DEPTH, KV_RANK, wide)
    wvt = wkv_h[..., QK_NOPE:].reshape(DEPTH, KV_RANK, ATT_WIDTH).transpose(0, 2, 1)
    groups = len(POOL_WINDOWS)
    eye = jnp.eye(groups, dtype=F32)
    pool_bd = (pool_w[:, :, :, None, :] * eye[None, :, None, :, None]).reshape(DEPTH, POOL_WIDTH, POOL_WIDTH)
    return {
        'norm_g': norm_g.reshape(DEPTH, 1, D_MODEL), 'w1': w1, 'wg': wg,
        'q_norm_g': q_norm_g.reshape(DEPTH, 1, Q_RANK), 'wq': wq,
        'kv_norm_g': kv_norm_g.reshape(DEPTH, 1, KV_RANK), 'wk': wk.astype(BF16), 'wvt': wvt.astype(BF16),
        'pool_w': pool_bd.astype(BF16), 'pool_scale': pool_scale.reshape(DEPTH, 1, POOL_WIDTH),
        'w_br_a': w_br_a.astype(BF16), 'w_br_b': w_br_b.astype(BF16), 'w_br_c': w_br_c.astype(BF16),
        'w_out': w_out.astype(BF16),
    }


def _rope_tables(seq):
    f32 = np.float32
    t = np.arange(seq)
    row = (t // GRID_W).astype(f32)
    col = (t % GRID_W).astype(f32)
    half = QK_ROPE // 2
    freqs = f32(ROPE_THETA) ** (-np.arange(0, half, 2, dtype=f32) / f32(half))
    ar = row[:, None] * freqs
    ac = col[:, None] * freqs
    cos = np.ones((seq, HEAD_SLAB), f32)
    sin = np.zeros((seq, HEAD_SLAB), f32)
    cos[:, QK_NOPE:QK_NOPE + QK_ROPE] = np.concatenate([np.cos(ar), np.cos(ar), np.cos(ac), np.cos(ac)], axis=-1)
    sin[:, QK_NOPE:QK_NOPE + QK_ROPE] = np.concatenate([np.sin(ar), np.sin(ar), np.sin(ac), np.sin(ac)], axis=-1)
    return cos, sin


TOKEN_TILE = 1024
SAMPLE_Q_TILE = 512
SAMPLE_HEADS_PER_STEP = 8
SAMPLE_KEY_CHUNK = 2048
PROMPT_SEQS_PER_STEP = 4


def kernel(x_prompt, x_sample, cache_ckv, cache_krope, c, c_ctx, norm_g, w_mod, b_mod, w_in, pool_w, pool_scale,
           q_norm_g, w_q_up, kv_norm_g, w_kv_up, w_br_a, w_br_b, w_br_c, w_out, final_norm_g):
    batch, seq, _ = x_prompt.shape
    dec_batch, dec_seq, _ = x_sample.shape
    past = cache_ckv.shape[2]
    tm = TOKEN_TILE

    mod_rows = 8
    cvec = jnp.concatenate([c_ctx[None, :], c, jnp.zeros((mod_rows - 1 - dec_batch, D_MODEL), F32)], axis=0)
    mod = _modulation(cvec, w_mod, b_mod).reshape(DEPTH, mod_rows, 3, D_MODEL)
    prompt_row = lambda i: 0
    tiles_per_sample = dec_seq // tm
    assert past % HEAD_SLAB == 0 and dec_seq % SAMPLE_KEY_CHUNK == 0
    sample_row = lambda i: 1 + i // tiles_per_sample

    cos, sin = _rope_tables(dec_seq)
    rope = (cos, sin, lambda i: i % tiles_per_sample)
    final_g = final_norm_g.reshape(1, D_MODEL)
    pw = _pack_weights(norm_g, w_in, pool_w, pool_scale, q_norm_g, w_q_up, kv_norm_g, w_kv_up, w_br_a, w_br_b,
                       w_br_c, w_out)
    cache_kr_slab = jnp.pad(cache_krope, ((0, 0), (0, 0), (0, 0)) + _ROPE_PAD)

    hp = x_prompt.reshape(batch * seq, D_MODEL)
    hs = x_sample.reshape(dec_batch * dec_seq, D_MODEL)
    ckv_list, kr_list = [], []
    for l in range(DEPTH):
        last = final_g if l == DEPTH - 1 else None

        a_in, sa, b_in, sb, q, k, vt, ckv, kr, sc = _inproj(hp, mod, prompt_row, pw, l, None, tm)
        ckv_list.append(ckv.reshape(batch, seq, KV_RANK))
        kr_list.append(kr.reshape(batch, seq, QK_ROPE))
        xa = _fourier_direct(a_in, sa, seq, PROMPT_SEQS_PER_STEP)
        xb = _pool(b_in, sb, pw, l, seq, PROMPT_SEQS_PER_STEP)
        xc = _attention(q, k, vt, sc, None, batch, seq, seq, seq, N_HEADS, seq,
                        lookahead=N_HEADS * PROMPT_SEQS_PER_STEP, seqs_per_step=PROMPT_SEQS_PER_STEP)
        hp = _out(hp, mod, prompt_row, xa, xb, xc, pw, l, last, tm)

        a_in, sa, b_in, sb, q, k, vt, _, _, sc = _inproj(hs, mod, sample_row, pw, l, rope, tm)
        cache = _cache_kv(cache_ckv, cache_kr_slab, pw, l)
        xa = _fourier_fft(a_in, sa, dec_seq)
        xb = _pool(b_in, sb, pw, l, dec_seq, 1)
        xc = _attention(q, k, vt, sc, cache, dec_batch, dec_seq, dec_seq, SAMPLE_Q_TILE, SAMPLE_HEADS_PER_STEP,
                        SAMPLE_KEY_CHUNK, lookahead=1)
        hs = _out(hs, mod, sample_row, xa, xb, xc, pw, l, last, tm)

    y_prompt = hp.reshape(batch, seq, D_MODEL)
    y_sample = hs.reshape(dec_batch, dec_seq, D_MODEL)
    return (y_prompt, y_sample, jnp.stack(ckv_list, axis=1), jnp.stack(kr_list, axis=1))
```

```python
import functools

import numpy as np
import jax
import jax.numpy as jnp
from jax import lax
from jax.experimental import pallas as pl
from jax.experimental.pallas import tpu as pltpu

D_MODEL = 1024
DEPTH = 2
GRID_W = 64
EPS = 1e-6
FN_WIDTH = 256
FN_GC = 64
POOL_WINDOWS = (2, 4, 8, 16)
POOL_WIDTH = 256
POOL_GC = 64
N_HEADS = 8
QK_NOPE = 64
QK_ROPE = 32
V_DIM = 64
Q_RANK = 256
KV_RANK = 128
ATT_WIDTH = 512
ROPE_THETA = 10000.0
HEAD_SLAB = 128
QK_SCALE = (QK_NOPE + QK_ROPE) ** -0.5
Q_PRESCALE = QK_SCALE * float(np.log2(np.e))

VMEM_LIMIT_BYTES = 56 * 1024 * 1024

F32 = jnp.float32
BF16 = jnp.bfloat16

_OFF_A, _OFF_B, _OFF_Q, _OFF_KV, _OFF_KR, _OFF_CZ, _OFF_G = 0, 512, 1024, 1280, 1408, 1440, 1952
_W1_A = (0, 512)
_W1_B = (512, 1024)
_W1_Q = (1024, 1280)
_W1_CZ = (1280, 1792)
_W1_KV = (1792, 2048)
W1_WIDTH = 2048


def _params(n_parallel):
    return pltpu.CompilerParams(dimension_semantics=("arbitrary",) * n_parallel,
                                vmem_limit_bytes=VMEM_LIMIT_BYTES)


def _dot(a, b):
    return jnp.dot(a, b, preferred_element_type=F32)


def _silu(x):
    return x * jax.nn.sigmoid(x)


def _rms(x, g):
    r = lax.rsqrt(jnp.mean(x * x, axis=-1, keepdims=True) + EPS)
    return (x * r) * g


def _modulated_norm(h, norm_g, mod_ref):
    shift = mod_ref[0:1, :]
    scale = mod_ref[1:2, :]
    return _rms(h, norm_g) * (1.0 + scale) + shift


def _split_bf16(x):
    hi = x.astype(BF16)
    lo = (x - hi.astype(F32)).astype(BF16)
    return hi, lo


def _dot3_right(x, m_hi, m_lo):
    x_hi, x_lo = _split_bf16(x)
    return _dot(x_hi, m_hi) + _dot(x_lo, m_hi) + _dot(x_hi, m_lo)


def _dot3_left(m_hi, m_lo, x):
    x_hi, x_lo = _split_bf16(x)
    return _dot(m_hi, x_hi) + _dot(m_hi, x_lo) + _dot(m_lo, x_hi)


def _mod_kernel(c_ref, w_ref, b_ref, o_ref):
    s = _silu(c_ref[...]).astype(BF16)
    o_ref[...] = _dot(s, w_ref[...].astype(BF16)) + b_ref[...]


def _modulation(cvec, w_mod, b_mod):
    rows = cvec.shape[0]
    tn = 768
    return pl.pallas_call(
        _mod_kernel,
        grid=(DEPTH, 3 * D_MODEL // tn),
        in_specs=[pl.BlockSpec((rows, D_MODEL), lambda l, j: (0, 0)),
                  pl.BlockSpec((None, D_MODEL, tn), lambda l, j: (l, 0, j)),
                  pl.BlockSpec((None, 1, tn), lambda l, j: (l, 0, j))],
        out_specs=pl.BlockSpec((None, rows, tn), lambda l, j: (l, 0, j)),
        out_shape=jax.ShapeDtypeStruct((DEPTH, rows, 3 * D_MODEL), F32),
        compiler_params=_params(2),
        name="modulation",
    )(cvec, w_mod, b_mod.reshape(DEPTH, 1, 3 * D_MODEL))


_NT = (((1,), (1,)), ((), ()))


def _key_value_heads(ckv_bf16, kr_slab, wk_ref, wvt_ref, k_ref, vt_ref):
    kn = _dot(ckv_bf16, wk_ref[...])
    for h in range(N_HEADS):
        sl = slice(HEAD_SLAB * h, HEAD_SLAB * (h + 1))
        k_ref[:, sl] = (kn[:, sl] + kr_slab).astype(BF16)
    vt_ref[...] = lax.dot_general(wvt_ref[...], ckv_bf16, _NT, preferred_element_type=F32).astype(BF16)


def _inproj_kernel(*refs, use_rope):
    if use_rope:
        (h_ref, mod_ref, ng_ref, w1_ref, qg_ref, wq_ref, kvg_ref, wk_ref, wvt_ref, cos_ref, sin_ref,
         ain_ref, sa_ref, bin_ref, sb_ref, q_ref, k_ref, vt_ref, ckv_ref, kr_ref, sc_ref) = refs
    else:
        (h_ref, mod_ref, ng_ref, w1_ref, qg_ref, wq_ref, kvg_ref, wk_ref, wvt_ref,
         ain_ref, sa_ref, bin_ref, sb_ref, q_ref, k_ref, vt_ref, ckv_ref, kr_ref, sc_ref) = refs

    xn = _modulated_norm(h_ref[...], ng_ref[...], mod_ref).astype(BF16)

    def proj(cols):
        return lax.dot_general(xn, w1_ref[cols[0]:cols[1], :], _NT, preferred_element_type=F32)

    a = proj(_W1_A)
    ain_ref[...] = a[:, :FN_WIDTH]
    sa_ref[...] = _silu(a[:, FN_WIDTH:]).astype(BF16)
    b = proj(_W1_B)
    bin_ref[...] = b[:, :POOL_WIDTH]
    sb_ref[...] = _silu(b[:, POOL_WIDTH:]).astype(BF16)
    sc_ref[...] = _silu(proj(_W1_CZ)).astype(BF16)

    qn = _rms(proj(_W1_Q), qg_ref[...]).astype(BF16)
    q = _dot(qn, wq_ref[...])
    lane = lax.broadcasted_iota(jnp.int32, (q.shape[0], HEAD_SLAB), 1)
    if use_rope:
        cos = cos_ref[...]
        sin = sin_ref[...]
        takes_upper = (lane % (QK_ROPE // 2)) < (QK_ROPE // 4)
        for h in range(N_HEADS):
            sl = slice(HEAD_SLAB * h, HEAD_SLAB * (h + 1))
            q_h = q[:, sl]
            upper = pltpu.roll(q_h, HEAD_SLAB - QK_ROPE // 4, axis=1)
            lower = pltpu.roll(q_h, QK_ROPE // 4, axis=1)
            q_rot = jnp.where(takes_upper, -upper, lower)
            q_ref[:, sl] = ((q_h * cos + q_rot * sin) * Q_PRESCALE).astype(BF16)
    else:
        q_ref[...] = (q * Q_PRESCALE).astype(BF16)

    kv = proj(_W1_KV)
    ckv = _rms(kv[:, :KV_RANK], kvg_ref[...])
    ckv_ref[...] = ckv
    slab = kv[:, KV_RANK:]
    kr_ref[...] = slab[:, QK_NOPE:QK_NOPE + QK_ROPE]
    kr = jnp.where(lane < QK_NOPE + QK_ROPE, slab, 0.0)
    if use_rope:
        kr_rot = pltpu.roll(slab, HEAD_SLAB - QK_ROPE, axis=1)
        kr = kr * cos + kr_rot * sin
    _key_value_heads(ckv.astype(BF16), kr, wk_ref, wvt_ref, k_ref, vt_ref)


def _layer_spec(arr, l):
    return pl.BlockSpec((None,) + arr.shape[1:], lambda *_: (l,) + (0,) * (arr.ndim - 1),
                        pipeline_mode=pl.Buffered(1))


def _mod_spec(l, mod_row):
    return pl.BlockSpec((None, None, 3, D_MODEL), lambda i: (l, mod_row(i), 0, 0))


def _inproj(h, mod, mod_row, pw, l, rope, tm):
    t = h.shape[0]
    use_rope = rope is not None
    row = lambda i: (i, 0)
    names = ['norm_g', 'w1', 'q_norm_g', 'wq', 'kv_norm_g', 'wk', 'wvt']
    in_specs = [pl.BlockSpec((tm, D_MODEL), row), _mod_spec(l, mod_row)] + [_layer_spec(pw[n], l) for n in names]
    args = [h, mod] + [pw[n] for n in names]
    if use_rope:
        cos, sin, rope_tile = rope
        in_specs += [pl.BlockSpec((tm, HEAD_SLAB), lambda i: (rope_tile(i), 0))] * 2
        args += [cos, sin]
    wide = N_HEADS * HEAD_SLAB
    token_outs = lambda ws: ([pl.BlockSpec((tm, w), row) for w, _ in ws],
                             [jax.ShapeDtypeStruct((t, w), dt) for w, dt in ws])
    specs_a, shapes_a = token_outs([(FN_WIDTH, F32), (FN_WIDTH, BF16), (POOL_WIDTH, F32), (POOL_WIDTH, BF16),
                                    (wide, BF16)])
    specs_b, shapes_b = token_outs([(KV_RANK, F32), (QK_ROPE, F32), (ATT_WIDTH, BF16)])
    out_specs = specs_a + [pl.BlockSpec((tm, wide), row), pl.BlockSpec((ATT_WIDTH, tm), lambda i: (0, i))] + specs_b
    out_shape = shapes_a + [jax.ShapeDtypeStruct((t, wide), BF16), jax.ShapeDtypeStruct((ATT_WIDTH, t), BF16)] + shapes_b
    return pl.pallas_call(
        functools.partial(_inproj_kernel, use_rope=use_rope),
        grid=(t // tm,),
        in_specs=in_specs,
        out_specs=out_specs,
        out_shape=out_shape,
        compiler_params=_params(1),
        name="inproj_rope" if use_rope else "inproj",
    )(*args)


def _cache_kv_kernel(ckv_ref, kr_ref, wk_ref, wvt_ref, k_ref, vt_ref):
    _key_value_heads(ckv_ref[...].astype(BF16), kr_ref[...], wk_ref, wvt_ref, k_ref, vt_ref)


def _cache_kv(cache_ckv, cache_kr_slab, pw, l):
    batch, _, past, _ = cache_ckv.shape
    wide = N_HEADS * HEAD_SLAB
    cache_map = lambda b: (b, l, 0, 0)
    return pl.pallas_call(
        _cache_kv_kernel,
        grid=(batch,),
        in_specs=[pl.BlockSpec((None, None, past, KV_RANK), cache_map),
                  pl.BlockSpec((None, None, past, HEAD_SLAB), cache_map),
                  _layer_spec(pw['wk'], l), _layer_spec(pw['wvt'], l)],
        out_specs=[pl.BlockSpec((past, wide), lambda b: (b, 0)), pl.BlockSpec((ATT_WIDTH, past), lambda b: (0, b))],
        out_shape=[jax.ShapeDtypeStruct((batch * past, wide), BF16),
                   jax.ShapeDtypeStruct((ATT_WIDTH, batch * past), BF16)],
        compiler_params=_params(1),
        name="cache_kv",
    )(cache_ckv, cache_kr_slab, pw['wk'], pw['wvt'])


def _hi_lo(m):
    m = np.asarray(m, np.float32)
    hi = m.astype(BF16)
    return hi, (m - hi.astype(np.float32)).astype(BF16)


def _dft_cos_sin(n):
    k = np.arange(n)
    ang = 2.0 * np.pi * ((k[:, None] * k[None, :]) % n) / n
    return np.cos(ang), np.sin(ang)


def _channel_dft_tables():
    c, s = _dft_cos_sin(FN_GC)
    eye = np.eye(FN_WIDTH // FN_GC)
    return np.kron(eye, c), np.kron(eye, s)


def _fourier_direct_kernel(a_ref, sa_ref, cc_hi, cc_lo, ss_hi, ss_lo, m_hi, m_lo, o_ref, *, seq, norm):
    n = a_ref.shape[0] // seq
    a = a_ref[...]
    tc = _dot3_right(a, cc_hi[...], cc_lo[...])
    ts = _dot3_right(a, ss_hi[...], ss_lo[...])
    side_by_side = lambda x: jnp.concatenate([x[i * seq:(i + 1) * seq] for i in range(n)], axis=1)
    f = _dot3_left(m_hi[...], m_lo[...], jnp.concatenate([side_by_side(tc), side_by_side(ts)], axis=0))
    for i in range(n):
        rows = slice(i * seq, (i + 1) * seq)
        f_i = f[:, i * FN_WIDTH:(i + 1) * FN_WIDTH]
        o_ref[rows, :] = ((f_i * norm) * sa_ref[rows, :].astype(F32)).astype(BF16)


def _fourier_direct(a, sa, seq, seqs_per_step):
    t = a.shape[0]
    rows = seq * seqs_per_step
    cc, ss = _channel_dft_tables()
    cl, sl = _dft_cos_sin(seq)
    tables = [x for m in (cc, ss, np.concatenate([cl, -sl], axis=1)) for x in _hi_lo(m)]
    row = lambda i: (i, 0)
    const = lambda i: (0, 0)
    tspecs = [pl.BlockSpec((FN_WIDTH, FN_WIDTH), const)] * 4 + [pl.BlockSpec((seq, 2 * seq), const)] * 2
    return pl.pallas_call(
        functools.partial(_fourier_direct_kernel, seq=seq, norm=float((seq * FN_GC) ** -0.5)),
        grid=(t // rows,),
        in_specs=[pl.BlockSpec((rows, FN_WIDTH), row), pl.BlockSpec((rows, FN_WIDTH), row)] + tspecs,
        out_specs=pl.BlockSpec((rows, FN_WIDTH), row),
        out_shape=jax.ShapeDtypeStruct((t, FN_WIDTH), BF16),
        compiler_params=_params(1),
        name="fourier_direct",
    )(a, sa, *tables)


FFT_R = 64
FFT_PITCH = FFT_R + 4


def _fourier_fft_kernel(a_ref, sa_ref, cc_hi, cc_lo, ss_hi, ss_lo, m1_hi, m1_lo, m2_hi, m2_lo, twc_ref, tws_ref,
                        o_ref, zr0, zr1, zi0, zi1, yr0, yr1, yi0, yi1, *, seq, norm):
    r = FFT_R
    half = FN_WIDTH // 2
    chunk = 512

    def put(refs, rows, x):
        refs[0][rows, :] = x[:, :half]
        refs[1][rows, :] = x[:, half:]

    def get(refs, rows):
        return jnp.concatenate([refs[0][rows, :], refs[1][rows, :]], axis=1)

    block = lambda j: slice(FFT_PITCH * j, FFT_PITCH * j + r)
    across = lambda i: pl.ds(i, r, stride=FFT_PITCH)

    zr, zi, yr_s, yi_s = (zr0, zr1), (zi0, zi1), (yr0, yr1), (yi0, yi1)
    for c in range(seq // chunk):
        a = a_ref[c * chunk:(c + 1) * chunk, :]
        zr_c = _dot3_right(a, cc_hi[...], cc_lo[...])
        zi_c = -_dot3_right(a, ss_hi[...], ss_lo[...])
        for j in range(chunk // r):
            n1 = c * (chunk // r) + j
            put(zr, block(n1), zr_c[j * r:(j + 1) * r])
            put(zi, block(n1), zi_c[j * r:(j + 1) * r])
    for n2 in range(r):
        z = jnp.concatenate([get(zr, across(n2)), get(zi, across(n2))], axis=0)
        y = _dot3_left(m1_hi[...], m1_lo[...], z)
        yr, yi = y[:r], y[r:]
        tw = slice(n2 * r, (n2 + 1) * r)
        cos = jnp.concatenate([twc_ref[tw, :]] * 2, axis=1)
        sin = jnp.concatenate([tws_ref[tw, :]] * 2, axis=1)
        put(yr_s, block(n2), yr * cos + yi * sin)
        put(yi_s, block(n2), yi * cos - yr * sin)
    for k1 in range(r):
        y = jnp.concatenate([get(yr_s, across(k1)), get(yi_s, across(k1))], axis=0)
        put(zr, across(k1), _dot3_left(m2_hi[...], m2_lo[...], y))
    for k2 in range(r):
        rows = slice(k2 * r, (k2 + 1) * r)
        o_ref[rows, :] = ((get(zr, block(k2)) * norm) * sa_ref[rows, :].astype(F32)).astype(BF16)


def _fourier_fft(a, sa, seq):
    assert seq == FFT_R * FFT_R
    t = a.shape[0]
    cc, ss = _channel_dft_tables()
    c, s = _dft_cos_sin(FFT_R)
    m1 = np.block([[c, s], [-s, c]])
    m2 = np.concatenate([c, s], axis=1)
    n2 = np.arange(FFT_R)[:, None]
    k1 = np.arange(FFT_R)[None, :]
    ang = (2.0 * np.pi * (n2 * k1) / seq).reshape(seq, 1)
    twc = np.ascontiguousarray(np.broadcast_to(np.cos(ang).astype(np.float32), (seq, 128)))
    tws = np.ascontiguousarray(np.broadcast_to(np.sin(ang).astype(np.float32), (seq, 128)))
    tables = [x for m in (cc, ss, m1, m2) for x in _hi_lo(m)]
    row = lambda i: (i, 0)
    const = lambda i: (0, 0)
    tspecs = ([pl.BlockSpec((FN_WIDTH, FN_WIDTH), const)] * 4 + [pl.BlockSpec((2 * FFT_R, 2 * FFT_R), const)] * 2
              + [pl.BlockSpec((FFT_R, 2 * FFT_R), const)] * 2 + [pl.BlockSpec((seq, 128), const)] * 2)
    return pl.pallas_call(
        functools.partial(_fourier_fft_kernel, seq=seq, norm=float((seq * FN_GC) ** -0.5)),
        grid=(t // seq,),
        in_specs=[pl.BlockSpec((seq, FN_WIDTH), row), pl.BlockSpec((seq, FN_WIDTH), row)] + tspecs,
        out_specs=pl.BlockSpec((seq, FN_WIDTH), row),
        out_shape=jax.ShapeDtypeStruct((t, FN_WIDTH), BF16),
        scratch_shapes=[pltpu.VMEM((FFT_PITCH * FFT_R, FN_WIDTH // 2), F32)] * 8,
        compiler_params=_params(1),
        name="fourier_fft",
    )(a, sa, *tables, twc, tws)


POOL_HALO = 8
POOL_CHUNK = 256


def _pool_kernel(b_ref, sb_ref, pw_ref, ps_ref, o_ref, pad_ref, *, seq):
    zeros = jnp.zeros((POOL_HALO, POOL_WIDTH), F32)
    lane = lax.broadcasted_iota(jnp.int32, (POOL_CHUNK, 128), 1)
    low_group = lane < POOL_GC
    for i in range(pad_ref.shape[0]):
        pad_ref[i, 0:POOL_HALO, :] = zeros
        pad_ref[i, POOL_HALO + seq:, :] = zeros
        pad_ref[i, POOL_HALO:POOL_HALO + seq, :] = b_ref[i * seq:(i + 1) * seq, :]

    for i, c in [(i, c) for i in range(pad_ref.shape[0]) for c in range(seq // POOL_CHUNK)]:
        r0 = c * POOL_CHUNK
        t = lax.broadcasted_iota(jnp.int32, (POOL_CHUNK, 128), 0) + r0

        interior = r0 >= POOL_HALO and r0 + POOL_CHUNK + POOL_HALO <= seq

        def inv_count(w):
            if interior:
                return 1.0 / w
            left = w // 2
            right = w - 1 - left
            lo = jnp.maximum(t - left, 0)
            hi = jnp.minimum(t + right, seq - 1)
            return 1.0 / (hi - lo + 1).astype(F32)

        def ld(off, col):
            start = POOL_HALO + r0 + off
            return pad_ref[i, start:start + POOL_CHUNK, 128 * col:128 * (col + 1)]

        u0 = ld(0, 0)
        p2 = ld(-1, 0) + u0
        p4 = p2 + ld(-2, 0) + ld(1, 0)
        pooled0 = jnp.where(low_group, p2 * inv_count(2), p4 * inv_count(4)) - u0
        u1 = ld(0, 1)
        p8 = u1
        for off in (-4, -3, -2, -1, 1, 2, 3):
            p8 = p8 + ld(off, 1)
        p16 = p8
        for off in (-8, -7, -6, -5, 4, 5, 6, 7):
            p16 = p16 + ld(off, 1)
        pooled1 = jnp.where(low_group, p8 * inv_count(8), p16 * inv_count(16)) - u1

        pooled = jnp.concatenate([pooled0, pooled1], axis=1).astype(BF16)
        mixed = _dot(pooled, pw_ref[...]) * ps_ref[...]
        rows = slice(i * seq + r0, i * seq + r0 + POOL_CHUNK)
        o_ref[rows, :] = (mixed * sb_ref[rows, :].astype(F32)).astype(BF16)


def _pool(b, sb, pw, l, seq, seqs_per_step):
    t = b.shape[0]
    rows = seq * seqs_per_step
    row = lambda i: (i, 0)
    return pl.pallas_call(
        functools.partial(_pool_kernel, seq=seq),
        grid=(t // rows,),
        in_specs=[pl.BlockSpec((rows, POOL_WIDTH), row), pl.BlockSpec((rows, POOL_WIDTH), row),
                  _layer_spec(pw['pool_w'], l), _layer_spec(pw['pool_scale'], l)],
        out_specs=pl.BlockSpec((rows, POOL_WIDTH), row),
        out_shape=jax.ShapeDtypeStruct((t, POOL_WIDTH), BF16),
        scratch_shapes=[pltpu.VMEM((seqs_per_step, seq + 2 * POOL_HALO, POOL_WIDTH), F32)],
        compiler_params=_params(1),
        name="pool",
    )(b, sb, pw['pool_w'], pw['pool_scale'])


def _attn_kernel(*refs, heads, chunk, use_cache, lookahead, seqs):
    if use_cache:
        q_ref, k_ref, vt_ref, kc_ref, vct_ref, sc_ref, o_ref, s_ref, m_ref, l_ref, acc_ref = refs
        sources = ((k_ref, vt_ref), (kc_ref, vct_ref))
    else:
        q_ref, k_ref, vt_ref, sc_ref, o_ref, s_ref, m_ref, l_ref, acc_ref = refs
        sources = ((k_ref, vt_ref),)
    tq = q_ref.shape[0] // seqs
    slots = s_ref.shape[0]
    problems = [(i, h) for i in range(seqs) for h in range(heads)]

    def chunks_of(i):
        out, row = [], 0
        for keys, values in sources:
            n = keys.shape[0] // seqs
            for off in range(0, n, chunk):
                size = min(chunk, n - off)
                out.append((keys, values, i * n + off, size, row))
                row += size
        return out

    n_chunks = len(chunks_of(0))

    def scores(j, c):
        i, h = problems[j]
        keys, _, off, size, row = chunks_of(i)[c]
        sl = slice(HEAD_SLAB * h, HEAD_SLAB * (h + 1))
        q = q_ref[i * tq:(i + 1) * tq, sl]
        s = lax.dot_general(keys[off:off + size, sl], q, _NT, preferred_element_type=F32)
        s_ref[j % slots, row:row + size, :] = s
        m_ref[j % slots] = jnp.maximum(m_ref[j % slots], jnp.max(s.reshape(size // 8, 8, tq), axis=0))

    def weigh(j, c, m):
        i, h = problems[j]
        _, values, off, size, row = chunks_of(i)[c]
        p = jnp.exp2(s_ref[j % slots, row:row + size, :] - m)
        l_ref[j % slots] += jnp.sum(p.reshape(size // 8, 8, tq), axis=0)
        acc_ref[j % slots] += _dot(values[V_DIM * h:V_DIM * (h + 1), off:off + size], p.astype(BF16))

    outs = []
    for t in range(len(problems) + lookahead):
        j_w, j_s = t - lookahead, t
        if j_s < len(problems):
            m_ref[j_s % slots] = jnp.full((8, tq), -jnp.inf, F32)
        if j_w >= 0:
            m = jnp.max(m_ref[j_w % slots], axis=0, keepdims=True)
            l_ref[j_w % slots] = jnp.zeros((8, tq), F32)
            acc_ref[j_w % slots] = jnp.zeros((V_DIM, tq), F32)
        for c in range(n_chunks):
            if j_w >= 0:
                weigh(j_w, c, m)
            if j_s < len(problems):
                scores(j_s, c)
        if j_w >= 0:
            i, h = problems[j_w]
            denom = jnp.sum(l_ref[j_w % slots], axis=0, keepdims=True)
            outs.append(acc_ref[j_w % slots] * (1.0 / denom))
            if h % 2 == 1:
                o_pair = jnp.concatenate(outs, axis=0).T
                outs = []
                rows = slice(i * tq, (i + 1) * tq)
                sl = slice(HEAD_SLAB * (h // 2), HEAD_SLAB * (h // 2 + 1))
                o_ref[rows, sl] = (o_pair * sc_ref[rows, sl].astype(F32)).astype(BF16)


def _attention(q, k, vt, sc, cache, batch, lq, lk, tq, heads_per_step, chunk, lookahead, seqs_per_step=1):
    use_cache = cache is not None
    assert seqs_per_step == 1 or (tq == lq and not use_cache)
    n = seqs_per_step
    nq = lq // tq
    n_hp = N_HEADS // heads_per_step
    qw = heads_per_step * HEAD_SLAB
    ow = heads_per_step * V_DIM
    q_map = lambda b, g, i: (b * nq + i, g)
    k_map = lambda b, g, i: (b, g)
    vt_map = lambda b, g, i: (g, b)
    in_specs = [pl.BlockSpec((n * tq, qw), q_map), pl.BlockSpec((n * lk, qw), k_map),
                pl.BlockSpec((ow, n * lk), vt_map)]
    args = [q, k, vt]
    lc = 0
    if use_cache:
        lc = cache[0].shape[0] // batch
        in_specs += [pl.BlockSpec((lc, qw), k_map), pl.BlockSpec((ow, lc), vt_map)]
        args += list(cache)
    in_specs.append(pl.BlockSpec((n * tq, ow), q_map))
    args.append(sc)
    slots = min(n * heads_per_step, lookahead + 1)
    return pl.pallas_call(
        functools.partial(_attn_kernel, heads=heads_per_step, chunk=chunk, use_cache=use_cache, lookahead=lookahead,
                          seqs=n),
        grid=(batch // n, n_hp, nq),
        in_specs=in_specs,
        out_specs=pl.BlockSpec((n * tq, ow), q_map),
        out_shape=jax.ShapeDtypeStruct((batch * lq, ATT_WIDTH), BF16),
        scratch_shapes=[pltpu.VMEM((slots, lk + lc, tq), F32), pltpu.VMEM((slots, 8, tq), F32),
                        pltpu.VMEM((slots, 8, tq), F32), pltpu.VMEM((slots, V_DIM, tq), F32)],
        compiler_params=_params(3),
        name="attention_cache" if use_cache else "attention",
    )(*args)


def _out_kernel(*refs, final):
    if final:
        h_ref, mod_ref, ng_ref, xa_ref, xb_ref, xc_ref, wa_ref, wb_ref, wc_ref, wg_ref, wo_ref, fg_ref, o_ref = refs
    else:
        h_ref, mod_ref, ng_ref, xa_ref, xb_ref, xc_ref, wa_ref, wb_ref, wc_ref, wg_ref, wo_ref, o_ref = refs
    h = h_ref[...]
    xn = _modulated_norm(h, ng_ref[...], mod_ref).astype(BF16)
    y = None
    for i, (x_ref, w_ref) in enumerate(((xa_ref, wa_ref), (xb_ref, wb_ref), (xc_ref, wc_ref))):
        wg_i = wg_ref[i * D_MODEL:(i + 1) * D_MODEL, :]
        g = jax.nn.sigmoid(lax.dot_general(xn, wg_i, _NT, preferred_element_type=F32))
        term = g * _dot(x_ref[...], w_ref[...])
        y = term if y is None else y + term
    h_new = h + mod_ref[2:3, :] * _dot(y.astype(BF16), wo_ref[...])
    if final:
        o_ref[...] = _rms(h_new, fg_ref[...])
    else:
        o_ref[...] = h_new


def _out(h, mod, mod_row, xa, xb, xc, pw, l, final_g, tm):
    t = h.shape[0]
    final = final_g is not None
    row = lambda i: (i, 0)
    names = ['w_br_a', 'w_br_b', 'w_br_c', 'wg', 'w_out']
    in_specs = ([pl.BlockSpec((tm, D_MODEL), row), _mod_spec(l, mod_row), _layer_spec(pw['norm_g'], l),
                 pl.BlockSpec((tm, FN_WIDTH), row), pl.BlockSpec((tm, POOL_WIDTH), row),
                 pl.BlockSpec((tm, ATT_WIDTH), row)] + [_layer_spec(pw[n], l) for n in names])
    args = [h, mod, pw['norm_g'], xa, xb, xc] + [pw[n] for n in names]
    if final:
        in_specs.append(pl.BlockSpec((1, D_MODEL), lambda i: (0, 0)))
        args.append(final_g)
    return pl.pallas_call(
        functools.partial(_out_kernel, final=final),
        grid=(t // tm,),
        in_specs=in_specs,
        out_specs=pl.BlockSpec((tm, D_MODEL), row),
        out_shape=jax.ShapeDtypeStruct((t, D_MODEL), F32),
        compiler_params=_params(1),
        name="out_final" if final else "out",
    )(*args)


_ROPE_PAD = ((QK_NOPE, HEAD_SLAB - QK_NOPE - QK_ROPE),)
W1_ROW_TILE = 256
W1_PLAIN_TILES = _W1_KV[0] // W1_ROW_TILE
WG_ROW_TILE = 512


def _w1_source_row(k):
    direct = _OFF_KV // W1_ROW_TILE
    return jnp.where(k < direct, k * W1_ROW_TILE,
                     jnp.where(k < W1_PLAIN_TILES, _OFF_CZ + (k - direct) * W1_ROW_TILE, _OFF_KV))


def _pack_w1_kernel(w_ref, o_ref):
    k = pl.program_id(1)
    x = w_ref[0]

    @pl.when(k < W1_PLAIN_TILES)
    def _():
        o_ref[...] = x.astype(BF16)

    @pl.when(k == W1_PLAIN_TILES)
    def _():
        kr = x[KV_RANK:KV_RANK + QK_ROPE, :]
        q = QK_ROPE // 4
        rot = jnp.concatenate([-kr[q:2 * q], kr[:q], -kr[3 * q:], kr[2 * q:3 * q]], axis=0)
        zeros = jnp.zeros((QK_NOPE, D_MODEL), F32)
        o_ref[...] = jnp.concatenate([x[:KV_RANK], zeros, kr, rot], axis=0).astype(BF16)


def _pack_cast_kernel(w_ref, o_ref):
    o_ref[...] = w_ref[0].astype(BF16)


def _pack_w_in(w_in):
    wt = jnp.swapaxes(w_in, 1, 2)
    g_width = w_in.shape[2] - _OFF_G
    w1t = pl.pallas_call(
        _pack_w1_kernel,
        grid=(DEPTH, W1_WIDTH // W1_ROW_TILE),
        in_specs=[pl.BlockSpec((pl.Element(1), pl.Element(W1_ROW_TILE), pl.Element(D_MODEL)),
                               lambda l, k: (l, pl.multiple_of(_w1_source_row(k), 32), 0))],
        out_specs=pl.BlockSpec((None, W1_ROW_TILE, D_MODEL), lambda l, k: (l, k, 0)),
        out_shape=jax.ShapeDtypeStruct((DEPTH, W1_WIDTH, D_MODEL), BF16),
        compiler_params=_params(2),
        name="pack_w1",
    )(wt)
    wgt = pl.pallas_call(
        _pack_cast_kernel,
        grid=(DEPTH, g_width // WG_ROW_TILE),
        in_specs=[pl.BlockSpec((pl.Element(1), pl.Element(WG_ROW_TILE), pl.Element(D_MODEL)),
                               lambda l, k: (l, pl.multiple_of(_OFF_G + k * WG_ROW_TILE, 32), 0))],
        out_specs=pl.BlockSpec((None, WG_ROW_TILE, D_MODEL), lambda l, k: (l, k, 0)),
        out_shape=jax.ShapeDtypeStruct((DEPTH, g_width, D_MODEL), BF16),
        compiler_params=_params(2),
        name="pack_wg",
    )(wt)
    return w1t, wgt


def _pack_weights(norm_g, w_in, pool_w, pool_scale, q_norm_g, w_q_up, kv_norm_g, w_kv_up, w_br_a, w_br_b, w_br_c,
                  w_out):
    lead = ((0, 0), (0, 0))
    w1, wg = _pack_w_in(w_in)
    wide = N_HEADS * HEAD_SLAB
    wq_h = w_q_up.reshape(DEPTH, Q_RANK, N_HEADS, QK_NOPE + QK_ROPE)
    wq = jnp.pad(wq_h, lead + ((0, 0), (0, _ROPE_PAD[0][1]))).reshape(DEPTH, Q_RANK, wide).astype(BF16)
    wkv_h = w_kv_up.reshape(DEPTH, KV_RANK, N_HEADS, QK_NOPE + V_DIM)
    wk = jnp.pad(wkv_h[..., :QK_NOPE], lead + ((0, 0), (0, HEAD_SLAB - QK_NOPE))).reshape(DEPTH, KV_RANK, wide)
    wvt = wkv_h[..., QK_NOPE:].reshape(DEPTH, KV_RANK, ATT_WIDTH).transpose(0, 2, 1)
    groups = len(POOL_WINDOWS)
    eye = jnp.eye(groups, dtype=F32)
    pool_bd = (pool_w[:, :, :, None, :] * eye[None, :, None, :, None]).reshape(DEPTH, POOL_WIDTH, POOL_WIDTH)
    return {
        'norm_g': norm_g.reshape(DEPTH, 1, D_MODEL), 'w1': w1, 'wg': wg,
        'q_norm_g': q_norm_g.reshape(DEPTH, 1, Q_RANK), 'wq': wq,
        'kv_norm_g': kv_norm_g.reshape(DEPTH, 1, KV_RANK), 'wk': wk.astype(BF16), 'wvt': wvt.astype(BF16),
        'pool_w': pool_bd.astype(BF16), 'pool_scale': pool_scale.reshape(DEPTH, 1, POOL_WIDTH),
        'w_br_a': w_br_a.astype(BF16), 'w_br_b': w_br_b.astype(BF16), 'w_br_c': w_br_c.astype(BF16),
        'w_out': w_out.astype(BF16),
    }


def _rope_tables(seq):
    f32 = np.float32
    t = np.arange(seq)
    row = (t // GRID_W).astype(f32)
    col = (t % GRID_W).astype(f32)
    half = QK_ROPE // 2
    freqs = f32(ROPE_THETA) ** (-np.arange(0, half, 2, dtype=f32) / f32(half))
    ar = row[:, None] * freqs
    ac = col[:, None] * freqs
    cos = np.ones((seq, HEAD_SLAB), f32)
    sin = np.zeros((seq, HEAD_SLAB), f32)
    cos[:, QK_NOPE:QK_NOPE + QK_ROPE] = np.concatenate([np.cos(ar), np.cos(ar), np.cos(ac), np.cos(ac)], axis=-1)
    sin[:, QK_NOPE:QK_NOPE + QK_ROPE] = np.concatenate([np.sin(ar), np.sin(ar), np.sin(ac), np.sin(ac)], axis=-1)
    return cos, sin


TOKEN_TILE = 1024
SAMPLE_Q_TILE = 512
SAMPLE_HEADS_PER_STEP = 8
SAMPLE_KEY_CHUNK = 2048
PROMPT_SEQS_PER_STEP = 8


def kernel(x_prompt, x_sample, cache_ckv, cache_krope, c, c_ctx, norm_g, w_mod, b_mod, w_in, pool_w, pool_scale,
           q_norm_g, w_q_up, kv_norm_g, w_kv_up, w_br_a, w_br_b, w_br_c, w_out, final_norm_g):
    batch, seq, _ = x_prompt.shape
    dec_batch, dec_seq, _ = x_sample.shape
    past = cache_ckv.shape[2]
    tm = TOKEN_TILE

    mod_rows = 8
    cvec = jnp.concatenate([c_ctx[None, :], c, jnp.zeros((mod_rows - 1 - dec_batch, D_MODEL), F32)], axis=0)
    mod = _modulation(cvec, w_mod, b_mod).reshape(DEPTH, mod_rows, 3, D_MODEL)
    prompt_row = lambda i: 0
    tiles_per_sample = dec_seq // tm
    assert past % HEAD_SLAB == 0 and dec_seq % SAMPLE_KEY_CHUNK == 0
    sample_row = lambda i: 1 + i // tiles_per_sample

    cos, sin = _rope_tables(dec_seq)
    rope = (cos, sin, lambda i: i % tiles_per_sample)
    final_g = final_norm_g.reshape(1, D_MODEL)
    pw = _pack_weights(norm_g, w_in, pool_w, pool_scale, q_norm_g, w_q_up, kv_norm_g, w_kv_up, w_br_a, w_br_b,
                       w_br_c, w_out)
    cache_kr_slab = jnp.pad(cache_krope, ((0, 0), (0, 0), (0, 0)) + _ROPE_PAD)

    hp = x_prompt.reshape(batch * seq, D_MODEL)
    hs = x_sample.reshape(dec_batch * dec_seq, D_MODEL)
    ckv_list, kr_list = [], []
    for l in range(DEPTH):
        last = final_g if l == DEPTH - 1 else None

        a_in, sa, b_in, sb, q, k, vt, ckv, kr, sc = _inproj(hp, mod, prompt_row, pw, l, None, tm)
        ckv_list.append(ckv.reshape(batch, seq, KV_RANK))
        kr_list.append(kr.reshape(batch, seq, QK_ROPE))
        xa = _fourier_direct(a_in, sa, seq, PROMPT_SEQS_PER_STEP)
        xb = _pool(b_in, sb, pw, l, seq, PROMPT_SEQS_PER_STEP)
        xc = _attention(q, k, vt, sc, None, batch, seq, seq, seq, N_HEADS, seq,
                        lookahead=N_HEADS * PROMPT_SEQS_PER_STEP, seqs_per_step=PROMPT_SEQS_PER_STEP)
        hp = _out(hp, mod, prompt_row, xa, xb, xc, pw, l, last, tm)

        a_in, sa, b_in, sb, q, k, vt, _, _, sc = _inproj(hs, mod, sample_row, pw, l, rope, tm)
        cache = _cache_kv(cache_ckv, cache_kr_slab, pw, l)
        xa = _fourier_fft(a_in, sa, dec_seq)
        xb = _pool(b_in, sb, pw, l, dec_seq, 1)
        xc = _attention(q, k, vt, sc, cache, dec_batch, dec_seq, dec_seq, SAMPLE_Q_TILE, SAMPLE_HEADS_PER_STEP,
                        SAMPLE_KEY_CHUNK, lookahead=1)
        hs = _out(hs, mod, sample_row, xa, xb, xc, pw, l, last, tm)

    y_prompt = hp.reshape(batch, seq, D_MODEL)
    y_sample = hs.reshape(dec_batch, dec_seq, D_MODEL)
    return (y_prompt, y_sample, jnp.stack(ckv_list, axis=1), jnp.stack(kr_list, axis=1))
```

```python
import functools

import numpy as np
import jax
import jax.numpy as jnp
from jax import lax
from jax.experimental import pallas as pl
from jax.experimental.pallas import tpu as pltpu

D_MODEL = 1024
DEPTH = 2
GRID_W = 64
EPS = 1e-6
FN_WIDTH = 256
FN_GC = 64
POOL_WINDOWS = (2, 4, 8, 16)
POOL_WIDTH = 256
POOL_GC = 64
N_HEADS = 8
QK_NOPE = 64
QK_ROPE = 32
V_DIM = 64
Q_RANK = 256
KV_RANK = 128
ATT_WIDTH = 512
ROPE_THETA = 10000.0
HEAD_SLAB = 128
QK_SCALE = (QK_NOPE + QK_ROPE) ** -0.5
Q_PRESCALE = QK_SCALE * float(np.log2(np.e))

VMEM_LIMIT_BYTES = 56 * 1024 * 1024

F32 = jnp.float32
BF16 = jnp.bfloat16

_OFF_A, _OFF_B, _OFF_Q, _OFF_KV, _OFF_KR, _OFF_CZ, _OFF_G = 0, 512, 1024, 1280, 1408, 1440, 1952
_W1_A = (0, 512)
_W1_B = (512, 1024)
_W1_Q = (1024, 1280)
_W1_CZ = (1280, 1792)
_W1_KV = (1792, 2048)
W1_WIDTH = 2048


def _params(n_parallel):
    return pltpu.CompilerParams(dimension_semantics=("arbitrary",) * n_parallel,
                                vmem_limit_bytes=VMEM_LIMIT_BYTES)


def _dot(a, b):
    return jnp.dot(a, b, preferred_element_type=F32)


def _silu(x):
    return x * jax.nn.sigmoid(x)


def _rms(x, g):
    r = lax.rsqrt(jnp.mean(x * x, axis=-1, keepdims=True) + EPS)
    return (x * r) * g


def _modulated_norm(h, norm_g, mod_ref):
    shift = mod_ref[0:1, :]
    scale = mod_ref[1:2, :]
    return _rms(h, norm_g) * (1.0 + scale) + shift


def _split_bf16(x):
    hi = x.astype(BF16)
    lo = (x - hi.astype(F32)).astype(BF16)
    return hi, lo


def _dot3_right(x, m_hi, m_lo):
    x_hi, x_lo = _split_bf16(x)
    return _dot(x_hi, m_hi) + _dot(x_lo, m_hi) + _dot(x_hi, m_lo)


def _dot3_left(m_hi, m_lo, x):
    x_hi, x_lo = _split_bf16(x)
    return _dot(m_hi, x_hi) + _dot(m_hi, x_lo) + _dot(m_lo, x_hi)


def _mod_kernel(c_ref, w_ref, b_ref, o_ref):
    s = _silu(c_ref[...]).astype(BF16)
    o_ref[...] = _dot(s, w_ref[...].astype(BF16)) + b_ref[...]


def _modulation(cvec, w_mod, b_mod):
    rows = cvec.shape[0]
    tn = 768
    return pl.pallas_call(
        _mod_kernel,
        grid=(DEPTH, 3 * D_MODEL // tn),
        in_specs=[pl.BlockSpec((rows, D_MODEL), lambda l, j: (0, 0)),
                  pl.BlockSpec((None, D_MODEL, tn), lambda l, j: (l, 0, j)),
                  pl.BlockSpec((None, 1, tn), lambda l, j: (l, 0, j))],
        out_specs=pl.BlockSpec((None, rows, tn), lambda l, j: (l, 0, j)),
        out_shape=jax.ShapeDtypeStruct((DEPTH, rows, 3 * D_MODEL), F32),
        compiler_params=_params(2),
        name="modulation",
    )(cvec, w_mod, b_mod.reshape(DEPTH, 1, 3 * D_MODEL))


_NT = (((1,), (1,)), ((), ()))


def _key_value_heads(ckv_bf16, kr_slab, wk_ref, wvt_ref, k_ref, vt_ref):
    kn = _dot(ckv_bf16, wk_ref[...])
    for h in range(N_HEADS):
        sl = slice(HEAD_SLAB * h, HEAD_SLAB * (h + 1))
        k_ref[:, sl] = (kn[:, sl] + kr_slab).astype(BF16)
    vt_ref[...] = lax.dot_general(wvt_ref[...], ckv_bf16, _NT, preferred_element_type=F32).astype(BF16)


def _inproj_kernel(*refs, use_rope):
    if use_rope:
        (h_ref, mod_ref, ng_ref, w1_ref, qg_ref, wq_ref, kvg_ref, wk_ref, wvt_ref, cos_ref, sin_ref,
         ain_ref, sa_ref, bin_ref, sb_ref, q_ref, k_ref, vt_ref, ckv_ref, kr_ref, sc_ref) = refs
    else:
        (h_ref, mod_ref, ng_ref, w1_ref, qg_ref, wq_ref, kvg_ref, wk_ref, wvt_ref,
         ain_ref, sa_ref, bin_ref, sb_ref, q_ref, k_ref, vt_ref, ckv_ref, kr_ref, sc_ref) = refs

    xn = _modulated_norm(h_ref[...], ng_ref[...], mod_ref).astype(BF16)

    def proj(cols):
        return lax.dot_general(xn, w1_ref[cols[0]:cols[1], :], _NT, preferred_element_type=F32)

    a = proj(_W1_A)
    ain_ref[...] = a[:, :FN_WIDTH]
    sa_ref[...] = _silu(a[:, FN_WIDTH:]).astype(BF16)
    b = proj(_W1_B)
    bin_ref[...] = b[:, :POOL_WIDTH]
    sb_ref[...] = _silu(b[:, POOL_WIDTH:]).astype(BF16)
    sc_ref[...] = _silu(proj(_W1_CZ)).astype(BF16)

    qn = _rms(proj(_W1_Q), qg_ref[...]).astype(BF16)
    q = _dot(qn, wq_ref[...])
    lane = lax.broadcasted_iota(jnp.int32, (q.shape[0], HEAD_SLAB), 1)
    if use_rope:
        cos = cos_ref[...]
        sin = sin_ref[...]
        takes_upper = (lane % (QK_ROPE // 2)) < (QK_ROPE // 4)
        for h in range(N_HEADS):
            sl = slice(HEAD_SLAB * h, HEAD_SLAB * (h + 1))
            q_h = q[:, sl]
            upper = pltpu.roll(q_h, HEAD_SLAB - QK_ROPE // 4, axis=1)
            lower = pltpu.roll(q_h, QK_ROPE // 4, axis=1)
            q_rot = jnp.where(takes_upper, -upper, lower)
            q_ref[:, sl] = ((q_h * cos + q_rot * sin) * Q_PRESCALE).astype(BF16)
    else:
        q_ref[...] = (q * Q_PRESCALE).astype(BF16)

    kv = proj(_W1_KV)
    ckv = _rms(kv[:, :KV_RANK], kvg_ref[...])
    ckv_ref[...] = ckv
    slab = kv[:, KV_RANK:]
    kr_ref[...] = slab[:, QK_NOPE:QK_NOPE + QK_ROPE]
    kr = jnp.where(lane < QK_NOPE + QK_ROPE, slab, 0.0)
    if use_rope:
        kr_rot = pltpu.roll(slab, HEAD_SLAB - QK_ROPE, axis=1)
        kr = kr * cos + kr_rot * sin
    _key_value_heads(ckv.astype(BF16), kr, wk_ref, wvt_ref, k_ref, vt_ref)


def _layer_spec(arr, l):
    return pl.BlockSpec((None,) + arr.shape[1:], lambda *_: (l,) + (0,) * (arr.ndim - 1),
                        pipeline_mode=pl.Buffered(1))


def _mod_spec(l, mod_row):
    return pl.BlockSpec((None, None, 3, D_MODEL), lambda i: (l, mod_row(i), 0, 0))


def _inproj(h, mod, mod_row, pw, l, rope, tm):
    t = h.shape[0]
    use_rope = rope is not None
    row = lambda i: (i, 0)
    names = ['norm_g', 'w1', 'q_norm_g', 'wq', 'kv_norm_g', 'wk', 'wvt']
    in_specs = [pl.BlockSpec((tm, D_MODEL), row), _mod_spec(l, mod_row)] + [_layer_spec(pw[n], l) for n in names]
    args = [h, mod] + [pw[n] for n in names]
    if use_rope:
        cos, sin, rope_tile = rope
        in_specs += [pl.BlockSpec((tm, HEAD_SLAB), lambda i: (rope_tile(i), 0))] * 2
        args += [cos, sin]
    wide = N_HEADS * HEAD_SLAB
    token_outs = lambda ws: ([pl.BlockSpec((tm, w), row) for w, _ in ws],
                             [jax.ShapeDtypeStruct((t, w), dt) for w, dt in ws])
    specs_a, shapes_a = token_outs([(FN_WIDTH, F32), (FN_WIDTH, BF16), (POOL_WIDTH, F32), (POOL_WIDTH, BF16),
                                    (wide, BF16)])
    specs_b, shapes_b = token_outs([(KV_RANK, F32), (QK_ROPE, F32), (ATT_WIDTH, BF16)])
    out_specs = specs_a + [pl.BlockSpec((tm, wide), row), pl.BlockSpec((ATT_WIDTH, tm), lambda i: (0, i))] + specs_b
    out_shape = shapes_a + [jax.ShapeDtypeStruct((t, wide), BF16), jax.ShapeDtypeStruct((ATT_WIDTH, t), BF16)] + shapes_b
    return pl.pallas_call(
        functools.partial(_inproj_kernel, use_rope=use_rope),
        grid=(t // tm,),
        in_specs=in_specs,
        out_specs=out_specs,
        out_shape=out_shape,
        compiler_params=_params(1),
        name="inproj_rope" if use_rope else "inproj",
    )(*args)


def _cache_kv_kernel(ckv_ref, kr_ref, wk_ref, wvt_ref, k_ref, vt_ref):
    _key_value_heads(ckv_ref[...].astype(BF16), kr_ref[...], wk_ref, wvt_ref, k_ref, vt_ref)


def _cache_kv(cache_ckv, cache_kr_slab, pw, l):
    batch, _, past, _ = cache_ckv.shape
    wide = N_HEADS * HEAD_SLAB
    cache_map = lambda b: (b, l, 0, 0)
    return pl.pallas_call(
        _cache_kv_kernel,
        grid=(batch,),
        in_specs=[pl.BlockSpec((None, None, past, KV_RANK), cache_map),
                  pl.BlockSpec((None, None, past, HEAD_SLAB), cache_map),
                  _layer_spec(pw['wk'], l), _layer_spec(pw['wvt'], l)],
        out_specs=[pl.BlockSpec((past, wide), lambda b: (b, 0)), pl.BlockSpec((ATT_WIDTH, past), lambda b: (0, b))],
        out_shape=[jax.ShapeDtypeStruct((batch * past, wide), BF16),
                   jax.ShapeDtypeStruct((ATT_WIDTH, batch * past), BF16)],
        compiler_params=_params(1),
        name="cache_kv",
    )(cache_ckv, cache_kr_slab, pw['wk'], pw['wvt'])


def _hi_lo(m):
    m = np.asarray(m, np.float32)
    hi = m.astype(BF16)
    return hi, (m - hi.astype(np.float32)).astype(BF16)


def _dft_cos_sin(n):
    k = np.arange(n)
    ang = 2.0 * np.pi * ((k[:, None] * k[None, :]) % n) / n
    return np.cos(ang), np.sin(ang)


def _channel_dft_tables():
    c, s = _dft_cos_sin(FN_GC)
    eye = np.eye(FN_WIDTH // FN_GC)
    return np.kron(eye, c), np.kron(eye, s)


def _fourier_direct_kernel(a_ref, sa_ref, cc_hi, cc_lo, ss_hi, ss_lo, m_hi, m_lo, o_ref, *, seq, norm):
    n = a_ref.shape[0] // seq
    a = a_ref[...]
    tc = _dot3_right(a, cc_hi[...], cc_lo[...])
    ts = _dot3_right(a, ss_hi[...], ss_lo[...])
    side_by_side = lambda x: jnp.concatenate([x[i * seq:(i + 1) * seq] for i in range(n)], axis=1)
    f = _dot3_left(m_hi[...], m_lo[...], jnp.concatenate([side_by_side(tc), side_by_side(ts)], axis=0))
    for i in range(n):
        rows = slice(i * seq, (i + 1) * seq)
        f_i = f[:, i * FN_WIDTH:(i + 1) * FN_WIDTH]
        o_ref[rows, :] = ((f_i * norm) * sa_ref[rows, :].astype(F32)).astype(BF16)


def _fourier_direct(a, sa, seq, seqs_per_step):
    t = a.shape[0]
    rows = seq * seqs_per_step
    cc, ss = _channel_dft_tables()
    cl, sl = _dft_cos_sin(seq)
    tables = [x for m in (cc, ss, np.concatenate([cl, -sl], axis=1)) for x in _hi_lo(m)]
    row = lambda i: (i, 0)
    const = lambda i: (0, 0)
    tspecs = [pl.BlockSpec((FN_WIDTH, FN_WIDTH), const)] * 4 + [pl.BlockSpec((seq, 2 * seq), const)] * 2
    return pl.pallas_call(
        functools.partial(_fourier_direct_kernel, seq=seq, norm=float((seq * FN_GC) ** -0.5)),
        grid=(t // rows,),
        in_specs=[pl.BlockSpec((rows, FN_WIDTH), row), pl.BlockSpec((rows, FN_WIDTH), row)] + tspecs,
        out_specs=pl.BlockSpec((rows, FN_WIDTH), row),
        out_shape=jax.ShapeDtypeStruct((t, FN_WIDTH), BF16),
        compiler_params=_params(1),
        name="fourier_direct",
    )(a, sa, *tables)


FFT_R = 64
FFT_PITCH = FFT_R + 4


def _fourier_fft_kernel(a_ref, sa_ref, cc_hi, cc_lo, ss_hi, ss_lo, m1_hi, m1_lo, m2_hi, m2_lo, twc_ref, tws_ref,
                        o_ref, zr0, zr1, zi0, zi1, yr0, yr1, yi0, yi1, *, seq, norm):
    r = FFT_R
    half = FN_WIDTH // 2
    chunk = 512

    def put(refs, rows, x):
        refs[0][rows, :] = x[:, :half]
        refs[1][rows, :] = x[:, half:]

    def get(refs, rows):
        return jnp.concatenate([refs[0][rows, :], refs[1][rows, :]], axis=1)

    block = lambda j: slice(FFT_PITCH * j, FFT_PITCH * j + r)
    across = lambda i: pl.ds(i, r, stride=FFT_PITCH)

    zr, zi, yr_s, yi_s = (zr0, zr1), (zi0, zi1), (yr0, yr1), (yi0, yi1)
    for c in range(seq // chunk):
        a = a_ref[c * chunk:(c + 1) * chunk, :]
        zr_c = _dot3_right(a, cc_hi[...], cc_lo[...])
        zi_c = -_dot3_right(a, ss_hi[...], ss_lo[...])
        for j in range(chunk // r):
            n1 = c * (chunk // r) + j
            put(zr, block(n1), zr_c[j * r:(j + 1) * r])
            put(zi, block(n1), zi_c[j * r:(j + 1) * r])
    for n2 in range(r):
        z = jnp.concatenate([get(zr, across(n2)), get(zi, across(n2))], axis=0)
        y = _dot3_left(m1_hi[...], m1_lo[...], z)
        yr, yi = y[:r], y[r:]
        tw = slice(n2 * r, (n2 + 1) * r)
        cos = jnp.concatenate([twc_ref[tw, :]] * 2, axis=1)
        sin = jnp.concatenate([tws_ref[tw, :]] * 2, axis=1)
        put(yr_s, block(n2), yr * cos + yi * sin)
        put(yi_s, block(n2), yi * cos - yr * sin)
    for k1 in range(r):
        y = jnp.concatenate([get(yr_s, across(k1)), get(yi_s, across(k1))], axis=0)
        put(zr, across(k1), _dot3_left(m2_hi[...], m2_lo[...], y))
    for k2 in range(r):
        rows = slice(k2 * r, (k2 + 1) * r)
        o_ref[rows, :] = ((get(zr, block(k2)) * norm) * sa_ref[rows, :].astype(F32)).astype(BF16)


def _fourier_fft(a, sa, seq):
    assert seq == FFT_R * FFT_R
    t = a.shape[0]
    cc, ss = _channel_dft_tables()
    c, s = _dft_cos_sin(FFT_R)
    m1 = np.block([[c, s], [-s, c]])
    m2 = np.concatenate([c, s], axis=1)
    n2 = np.arange(FFT_R)[:, None]
    k1 = np.arange(FFT_R)[None, :]
    ang = (2.0 * np.pi * (n2 * k1) / seq).reshape(seq, 1)
    twc = np.ascontiguousarray(np.broadcast_to(np.cos(ang).astype(np.float32), (seq, 128)))
    tws = np.ascontiguousarray(np.broadcast_to(np.sin(ang).astype(np.float32), (seq, 128)))
    tables = [x for m in (cc, ss, m1, m2) for x in _hi_lo(m)]
    row = lambda i: (i, 0)
    const = lambda i: (0, 0)
    tspecs = ([pl.BlockSpec((FN_WIDTH, FN_WIDTH), const)] * 4 + [pl.BlockSpec((2 * FFT_R, 2 * FFT_R), const)] * 2
              + [pl.BlockSpec((FFT_R, 2 * FFT_R), const)] * 2 + [pl.BlockSpec((seq, 128), const)] * 2)
    return pl.pallas_call(
        functools.partial(_fourier_fft_kernel, seq=seq, norm=float((seq * FN_GC) ** -0.5)),
        grid=(t // seq,),
        in_specs=[pl.BlockSpec((seq, FN_WIDTH), row), pl.BlockSpec((seq, FN_WIDTH), row)] + tspecs,
        out_specs=pl.BlockSpec((seq, FN_WIDTH), row),
        out_shape=jax.ShapeDtypeStruct((t, FN_WIDTH), BF16),
        scratch_shapes=[pltpu.VMEM((FFT_PITCH * FFT_R, FN_WIDTH // 2), F32)] * 8,
        compiler_params=_params(1),
        name="fourier_fft",
    )(a, sa, *tables, twc, tws)


POOL_HALO = 8
POOL_CHUNK = 256


def _pool_kernel(b_ref, sb_ref, pw_ref, ps_ref, o_ref, pad_ref, *, seq):
    zeros = jnp.zeros((POOL_HALO, POOL_WIDTH), F32)
    lane = lax.broadcasted_iota(jnp.int32, (POOL_CHUNK, 128), 1)
    low_group = lane < POOL_GC
    for i in range(pad_ref.shape[0]):
        pad_ref[i, 0:POOL_HALO, :] = zeros
        pad_ref[i, POOL_HALO + seq:, :] = zeros
        pad_ref[i, POOL_HALO:POOL_HALO + seq, :] = b_ref[i * seq:(i + 1) * seq, :]

    for i, c in [(i, c) for i in range(pad_ref.shape[0]) for c in range(seq // POOL_CHUNK)]:
        r0 = c * POOL_CHUNK
        t = lax.broadcasted_iota(jnp.int32, (POOL_CHUNK, 128), 0) + r0

        interior = r0 >= POOL_HALO and r0 + POOL_CHUNK + POOL_HALO <= seq

        def inv_count(w):
            if interior:
                return 1.0 / w
            left = w // 2
            right = w - 1 - left
            lo = jnp.maximum(t - left, 0)
            hi = jnp.minimum(t + right, seq - 1)
            return 1.0 / (hi - lo + 1).astype(F32)

        def ld(off, col):
            start = POOL_HALO + r0 + off
            return pad_ref[i, start:start + POOL_CHUNK, 128 * col:128 * (col + 1)]

        u0 = ld(0, 0)
        p2 = ld(-1, 0) + u0
        p4 = p2 + ld(-2, 0) + ld(1, 0)
        pooled0 = jnp.where(low_group, p2 * inv_count(2), p4 * inv_count(4)) - u0
        u1 = ld(0, 1)
        p8 = u1
        for off in (-4, -3, -2, -1, 1, 2, 3):
            p8 = p8 + ld(off, 1)
        p16 = p8
        for off in (-8, -7, -6, -5, 4, 5, 6, 7):
            p16 = p16 + ld(off, 1)
        pooled1 = jnp.where(low_group, p8 * inv_count(8), p16 * inv_count(16)) - u1

        pooled = jnp.concatenate([pooled0, pooled1], axis=1).astype(BF16)
        mixed = _dot(pooled, pw_ref[...]) * ps_ref[...]
        rows = slice(i * seq + r0, i * seq + r0 + POOL_CHUNK)
        o_ref[rows, :] = (mixed * sb_ref[rows, :].astype(F32)).astype(BF16)


def _pool(b, sb, pw, l, seq, seqs_per_step):
    t = b.shape[0]
    rows = seq * seqs_per_step
    row = lambda i: (i, 0)
    return pl.pallas_call(
        functools.partial(_pool_kernel, seq=seq),
        grid=(t // rows,),
        in_specs=[pl.BlockSpec((rows, POOL_WIDTH), row), pl.BlockSpec((rows, POOL_WIDTH), row),
                  _layer_spec(pw['pool_w'], l), _layer_spec(pw['pool_scale'], l)],
        out_specs=pl.BlockSpec((rows, POOL_WIDTH), row),
        out_shape=jax.ShapeDtypeStruct((t, POOL_WIDTH), BF16),
        scratch_shapes=[pltpu.VMEM((seqs_per_step, seq + 2 * POOL_HALO, POOL_WIDTH), F32)],
        compiler_params=_params(1),
        name="pool",
    )(b, sb, pw['pool_w'], pw['pool_scale'])


def _attn_kernel(*refs, heads, chunk, use_cache, lookahead, seqs):
    if use_cache:
        q_ref, k_ref, vt_ref, kc_ref, vct_ref, sc_ref, o_ref, s_ref, m_ref, l_ref, acc_ref = refs
        sources = ((k_ref, vt_ref), (kc_ref, vct_ref))
    else:
        q_ref, k_ref, vt_ref, sc_ref, o_ref, s_ref, m_ref, l_ref, acc_ref = refs
        sources = ((k_ref, vt_ref),)
    tq = q_ref.shape[0] // seqs
    slots = s_ref.shape[0]
    problems = [(i, h) for i in range(seqs) for h in range(heads)]

    def chunks_of(i):
        out, row = [], 0
        for keys, values in sources:
            n = keys.shape[0] // seqs
            for off in range(0, n, chunk):
                size = min(chunk, n - off)
                out.append((keys, values, i * n + off, size, row))
                row += size
        return out

    n_chunks = len(chunks_of(0))

    def scores(j, c):
        i, h = problems[j]
        keys, _, off, size, row = chunks_of(i)[c]
        sl = slice(HEAD_SLAB * h, HEAD_SLAB * (h + 1))
        q = q_ref[i * tq:(i + 1) * tq, sl]
        s = lax.dot_general(keys[off:off + size, sl], q, _NT, preferred_element_type=F32)
        s_ref[j % slots, row:row + size, :] = s
        m_ref[j % slots] = jnp.maximum(m_ref[j % slots], jnp.max(s.reshape(size // 8, 8, tq), axis=0))

    def weigh(j, c, m):
        i, h = problems[j]
        _, values, off, size, row = chunks_of(i)[c]
        p = jnp.exp2(s_ref[j % slots, row:row + size, :] - m)
        l_ref[j % slots] += jnp.sum(p.reshape(size // 8, 8, tq), axis=0)
        acc_ref[j % slots] += _dot(values[V_DIM * h:V_DIM * (h + 1), off:off + size], p.astype(BF16))

    outs = []
    for t in range(len(problems) + lookahead):
        j_w, j_s = t - lookahead, t
        if j_s < len(problems):
            m_ref[j_s % slots] = jnp.full((8, tq), -jnp.inf, F32)
        if j_w >= 0:
            m = jnp.max(m_ref[j_w % slots], axis=0, keepdims=True)
            l_ref[j_w % slots] = jnp.zeros((8, tq), F32)
            acc_ref[j_w % slots] = jnp.zeros((V_DIM, tq), F32)
        for c in range(n_chunks):
            if j_w >= 0:
                weigh(j_w, c, m)
            if j_s < len(problems):
                scores(j_s, c)
        if j_w >= 0:
            i, h = problems[j_w]
            denom = jnp.sum(l_ref[j_w % slots], axis=0, keepdims=True)
            outs.append(acc_ref[j_w % slots] * (1.0 / denom))
            if h % 2 == 1:
                o_pair = jnp.concatenate(outs, axis=0).T
                outs = []
                rows = slice(i * tq, (i + 1) * tq)
                sl = slice(HEAD_SLAB * (h // 2), HEAD_SLAB * (h // 2 + 1))
                o_ref[rows, sl] = (o_pair * sc_ref[rows, sl].astype(F32)).astype(BF16)


def _attention(q, k, vt, sc, cache, batch, lq, lk, tq, heads_per_step, chunk, lookahead, seqs_per_step=1):
    use_cache = cache is not None
    assert seqs_per_step == 1 or (tq == lq and not use_cache)
    n = seqs_per_step
    nq = lq // tq
    n_hp = N_HEADS // heads_per_step
    qw = heads_per_step * HEAD_SLAB
    ow = heads_per_step * V_DIM
    q_map = lambda b, g, i: (b * nq + i, g)
    k_map = lambda b, g, i: (b, g)
    vt_map = lambda b, g, i: (g, b)
    in_specs = [pl.BlockSpec((n * tq, qw), q_map), pl.BlockSpec((n * lk, qw), k_map),
                pl.BlockSpec((ow, n * lk), vt_map)]
    args = [q, k, vt]
    lc = 0
    if use_cache:
        lc = cache[0].shape[0] // batch
        in_specs += [pl.BlockSpec((lc, qw), k_map), pl.BlockSpec((ow, lc), vt_map)]
        args += list(cache)
    in_specs.append(pl.BlockSpec((n * tq, ow), q_map))
    args.append(sc)
    slots = min(n * heads_per_step, lookahead + 1)
    return pl.pallas_call(
        functools.partial(_attn_kernel, heads=heads_per_step, chunk=chunk, use_cache=use_cache, lookahead=lookahead,
                          seqs=n),
        grid=(batch // n, n_hp, nq),
        in_specs=in_specs,
        out_specs=pl.BlockSpec((n * tq, ow), q_map),
        out_shape=jax.ShapeDtypeStruct((batch * lq, ATT_WIDTH), BF16),
        scratch_shapes=[pltpu.VMEM((slots, lk + lc, tq), F32), pltpu.VMEM((slots, 8, tq), F32),
                        pltpu.VMEM((slots, 8, tq), F32), pltpu.VMEM((slots, V_DIM, tq), F32)],
        compiler_params=_params(3),
        name="attention_cache" if use_cache else "attention",
    )(*args)


def _out_kernel(*refs, final):
    if final:
        h_ref, mod_ref, ng_ref, xa_ref, xb_ref, xc_ref, wa_ref, wb_ref, wc_ref, wg_ref, wo_ref, fg_ref, o_ref = refs
    else:
        h_ref, mod_ref, ng_ref, xa_ref, xb_ref, xc_ref, wa_ref, wb_ref, wc_ref, wg_ref, wo_ref, o_ref = refs
    h = h_ref[...]
    xn = _modulated_norm(h, ng_ref[...], mod_ref).astype(BF16)
    y = None
    for i, (x_ref, w_ref) in enumerate(((xa_ref, wa_ref), (xb_ref, wb_ref), (xc_ref, wc_ref))):
        wg_i = wg_ref[i * D_MODEL:(i + 1) * D_MODEL, :]
        g = jax.nn.sigmoid(lax.dot_general(xn, wg_i, _NT, preferred_element_type=F32))
        term = g * _dot(x_ref[...], w_ref[...])
        y = term if y is None else y + term
    h_new = h + mod_ref[2:3, :] * _dot(y.astype(BF16), wo_ref[...])
    if final:
        o_ref[...] = _rms(h_new, fg_ref[...])
    else:
        o_ref[...] = h_new


def _out(h, mod, mod_row, xa, xb, xc, pw, l, final_g, tm):
    t = h.shape[0]
    final = final_g is not None
    row = lambda i: (i, 0)
    names = ['w_br_a', 'w_br_b', 'w_br_c', 'wg', 'w_out']
    in_specs = ([pl.BlockSpec((tm, D_MODEL), row), _mod_spec(l, mod_row), _layer_spec(pw['norm_g'], l),
                 pl.BlockSpec((tm, FN_WIDTH), row), pl.BlockSpec((tm, POOL_WIDTH), row),
                 pl.BlockSpec((tm, ATT_WIDTH), row)] + [_layer_spec(pw[n], l) for n in names])
    args = [h, mod, pw['norm_g'], xa, xb, xc] + [pw[n] for n in names]
    if final:
        in_specs.append(pl.BlockSpec((1, D_MODEL), lambda i: (0, 0)))
        args.append(final_g)
    return pl.pallas_call(
        functools.partial(_out_kernel, final=final),
        grid=(t // tm,),
        in_specs=in_specs,
        out_specs=pl.BlockSpec((tm, D_MODEL), row),
        out_shape=jax.ShapeDtypeStruct((t, D_MODEL), F32),
        compiler_params=_params(1),
        name="out_final" if final else "out",
    )(*args)


_ROPE_PAD = ((QK_NOPE, HEAD_SLAB - QK_NOPE - QK_ROPE),)
W1_ROW_TILE = 256
W1_PLAIN_TILES = _W1_KV[0] // W1_ROW_TILE
WG_ROW_TILE = 512


def _w1_source_row(k):
    direct = _OFF_KV // W1_ROW_TILE
    return jnp.where(k < direct, k * W1_ROW_TILE,
                     jnp.where(k < W1_PLAIN_TILES, _OFF_CZ + (k - direct) * W1_ROW_TILE, _OFF_KV))


def _pack_w1_kernel(w_ref, o_ref):
    k = pl.program_id(1)
    x = w_ref[0]

    @pl.when(k < W1_PLAIN_TILES)
    def _():
        o_ref[...] = x.astype(BF16)

    @pl.when(k == W1_PLAIN_TILES)
    def _():
        kr = x[KV_RANK:KV_RANK + QK_ROPE, :]
        q = QK_ROPE // 4
        rot = jnp.concatenate([-kr[q:2 * q], kr[:q], -kr[3 * q:], kr[2 * q:3 * q]], axis=0)
        zeros = jnp.zeros((QK_NOPE, D_MODEL), F32)
        o_ref[...] = jnp.concatenate([x[:KV_RANK], zeros, kr, rot], axis=0).astype(BF16)


def _pack_cast_kernel(w_ref, o_ref):
    o_ref[...] = w_ref[0].astype(BF16)


def _pack_w_in(w_in):
    wt = jnp.swapaxes(w_in, 1, 2)
    g_width = w_in.shape[2] - _OFF_G
    w1t = pl.pallas_call(
        _pack_w1_kernel,
        grid=(DEPTH, W1_WIDTH // W1_ROW_TILE),
        in_specs=[pl.BlockSpec((pl.Element(1), pl.Element(W1_ROW_TILE), pl.Element(D_MODEL)),
                               lambda l, k: (l, pl.multiple_of(_w1_source_row(k), 32), 0))],
        out_specs=pl.BlockSpec((None, W1_ROW_TILE, D_MODEL), lambda l, k: (l, k, 0)),
        out_shape=jax.ShapeDtypeStruct((DEPTH, W1_WIDTH, D_MODEL), BF16),
        compiler_params=_params(2),
        name="pack_w1",
    )(wt)
    wgt = pl.pallas_call(
        _pack_cast_kernel,
        grid=(DEPTH, g_width // WG_ROW_TILE),
        in_specs=[pl.BlockSpec((pl.Element(1), pl.Element(WG_ROW_TILE), pl.Element(D_MODEL)),
                               lambda l, k: (l, pl.multiple_of(_OFF_G + k * WG_ROW_TILE, 32), 0))],
        out_specs=pl.BlockSpec((None, WG_ROW_TILE, D_MODEL), lambda l, k: (l, k, 0)),
        out_shape=jax.ShapeDtypeStruct((DEPTH, g_width, D_MODEL), BF16),
        compiler_params=_params(2),
        name="pack_wg",
    )(wt)
    return w1t, wgt


def _pack_weights(norm_g, w_in, pool_w, pool_scale, q_norm_g, w_q_up, kv_norm_g, w_kv_up, w_br_a, w_br_b, w_br_c,
                  w_out):
    lead = ((0, 0), (0, 0))
    w1, wg = _pack_w_in(w_in)
    wide = N_HEADS * HEAD_SLAB
    wq_h = w_q_up.reshape(DEPTH, Q_RANK, N_HEADS, QK_NOPE + QK_ROPE)
    wq = jnp.pad(wq_h, lead + ((0, 0), (0, _ROPE_PAD[0][1]))).reshape(DEPTH, Q_RANK, wide).astype(BF16)
    wkv_h = w_kv_up.reshape(DEPTH, KV_RANK, N_HEADS, QK_NOPE + V_DIM)
    wk = jnp.pad(wkv_h[..., :QK_NOPE], lead + ((0, 0), (0, HEAD_SLAB - QK_NOPE))).reshape(DEPTH, KV_RANK, wide)
    wvt = wkv_h[..., QK_NOPE:].reshape(DEPTH, KV_RANK, ATT_WIDTH).transpose(0, 2, 1)
    groups = len(POOL_WINDOWS)
    eye = jnp.eye(groups, dtype=F32)
    pool_bd = (pool_w[:, :, :, None, :] * eye[None, :, None, :, None]).reshape(DEPTH, POOL_WIDTH, POOL_WIDTH)
    return {
        'norm_g': norm_g.reshape(DEPTH, 1, D_MODEL), 'w1': w1, 'wg': wg,
        'q_norm_g': q_norm_g.reshape(DEPTH, 1, Q_RANK), 'wq': wq,
        'kv_norm_g': kv_norm_g.reshape(DEPTH, 1, KV_RANK), 'wk': wk.astype(BF16), 'wvt': wvt.astype(BF16),
        'pool_w': pool_bd.astype(BF16), 'pool_scale': pool_scale.reshape(DEPTH, 1, POOL_WIDTH),
        'w_br_a': w_br_a.astype(BF16), 'w_br_b': w_br_b.astype(BF16), 'w_br_c': w_br_c.astype(BF16),
        'w_out': w_out.astype(BF16),
    }


def _rope_tables(seq):
    f32 = np.float32
    t = np.arange(seq)
    row = (t // GRID_W).astype(f32)
    col = (t % GRID_W).astype(f32)
    half = QK_ROPE // 2
    freqs = f32(ROPE_THETA) ** (-np.arange(0, half, 2, dtype=f32) / f32(half))
    ar = row[:, None] * freqs
    ac = col[:, None] * freqs
    cos = np.ones((seq, HEAD_SLAB), f32)
    sin = np.zeros((seq, HEAD_SLAB), f32)
    cos[:, QK_NOPE:QK_NOPE + QK_ROPE] = np.concatenate([np.cos(ar), np.cos(ar), np.cos(ac), np.cos(ac)], axis=-1)
    sin[:, QK_NOPE:QK_NOPE + QK_ROPE] = np.concatenate([np.sin(ar), np.sin(ar), np.sin(ac), np.sin(ac)], axis=-1)
    return cos, sin


TOKEN_TILE = 1024
SAMPLE_Q_TILE = 512
SAMPLE_HEADS_PER_STEP = 8
SAMPLE_KEY_CHUNK = 2048
PROMPT_SEQS_PER_STEP = 4


def kernel(x_prompt, x_sample, cache_ckv, cache_krope, c, c_ctx, norm_g, w_mod, b_mod, w_in, pool_w, pool_scale,
           q_norm_g, w_q_up, kv_norm_g, w_kv_up, w_br_a, w_br_b, w_br_c, w_out, final_norm_g):
    batch, seq, _ = x_prompt.shape
    dec_batch, dec_seq, _ = x_sample.shape
    past = cache_ckv.shape[2]
    tm = TOKEN_TILE

    mod_rows = 8
    cvec = jnp.concatenate([c_ctx[None, :], c, jnp.zeros((mod_rows - 1 - dec_batch, D_MODEL), F32)], axis=0)
    mod = _modulation(cvec, w_mod, b_mod).reshape(DEPTH, mod_rows, 3, D_MODEL)
    prompt_row = lambda i: 0
    tiles_per_sample = dec_seq // tm
    assert past % HEAD_SLAB == 0 and dec_seq % SAMPLE_KEY_CHUNK == 0
    sample_row = lambda i: 1 + i // tiles_per_sample

    cos, sin = _rope_tables(dec_seq)
    rope = (cos, sin, lambda i: i % tiles_per_sample)
    final_g = final_norm_g.reshape(1, D_MODEL)
    pw = _pack_weights(norm_g, w_in, pool_w, pool_scale, q_norm_g, w_q_up, kv_norm_g, w_kv_up, w_br_a, w_br_b,
                       w_br_c, w_out)
    cache_kr_slab = jnp.pad(cache_krope, ((0, 0), (0, 0), (0, 0)) + _ROPE_PAD)

    hp = x_prompt.reshape(batch * seq, D_MODEL)
    hs = x_sample.reshape(dec_batch * dec_seq, D_MODEL)
    ckv_list, kr_list = [], []
    for l in range(DEPTH):
        last = final_g if l == DEPTH - 1 else None

        a_in, sa, b_in, sb, q, k, vt, ckv, kr, sc = _inproj(hp, mod, prompt_row, pw, l, None, tm)
        ckv_list.append(ckv.reshape(batch, seq, KV_RANK))
        kr_list.append(kr.reshape(batch, seq, QK_ROPE))
        xa = _fourier_direct(a_in, sa, seq, PROMPT_SEQS_PER_STEP)
        xb = _pool(b_in, sb, pw, l, seq, PROMPT_SEQS_PER_STEP)
        xc = _attention(q, k, vt, sc, None, batch, seq, seq, seq, N_HEADS, seq,
                        lookahead=N_HEADS * PROMPT_SEQS_PER_STEP, seqs_per_step=PROMPT_SEQS_PER_STEP)
        hp = _out(hp, mod, prompt_row, xa, xb, xc, pw, l, last, tm)

        a_in, sa, b_in, sb, q, k, vt, _, _, sc = _inproj(hs, mod, sample_row, pw, l, rope, tm)
        cache = _cache_kv(cache_ckv, cache_kr_slab, pw, l)
        xa = _fourier_fft(a_in, sa, dec_seq)
        xb = _pool(b_in, sb, pw, l, dec_seq, 1)
        xc = _attention(q, k, vt, sc, cache, dec_batch, dec_seq, dec_seq, SAMPLE_Q_TILE, SAMPLE_HEADS_PER_STEP,
                        SAMPLE_KEY_CHUNK, lookahead=1)
        hs = _out(hs, mod, sample_row, xa, xb, xc, pw, l, last, tm)

    y_prompt = hp.reshape(batch, seq, D_MODEL)
    y_sample = hs.reshape(dec_batch, dec_seq, D_MODEL)
    return (y_prompt, y_sample, jnp.stack(ckv_list, axis=1), jnp.stack(kr_list, axis=1))
```

```python
import functools

import numpy as np
import jax
import jax.numpy as jnp
from jax import lax
from jax.experimental import pallas as pl
from jax.experimental.pallas import tpu as pltpu

D_MODEL = 1024
DEPTH = 2
GRID_W = 64
EPS = 1e-6
FN_WIDTH = 256
FN_GC = 64
POOL_WINDOWS = (2, 4, 8, 16)
POOL_WIDTH = 256
POOL_GC = 64
N_HEADS = 8
QK_NOPE = 64
QK_ROPE = 32
V_DIM = 64
Q_RANK = 256
KV_RANK = 128
ATT_WIDTH = 512
ROPE_THETA = 10000.0
HEAD_SLAB = 128
QK_SCALE = (QK_NOPE + QK_ROPE) ** -0.5
Q_PRESCALE = QK_SCALE * float(np.log2(np.e))

VMEM_LIMIT_BYTES = 56 * 1024 * 1024

F32 = jnp.float32
BF16 = jnp.bfloat16

_OFF_A, _OFF_B, _OFF_Q, _OFF_KV, _OFF_KR, _OFF_CZ, _OFF_G = 0, 512, 1024, 1280, 1408, 1440, 1952
_W1_A = (0, 512)
_W1_B = (512, 1024)
_W1_Q = (1024, 1280)
_W1_CZ = (1280, 1792)
_W1_KV = (1792, 2048)
W1_WIDTH = 2048


def _params(n_parallel):
    return pltpu.CompilerParams(dimension_semantics=("arbitrary",) * n_parallel,
                                vmem_limit_bytes=VMEM_LIMIT_BYTES)


def _dot(a, b):
    return jnp.dot(a, b, preferred_element_type=F32)


def _silu(x):
    return x * jax.nn.sigmoid(x)


def _rms(x, g):
    r = lax.rsqrt(jnp.mean(x * x, axis=-1, keepdims=True) + EPS)
    return (x * r) * g


def _modulated_norm(h, norm_g, mod_ref):
    shift = mod_ref[0:1, :]
    scale = mod_ref[1:2, :]
    return _rms(h, norm_g) * (1.0 + scale) + shift


def _split_bf16(x):
    hi = x.astype(BF16)
    lo = (x - hi.astype(F32)).astype(BF16)
    return hi, lo


def _dot3_right(x, m_hi, m_lo):
    x_hi, x_lo = _split_bf16(x)
    return _dot(x_hi, m_hi) + _dot(x_lo, m_hi) + _dot(x_hi, m_lo)


def _dot3_left(m_hi, m_lo, x):
    x_hi, x_lo = _split_bf16(x)
    return _dot(m_hi, x_hi) + _dot(m_hi, x_lo) + _dot(m_lo, x_hi)


def _mod_kernel(c_ref, w_ref, b_ref, o_ref):
    s = _silu(c_ref[...]).astype(BF16)
    o_ref[...] = _dot(s, w_ref[...].astype(BF16)) + b_ref[...]


def _modulation(cvec, w_mod, b_mod):
    rows = cvec.shape[0]
    tn = 768
    return pl.pallas_call(
        _mod_kernel,
        grid=(DEPTH, 3 * D_MODEL // tn),
        in_specs=[pl.BlockSpec((rows, D_MODEL), lambda l, j: (0, 0)),
                  pl.BlockSpec((None, D_MODEL, tn), lambda l, j: (l, 0, j)),
                  pl.BlockSpec((None, 1, tn), lambda l, j: (l, 0, j))],
        out_specs=pl.BlockSpec((None, rows, tn), lambda l, j: (l, 0, j)),
        out_shape=jax.ShapeDtypeStruct((DEPTH, rows, 3 * D_MODEL), F32),
        compiler_params=_params(2),
        name="modulation",
    )(cvec, w_mod, b_mod.reshape(DEPTH, 1, 3 * D_MODEL))


_NT = (((1,), (1,)), ((), ()))


def _key_value_heads(ckv_bf16, kr_slab, wk_ref, wvt_ref, k_ref, vt_ref):
    kn = _dot(ckv_bf16, wk_ref[...])
    for h in range(N_HEADS):
        sl = slice(HEAD_SLAB * h, HEAD_SLAB * (h + 1))
        k_ref[:, sl] = (kn[:, sl] + kr_slab).astype(BF16)
    vt_ref[...] = lax.dot_general(wvt_ref[...], ckv_bf16, _NT, preferred_element_type=F32).astype(BF16)


def _inproj_kernel(*refs, use_rope):
    if use_rope:
        (h_ref, mod_ref, ng_ref, w1_ref, qg_ref, wq_ref, kvg_ref, wk_ref, wvt_ref, cos_ref, sin_ref,
         ain_ref, sa_ref, bin_ref, sb_ref, q_ref, k_ref, vt_ref, ckv_ref, kr_ref, sc_ref) = refs
    else:
        (h_ref, mod_ref, ng_ref, w1_ref, qg_ref, wq_ref, kvg_ref, wk_ref, wvt_ref,
         ain_ref, sa_ref, bin_ref, sb_ref, q_ref, k_ref, vt_ref, ckv_ref, kr_ref, sc_ref) = refs

    xn = _modulated_norm(h_ref[...], ng_ref[...], mod_ref).astype(BF16)

    def proj(cols):
        return lax.dot_general(xn, w1_ref[cols[0]:cols[1], :], _NT, preferred_element_type=F32)

    a = proj(_W1_A)
    ain_ref[...] = a[:, :FN_WIDTH]
    sa_ref[...] = _silu(a[:, FN_WIDTH:]).astype(BF16)
    b = proj(_W1_B)
    bin_ref[...] = b[:, :POOL_WIDTH]
    sb_ref[...] = _silu(b[:, POOL_WIDTH:]).astype(BF16)
    sc_ref[...] = _silu(proj(_W1_CZ)).astype(BF16)

    qn = _rms(proj(_W1_Q), qg_ref[...]).astype(BF16)
    q = _dot(qn, wq_ref[...])
    lane = lax.broadcasted_iota(jnp.int32, (q.shape[0], HEAD_SLAB), 1)
    if use_rope:
        cos = cos_ref[...]
        sin = sin_ref[...]
        takes_upper = (lane % (QK_ROPE // 2)) < (QK_ROPE // 4)
        for h in range(N_HEADS):
            sl = slice(HEAD_SLAB * h, HEAD_SLAB * (h + 1))
            q_h = q[:, sl]
            upper = pltpu.roll(q_h, HEAD_SLAB - QK_ROPE // 4, axis=1)
            lower = pltpu.roll(q_h, QK_ROPE // 4, axis=1)
            q_rot = jnp.where(takes_upper, -upper, lower)
            q_ref[:, sl] = ((q_h * cos + q_rot * sin) * Q_PRESCALE).astype(BF16)
    else:
        q_ref[...] = (q * Q_PRESCALE).astype(BF16)

    kv = proj(_W1_KV)
    ckv = _rms(kv[:, :KV_RANK], kvg_ref[...])
    ckv_ref[...] = ckv
    slab = kv[:, KV_RANK:]
    kr_ref[...] = slab[:, QK_NOPE:QK_NOPE + QK_ROPE]
    kr = jnp.where(lane < QK_NOPE + QK_ROPE, slab, 0.0)
    if use_rope:
        kr_rot = pltpu.roll(slab, HEAD_SLAB - QK_ROPE, axis=1)
        kr = kr * cos + kr_rot * sin
    _key_value_heads(ckv.astype(BF16), kr, wk_ref, wvt_ref, k_ref, vt_ref)


def _layer_spec(arr, l):
    return pl.BlockSpec((None,) + arr.shape[1:], lambda *_: (l,) + (0,) * (arr.ndim - 1),
                        pipeline_mode=pl.Buffered(1))


def _mod_spec(l, mod_row):
    return pl.BlockSpec((None, None, 3, D_MODEL), lambda i: (l, mod_row(i), 0, 0))


def _inproj(h, mod, mod_row, pw, l, rope, tm):
    t = h.shape[0]
    use_rope = rope is not None
    row = lambda i: (i, 0)
    names = ['norm_g', 'w1', 'q_norm_g', 'wq', 'kv_norm_g', 'wk', 'wvt']
    in_specs = [pl.BlockSpec((tm, D_MODEL), row), _mod_spec(l, mod_row)] + [_layer_spec(pw[n], l) for n in names]
    args = [h, mod] + [pw[n] for n in names]
    if use_rope:
        cos, sin, rope_tile = rope
        in_specs += [pl.BlockSpec((tm, HEAD_SLAB), lambda i: (rope_tile(i), 0))] * 2
        args += [cos, sin]
    wide = N_HEADS * HEAD_SLAB
    token_outs = lambda ws: ([pl.BlockSpec((tm, w), row) for w, _ in ws],
                             [jax.ShapeDtypeStruct((t, w), dt) for w, dt in ws])
    specs_a, shapes_a = token_outs([(FN_WIDTH, F32), (FN_WIDTH, BF16), (POOL_WIDTH, F32), (POOL_WIDTH, BF16),
                                    (wide, BF16)])
    specs_b, shapes_b = token_outs([(KV_RANK, F32), (QK_ROPE, F32), (ATT_WIDTH, BF16)])
    out_specs = specs_a + [pl.BlockSpec((tm, wide), row), pl.BlockSpec((ATT_WIDTH, tm), lambda i: (0, i))] + specs_b
    out_shape = shapes_a + [jax.ShapeDtypeStruct((t, wide), BF16), jax.ShapeDtypeStruct((ATT_WIDTH, t), BF16)] + shapes_b
    return pl.pallas_call(
        functools.partial(_inproj_kernel, use_rope=use_rope),
        grid=(t // tm,),
        in_specs=in_specs,
        out_specs=out_specs,
        out_shape=out_shape,
        compiler_params=_params(1),
        name="inproj_rope" if use_rope else "inproj",
    )(*args)


def _cache_kv_kernel(ckv_ref, kr_ref, wk_ref, wvt_ref, k_ref, vt_ref):
    _key_value_heads(ckv_ref[...].astype(BF16), kr_ref[...], wk_ref, wvt_ref, k_ref, vt_ref)


def _cache_kv(cache_ckv, cache_kr_slab, pw, l):
    batch, _, past, _ = cache_ckv.shape
    wide = N_HEADS * HEAD_SLAB
    cache_map = lambda b: (b, l, 0, 0)
    return pl.pallas_call(
        _cache_kv_kernel,
        grid=(batch,),
        in_specs=[pl.BlockSpec((None, None, past, KV_RANK), cache_map),
                  pl.BlockSpec((None, None, past, HEAD_SLAB), cache_map),
                  _layer_spec(pw['wk'], l), _layer_spec(pw['wvt'], l)],
        out_specs=[pl.BlockSpec((past, wide), lambda b: (b, 0)), pl.BlockSpec((ATT_WIDTH, past), lambda b: (0, b))],
        out_shape=[jax.ShapeDtypeStruct((batch * past, wide), BF16),
                   jax.ShapeDtypeStruct((ATT_WIDTH, batch * past), BF16)],
        compiler_params=_params(1),
        name="cache_kv",
    )(cache_ckv, cache_kr_slab, pw['wk'], pw['wvt'])


def _hi_lo(m):
    m = np.asarray(m, np.float32)
    hi = m.astype(BF16)
    return hi, (m - hi.astype(np.float32)).astype(BF16)


def _dft_cos_sin(n):
    k = np.arange(n)
    ang = 2.0 * np.pi * ((k[:, None] * k[None, :]) % n) / n
    return np.cos(ang), np.sin(ang)


def _channel_dft_tables():
    c, s = _dft_cos_sin(FN_GC)
    eye = np.eye(FN_WIDTH // FN_GC)
    return np.kron(eye, c), np.kron(eye, s)


def _fourier_direct_kernel(a_ref, sa_ref, cc_hi, cc_lo, ss_hi, ss_lo, m_hi, m_lo, o_ref, *, seq, norm):
    n = a_ref.shape[0] // seq
    a = a_ref[...]
    tc = _dot3_right(a, cc_hi[...], cc_lo[...])
    ts = _dot3_right(a, ss_hi[...], ss_lo[...])
    side_by_side = lambda x: jnp.concatenate([x[i * seq:(i + 1) * seq] for i in range(n)], axis=1)
    f = _dot3_left(m_hi[...], m_lo[...], jnp.concatenate([side_by_side(tc), side_by_side(ts)], axis=0))
    for i in range(n):
        rows = slice(i * seq, (i + 1) * seq)
        f_i = f[:, i * FN_WIDTH:(i + 1) * FN_WIDTH]
        o_ref[rows, :] = ((f_i * norm) * sa_ref[rows, :].astype(F32)).astype(BF16)


def _fourier_direct(a, sa, seq, seqs_per_step):
    t = a.shape[0]
    rows = seq * seqs_per_step
    cc, ss = _channel_dft_tables()
    cl, sl = _dft_cos_sin(seq)
    tables = [x for m in (cc, ss, np.concatenate([cl, -sl], axis=1)) for x in _hi_lo(m)]
    row = lambda i: (i, 0)
    const = lambda i: (0, 0)
    tspecs = [pl.BlockSpec((FN_WIDTH, FN_WIDTH), const)] * 4 + [pl.BlockSpec((seq, 2 * seq), const)] * 2
    return pl.pallas_call(
        functools.partial(_fourier_direct_kernel, seq=seq, norm=float((seq * FN_GC) ** -0.5)),
        grid=(t // rows,),
        in_specs=[pl.BlockSpec((rows, FN_WIDTH), row), pl.BlockSpec((rows, FN_WIDTH), row)] + tspecs,
        out_specs=pl.BlockSpec((rows, FN_WIDTH), row),
        out_shape=jax.ShapeDtypeStruct((t, FN_WIDTH), BF16),
        compiler_params=_params(1),
        name="fourier_direct",
    )(a, sa, *tables)


FFT_R = 64
FFT_PITCH = FFT_R + 4


def _fourier_fft_kernel(a_ref, sa_ref, cc_hi, cc_lo, ss_hi, ss_lo, m1_hi, m1_lo, m2_hi, m2_lo, twc_ref, tws_ref,
                        o_ref, zr0, zr1, zi0, zi1, yr0, yr1, yi0, yi1, *, seq, norm):
    r = FFT_R
    half = FN_WIDTH // 2
    chunk = 512

    def put(refs, rows, x):
        refs[0][rows, :] = x[:, :half]
        refs[1][rows, :] = x[:, half:]

    def get(refs, rows):
        return jnp.concatenate([refs[0][rows, :], refs[1][rows, :]], axis=1)

    block = lambda j: slice(FFT_PITCH * j, FFT_PITCH * j + r)
    across = lambda i: pl.ds(i, r, stride=FFT_PITCH)

    zr, zi, yr_s, yi_s = (zr0, zr1), (zi0, zi1), (yr0, yr1), (yi0, yi1)
    for c in range(seq // chunk):
        a = a_ref[c * chunk:(c + 1) * chunk, :]
        zr_c = _dot3_right(a, cc_hi[...], cc_lo[...])
        zi_c = -_dot3_right(a, ss_hi[...], ss_lo[...])
        for j in range(chunk // r):
            n1 = c * (chunk // r) + j
            put(zr, block(n1), zr_c[j * r:(j + 1) * r])
            put(zi, block(n1), zi_c[j * r:(j + 1) * r])
    for n2 in range(r):
        z = jnp.concatenate([get(zr, across(n2)), get(zi, across(n2))], axis=0)
        y = _dot3_left(m1_hi[...], m1_lo[...], z)
        yr, yi = y[:r], y[r:]
        tw = slice(n2 * r, (n2 + 1) * r)
        cos = jnp.concatenate([twc_ref[tw, :]] * 2, axis=1)
        sin = jnp.concatenate([tws_ref[tw, :]] * 2, axis=1)
        put(yr_s, block(n2), yr * cos + yi * sin)
        put(yi_s, block(n2), yi * cos - yr * sin)
    for k1 in range(r):
        y = jnp.concatenate([get(yr_s, across(k1)), get(yi_s, across(k1))], axis=0)
        put(zr, across(k1), _dot3_left(m2_hi[...], m2_lo[...], y))
    for k2 in range(r):
        rows = slice(k2 * r, (k2 + 1) * r)
        o_ref[rows, :] = ((get(zr, block(k2)) * norm) * sa_ref[rows, :].astype(F32)).astype(BF16)


def _fourier_fft(a, sa, seq):
    assert seq == FFT_R * FFT_R
    t = a.shape[0]
    cc, ss = _channel_dft_tables()
    c, s = _dft_cos_sin(FFT_R)
    m1 = np.block([[c, s], [-s, c]])
    m2 = np.concatenate([c, s], axis=1)
    n2 = np.arange(FFT_R)[:, None]
    k1 = np.arange(FFT_R)[None, :]
    ang = (2.0 * np.pi * (n2 * k1) / seq).reshape(seq, 1)
    twc = np.ascontiguousarray(np.broadcast_to(np.cos(ang).astype(np.float32), (seq, 128)))
    tws = np.ascontiguousarray(np.broadcast_to(np.sin(ang).astype(np.float32), (seq, 128)))
    tables = [x for m in (cc, ss, m1, m2) for x in _hi_lo(m)]
    row = lambda i: (i, 0)
    const = lambda i: (0, 0)
    tspecs = ([pl.BlockSpec((FN_WIDTH, FN_WIDTH), const)] * 4 + [pl.BlockSpec((2 * FFT_R, 2 * FFT_R), const)] * 2
              + [pl.BlockSpec((FFT_R, 2 * FFT_R), const)] * 2 + [pl.BlockSpec((seq, 128), const)] * 2)
    return pl.pallas_call(
        functools.partial(_fourier_fft_kernel, seq=seq, norm=float((seq * FN_GC) ** -0.5)),
        grid=(t // seq,),
        in_specs=[pl.BlockSpec((seq, FN_WIDTH), row), pl.BlockSpec((seq, FN_WIDTH), row)] + tspecs,
        out_specs=pl.BlockSpec((seq, FN_WIDTH), row),
        out_shape=jax.ShapeDtypeStruct((t, FN_WIDTH), BF16),
        scratch_shapes=[pltpu.VMEM((FFT_PITCH * FFT_R, FN_WIDTH // 2), F32)] * 8,
        compiler_params=_params(1),
        name="fourier_fft",
    )(a, sa, *tables, twc, tws)


POOL_HALO = 8
POOL_CHUNK = 256


def _pool_kernel(b_ref, sb_ref, pw_ref, ps_ref, o_ref, pad_ref, *, seq):
    zeros = jnp.zeros((POOL_HALO, POOL_WIDTH), F32)
    lane = lax.broadcasted_iota(jnp.int32, (POOL_CHUNK, 128), 1)
    low_group = lane < POOL_GC
    for i in range(pad_ref.shape[0]):
        pad_ref[i, 0:POOL_HALO, :] = zeros
        pad_ref[i, POOL_HALO + seq:, :] = zeros
        pad_ref[i, POOL_HALO:POOL_HALO + seq, :] = b_ref[i * seq:(i + 1) * seq, :]

    for i, c in [(i, c) for i in range(pad_ref.shape[0]) for c in range(seq // POOL_CHUNK)]:
        r0 = c * POOL_CHUNK
        t = lax.broadcasted_iota(jnp.int32, (POOL_CHUNK, 128), 0) + r0

        interior = r0 >= POOL_HALO and r0 + POOL_CHUNK + POOL_HALO <= seq

        def inv_count(w):
            if interior:
                return 1.0 / w
            left = w // 2
            right = w - 1 - left
            lo = jnp.maximum(t - left, 0)
            hi = jnp.minimum(t + right, seq - 1)
            return 1.0 / (hi - lo + 1).astype(F32)

        def ld(off, col):
            start = POOL_HALO + r0 + off
            return pad_ref[i, start:start + POOL_CHUNK, 128 * col:128 * (col + 1)]

        u0 = ld(0, 0)
        p2 = ld(-1, 0) + u0
        p4 = p2 + ld(-2, 0) + ld(1, 0)
        pooled0 = jnp.where(low_group, p2 * inv_count(2), p4 * inv_count(4)) - u0
        u1 = ld(0, 1)
        p8 = u1
        for off in (-4, -3, -2, -1, 1, 2, 3):
            p8 = p8 + ld(off, 1)
        p16 = p8
        for off in (-8, -7, -6, -5, 4, 5, 6, 7):
            p16 = p16 + ld(off, 1)
        pooled1 = jnp.where(low_group, p8 * inv_count(8), p16 * inv_count(16)) - u1

        pooled = jnp.concatenate([pooled0, pooled1], axis=1).astype(BF16)
        mixed = _dot(pooled, pw_ref[...]) * ps_ref[...]
        rows = slice(i * seq + r0, i * seq + r0 + POOL_CHUNK)
        o_ref[rows, :] = (mixed * sb_ref[rows, :].astype(F32)).astype(BF16)


def _pool(b, sb, pw, l, seq, seqs_per_step):
    t = b.shape[0]
    rows = seq * seqs_per_step
    row = lambda i: (i, 0)
    return pl.pallas_call(
        functools.partial(_pool_kernel, seq=seq),
        grid=(t // rows,),
        in_specs=[pl.BlockSpec((rows, POOL_WIDTH), row), pl.BlockSpec((rows, POOL_WIDTH), row),
                  _layer_spec(pw['pool_w'], l), _layer_spec(pw['pool_scale'], l)],
        out_specs=pl.BlockSpec((rows, POOL_WIDTH), row),
        out_shape=jax.ShapeDtypeStruct((t, POOL_WIDTH), BF16),
        scratch_shapes=[pltpu.VMEM((seqs_per_step, seq + 2 * POOL_HALO, POOL_WIDTH), F32)],
        compiler_params=_params(1),
        name="pool",
    )(b, sb, pw['pool_w'], pw['pool_scale'])


def _attn_kernel(*refs, heads, chunk, use_cache, lookahead, seqs, q_tiles):
    if use_cache:
        q_ref, k_ref, vt_ref, kc_ref, vct_ref, sc_ref, o_ref, s_ref, m_ref, l_ref, acc_ref = refs
        sources = ((k_ref, vt_ref), (kc_ref, vct_ref))
    else:
        q_ref, k_ref, vt_ref, sc_ref, o_ref, s_ref, m_ref, l_ref, acc_ref = refs
        sources = ((k_ref, vt_ref),)
    tq = q_ref.shape[0] // (seqs * q_tiles)
    slots = s_ref.shape[0]
    problems = [(i, h) for i in range(seqs * q_tiles) for h in range(heads)]

    def chunks_of(i):
        out, row = [], 0
        for keys, values in sources:
            n = keys.shape[0] // seqs
            for off in range(0, n, chunk):
                size = min(chunk, n - off)
                out.append((keys, values, (i // q_tiles) * n + off, size, row))
                row += size
        return out

    n_chunks = len(chunks_of(0))

    def scores(j, c):
        i, h = problems[j]
        keys, _, off, size, row = chunks_of(i)[c]
        sl = slice(HEAD_SLAB * h, HEAD_SLAB * (h + 1))
        q = q_ref[i * tq:(i + 1) * tq, sl]
        s = lax.dot_general(keys[off:off + size, sl], q, _NT, preferred_element_type=F32)
        s_ref[j % slots, row:row + size, :] = s
        m_ref[j % slots] = jnp.maximum(m_ref[j % slots], jnp.max(s.reshape(size // 8, 8, tq), axis=0))

    def weigh(j, c, m):
        i, h = problems[j]
        _, values, off, size, row = chunks_of(i)[c]
        p = jnp.exp2(s_ref[j % slots, row:row + size, :] - m)
        l_ref[j % slots] += jnp.sum(p.reshape(size // 8, 8, tq), axis=0)
        acc_ref[j % slots] += _dot(values[V_DIM * h:V_DIM * (h + 1), off:off + size], p.astype(BF16))

    outs = []
    for t in range(len(problems) + lookahead):
        j_w, j_s = t - lookahead, t
        if j_s < len(problems):
            m_ref[j_s % slots] = jnp.full((8, tq), -jnp.inf, F32)
        if j_w >= 0:
            m = jnp.max(m_ref[j_w % slots], axis=0, keepdims=True)
            l_ref[j_w % slots] = jnp.zeros((8, tq), F32)
            acc_ref[j_w % slots] = jnp.zeros((V_DIM, tq), F32)
        for c in range(n_chunks):
            if j_w >= 0:
                weigh(j_w, c, m)
            if j_s < len(problems):
                scores(j_s, c)
        if j_w >= 0:
            i, h = problems[j_w]
            denom = jnp.sum(l_ref[j_w % slots], axis=0, keepdims=True)
            outs.append(acc_ref[j_w % slots] * (1.0 / denom))
            if h % 2 == 1:
                o_pair = jnp.concatenate(outs, axis=0).T
                outs = []
                rows = slice(i * tq, (i + 1) * tq)
                sl = slice(HEAD_SLAB * (h // 2), HEAD_SLAB * (h // 2 + 1))
                o_ref[rows, sl] = (o_pair * sc_ref[rows, sl].astype(F32)).astype(BF16)


def _attention(q, k, vt, sc, cache, batch, lq, lk, tq, heads_per_step, chunk, lookahead, seqs_per_step=1,
               q_tiles_per_step=1):
    use_cache = cache is not None
    assert seqs_per_step == 1 or (tq == lq and not use_cache and q_tiles_per_step == 1)
    n = seqs_per_step
    nq = lq // (tq * q_tiles_per_step)
    n_hp = N_HEADS // heads_per_step
    qw = heads_per_step * HEAD_SLAB
    ow = heads_per_step * V_DIM
    q_rows = n * q_tiles_per_step * tq
    q_map = lambda b, g, i: (b * nq + i, g)
    k_map = lambda b, g, i: (b, g)
    vt_map = lambda b, g, i: (g, b)
    in_specs = [pl.BlockSpec((q_rows, qw), q_map), pl.BlockSpec((n * lk, qw), k_map),
                pl.BlockSpec((ow, n * lk), vt_map)]
    args = [q, k, vt]
    lc = 0
    if use_cache:
        lc = cache[0].shape[0] // batch
        in_specs += [pl.BlockSpec((lc, qw), k_map), pl.BlockSpec((ow, lc), vt_map)]
        args += list(cache)
    in_specs.append(pl.BlockSpec((q_rows, ow), q_map))
    args.append(sc)
    slots = min(n * q_tiles_per_step * heads_per_step, lookahead + 1)
    return pl.pallas_call(
        functools.partial(_attn_kernel, heads=heads_per_step, chunk=chunk, use_cache=use_cache, lookahead=lookahead,
                          seqs=n, q_tiles=q_tiles_per_step),
        grid=(batch // n, n_hp, nq),
        in_specs=in_specs,
        out_specs=pl.BlockSpec((q_rows, ow), q_map),
        out_shape=jax.ShapeDtypeStruct((batch * lq, ATT_WIDTH), BF16),
        scratch_shapes=[pltpu.VMEM((slots, lk + lc, tq), F32), pltpu.VMEM((slots, 8, tq), F32),
                        pltpu.VMEM((slots, 8, tq), F32), pltpu.VMEM((slots, V_DIM, tq), F32)],
        compiler_params=_params(3),
        name="attention_cache" if use_cache else "attention",
    )(*args)


def _out_kernel(*refs, final):
    if final:
        h_ref, mod_ref, ng_ref, xa_ref, xb_ref, xc_ref, wa_ref, wb_ref, wc_ref, wg_ref, wo_ref, fg_ref, o_ref = refs
    else:
        h_ref, mod_ref, ng_ref, xa_ref, xb_ref, xc_ref, wa_ref, wb_ref, wc_ref, wg_ref, wo_ref, o_ref = refs
    h = h_ref[...]
    xn = _modulated_norm(h, ng_ref[...], mod_ref).astype(BF16)
    y = None
    for i, (x_ref, w_ref) in enumerate(((xa_ref, wa_ref), (xb_ref, wb_ref), (xc_ref, wc_ref))):
        wg_i = wg_ref[i * D_MODEL:(i + 1) * D_MODEL, :]
        g = jax.nn.sigmoid(lax.dot_general(xn, wg_i, _NT, preferred_element_type=F32))
        term = g * _dot(x_ref[...], w_ref[...])
        y = term if y is None else y + term
    h_new = h + mod_ref[2:3, :] * _dot(y.astype(BF16), wo_ref[...])
    if final:
        o_ref[...] = _rms(h_new, fg_ref[...])
    else:
        o_ref[...] = h_new


def _out(h, mod, mod_row, xa, xb, xc, pw, l, final_g, tm):
    t = h.shape[0]
    final = final_g is not None
    row = lambda i: (i, 0)
    names = ['w_br_a', 'w_br_b', 'w_br_c', 'wg', 'w_out']
    in_specs = ([pl.BlockSpec((tm, D_MODEL), row), _mod_spec(l, mod_row), _layer_spec(pw['norm_g'], l),
                 pl.BlockSpec((tm, FN_WIDTH), row), pl.BlockSpec((tm, POOL_WIDTH), row),
                 pl.BlockSpec((tm, ATT_WIDTH), row)] + [_layer_spec(pw[n], l) for n in names])
    args = [h, mod, pw['norm_g'], xa, xb, xc] + [pw[n] for n in names]
    if final:
        in_specs.append(pl.BlockSpec((1, D_MODEL), lambda i: (0, 0)))
        args.append(final_g)
    return pl.pallas_call(
        functools.partial(_out_kernel, final=final),
        grid=(t // tm,),
        in_specs=in_specs,
        out_specs=pl.BlockSpec((tm, D_MODEL), row),
        out_shape=jax.ShapeDtypeStruct((t, D_MODEL), F32),
        compiler_params=_params(1),
        name="out_final" if final else "out",
    )(*args)


_ROPE_PAD = ((QK_NOPE, HEAD_SLAB - QK_NOPE - QK_ROPE),)
W1_ROW_TILE = 256
W1_PLAIN_TILES = _W1_KV[0] // W1_ROW_TILE
WG_ROW_TILE = 512


def _w1_source_row(k):
    direct = _OFF_KV // W1_ROW_TILE
    return jnp.where(k < direct, k * W1_ROW_TILE,
                     jnp.where(k < W1_PLAIN_TILES, _OFF_CZ + (k - direct) * W1_ROW_TILE, _OFF_KV))


def _pack_w1_kernel(w_ref, o_ref):
    k = pl.program_id(1)
    x = w_ref[0]

    @pl.when(k < W1_PLAIN_TILES)
    def _():
        o_ref[...] = x.astype(BF16)

    @pl.when(k == W1_PLAIN_TILES)
    def _():
        kr = x[KV_RANK:KV_RANK + QK_ROPE, :]
        q = QK_ROPE // 4
        rot = jnp.concatenate([-kr[q:2 * q], kr[:q], -kr[3 * q:], kr[2 * q:3 * q]], axis=0)
        zeros = jnp.zeros((QK_NOPE, D_MODEL), F32)
        o_ref[...] = jnp.concatenate([x[:KV_RANK], zeros, kr, rot], axis=0).astype(BF16)


def _pack_cast_kernel(w_ref, o_ref):
    o_ref[...] = w_ref[0].astype(BF16)


def _pack_w_in(w_in):
    wt = jnp.swapaxes(w_in, 1, 2)
    g_width = w_in.shape[2] - _OFF_G
    w1t = pl.pallas_call(
        _pack_w1_kernel,
        grid=(DEPTH, W1_WIDTH // W1_ROW_TILE),
        in_specs=[pl.BlockSpec((pl.Element(1), pl.Element(W1_ROW_TILE), pl.Element(D_MODEL)),
                               lambda l, k: (l, pl.multiple_of(_w1_source_row(k), 32), 0))],
        out_specs=pl.BlockSpec((None, W1_ROW_TILE, D_MODEL), lambda l, k: (l, k, 0)),
        out_shape=jax.ShapeDtypeStruct((DEPTH, W1_WIDTH, D_MODEL), BF16),
        compiler_params=_params(2),
        name="pack_w1",
    )(wt)
    wgt = pl.pallas_call(
        _pack_cast_kernel,
        grid=(DEPTH, g_width // WG_ROW_TILE),
        in_specs=[pl.BlockSpec((pl.Element(1), pl.Element(WG_ROW_TILE), pl.Element(D_MODEL)),
                               lambda l, k: (l, pl.multiple_of(_OFF_G + k * WG_ROW_TILE, 32), 0))],
        out_specs=pl.BlockSpec((None, WG_ROW_TILE, D_MODEL), lambda l, k: (l, k, 0)),
        out_shape=jax.ShapeDtypeStruct((DEPTH, g_width, D_MODEL), BF16),
        compiler_params=_params(2),
        name="pack_wg",
    )(wt)
    return w1t, wgt


def _pack_weights(norm_g, w_in, pool_w, pool_scale, q_norm_g, w_q_up, kv_norm_g, w_kv_up, w_br_a, w_br_b, w_br_c,
                  w_out):
    lead = ((0, 0), (0, 0))
    w1, wg = _pack_w_in(w_in)
    wide = N_HEADS * HEAD_SLAB
    wq_h = w_q_up.reshape(DEPTH, Q_RANK, N_HEADS, QK_NOPE + QK_ROPE)
    wq = jnp.pad(wq_h, lead + ((0, 0), (0, _ROPE_PAD[0][1]))).reshape(DEPTH, Q_RANK, wide).astype(BF16)
    wkv_h = w_kv_up.reshape(DEPTH, KV_RANK, N_HEADS, QK_NOPE + V_DIM)
    wk = jnp.pad(wkv_h[..., :QK_NOPE], lead + ((0, 0), (0, HEAD_SLAB - QK_NOPE))).reshape(DEPTH, KV_RANK, wide)
    wvt = wkv_h[..., QK_NOPE:].reshape(DEPTH, KV_RANK, ATT_WIDTH).transpose(0, 2, 1)
    groups = len(POOL_WINDOWS)
    eye = jnp.eye(groups, dtype=F32)
    pool_bd = (pool_w[:, :, :, None, :] * eye[None, :, None, :, None]).reshape(DEPTH, POOL_WIDTH, POOL_WIDTH)
    return {
        'norm_g': norm_g.reshape(DEPTH, 1, D_MODEL), 'w1': w1, 'wg': wg,
        'q_norm_g': q_norm_g.reshape(DEPTH, 1, Q_RANK), 'wq': wq,
        'kv_norm_g': kv_norm_g.reshape(DEPTH, 1, KV_RANK), 'wk': wk.astype(BF16), 'wvt': wvt.astype(BF16),
        'pool_w': pool_bd.astype(BF16), 'pool_scale': pool_scale.reshape(DEPTH, 1, POOL_WIDTH),
        'w_br_a': w_br_a.astype(BF16), 'w_br_b': w_br_b.astype(BF16), 'w_br_c': w_br_c.astype(BF16),
        'w_out': w_out.astype(BF16),
    }


def _rope_tables(seq):
    f32 = np.float32
    t = np.arange(seq)
    row = (t // GRID_W).astype(f32)
    col = (t % GRID_W).astype(f32)
    half = QK_ROPE // 2
    freqs = f32(ROPE_THETA) ** (-np.arange(0, half, 2, dtype=f32) / f32(half))
    ar = row[:, None] * freqs
    ac = col[:, None] * freqs
    cos = np.ones((seq, HEAD_SLAB), f32)
    sin = np.zeros((seq, HEAD_SLAB), f32)
    cos[:, QK_NOPE:QK_NOPE + QK_ROPE] = np.concatenate([np.cos(ar), np.cos(ar), np.cos(ac), np.cos(ac)], axis=-1)
    sin[:, QK_NOPE:QK_NOPE + QK_ROPE] = np.concatenate([np.sin(ar), np.sin(ar), np.sin(ac), np.sin(ac)], axis=-1)
    return cos, sin


TOKEN_TILE = 1024
SAMPLE_Q_TILE = 512
SAMPLE_HEADS_PER_STEP = 8
SAMPLE_Q_TILES_PER_STEP = 2
SAMPLE_KEY_CHUNK = 2048
PROMPT_SEQS_PER_STEP = 4


def kernel(x_prompt, x_sample, cache_ckv, cache_krope, c, c_ctx, norm_g, w_mod, b_mod, w_in, pool_w, pool_scale,
           q_norm_g, w_q_up, kv_norm_g, w_kv_up, w_br_a, w_br_b, w_br_c, w_out, final_norm_g):
    batch, seq, _ = x_prompt.shape
    dec_batch, dec_seq, _ = x_sample.shape
    past = cache_ckv.shape[2]
    tm = TOKEN_TILE

    mod_rows = 8
    cvec = jnp.concatenate([c_ctx[None, :], c, jnp.zeros((mod_rows - 1 - dec_batch, D_MODEL), F32)], axis=0)
    mod = _modulation(cvec, w_mod, b_mod).reshape(DEPTH, mod_rows, 3, D_MODEL)
    prompt_row = lambda i: 0
    tiles_per_sample = dec_seq // tm
    assert past % HEAD_SLAB == 0 and dec_seq % SAMPLE_KEY_CHUNK == 0
    sample_row = lambda i: 1 + i // tiles_per_sample

    cos, sin = _rope_tables(dec_seq)
    rope = (cos, sin, lambda i: i % tiles_per_sample)
    final_g = final_norm_g.reshape(1, D_MODEL)
    pw = _pack_weights(norm_g, w_in, pool_w, pool_scale, q_norm_g, w_q_up, kv_norm_g, w_kv_up, w_br_a, w_br_b,
                       w_br_c, w_out)
    cache_kr_slab = jnp.pad(cache_krope, ((0, 0), (0, 0), (0, 0)) + _ROPE_PAD)

    hp = x_prompt.reshape(batch * seq, D_MODEL)
    hs = x_sample.reshape(dec_batch * dec_seq, D_MODEL)
    ckv_list, kr_list = [], []
    for l in range(DEPTH):
        last = final_g if l == DEPTH - 1 else None

        a_in, sa, b_in, sb, q, k, vt, ckv, kr, sc = _inproj(hp, mod, prompt_row, pw, l, None, tm)
        ckv_list.append(ckv.reshape(batch, seq, KV_RANK))
        kr_list.append(kr.reshape(batch, seq, QK_ROPE))
        xa = _fourier_direct(a_in, sa, seq, PROMPT_SEQS_PER_STEP)
        xb = _pool(b_in, sb, pw, l, seq, PROMPT_SEQS_PER_STEP)
        xc = _attention(q, k, vt, sc, None, batch, seq, seq, seq, N_HEADS, seq,
                        lookahead=N_HEADS * PROMPT_SEQS_PER_STEP, seqs_per_step=PROMPT_SEQS_PER_STEP)
        hp = _out(hp, mod, prompt_row, xa, xb, xc, pw, l, last, tm)

        a_in, sa, b_in, sb, q, k, vt, _, _, sc = _inproj(hs, mod, sample_row, pw, l, rope, tm)
        cache = _cache_kv(cache_ckv, cache_kr_slab, pw, l)
        xa = _fourier_fft(a_in, sa, dec_seq)
        xb = _pool(b_in, sb, pw, l, dec_seq, 1)
        xc = _attention(q, k, vt, sc, cache, dec_batch, dec_seq, dec_seq, SAMPLE_Q_TILE, SAMPLE_HEADS_PER_STEP,
                        SAMPLE_KEY_CHUNK, lookahead=1, q_tiles_per_step=SAMPLE_Q_TILES_PER_STEP)
        hs = _out(hs, mod, sample_row, xa, xb, xc, pw, l, last, tm)

    y_prompt = hp.reshape(batch, seq, D_MODEL)
    y_sample = hs.reshape(dec_batch, dec_seq, D_MODEL)
    return (y_prompt, y_sample, jnp.stack(ckv_list, axis=1), jnp.stack(kr_list, axis=1))
```

```python
import functools

import numpy as np
import jax
import jax.numpy as jnp
from jax import lax
from jax.experimental import pallas as pl
from jax.experimental.pallas import tpu as pltpu

D_MODEL = 1024
DEPTH = 2
GRID_W = 64
EPS = 1e-6
FN_WIDTH = 256
FN_GC = 64
POOL_WINDOWS = (2, 4, 8, 16)
POOL_WIDTH = 256
POOL_GC = 64
N_HEADS = 8
QK_NOPE = 64
QK_ROPE = 32
V_DIM = 64
Q_RANK = 256
KV_RANK = 128
ATT_WIDTH = 512
ROPE_THETA = 10000.0
HEAD_SLAB = 128
QK_SCALE = (QK_NOPE + QK_ROPE) ** -0.5
Q_PRESCALE = QK_SCALE * float(np.log2(np.e))

VMEM_LIMIT_BYTES = 56 * 1024 * 1024

F32 = jnp.float32
BF16 = jnp.bfloat16

_OFF_A, _OFF_B, _OFF_Q, _OFF_KV, _OFF_KR, _OFF_CZ, _OFF_G = 0, 512, 1024, 1280, 1408, 1440, 1952
_W1_A = (0, 512)
_W1_B = (512, 1024)
_W1_Q = (1024, 1280)
_W1_CZ = (1280, 1792)
_W1_KV = (1792, 2048)
W1_WIDTH = 2048


def _params(n_parallel):
    return pltpu.CompilerParams(dimension_semantics=("arbitrary",) * n_parallel,
                                vmem_limit_bytes=VMEM_LIMIT_BYTES)


def _dot(a, b):
    return jnp.dot(a, b, preferred_element_type=F32)


def _silu(x):
    return x * jax.nn.sigmoid(x)


def _rms(x, g):
    r = lax.rsqrt(jnp.mean(x * x, axis=-1, keepdims=True) + EPS)
    return (x * r) * g


def _modulated_norm(h, norm_g, mod_ref):
    shift = mod_ref[0:1, :]
    scale = mod_ref[1:2, :]
    return _rms(h, norm_g) * (1.0 + scale) + shift


def _split_bf16(x):
    hi = x.astype(BF16)
    lo = (x - hi.astype(F32)).astype(BF16)
    return hi, lo


def _dot3_right(x, m_hi, m_lo):
    x_hi, x_lo = _split_bf16(x)
    return _dot(x_hi, m_hi) + _dot(x_lo, m_hi) + _dot(x_hi, m_lo)


def _dot3_left(m_hi, m_lo, x):
    x_hi, x_lo = _split_bf16(x)
    return _dot(m_hi, x_hi) + _dot(m_hi, x_lo) + _dot(m_lo, x_hi)


def _mod_kernel(c_ref, w_ref, b_ref, o_ref):
    s = _silu(c_ref[...]).astype(BF16)
    o_ref[...] = _dot(s, w_ref[...].astype(BF16)) + b_ref[...]


def _modulation(cvec, w_mod, b_mod):
    rows = cvec.shape[0]
    tn = 768
    return pl.pallas_call(
        _mod_kernel,
        grid=(DEPTH, 3 * D_MODEL // tn),
        in_specs=[pl.BlockSpec((rows, D_MODEL), lambda l, j: (0, 0)),
                  pl.BlockSpec((None, D_MODEL, tn), lambda l, j: (l, 0, j)),
                  pl.BlockSpec((None, 1, tn), lambda l, j: (l, 0, j))],
        out_specs=pl.BlockSpec((None, rows, tn), lambda l, j: (l, 0, j)),
        out_shape=jax.ShapeDtypeStruct((DEPTH, rows, 3 * D_MODEL), F32),
        compiler_params=_params(2),
        name="modulation",
    )(cvec, w_mod, b_mod.reshape(DEPTH, 1, 3 * D_MODEL))


_NT = (((1,), (1,)), ((), ()))


def _key_value_heads(ckv_bf16, kr_slab, wk_ref, wvt_ref, k_ref, vt_ref):
    kn = _dot(ckv_bf16, wk_ref[...])
    for h in range(N_HEADS):
        sl = slice(HEAD_SLAB * h, HEAD_SLAB * (h + 1))
        k_ref[:, sl] = (kn[:, sl] + kr_slab).astype(BF16)
    vt_ref[...] = lax.dot_general(wvt_ref[...], ckv_bf16, _NT, preferred_element_type=F32).astype(BF16)


def _inproj_kernel(*refs, use_rope):
    if use_rope:
        (h_ref, mod_ref, ng_ref, w1_ref, qg_ref, wq_ref, kvg_ref, wk_ref, wvt_ref, cos_ref, sin_ref,
         ain_ref, sa_ref, bin_ref, sb_ref, q_ref, k_ref, vt_ref, ckv_ref, kr_ref, sc_ref) = refs
    else:
        (h_ref, mod_ref, ng_ref, w1_ref, qg_ref, wq_ref, kvg_ref, wk_ref, wvt_ref,
         ain_ref, sa_ref, bin_ref, sb_ref, q_ref, k_ref, vt_ref, ckv_ref, kr_ref, sc_ref) = refs

    xn = _modulated_norm(h_ref[...], ng_ref[...], mod_ref).astype(BF16)

    def proj(cols):
        return lax.dot_general(xn, w1_ref[cols[0]:cols[1], :], _NT, preferred_element_type=F32)

    a = proj(_W1_A)
    ain_ref[...] = a[:, :FN_WIDTH]
    sa_ref[...] = _silu(a[:, FN_WIDTH:]).astype(BF16)
    b = proj(_W1_B)
    bin_ref[...] = b[:, :POOL_WIDTH]
    sb_ref[...] = _silu(b[:, POOL_WIDTH:]).astype(BF16)
    sc_ref[...] = _silu(proj(_W1_CZ)).astype(BF16)

    qn = _rms(proj(_W1_Q), qg_ref[...]).astype(BF16)
    q = _dot(qn, wq_ref[...])
    lane = lax.broadcasted_iota(jnp.int32, (q.shape[0], HEAD_SLAB), 1)
    if use_rope:
        cos = cos_ref[...]
        sin = sin_ref[...]
        takes_upper = (lane % (QK_ROPE // 2)) < (QK_ROPE // 4)
        for h in range(N_HEADS):
            sl = slice(HEAD_SLAB * h, HEAD_SLAB * (h + 1))
            q_h = q[:, sl]
            upper = pltpu.roll(q_h, HEAD_SLAB - QK_ROPE // 4, axis=1)
            lower = pltpu.roll(q_h, QK_ROPE // 4, axis=1)
            q_rot = jnp.where(takes_upper, -upper, lower)
            q_ref[:, sl] = ((q_h * cos + q_rot * sin) * Q_PRESCALE).astype(BF16)
    else:
        q_ref[...] = (q * Q_PRESCALE).astype(BF16)

    kv = proj(_W1_KV)
    ckv = _rms(kv[:, :KV_RANK], kvg_ref[...])
    ckv_ref[...] = ckv
    slab = kv[:, KV_RANK:]
    kr_ref[...] = slab[:, QK_NOPE:QK_NOPE + QK_ROPE]
    kr = jnp.where(lane < QK_NOPE + QK_ROPE, slab, 0.0)
    if use_rope:
        kr_rot = pltpu.roll(slab, HEAD_SLAB - QK_ROPE, axis=1)
        kr = kr * cos + kr_rot * sin
    _key_value_heads(ckv.astype(BF16), kr, wk_ref, wvt_ref, k_ref, vt_ref)


def _layer_spec(arr, l):
    return pl.BlockSpec((None,) + arr.shape[1:], lambda *_: (l,) + (0,) * (arr.ndim - 1),
                        pipeline_mode=pl.Buffered(1))


def _mod_spec(l, mod_row):
    return pl.BlockSpec((None, None, 3, D_MODEL), lambda i: (l, mod_row(i), 0, 0))


def _inproj(h, mod, mod_row, pw, l, rope, tm):
    t = h.shape[0]
    use_rope = rope is not None
    row = lambda i: (i, 0)
    names = ['norm_g', 'w1', 'q_norm_g', 'wq', 'kv_norm_g', 'wk', 'wvt']
    in_specs = [pl.BlockSpec((tm, D_MODEL), row), _mod_spec(l, mod_row)] + [_layer_spec(pw[n], l) for n in names]
    args = [h, mod] + [pw[n] for n in names]
    if use_rope:
        cos, sin, rope_tile = rope
        in_specs += [pl.BlockSpec((tm, HEAD_SLAB), lambda i: (rope_tile(i), 0))] * 2
        args += [cos, sin]
    wide = N_HEADS * HEAD_SLAB
    token_outs = lambda ws: ([pl.BlockSpec((tm, w), row) for w, _ in ws],
                             [jax.ShapeDtypeStruct((t, w), dt) for w, dt in ws])
    specs_a, shapes_a = token_outs([(FN_WIDTH, F32), (FN_WIDTH, BF16), (POOL_WIDTH, F32), (POOL_WIDTH, BF16),
                                    (wide, BF16)])
    specs_b, shapes_b = token_outs([(KV_RANK, F32), (QK_ROPE, F32), (ATT_WIDTH, BF16)])
    out_specs = specs_a + [pl.BlockSpec((tm, wide), row), pl.BlockSpec((ATT_WIDTH, tm), lambda i: (0, i))] + specs_b
    out_shape = shapes_a + [jax.ShapeDtypeStruct((t, wide), BF16), jax.ShapeDtypeStruct((ATT_WIDTH, t), BF16)] + shapes_b
    return pl.pallas_call(
        functools.partial(_inproj_kernel, use_rope=use_rope),
        grid=(t // tm,),
        in_specs=in_specs,
        out_specs=out_specs,
        out_shape=out_shape,
        compiler_params=_params(1),
        name="inproj_rope" if use_rope else "inproj",
    )(*args)


def _cache_kv_kernel(ckv_ref, kr_ref, wk_ref, wvt_ref, k_ref, vt_ref):
    _key_value_heads(ckv_ref[...].astype(BF16), kr_ref[...], wk_ref, wvt_ref, k_ref, vt_ref)


def _cache_kv(cache_ckv, cache_kr_slab, pw, l):
    batch, _, past, _ = cache_ckv.shape
    wide = N_HEADS * HEAD_SLAB
    cache_map = lambda b: (b, l, 0, 0)
    return pl.pallas_call(
        _cache_kv_kernel,
        grid=(batch,),
        in_specs=[pl.BlockSpec((None, None, past, KV_RANK), cache_map),
                  pl.BlockSpec((None, None, past, HEAD_SLAB), cache_map),
                  _layer_spec(pw['wk'], l), _layer_spec(pw['wvt'], l)],
        out_specs=[pl.BlockSpec((past, wide), lambda b: (b, 0)), pl.BlockSpec((ATT_WIDTH, past), lambda b: (0, b))],
        out_shape=[jax.ShapeDtypeStruct((batch * past, wide), BF16),
                   jax.ShapeDtypeStruct((ATT_WIDTH, batch * past), BF16)],
        compiler_params=_params(1),
        name="cache_kv",
    )(cache_ckv, cache_kr_slab, pw['wk'], pw['wvt'])


def _hi_lo(m):
    m = np.asarray(m, np.float32)
    hi = m.astype(BF16)
    return hi, (m - hi.astype(np.float32)).astype(BF16)


def _dft_cos_sin(n):
    k = np.arange(n)
    ang = 2.0 * np.pi * ((k[:, None] * k[None, :]) % n) / n
    return np.cos(ang), np.sin(ang)


def _channel_dft_tables():
    c, s = _dft_cos_sin(FN_GC)
    eye = np.eye(FN_WIDTH // FN_GC)
    return np.kron(eye, c), np.kron(eye, s)


def _fourier_direct_kernel(a_ref, sa_ref, cc_hi, cc_lo, ss_hi, ss_lo, m_hi, m_lo, o_ref, *, seq, norm):
    n = a_ref.shape[0] // seq
    a = a_ref[...]
    tc = _dot3_right(a, cc_hi[...], cc_lo[...])
    ts = _dot3_right(a, ss_hi[...], ss_lo[...])
    side_by_side = lambda x: jnp.concatenate([x[i * seq:(i + 1) * seq] for i in range(n)], axis=1)
    f = _dot3_left(m_hi[...], m_lo[...], jnp.concatenate([side_by_side(tc), side_by_side(ts)], axis=0))
    for i in range(n):
        rows = slice(i * seq, (i + 1) * seq)
        f_i = f[:, i * FN_WIDTH:(i + 1) * FN_WIDTH]
        o_ref[rows, :] = ((f_i * norm) * sa_ref[rows, :].astype(F32)).astype(BF16)


def _fourier_direct(a, sa, seq, seqs_per_step):
    t = a.shape[0]
    rows = seq * seqs_per_step
    cc, ss = _channel_dft_tables()
    cl, sl = _dft_cos_sin(seq)
    tables = [x for m in (cc, ss, np.concatenate([cl, -sl], axis=1)) for x in _hi_lo(m)]
    row = lambda i: (i, 0)
    const = lambda i: (0, 0)
    tspecs = [pl.BlockSpec((FN_WIDTH, FN_WIDTH), const)] * 4 + [pl.BlockSpec((seq, 2 * seq), const)] * 2
    return pl.pallas_call(
        functools.partial(_fourier_direct_kernel, seq=seq, norm=float((seq * FN_GC) ** -0.5)),
        grid=(t // rows,),
        in_specs=[pl.BlockSpec((rows, FN_WIDTH), row), pl.BlockSpec((rows, FN_WIDTH), row)] + tspecs,
        out_specs=pl.BlockSpec((rows, FN_WIDTH), row),
        out_shape=jax.ShapeDtypeStruct((t, FN_WIDTH), BF16),
        compiler_params=_params(1),
        name="fourier_direct",
    )(a, sa, *tables)


FFT_R = 64
FFT_PITCH = FFT_R + 4


def _fourier_fft_kernel(a_ref, sa_ref, cc_hi, cc_lo, ss_hi, ss_lo, m1_hi, m1_lo, m2_hi, m2_lo, twc_ref, tws_ref,
                        o_ref, zr0, zr1, zi0, zi1, yr0, yr1, yi0, yi1, *, seq, norm):
    r = FFT_R
    half = FN_WIDTH // 2
    chunk = 512

    def put(refs, rows, x):
        refs[0][rows, :] = x[:, :half]
        refs[1][rows, :] = x[:, half:]

    def get(refs, rows):
        return jnp.concatenate([refs[0][rows, :], refs[1][rows, :]], axis=1)

    block = lambda j: slice(FFT_PITCH * j, FFT_PITCH * j + r)
    across = lambda i: pl.ds(i, r, stride=FFT_PITCH)

    zr, zi, yr_s, yi_s = (zr0, zr1), (zi0, zi1), (yr0, yr1), (yi0, yi1)
    for c in range(seq // chunk):
        a = a_ref[c * chunk:(c + 1) * chunk, :]
        zr_c = _dot3_right(a, cc_hi[...], cc_lo[...])
        zi_c = -_dot3_right(a, ss_hi[...], ss_lo[...])
        for j in range(chunk // r):
            n1 = c * (chunk // r) + j
            put(zr, block(n1), zr_c[j * r:(j + 1) * r])
            put(zi, block(n1), zi_c[j * r:(j + 1) * r])
    for n2 in range(r):
        z = jnp.concatenate([get(zr, across(n2)), get(zi, across(n2))], axis=0)
        y = _dot3_left(m1_hi[...], m1_lo[...], z)
        yr, yi = y[:r], y[r:]
        tw = slice(n2 * r, (n2 + 1) * r)
        cos = jnp.concatenate([twc_ref[tw, :]] * 2, axis=1)
        sin = jnp.concatenate([tws_ref[tw, :]] * 2, axis=1)
        put(yr_s, block(n2), yr * cos + yi * sin)
        put(yi_s, block(n2), yi * cos - yr * sin)
    for k1 in range(r):
        y = jnp.concatenate([get(yr_s, across(k1)), get(yi_s, across(k1))], axis=0)
        put(zr, across(k1), _dot3_left(m2_hi[...], m2_lo[...], y))
    for k2 in range(r):
        rows = slice(k2 * r, (k2 + 1) * r)
        o_ref[rows, :] = ((get(zr, block(k2)) * norm) * sa_ref[rows, :].astype(F32)).astype(BF16)


def _fourier_fft(a, sa, seq):
    assert seq == FFT_R * FFT_R
    t = a.shape[0]
    cc, ss = _channel_dft_tables()
    c, s = _dft_cos_sin(FFT_R)
    m1 = np.block([[c, s], [-s, c]])
    m2 = np.concatenate([c, s], axis=1)
    n2 = np.arange(FFT_R)[:, None]
    k1 = np.arange(FFT_R)[None, :]
    ang = (2.0 * np.pi * (n2 * k1) / seq).reshape(seq, 1)
    twc = np.ascontiguousarray(np.broadcast_to(np.cos(ang).astype(np.float32), (seq, 128)))
    tws = np.ascontiguousarray(np.broadcast_to(np.sin(ang).astype(np.float32), (seq, 128)))
    tables = [x for m in (cc, ss, m1, m2) for x in _hi_lo(m)]
    row = lambda i: (i, 0)
    const = lambda i: (0, 0)
    tspecs = ([pl.BlockSpec((FN_WIDTH, FN_WIDTH), const)] * 4 + [pl.BlockSpec((2 * FFT_R, 2 * FFT_R), const)] * 2
              + [pl.BlockSpec((FFT_R, 2 * FFT_R), const)] * 2 + [pl.BlockSpec((seq, 128), const)] * 2)
    return pl.pallas_call(
        functools.partial(_fourier_fft_kernel, seq=seq, norm=float((seq * FN_GC) ** -0.5)),
        grid=(t // seq,),
        in_specs=[pl.BlockSpec((seq, FN_WIDTH), row), pl.BlockSpec((seq, FN_WIDTH), row)] + tspecs,
        out_specs=pl.BlockSpec((seq, FN_WIDTH), row),
        out_shape=jax.ShapeDtypeStruct((t, FN_WIDTH), BF16),
        scratch_shapes=[pltpu.VMEM((FFT_PITCH * FFT_R, FN_WIDTH // 2), F32)] * 8,
        compiler_params=_params(1),
        name="fourier_fft",
    )(a, sa, *tables, twc, tws)


POOL_HALO = 8
POOL_CHUNK = 256


def _pool_kernel(b_ref, sb_ref, pw_ref, ps_ref, o_ref, pad_ref, *, seq):
    zeros = jnp.zeros((POOL_HALO, POOL_WIDTH), F32)
    lane = lax.broadcasted_iota(jnp.int32, (POOL_CHUNK, 128), 1)
    low_group = lane < POOL_GC
    for i in range(pad_ref.shape[0]):
        pad_ref[i, 0:POOL_HALO, :] = zeros
        pad_ref[i, POOL_HALO + seq:, :] = zeros
        pad_ref[i, POOL_HALO:POOL_HALO + seq, :] = b_ref[i * seq:(i + 1) * seq, :]

    for i, c in [(i, c) for i in range(pad_ref.shape[0]) for c in range(seq // POOL_CHUNK)]:
        r0 = c * POOL_CHUNK
        t = lax.broadcasted_iota(jnp.int32, (POOL_CHUNK, 128), 0) + r0

        interior = r0 >= POOL_HALO and r0 + POOL_CHUNK + POOL_HALO <= seq

        def inv_count(w):
            if interior:
                return 1.0 / w
            left = w // 2
            right = w - 1 - left
            lo = jnp.maximum(t - left, 0)
            hi = jnp.minimum(t + right, seq - 1)
            return 1.0 / (hi - lo + 1).astype(F32)

        def ld(off, col):
            start = POOL_HALO + r0 + off
            return pad_ref[i, start:start + POOL_CHUNK, 128 * col:128 * (col + 1)]

        u0 = ld(0, 0)
        p2 = ld(-1, 0) + u0
        p4 = p2 + ld(-2, 0) + ld(1, 0)
        pooled0 = jnp.where(low_group, p2 * inv_count(2), p4 * inv_count(4)) - u0
        u1 = ld(0, 1)
        p8 = u1
        for off in (-4, -3, -2, -1, 1, 2, 3):
            p8 = p8 + ld(off, 1)
        p16 = p8
        for off in (-8, -7, -6, -5, 4, 5, 6, 7):
            p16 = p16 + ld(off, 1)
        pooled1 = jnp.where(low_group, p8 * inv_count(8), p16 * inv_count(16)) - u1

        pooled = jnp.concatenate([pooled0, pooled1], axis=1).astype(BF16)
        mixed = _dot(pooled, pw_ref[...]) * ps_ref[...]
        rows = slice(i * seq + r0, i * seq + r0 + POOL_CHUNK)
        o_ref[rows, :] = (mixed * sb_ref[rows, :].astype(F32)).astype(BF16)


def _pool(b, sb, pw, l, seq, seqs_per_step):
    t = b.shape[0]
    rows = seq * seqs_per_step
    row = lambda i: (i, 0)
    return pl.pallas_call(
        functools.partial(_pool_kernel, seq=seq),
        grid=(t // rows,),
        in_specs=[pl.BlockSpec((rows, POOL_WIDTH), row), pl.BlockSpec((rows, POOL_WIDTH), row),
                  _layer_spec(pw['pool_w'], l), _layer_spec(pw['pool_scale'], l)],
        out_specs=pl.BlockSpec((rows, POOL_WIDTH), row),
        out_shape=jax.ShapeDtypeStruct((t, POOL_WIDTH), BF16),
        scratch_shapes=[pltpu.VMEM((seqs_per_step, seq + 2 * POOL_HALO, POOL_WIDTH), F32)],
        compiler_params=_params(1),
        name="pool",
    )(b, sb, pw['pool_w'], pw['pool_scale'])


WEIGH_STRIP = 256


def _attn_kernel(*refs, heads, chunk, use_cache, lookahead, seqs):
    if use_cache:
        q_ref, k_ref, vt_ref, kc_ref, vct_ref, sc_ref, o_ref, s_ref, m_ref, l_ref, acc_ref = refs
        sources = ((k_ref, vt_ref), (kc_ref, vct_ref))
    else:
        q_ref, k_ref, vt_ref, sc_ref, o_ref, s_ref, m_ref, l_ref, acc_ref = refs
        sources = ((k_ref, vt_ref),)
    tq = q_ref.shape[0] // seqs
    slots = s_ref.shape[0]
    problems = [(i, h) for i in range(seqs) for h in range(heads)]

    def chunks_of(i):
        out, row = [], 0
        for keys, values in sources:
            n = keys.shape[0] // seqs
            for off in range(0, n, chunk):
                size = min(chunk, n - off)
                out.append((keys, values, i * n + off, size, row))
                row += size
        return out

    n_chunks = len(chunks_of(0))

    def scores(j, c):
        i, h = problems[j]
        keys, _, off, size, row = chunks_of(i)[c]
        sl = slice(HEAD_SLAB * h, HEAD_SLAB * (h + 1))
        q = q_ref[i * tq:(i + 1) * tq, sl]
        s = lax.dot_general(keys[off:off + size, sl], q, _NT, preferred_element_type=F32)
        s_ref[j % slots, row:row + size, :] = s
        m_ref[j % slots] = jnp.maximum(m_ref[j % slots], jnp.max(s.reshape(size // 8, 8, tq), axis=0))

    def weigh(j, c, m):
        i, h = problems[j]
        _, values, off, size, row = chunks_of(i)[c]
        v_h = values[V_DIM * h:V_DIM * (h + 1), off:off + size]
        strip = min(tq, WEIGH_STRIP)
        for q0 in range(0, tq, strip):
            cols = slice(q0, q0 + strip)
            p = jnp.exp2(s_ref[j % slots, row:row + size, cols] - m[:, cols])
            l_ref[j % slots, :, cols] += jnp.sum(p.reshape(size // 8, 8, strip), axis=0)
            acc_ref[j % slots, :, cols] += _dot(v_h, p.astype(BF16))

    outs = []
    for t in range(len(problems) + lookahead):
        j_w, j_s = t - lookahead, t
        if j_s < len(problems):
            m_ref[j_s % slots] = jnp.full((8, tq), -jnp.inf, F32)
        if j_w >= 0:
            m = jnp.max(m_ref[j_w % slots], axis=0, keepdims=True)
            l_ref[j_w % slots] = jnp.zeros((8, tq), F32)
            acc_ref[j_w % slots] = jnp.zeros((V_DIM, tq), F32)
        for c in range(n_chunks):
            if j_w >= 0:
                weigh(j_w, c, m)
            if j_s < len(problems):
                scores(j_s, c)
        if j_w >= 0:
            i, h = problems[j_w]
            denom = jnp.sum(l_ref[j_w % slots], axis=0, keepdims=True)
            outs.append(acc_ref[j_w % slots] * (1.0 / denom))
            if h % 2 == 1:
                o_pair = jnp.concatenate(outs, axis=0).T
                outs = []
                rows = slice(i * tq, (i + 1) * tq)
                sl = slice(HEAD_SLAB * (h // 2), HEAD_SLAB * (h // 2 + 1))
                o_ref[rows, sl] = (o_pair * sc_ref[rows, sl].astype(F32)).astype(BF16)


def _attention(q, k, vt, sc, cache, batch, lq, lk, tq, heads_per_step, chunk, lookahead, seqs_per_step=1):
    use_cache = cache is not None
    assert seqs_per_step == 1 or (tq == lq and not use_cache)
    n = seqs_per_step
    nq = lq // tq
    n_hp = N_HEADS // heads_per_step
    qw = heads_per_step * HEAD_SLAB
    ow = heads_per_step * V_DIM
    q_map = lambda b, g, i: (b * nq + i, g)
    k_map = lambda b, g, i: (b, g)
    vt_map = lambda b, g, i: (g, b)
    in_specs = [pl.BlockSpec((n * tq, qw), q_map), pl.BlockSpec((n * lk, qw), k_map),
                pl.BlockSpec((ow, n * lk), vt_map)]
    args = [q, k, vt]
    lc = 0
    if use_cache:
        lc = cache[0].shape[0] // batch
        in_specs += [pl.BlockSpec((lc, qw), k_map), pl.BlockSpec((ow, lc), vt_map)]
        args += list(cache)
    in_specs.append(pl.BlockSpec((n * tq, ow), q_map))
    args.append(sc)
    slots = min(n * heads_per_step, lookahead + 1)
    return pl.pallas_call(
        functools.partial(_attn_kernel, heads=heads_per_step, chunk=chunk, use_cache=use_cache, lookahead=lookahead,
                          seqs=n),
        grid=(batch // n, n_hp, nq),
        in_specs=in_specs,
        out_specs=pl.BlockSpec((n * tq, ow), q_map),
        out_shape=jax.ShapeDtypeStruct((batch * lq, ATT_WIDTH), BF16),
        scratch_shapes=[pltpu.VMEM((slots, lk + lc, tq), F32), pltpu.VMEM((slots, 8, tq), F32),
                        pltpu.VMEM((slots, 8, tq), F32), pltpu.VMEM((slots, V_DIM, tq), F32)],
        compiler_params=_params(3),
        name="attention_cache" if use_cache else "attention",
    )(*args)


def _out_kernel(*refs, final):
    if final:
        h_ref, mod_ref, ng_ref, xa_ref, xb_ref, xc_ref, wa_ref, wb_ref, wc_ref, wg_ref, wo_ref, fg_ref, o_ref = refs
    else:
        h_ref, mod_ref, ng_ref, xa_ref, xb_ref, xc_ref, wa_ref, wb_ref, wc_ref, wg_ref, wo_ref, o_ref = refs
    h = h_ref[...]
    xn = _modulated_norm(h, ng_ref[...], mod_ref).astype(BF16)
    y = None
    for i, (x_ref, w_ref) in enumerate(((xa_ref, wa_ref), (xb_ref, wb_ref), (xc_ref, wc_ref))):
        wg_i = wg_ref[i * D_MODEL:(i + 1) * D_MODEL, :]
        g = jax.nn.sigmoid(lax.dot_general(xn, wg_i, _NT, preferred_element_type=F32))
        term = g * _dot(x_ref[...], w_ref[...])
        y = term if y is None else y + term
    h_new = h + mod_ref[2:3, :] * _dot(y.astype(BF16), wo_ref[...])
    if final:
        o_ref[...] = _rms(h_new, fg_ref[...])
    else:
        o_ref[...] = h_new


def _out(h, mod, mod_row, xa, xb, xc, pw, l, final_g, tm):
    t = h.shape[0]
    final = final_g is not None
    row = lambda i: (i, 0)
    names = ['w_br_a', 'w_br_b', 'w_br_c', 'wg', 'w_out']
    in_specs = ([pl.BlockSpec((tm, D_MODEL), row), _mod_spec(l, mod_row), _layer_spec(pw['norm_g'], l),
                 pl.BlockSpec((tm, FN_WIDTH), row), pl.BlockSpec((tm, POOL_WIDTH), row),
                 pl.BlockSpec((tm, ATT_WIDTH), row)] + [_layer_spec(pw[n], l) for n in names])
    args = [h, mod, pw['norm_g'], xa, xb, xc] + [pw[n] for n in names]
    if final:
        in_specs.append(pl.BlockSpec((1, D_MODEL), lambda i: (0, 0)))
        args.append(final_g)
    return pl.pallas_call(
        functools.partial(_out_kernel, final=final),
        grid=(t // tm,),
        in_specs=in_specs,
        out_specs=pl.BlockSpec((tm, D_MODEL), row),
        out_shape=jax.ShapeDtypeStruct((t, D_MODEL), F32),
        compiler_params=_params(1),
        name="out_final" if final else "out",
    )(*args)


_ROPE_PAD = ((QK_NOPE, HEAD_SLAB - QK_NOPE - QK_ROPE),)
W1_ROW_TILE = 256
W1_PLAIN_TILES = _W1_KV[0] // W1_ROW_TILE
WG_ROW_TILE = 512


def _w1_source_row(k):
    direct = _OFF_KV // W1_ROW_TILE
    return jnp.where(k < direct, k * W1_ROW_TILE,
                     jnp.where(k < W1_PLAIN_TILES, _OFF_CZ + (k - direct) * W1_ROW_TILE, _OFF_KV))


def _pack_w1_kernel(w_ref, o_ref):
    k = pl.program_id(1)
    x = w_ref[0]

    @pl.when(k < W1_PLAIN_TILES)
    def _():
        o_ref[...] = x.astype(BF16)

    @pl.when(k == W1_PLAIN_TILES)
    def _():
        kr = x[KV_RANK:KV_RANK + QK_ROPE, :]
        q = QK_ROPE // 4
        rot = jnp.concatenate([-kr[q:2 * q], kr[:q], -kr[3 * q:], kr[2 * q:3 * q]], axis=0)
        zeros = jnp.zeros((QK_NOPE, D_MODEL), F32)
        o_ref[...] = jnp.concatenate([x[:KV_RANK], zeros, kr, rot], axis=0).astype(BF16)


def _pack_cast_kernel(w_ref, o_ref):
    o_ref[...] = w_ref[0].astype(BF16)


def _pack_w_in(w_in):
    wt = jnp.swapaxes(w_in, 1, 2)
    g_width = w_in.shape[2] - _OFF_G
    w1t = pl.pallas_call(
        _pack_w1_kernel,
        grid=(DEPTH, W1_WIDTH // W1_ROW_TILE),
        in_specs=[pl.BlockSpec((pl.Element(1), pl.Element(W1_ROW_TILE), pl.Element(D_MODEL)),
                               lambda l, k: (l, pl.multiple_of(_w1_source_row(k), 32), 0))],
        out_specs=pl.BlockSpec((None, W1_ROW_TILE, D_MODEL), lambda l, k: (l, k, 0)),
        out_shape=jax.ShapeDtypeStruct((DEPTH, W1_WIDTH, D_MODEL), BF16),
        compiler_params=_params(2),
        name="pack_w1",
    )(wt)
    wgt = pl.pallas_call(
        _pack_cast_kernel,
        grid=(DEPTH, g_width // WG_ROW_TILE),
        in_specs=[pl.BlockSpec((pl.Element(1), pl.Element(WG_ROW_TILE), pl.Element(D_MODEL)),
                               lambda l, k: (l, pl.multiple_of(_OFF_G + k * WG_ROW_TILE, 32), 0))],
        out_specs=pl.BlockSpec((None, WG_ROW_TILE, D_MODEL), lambda l, k: (l, k, 0)),
        out_shape=jax.ShapeDtypeStruct((DEPTH, g_width, D_MODEL), BF16),
        compiler_params=_params(2),
        name="pack_wg",
    )(wt)
    return w1t, wgt


def _pack_weights(norm_g, w_in, pool_w, pool_scale, q_norm_g, w_q_up, kv_norm_g, w_kv_up, w_br_a, w_br_b, w_br_c,
                  w_out):
    lead = ((0, 0), (0, 0))
    w1, wg = _pack_w_in(w_in)
    wide = N_HEADS * HEAD_SLAB
    wq_h = w_q_up.reshape(DEPTH, Q_RANK, N_HEADS, QK_NOPE + QK_ROPE)
    wq = jnp.pad(wq_h, lead + ((0, 0), (0, _ROPE_PAD[0][1]))).reshape(DEPTH, Q_RANK, wide).astype(BF16)
    wkv_h = w_kv_up.reshape(DEPTH, KV_RANK, N_HEADS, QK_NOPE + V_DIM)
    wk = jnp.pad(wkv_h[..., :QK_NOPE], lead + ((0, 0), (0, HEAD_SLAB - QK_NOPE))).reshape(DEPTH, KV_RANK, wide)
    wvt = wkv_h[..., QK_NOPE:].reshape(DEPTH, KV_RANK, ATT_WIDTH).transpose(0, 2, 1)
    groups = len(POOL_WINDOWS)
    eye = jnp.eye(groups, dtype=F32)
    pool_bd = (pool_w[:, :, :, None, :] * eye[None, :, None, :, None]).reshape(DEPTH, POOL_WIDTH, POOL_WIDTH)
    return {
        'norm_g': norm_g.reshape(DEPTH, 1, D_MODEL), 'w1': w1, 'wg': wg,
        'q_norm_g': q_norm_g.reshape(DEPTH, 1, Q_RANK), 'wq': wq,
        'kv_norm_g': kv_norm_g.reshape(DEPTH, 1, KV_RANK), 'wk': wk.astype(BF16), 'wvt': wvt.astype(BF16),
        'pool_w': pool_bd.astype(BF16), 'pool_scale': pool_scale.reshape(DEPTH, 1, POOL_WIDTH),
        'w_br_a': w_br_a.astype(BF16), 'w_br_b': w_br_b.astype(BF16), 'w_br_c': w_br_c.astype(BF16),
        'w_out': w_out.astype(BF16),
    }


def _rope_tables(seq):
    f32 = np.float32
    t = np.arange(seq)
    row = (t // GRID_W).astype(f32)
    col = (t % GRID_W).astype(f32)
    half = QK_ROPE // 2
    freqs = f32(ROPE_THETA) ** (-np.arange(0, half, 2, dtype=f32) / f32(half))
    ar = row[:, None] * freqs
    ac = col[:, None] * freqs
    cos = np.ones((seq, HEAD_SLAB), f32)
    sin = np.zeros((seq, HEAD_SLAB), f32)
    cos[:, QK_NOPE:QK_NOPE + QK_ROPE] = np.concatenate([np.cos(ar), np.cos(ar), np.cos(ac), np.cos(ac)], axis=-1)
    sin[:, QK_NOPE:QK_NOPE + QK_ROPE] = np.concatenate([np.sin(ar), np.sin(ar), np.sin(ac), np.sin(ac)], axis=-1)
    return cos, sin


TOKEN_TILE = 1024
SAMPLE_Q_TILE = 512
SAMPLE_HEADS_PER_STEP = 8
SAMPLE_KEY_CHUNK = 2048
PROMPT_SEQS_PER_STEP = 4


def kernel(x_prompt, x_sample, cache_ckv, cache_krope, c, c_ctx, norm_g, w_mod, b_mod, w_in, pool_w, pool_scale,
           q_norm_g, w_q_up, kv_norm_g, w_kv_up, w_br_a, w_br_b, w_br_c, w_out, final_norm_g):
    batch, seq, _ = x_prompt.shape
    dec_batch, dec_seq, _ = x_sample.shape
    past = cache_ckv.shape[2]
    tm = TOKEN_TILE

    mod_rows = 8
    cvec = jnp.concatenate([c_ctx[None, :], c, jnp.zeros((mod_rows - 1 - dec_batch, D_MODEL), F32)], axis=0)
    mod = _modulation(cvec, w_mod, b_mod).reshape(DEPTH, mod_rows, 3, D_MODEL)
    prompt_row = lambda i: 0
    tiles_per_sample = dec_seq // tm
    assert past % HEAD_SLAB == 0 and dec_seq % SAMPLE_KEY_CHUNK == 0
    sample_row = lambda i: 1 + i // tiles_per_sample

    cos, sin = _rope_tables(dec_seq)
    rope = (cos, sin, lambda i: i % tiles_per_sample)
    final_g = final_norm_g.reshape(1, D_MODEL)
    pw = _pack_weights(norm_g, w_in, pool_w, pool_scale, q_norm_g, w_q_up, kv_norm_g, w_kv_up, w_br_a, w_br_b,
                       w_br_c, w_out)
    cache_kr_slab = jnp.pad(cache_krope, ((0, 0), (0, 0), (0, 0)) + _ROPE_PAD)

    hp = x_prompt.reshape(batch * seq, D_MODEL)
    hs = x_sample.reshape(dec_batch * dec_seq, D_MODEL)
    ckv_list, kr_list = [], []
    for l in range(DEPTH):
        last = final_g if l == DEPTH - 1 else None

        a_in, sa, b_in, sb, q, k, vt, ckv, kr, sc = _inproj(hp, mod, prompt_row, pw, l, None, tm)
        ckv_list.append(ckv.reshape(batch, seq, KV_RANK))
        kr_list.append(kr.reshape(batch, seq, QK_ROPE))
        xa = _fourier_direct(a_in, sa, seq, PROMPT_SEQS_PER_STEP)
        xb = _pool(b_in, sb, pw, l, seq, PROMPT_SEQS_PER_STEP)
        xc = _attention(q, k, vt, sc, None, batch, seq, seq, seq, N_HEADS, seq,
                        lookahead=N_HEADS * PROMPT_SEQS_PER_STEP, seqs_per_step=PROMPT_SEQS_PER_STEP)
        hp = _out(hp, mod, prompt_row, xa, xb, xc, pw, l, last, tm)

        a_in, sa, b_in, sb, q, k, vt, _, _, sc = _inproj(hs, mod, sample_row, pw, l, rope, tm)
        cache = _cache_kv(cache_ckv, cache_kr_slab, pw, l)
        xa = _fourier_fft(a_in, sa, dec_seq)
        xb = _pool(b_in, sb, pw, l, dec_seq, 1)
        xc = _attention(q, k, vt, sc, cache, dec_batch, dec_seq, dec_seq, SAMPLE_Q_TILE, SAMPLE_HEADS_PER_STEP,
                        SAMPLE_KEY_CHUNK, lookahead=1)
        hs = _out(hs, mod, sample_row, xa, xb, xc, pw, l, last, tm)

    y_prompt = hp.reshape(batch, seq, D_MODEL)
    y_sample = hs.reshape(dec_batch, dec_seq, D_MODEL)
    return (y_prompt, y_sample, jnp.stack(ckv_list, axis=1), jnp.stack(kr_list, axis=1))
```

```python
import functools

import numpy as np
import jax
import jax.numpy as jnp
from jax import lax
from jax.experimental import pallas as pl
from jax.experimental.pallas import tpu as pltpu

D_MODEL = 1024
DEPTH = 2
GRID_W = 64
EPS = 1e-6
FN_WIDTH = 256
FN_GC = 64
POOL_WINDOWS = (2, 4, 8, 16)
POOL_WIDTH = 256
POOL_GC = 64
N_HEADS = 8
QK_NOPE = 64
QK_ROPE = 32
V_DIM = 64
Q_RANK = 256
KV_RANK = 128
ATT_WIDTH = 512
ROPE_THETA = 10000.0
HEAD_SLAB = 128
QK_SCALE = (QK_NOPE + QK_ROPE) ** -0.5
Q_PRESCALE = QK_SCALE * float(np.log2(np.e))

VMEM_LIMIT_BYTES = 56 * 1024 * 1024

F32 = jnp.float32
BF16 = jnp.bfloat16

_OFF_A, _OFF_B, _OFF_Q, _OFF_KV, _OFF_KR, _OFF_CZ, _OFF_G = 0, 512, 1024, 1280, 1408, 1440, 1952
_W1_A = (0, 512)
_W1_B = (512, 1024)
_W1_Q = (1024, 1280)
_W1_CZ = (1280, 1792)
_W1_KV = (1792, 2048)
W1_WIDTH = 2048


def _params(n_parallel):
    return pltpu.CompilerParams(dimension_semantics=("arbitrary",) * n_parallel,
                                vmem_limit_bytes=VMEM_LIMIT_BYTES)


def _dot(a, b):
    return jnp.dot(a, b, preferred_element_type=F32)


def _silu(x):
    return x * jax.nn.sigmoid(x)


def _rms(x, g):
    r = lax.rsqrt(jnp.mean(x * x, axis=-1, keepdims=True) + EPS)
    return (x * r) * g


def _modulated_norm(h, norm_g, mod_ref):
    shift = mod_ref[0:1, :]
    scale = mod_ref[1:2, :]
    return _rms(h, norm_g) * (1.0 + scale) + shift


def _split_bf16(x):
    hi = x.astype(BF16)
    lo = (x - hi.astype(F32)).astype(BF16)
    return hi, lo


def _dot3_right(x, m_hi, m_lo):
    x_hi, x_lo = _split_bf16(x)
    return _dot(x_hi, m_hi) + _dot(x_lo, m_hi) + _dot(x_hi, m_lo)


def _dot3_left(m_hi, m_lo, x):
    x_hi, x_lo = _split_bf16(x)
    return _dot(m_hi, x_hi) + _dot(m_hi, x_lo) + _dot(m_lo, x_hi)


def _mod_kernel(c_ref, w_ref, b_ref, o_ref):
    s = _silu(c_ref[...]).astype(BF16)
    o_ref[...] = _dot(s, w_ref[...].astype(BF16)) + b_ref[...]


def _modulation(cvec, w_mod, b_mod):
    rows = cvec.shape[0]
    tn = 768
    return pl.pallas_call(
        _mod_kernel,
        grid=(DEPTH, 3 * D_MODEL // tn),
        in_specs=[pl.BlockSpec((rows, D_MODEL), lambda l, j: (0, 0)),
                  pl.BlockSpec((None, D_MODEL, tn), lambda l, j: (l, 0, j)),
                  pl.BlockSpec((None, 1, tn), lambda l, j: (l, 0, j))],
        out_specs=pl.BlockSpec((None, rows, tn), lambda l, j: (l, 0, j)),
        out_shape=jax.ShapeDtypeStruct((DEPTH, rows, 3 * D_MODEL), F32),
        compiler_params=_params(2),
        name="modulation",
    )(cvec, w_mod, b_mod.reshape(DEPTH, 1, 3 * D_MODEL))


_NT = (((1,), (1,)), ((), ()))


def _key_value_heads(ckv_bf16, kr_slab, wk_ref, wvt_ref, k_ref, vt_ref):
    kn = _dot(ckv_bf16, wk_ref[...])
    for h in range(N_HEADS):
        sl = slice(HEAD_SLAB * h, HEAD_SLAB * (h + 1))
        k_ref[:, sl] = (kn[:, sl] + kr_slab).astype(BF16)
    vt_ref[...] = lax.dot_general(wvt_ref[...], ckv_bf16, _NT, preferred_element_type=F32).astype(BF16)


def _inproj_kernel(*refs, use_rope):
    if use_rope:
        (h_ref, mod_ref, ng_ref, w1_ref, qg_ref, wq_ref, kvg_ref, wk_ref, wvt_ref, cos_ref, sin_ref,
         ain_ref, sa_ref, bin_ref, sb_ref, q_ref, k_ref, vt_ref, ckv_ref, kr_ref, sc_ref) = refs
    else:
        (h_ref, mod_ref, ng_ref, w1_ref, qg_ref, wq_ref, kvg_ref, wk_ref, wvt_ref,
         ain_ref, sa_ref, bin_ref, sb_ref, q_ref, k_ref, vt_ref, ckv_ref, kr_ref, sc_ref) = refs

    xn = _modulated_norm(h_ref[...], ng_ref[...], mod_ref).astype(BF16)

    def proj(cols):
        return lax.dot_general(xn, w1_ref[cols[0]:cols[1], :], _NT, preferred_element_type=F32)

    a = proj(_W1_A)
    ain_ref[...] = a[:, :FN_WIDTH]
    sa_ref[...] = _silu(a[:, FN_WIDTH:]).astype(BF16)
    b = proj(_W1_B)
    bin_ref[...] = b[:, :POOL_WIDTH]
    sb_ref[...] = _silu(b[:, POOL_WIDTH:]).astype(BF16)
    sc_ref[...] = _silu(proj(_W1_CZ)).astype(BF16)

    qn = _rms(proj(_W1_Q), qg_ref[...]).astype(BF16)
    q = _dot(qn, wq_ref[...])
    lane = lax.broadcasted_iota(jnp.int32, (q.shape[0], HEAD_SLAB), 1)
    if use_rope:
        cos = cos_ref[...]
        sin = sin_ref[...]
        takes_upper = (lane % (QK_ROPE // 2)) < (QK_ROPE // 4)
        for h in range(N_HEADS):
            sl = slice(HEAD_SLAB * h, HEAD_SLAB * (h + 1))
            q_h = q[:, sl]
            upper = pltpu.roll(q_h, HEAD_SLAB - QK_ROPE // 4, axis=1)
            lower = pltpu.roll(q_h, QK_ROPE // 4, axis=1)
            q_rot = jnp.where(takes_upper, -upper, lower)
            q_ref[:, sl] = ((q_h * cos + q_rot * sin) * Q_PRESCALE).astype(BF16)
    else:
        q_ref[...] = (q * Q_PRESCALE).astype(BF16)

    kv = proj(_W1_KV)
    ckv = _rms(kv[:, :KV_RANK], kvg_ref[...])
    ckv_ref[...] = ckv
    slab = kv[:, KV_RANK:]
    kr_ref[...] = slab[:, QK_NOPE:QK_NOPE + QK_ROPE]
    kr = jnp.where(lane < QK_NOPE + QK_ROPE, slab, 0.0)
    if use_rope:
        kr_rot = pltpu.roll(slab, HEAD_SLAB - QK_ROPE, axis=1)
        kr = kr * cos + kr_rot * sin
    _key_value_heads(ckv.astype(BF16), kr, wk_ref, wvt_ref, k_ref, vt_ref)


def _layer_spec(arr, l):
    return pl.BlockSpec((None,) + arr.shape[1:], lambda *_: (l,) + (0,) * (arr.ndim - 1),
                        pipeline_mode=pl.Buffered(1))


def _mod_spec(l, mod_row):
    return pl.BlockSpec((None, None, 3, D_MODEL), lambda i: (l, mod_row(i), 0, 0))


def _inproj(h, mod, mod_row, pw, l, rope, tm):
    t = h.shape[0]
    use_rope = rope is not None
    row = lambda i: (i, 0)
    names = ['norm_g', 'w1', 'q_norm_g', 'wq', 'kv_norm_g', 'wk', 'wvt']
    in_specs = [pl.BlockSpec((tm, D_MODEL), row), _mod_spec(l, mod_row)] + [_layer_spec(pw[n], l) for n in names]
    args = [h, mod] + [pw[n] for n in names]
    if use_rope:
        cos, sin, rope_tile = rope
        in_specs += [pl.BlockSpec((tm, HEAD_SLAB), lambda i: (rope_tile(i), 0))] * 2
        args += [cos, sin]
    wide = N_HEADS * HEAD_SLAB
    token_outs = lambda ws: ([pl.BlockSpec((tm, w), row) for w, _ in ws],
                             [jax.ShapeDtypeStruct((t, w), dt) for w, dt in ws])
    specs_a, shapes_a = token_outs([(FN_WIDTH, F32), (FN_WIDTH, BF16), (POOL_WIDTH, F32), (POOL_WIDTH, BF16),
                                    (wide, BF16)])
    specs_b, shapes_b = token_outs([(KV_RANK, F32), (QK_ROPE, F32), (ATT_WIDTH, BF16)])
    out_specs = specs_a + [pl.BlockSpec((tm, wide), row), pl.BlockSpec((ATT_WIDTH, tm), lambda i: (0, i))] + specs_b
    out_shape = shapes_a + [jax.ShapeDtypeStruct((t, wide), BF16), jax.ShapeDtypeStruct((ATT_WIDTH, t), BF16)] + shapes_b
    return pl.pallas_call(
        functools.partial(_inproj_kernel, use_rope=use_rope),
        grid=(t // tm,),
        in_specs=in_specs,
        out_specs=out_specs,
        out_shape=out_shape,
        compiler_params=_params(1),
        name="inproj_rope" if use_rope else "inproj",
    )(*args)


def _cache_kv_kernel(ckv_ref, kr_ref, wk_ref, wvt_ref, k_ref, vt_ref):
    _key_value_heads(ckv_ref[...].astype(BF16), kr_ref[...], wk_ref, wvt_ref, k_ref, vt_ref)


def _cache_kv(cache_ckv, cache_kr_slab, pw, l):
    batch, _, past, _ = cache_ckv.shape
    wide = N_HEADS * HEAD_SLAB
    cache_map = lambda b: (b, l, 0, 0)
    return pl.pallas_call(
        _cache_kv_kernel,
        grid=(batch,),
        in_specs=[pl.BlockSpec((None, None, past, KV_RANK), cache_map),
                  pl.BlockSpec((None, None, past, HEAD_SLAB), cache_map),
                  _layer_spec(pw['wk'], l), _layer_spec(pw['wvt'], l)],
        out_specs=[pl.BlockSpec((past, wide), lambda b: (b, 0)), pl.BlockSpec((ATT_WIDTH, past), lambda b: (0, b))],
        out_shape=[jax.ShapeDtypeStruct((batch * past, wide), BF16),
                   jax.ShapeDtypeStruct((ATT_WIDTH, batch * past), BF16)],
        compiler_params=_params(1),
        name="cache_kv",
    )(cache_ckv, cache_kr_slab, pw['wk'], pw['wvt'])


def _hi_lo(m):
    m = np.asarray(m, np.float32)
    hi = m.astype(BF16)
    return hi, (m - hi.astype(np.float32)).astype(BF16)


def _dft_cos_sin(n):
    k = np.arange(n)
    ang = 2.0 * np.pi * ((k[:, None] * k[None, :]) % n) / n
    return np.cos(ang), np.sin(ang)


def _channel_dft_tables():
    c, s = _dft_cos_sin(FN_GC)
    eye = np.eye(FN_WIDTH // FN_GC)
    return np.kron(eye, c), np.kron(eye, s)


def _fourier_direct_kernel(a_ref, sa_ref, cc_hi, cc_lo, ss_hi, ss_lo, m_hi, m_lo, o_ref, *, seq, norm):
    n = a_ref.shape[0] // seq
    a = a_ref[...]
    tc = _dot3_right(a, cc_hi[...], cc_lo[...])
    ts = _dot3_right(a, ss_hi[...], ss_lo[...])
    side_by_side = lambda x: jnp.concatenate([x[i * seq:(i + 1) * seq] for i in range(n)], axis=1)
    f = _dot3_left(m_hi[...], m_lo[...], jnp.concatenate([side_by_side(tc), side_by_side(ts)], axis=0))
    for i in range(n):
        rows = slice(i * seq, (i + 1) * seq)
        f_i = f[:, i * FN_WIDTH:(i + 1) * FN_WIDTH]
        o_ref[rows, :] = ((f_i * norm) * sa_ref[rows, :].astype(F32)).astype(BF16)


def _fourier_direct(a, sa, seq, seqs_per_step):
    t = a.shape[0]
    rows = seq * seqs_per_step
    cc, ss = _channel_dft_tables()
    cl, sl = _dft_cos_sin(seq)
    tables = [x for m in (cc, ss, np.concatenate([cl, -sl], axis=1)) for x in _hi_lo(m)]
    row = lambda i: (i, 0)
    const = lambda i: (0, 0)
    tspecs = [pl.BlockSpec((FN_WIDTH, FN_WIDTH), const)] * 4 + [pl.BlockSpec((seq, 2 * seq), const)] * 2
    return pl.pallas_call(
        functools.partial(_fourier_direct_kernel, seq=seq, norm=float((seq * FN_GC) ** -0.5)),
        grid=(t // rows,),
        in_specs=[pl.BlockSpec((rows, FN_WIDTH), row), pl.BlockSpec((rows, FN_WIDTH), row)] + tspecs,
        out_specs=pl.BlockSpec((rows, FN_WIDTH), row),
        out_shape=jax.ShapeDtypeStruct((t, FN_WIDTH), BF16),
        compiler_params=_params(1),
        name="fourier_direct",
    )(a, sa, *tables)


FFT_R = 64
FFT_PITCH = FFT_R + 4


def _fourier_fft_kernel(a_ref, sa_ref, cc_hi, cc_lo, ss_hi, ss_lo, m1_hi, m1_lo, m2_hi, m2_lo, twc_ref, tws_ref,
                        o_ref, zr0, zr1, zi0, zi1, yr0, yr1, yi0, yi1, *, seq, norm):
    r = FFT_R
    half = FN_WIDTH // 2
    chunk = 512

    def put(refs, rows, x):
        refs[0][rows, :] = x[:, :half]
        refs[1][rows, :] = x[:, half:]

    def get(refs, rows):
        return jnp.concatenate([refs[0][rows, :], refs[1][rows, :]], axis=1)

    block = lambda j: slice(FFT_PITCH * j, FFT_PITCH * j + r)
    across = lambda i: pl.ds(i, r, stride=FFT_PITCH)

    zr, zi, yr_s, yi_s = (zr0, zr1), (zi0, zi1), (yr0, yr1), (yi0, yi1)
    for c in range(seq // chunk):
        a = a_ref[c * chunk:(c + 1) * chunk, :]
        zr_c = _dot3_right(a, cc_hi[...], cc_lo[...])
        zi_c = -_dot3_right(a, ss_hi[...], ss_lo[...])
        for j in range(chunk // r):
            n1 = c * (chunk // r) + j
            put(zr, block(n1), zr_c[j * r:(j + 1) * r])
            put(zi, block(n1), zi_c[j * r:(j + 1) * r])
    for n2 in range(r):
        z = jnp.concatenate([get(zr, across(n2)), get(zi, across(n2))], axis=0)
        y = _dot3_left(m1_hi[...], m1_lo[...], z)
        yr, yi = y[:r], y[r:]
        tw = slice(n2 * r, (n2 + 1) * r)
        cos = jnp.concatenate([twc_ref[tw, :]] * 2, axis=1)
        sin = jnp.concatenate([tws_ref[tw, :]] * 2, axis=1)
        put(yr_s, block(n2), yr * cos + yi * sin)
        put(yi_s, block(n2), yi * cos - yr * sin)
    for k1 in range(r):
        y = jnp.concatenate([get(yr_s, across(k1)), get(yi_s, across(k1))], axis=0)
        put(zr, across(k1), _dot3_left(m2_hi[...], m2_lo[...], y))
    for k2 in range(r):
        rows = slice(k2 * r, (k2 + 1) * r)
        o_ref[rows, :] = ((get(zr, block(k2)) * norm) * sa_ref[rows, :].astype(F32)).astype(BF16)


def _fourier_fft(a, sa, seq):
    assert seq == FFT_R * FFT_R
    t = a.shape[0]
    cc, ss = _channel_dft_tables()
    c, s = _dft_cos_sin(FFT_R)
    m1 = np.block([[c, s], [-s, c]])
    m2 = np.concatenate([c, s], axis=1)
    n2 = np.arange(FFT_R)[:, None]
    k1 = np.arange(FFT_R)[None, :]
    ang = (2.0 * np.pi * (n2 * k1) / seq).reshape(seq, 1)
    twc = np.ascontiguousarray(np.broadcast_to(np.cos(ang).astype(np.float32), (seq, 128)))
    tws = np.ascontiguousarray(np.broadcast_to(np.sin(ang).astype(np.float32), (seq, 128)))
    tables = [x for m in (cc, ss, m1, m2) for x in _hi_lo(m)]
    row = lambda i: (i, 0)
    const = lambda i: (0, 0)
    tspecs = ([pl.BlockSpec((FN_WIDTH, FN_WIDTH), const)] * 4 + [pl.BlockSpec((2 * FFT_R, 2 * FFT_R), const)] * 2
              + [pl.BlockSpec((FFT_R, 2 * FFT_R), const)] * 2 + [pl.BlockSpec((seq, 128), const)] * 2)
    return pl.pallas_call(
        functools.partial(_fourier_fft_kernel, seq=seq, norm=float((seq * FN_GC) ** -0.5)),
        grid=(t // seq,),
        in_specs=[pl.BlockSpec((seq, FN_WIDTH), row), pl.BlockSpec((seq, FN_WIDTH), row)] + tspecs,
        out_specs=pl.BlockSpec((seq, FN_WIDTH), row),
        out_shape=jax.ShapeDtypeStruct((t, FN_WIDTH), BF16),
        scratch_shapes=[pltpu.VMEM((FFT_PITCH * FFT_R, FN_WIDTH // 2), F32)] * 8,
        compiler_params=_params(1),
        name="fourier_fft",
    )(a, sa, *tables, twc, tws)


POOL_HALO = 8
POOL_CHUNK = 256


def _pool_kernel(b_ref, sb_ref, pw_ref, ps_ref, o_ref, pad_ref, *, seq):
    zeros = jnp.zeros((POOL_HALO, POOL_WIDTH), F32)
    lane = lax.broadcasted_iota(jnp.int32, (POOL_CHUNK, 128), 1)
    low_group = lane < POOL_GC
    for i in range(pad_ref.shape[0]):
        pad_ref[i, 0:POOL_HALO, :] = zeros
        pad_ref[i, POOL_HALO + seq:, :] = zeros
        pad_ref[i, POOL_HALO:POOL_HALO + seq, :] = b_ref[i * seq:(i + 1) * seq, :]

    for i, c in [(i, c) for i in range(pad_ref.shape[0]) for c in range(seq // POOL_CHUNK)]:
        r0 = c * POOL_CHUNK
        t = lax.broadcasted_iota(jnp.int32, (POOL_CHUNK, 128), 0) + r0

        interior = r0 >= POOL_HALO and r0 + POOL_CHUNK + POOL_HALO <= seq

        def inv_count(w):
            if interior:
                return 1.0 / w
            left = w // 2
            right = w - 1 - left
            lo = jnp.maximum(t - left, 0)
            hi = jnp.minimum(t + right, seq - 1)
            return 1.0 / (hi - lo + 1).astype(F32)

        def ld(off, col):
            start = POOL_HALO + r0 + off
            return pad_ref[i, start:start + POOL_CHUNK, 128 * col:128 * (col + 1)]

        u0 = ld(0, 0)
        p2 = ld(-1, 0) + u0
        p4 = p2 + ld(-2, 0) + ld(1, 0)
        pooled0 = jnp.where(low_group, p2 * inv_count(2), p4 * inv_count(4)) - u0
        u1 = ld(0, 1)
        p8 = u1
        for off in (-4, -3, -2, -1, 1, 2, 3):
            p8 = p8 + ld(off, 1)
        p16 = p8
        for off in (-8, -7, -6, -5, 4, 5, 6, 7):
            p16 = p16 + ld(off, 1)
        pooled1 = jnp.where(low_group, p8 * inv_count(8), p16 * inv_count(16)) - u1

        pooled = jnp.concatenate([pooled0, pooled1], axis=1).astype(BF16)
        mixed = _dot(pooled, pw_ref[...]) * ps_ref[...]
        rows = slice(i * seq + r0, i * seq + r0 + POOL_CHUNK)
        o_ref[rows, :] = (mixed * sb_ref[rows, :].astype(F32)).astype(BF16)


def _pool(b, sb, pw, l, seq, seqs_per_step):
    t = b.shape[0]
    rows = seq * seqs_per_step
    row = lambda i: (i, 0)
    return pl.pallas_call(
        functools.partial(_pool_kernel, seq=seq),
        grid=(t // rows,),
        in_specs=[pl.BlockSpec((rows, POOL_WIDTH), row), pl.BlockSpec((rows, POOL_WIDTH), row),
                  _layer_spec(pw['pool_w'], l), _layer_spec(pw['pool_scale'], l)],
        out_specs=pl.BlockSpec((rows, POOL_WIDTH), row),
        out_shape=jax.ShapeDtypeStruct((t, POOL_WIDTH), BF16),
        scratch_shapes=[pltpu.VMEM((seqs_per_step, seq + 2 * POOL_HALO, POOL_WIDTH), F32)],
        compiler_params=_params(1),
        name="pool",
    )(b, sb, pw['pool_w'], pw['pool_scale'])


WEIGH_STRIP = 256


def _attn_kernel(*refs, heads, chunk, use_cache, lookahead, seqs):
    if use_cache:
        q_ref, k_ref, vt_ref, kc_ref, vct_ref, sc_ref, o_ref, s_ref, m_ref, l_ref, acc_ref = refs
        sources = ((k_ref, vt_ref), (kc_ref, vct_ref))
    else:
        q_ref, k_ref, vt_ref, sc_ref, o_ref, s_ref, m_ref, l_ref, acc_ref = refs
        sources = ((k_ref, vt_ref),)
    tq = q_ref.shape[0] // seqs
    slots = s_ref.shape[0]
    problems = [(i, h) for i in range(seqs) for h in range(heads)]

    def chunks_of(i):
        out, row = [], 0
        for keys, values in sources:
            n = keys.shape[0] // seqs
            for off in range(0, n, chunk):
                size = min(chunk, n - off)
                out.append((keys, values, i * n + off, size, row))
                row += size
        return out

    n_chunks = len(chunks_of(0))

    def scores(j, c):
        i, h = problems[j]
        keys, _, off, size, row = chunks_of(i)[c]
        sl = slice(HEAD_SLAB * h, HEAD_SLAB * (h + 1))
        q = q_ref[i * tq:(i + 1) * tq, sl]
        s = lax.dot_general(keys[off:off + size, sl], q, _NT, preferred_element_type=F32)
        s_ref[j % slots, row:row + size, :] = s
        m_ref[j % slots] = jnp.maximum(m_ref[j % slots], jnp.max(s.reshape(size // 8, 8, tq), axis=0))

    def weigh(j, c, m):
        i, h = problems[j]
        _, values, off, size, row = chunks_of(i)[c]
        v_h = values[V_DIM * h:V_DIM * (h + 1), off:off + size]
        strip = min(tq, WEIGH_STRIP)
        for q0 in range(0, tq, strip):
            cols = slice(q0, q0 + strip)
            p = jnp.exp2(s_ref[j % slots, row:row + size, cols] - m[:, cols])
            l_ref[j % slots, :, cols] += jnp.sum(p.reshape(size // 8, 8, strip), axis=0)
            acc_ref[j % slots, :, cols] += _dot(v_h, p.astype(BF16))

    outs = []
    for t in range(len(problems) + lookahead):
        j_w, j_s = t - lookahead, t
        if j_s < len(problems):
            m_ref[j_s % slots] = jnp.full((8, tq), -jnp.inf, F32)
        if j_w >= 0:
            m = jnp.max(m_ref[j_w % slots], axis=0, keepdims=True)
            l_ref[j_w % slots] = jnp.zeros((8, tq), F32)
            acc_ref[j_w % slots] = jnp.zeros((V_DIM, tq), F32)
        for c in range(n_chunks):
            if j_w >= 0:
                weigh(j_w, c, m)
            if j_s < len(problems):
                scores(j_s, c)
        if j_w >= 0:
            i, h = problems[j_w]
            denom = jnp.sum(l_ref[j_w % slots], axis=0, keepdims=True)
            outs.append(acc_ref[j_w % slots] * (1.0 / denom))
            if h % 2 == 1:
                o_pair = jnp.concatenate(outs, axis=0).T
                outs = []
                rows = slice(i * tq, (i + 1) * tq)
                sl = slice(HEAD_SLAB * (h // 2), HEAD_SLAB * (h // 2 + 1))
                o_ref[rows, sl] = (o_pair * sc_ref[rows, sl].astype(F32)).astype(BF16)


def _attention(q, k, vt, sc, cache, batch, lq, lk, tq, heads_per_step, chunk, lookahead, seqs_per_step=1):
    use_cache = cache is not None
    assert seqs_per_step == 1 or (tq == lq and not use_cache)
    n = seqs_per_step
    nq = lq // tq
    n_hp = N_HEADS // heads_per_step
    qw = heads_per_step * HEAD_SLAB
    ow = heads_per_step * V_DIM
    q_map = lambda b, g, i: (b * nq + i, g)
    k_map = lambda b, g, i: (b, g)
    vt_map = lambda b, g, i: (g, b)
    kv_mode = dict(pipeline_mode=pl.Buffered(1)) if nq > 1 else {}
    in_specs = [pl.BlockSpec((n * tq, qw), q_map), pl.BlockSpec((n * lk, qw), k_map, **kv_mode),
                pl.BlockSpec((ow, n * lk), vt_map, **kv_mode)]
    args = [q, k, vt]
    lc = 0
    if use_cache:
        lc = cache[0].shape[0] // batch
        in_specs += [pl.BlockSpec((lc, qw), k_map, **kv_mode), pl.BlockSpec((ow, lc), vt_map, **kv_mode)]
        args += list(cache)
    in_specs.append(pl.BlockSpec((n * tq, ow), q_map))
    args.append(sc)
    slots = min(n * heads_per_step, lookahead + 1)
    return pl.pallas_call(
        functools.partial(_attn_kernel, heads=heads_per_step, chunk=chunk, use_cache=use_cache, lookahead=lookahead,
                          seqs=n),
        grid=(batch // n, n_hp, nq),
        in_specs=in_specs,
        out_specs=pl.BlockSpec((n * tq, ow), q_map),
        out_shape=jax.ShapeDtypeStruct((batch * lq, ATT_WIDTH), BF16),
        scratch_shapes=[pltpu.VMEM((slots, lk + lc, tq), F32), pltpu.VMEM((slots, 8, tq), F32),
                        pltpu.VMEM((slots, 8, tq), F32), pltpu.VMEM((slots, V_DIM, tq), F32)],
        compiler_params=_params(3),
        name="attention_cache" if use_cache else "attention",
    )(*args)


def _out_kernel(*refs, final):
    if final:
        h_ref, mod_ref, ng_ref, xa_ref, xb_ref, xc_ref, wa_ref, wb_ref, wc_ref, wg_ref, wo_ref, fg_ref, o_ref = refs
    else:
        h_ref, mod_ref, ng_ref, xa_ref, xb_ref, xc_ref, wa_ref, wb_ref, wc_ref, wg_ref, wo_ref, o_ref = refs
    h = h_ref[...]
    xn = _modulated_norm(h, ng_ref[...], mod_ref).astype(BF16)
    y = None
    for i, (x_ref, w_ref) in enumerate(((xa_ref, wa_ref), (xb_ref, wb_ref), (xc_ref, wc_ref))):
        wg_i = wg_ref[i * D_MODEL:(i + 1) * D_MODEL, :]
        g = jax.nn.sigmoid(lax.dot_general(xn, wg_i, _NT, preferred_element_type=F32))
        term = g * _dot(x_ref[...], w_ref[...])
        y = term if y is None else y + term
    h_new = h + mod_ref[2:3, :] * _dot(y.astype(BF16), wo_ref[...])
    if final:
        o_ref[...] = _rms(h_new, fg_ref[...])
    else:
        o_ref[...] = h_new


def _out(h, mod, mod_row, xa, xb, xc, pw, l, final_g, tm):
    t = h.shape[0]
    final = final_g is not None
    row = lambda i: (i, 0)
    names = ['w_br_a', 'w_br_b', 'w_br_c', 'wg', 'w_out']
    in_specs = ([pl.BlockSpec((tm, D_MODEL), row), _mod_spec(l, mod_row), _layer_spec(pw['norm_g'], l),
                 pl.BlockSpec((tm, FN_WIDTH), row), pl.BlockSpec((tm, POOL_WIDTH), row),
                 pl.BlockSpec((tm, ATT_WIDTH), row)] + [_layer_spec(pw[n], l) for n in names])
    args = [h, mod, pw['norm_g'], xa, xb, xc] + [pw[n] for n in names]
    if final:
        in_specs.append(pl.BlockSpec((1, D_MODEL), lambda i: (0, 0)))
        args.append(final_g)
    return pl.pallas_call(
        functools.partial(_out_kernel, final=final),
        grid=(t // tm,),
        in_specs=in_specs,
        out_specs=pl.BlockSpec((tm, D_MODEL), row),
        out_shape=jax.ShapeDtypeStruct((t, D_MODEL), F32),
        compiler_params=_params(1),
        name="out_final" if final else "out",
    )(*args)


_ROPE_PAD = ((QK_NOPE, HEAD_SLAB - QK_NOPE - QK_ROPE),)
W1_ROW_TILE = 256
W1_PLAIN_TILES = _W1_KV[0] // W1_ROW_TILE
WG_ROW_TILE = 512


def _w1_source_row(k):
    direct = _OFF_KV // W1_ROW_TILE
    return jnp.where(k < direct, k * W1_ROW_TILE,
                     jnp.where(k < W1_PLAIN_TILES, _OFF_CZ + (k - direct) * W1_ROW_TILE, _OFF_KV))


def _pack_w1_kernel(w_ref, o_ref):
    k = pl.program_id(1)
    x = w_ref[0]

    @pl.when(k < W1_PLAIN_TILES)
    def _():
        o_ref[...] = x.astype(BF16)

    @pl.when(k == W1_PLAIN_TILES)
    def _():
        kr = x[KV_RANK:KV_RANK + QK_ROPE, :]
        q = QK_ROPE // 4
        rot = jnp.concatenate([-kr[q:2 * q], kr[:q], -kr[3 * q:], kr[2 * q:3 * q]], axis=0)
        zeros = jnp.zeros((QK_NOPE, D_MODEL), F32)
        o_ref[...] = jnp.concatenate([x[:KV_RANK], zeros, kr, rot], axis=0).astype(BF16)


def _pack_cast_kernel(w_ref, o_ref):
    o_ref[...] = w_ref[0].astype(BF16)


def _pack_w_in(w_in):
    wt = jnp.swapaxes(w_in, 1, 2)
    g_width = w_in.shape[2] - _OFF_G
    w1t = pl.pallas_call(
        _pack_w1_kernel,
        grid=(DEPTH, W1_WIDTH // W1_ROW_TILE),
        in_specs=[pl.BlockSpec((pl.Element(1), pl.Element(W1_ROW_TILE), pl.Element(D_MODEL)),
                               lambda l, k: (l, pl.multiple_of(_w1_source_row(k), 32), 0))],
        out_specs=pl.BlockSpec((None, W1_ROW_TILE, D_MODEL), lambda l, k: (l, k, 0)),
        out_shape=jax.ShapeDtypeStruct((DEPTH, W1_WIDTH, D_MODEL), BF16),
        compiler_params=_params(2),
        name="pack_w1",
    )(wt)
    wgt = pl.pallas_call(
        _pack_cast_kernel,
        grid=(DEPTH, g_width // WG_ROW_TILE),
        in_specs=[pl.BlockSpec((pl.Element(1), pl.Element(WG_ROW_TILE), pl.Element(D_MODEL)),
                               lambda l, k: (l, pl.multiple_of(_OFF_G + k * WG_ROW_TILE, 32), 0))],
        out_specs=pl.BlockSpec((None, WG_ROW_TILE, D_MODEL), lambda l, k: (l, k, 0)),
        out_shape=jax.ShapeDtypeStruct((DEPTH, g_width, D_MODEL), BF16),
        compiler_params=_params(2),
        name="pack_wg",
    )(wt)
    return w1t, wgt


def _pack_weights(norm_g, w_in, pool_w, pool_scale, q_norm_g, w_q_up, kv_norm_g, w_kv_up, w_br_a, w_br_b, w_br_c,
                  w_out):
    lead = ((0, 0), (0, 0))
    w1, wg = _pack_w_in(w_in)
    wide = N_HEADS * HEAD_SLAB
    wq_h = w_q_up.reshape(DEPTH, Q_RANK, N_HEADS, QK_NOPE + QK_ROPE)
    wq = jnp.pad(wq_h, lead + ((0, 0), (0, _ROPE_PAD[0][1]))).reshape(DEPTH, Q_RANK, wide).astype(BF16)
    wkv_h = w_kv_up.reshape(DEPTH, KV_RANK, N_HEADS, QK_NOPE + V_DIM)
    wk = jnp.pad(wkv_h[..., :QK_NOPE], lead + ((0, 0), (0, HEAD_SLAB - QK_NOPE))).reshape(DEPTH, KV_RANK, wide)
    wvt = wkv_h[..., QK_NOPE:].reshape(DEPTH, KV_RANK, ATT_WIDTH).transpose(0, 2, 1)
    groups = len(POOL_WINDOWS)
    eye = jnp.eye(groups, dtype=F32)
    pool_bd = (pool_w[:, :, :, None, :] * eye[None, :, None, :, None]).reshape(DEPTH, POOL_WIDTH, POOL_WIDTH)
    return {
        'norm_g': norm_g.reshape(DEPTH, 1, D_MODEL), 'w1': w1, 'wg': wg,
        'q_norm_g': q_norm_g.reshape(DEPTH, 1, Q_RANK), 'wq': wq,
        'kv_norm_g': kv_norm_g.reshape(DEPTH, 1, KV_RANK), 'wk': wk.astype(BF16), 'wvt': wvt.astype(BF16),
        'pool_w': pool_bd.astype(BF16), 'pool_scale': pool_scale.reshape(DEPTH, 1, POOL_WIDTH),
        'w_br_a': w_br_a.astype(BF16), 'w_br_b': w_br_b.astype(BF16), 'w_br_c': w_br_c.astype(BF16),
        'w_out': w_out.astype(BF16),
    }


def _rope_tables(seq):
    f32 = np.float32
    t = np.arange(seq)
    row = (t // GRID_W).astype(f32)
    col = (t % GRID_W).astype(f32)
    half = QK_ROPE // 2
    freqs = f32(ROPE_THETA) ** (-np.arange(0, half, 2, dtype=f32) / f32(half))
    ar = row[:, None] * freqs
    ac = col[:, None] * freqs
    cos = np.ones((seq, HEAD_SLAB), f32)
    sin = np.zeros((seq, HEAD_SLAB), f32)
    cos[:, QK_NOPE:QK_NOPE + QK_ROPE] = np.concatenate([np.cos(ar), np.cos(ar), np.cos(ac), np.cos(ac)], axis=-1)
    sin[:, QK_NOPE:QK_NOPE + QK_ROPE] = np.concatenate([np.sin(ar), np.sin(ar), np.sin(ac), np.sin(ac)], axis=-1)
    return cos, sin


TOKEN_TILE = 1024
SAMPLE_Q_TILE = 512
SAMPLE_HEADS_PER_STEP = 8
SAMPLE_KEY_CHUNK = 2048
PROMPT_SEQS_PER_STEP = 4


def kernel(x_prompt, x_sample, cache_ckv, cache_krope, c, c_ctx, norm_g, w_mod, b_mod, w_in, pool_w, pool_scale,
           q_norm_g, w_q_up, kv_norm_g, w_kv_up, w_br_a, w_br_b, w_br_c, w_out, final_norm_g):
    batch, seq, _ = x_prompt.shape
    dec_batch, dec_seq, _ = x_sample.shape
    past = cache_ckv.shape[2]
    tm = TOKEN_TILE

    mod_rows = 8
    cvec = jnp.concatenate([c_ctx[None, :], c, jnp.zeros((mod_rows - 1 - dec_batch, D_MODEL), F32)], axis=0)
    mod = _modulation(cvec, w_mod, b_mod).reshape(DEPTH, mod_rows, 3, D_MODEL)
    prompt_row = lambda i: 0
    tiles_per_sample = dec_seq // tm
    assert past % HEAD_SLAB == 0 and dec_seq % SAMPLE_KEY_CHUNK == 0
    sample_row = lambda i: 1 + i // tiles_per_sample

    cos, sin = _rope_tables(dec_seq)
    rope = (cos, sin, lambda i: i % tiles_per_sample)
    final_g = final_norm_g.reshape(1, D_MODEL)
    pw = _pack_weights(norm_g, w_in, pool_w, pool_scale, q_norm_g, w_q_up, kv_norm_g, w_kv_up, w_br_a, w_br_b,
                       w_br_c, w_out)
    cache_kr_slab = jnp.pad(cache_krope, ((0, 0), (0, 0), (0, 0)) + _ROPE_PAD)

    hp = x_prompt.reshape(batch * seq, D_MODEL)
    hs = x_sample.reshape(dec_batch * dec_seq, D_MODEL)
    ckv_list, kr_list = [], []
    for l in range(DEPTH):
        last = final_g if l == DEPTH - 1 else None

        a_in, sa, b_in, sb, q, k, vt, ckv, kr, sc = _inproj(hp, mod, prompt_row, pw, l, None, tm)
        ckv_list.append(ckv.reshape(batch, seq, KV_RANK))
        kr_list.append(kr.reshape(batch, seq, QK_ROPE))
        xa = _fourier_direct(a_in, sa, seq, PROMPT_SEQS_PER_STEP)
        xb = _pool(b_in, sb, pw, l, seq, PROMPT_SEQS_PER_STEP)
        xc = _attention(q, k, vt, sc, None, batch, seq, seq, seq, N_HEADS, seq,
                        lookahead=N_HEADS * PROMPT_SEQS_PER_STEP, seqs_per_step=PROMPT_SEQS_PER_STEP)
        hp = _out(hp, mod, prompt_row, xa, xb, xc, pw, l, last, tm)

        a_in, sa, b_in, sb, q, k, vt, _, _, sc = _inproj(hs, mod, sample_row, pw, l, rope, tm)
        cache = _cache_kv(cache_ckv, cache_kr_slab, pw, l)
        xa = _fourier_fft(a_in, sa, dec_seq)
        xb = _pool(b_in, sb, pw, l, dec_seq, 1)
        xc = _attention(q, k, vt, sc, cache, dec_batch, dec_seq, dec_seq, SAMPLE_Q_TILE, SAMPLE_HEADS_PER_STEP,
                        SAMPLE_KEY_CHUNK, lookahead=1)
        hs = _out(hs, mod, sample_row, xa, xb, xc, pw, l, last, tm)

    y_prompt = hp.reshape(batch, seq, D_MODEL)
    y_sample = hs.reshape(dec_batch, dec_seq, D_MODEL)
    return (y_prompt, y_sample, jnp.stack(ckv_list, axis=1), jnp.stack(kr_list, axis=1))
```
